```python
import math
import jax
import jax.numpy as jnp
from jax import lax
import numpy as np

D_MODEL = 1024
BATCH = 2
SEQ = 8192
DEPTH = 2

D_MIX = D_MODEL
POOL_WIDTH = D_MIX // 4
POOL_GROUPS = 4
POOL_GC = POOL_WIDTH // POOL_GROUPS
POOL_WINDOWS = (2, 4, 8, 16)
HEAD_DIM = 64
NSA_WIDTH = D_MIX // 2
NSA_HEADS = NSA_WIDTH // HEAD_DIM
NSA_KV_HEADS = 2
NSA_REP = NSA_HEADS // NSA_KV_HEADS
KV_WIDTH = NSA_KV_HEADS * HEAD_DIM
N_BRANCH = 3
GATE_WIDTH = NSA_HEADS * N_BRANCH
S5_WIDTH = D_MIX - POOL_WIDTH - NSA_WIDTH
S5_H = 16
S5_GROUPS = S5_WIDTH // S5_H
S5_P = 64
CMP_STRIDE = 16
CMP_LEN = 32
CMP_HIDDEN = 128
SLC_BLOCK = 64
N_SELECT = 16
WINDOW = 512
Q_BLOCK = 128
FORCE_BONUS = 1000.0
ROT_DIM = HEAD_DIM // 4
ROPE_THETA = 500000.0
EPS = 1e-6
NEG_INF = -1e30
D_FF = 2816
N_MOD = 9
N_IN = POOL_WIDTH + NSA_WIDTH + 6 * KV_WIDTH + GATE_WIDTH + S5_WIDTH
IN_OFFSETS = (POOL_WIDTH, POOL_WIDTH + NSA_WIDTH, POOL_WIDTH + NSA_WIDTH + 6 * KV_WIDTH,
              POOL_WIDTH + NSA_WIDTH + 6 * KV_WIDTH + GATE_WIDTH)
OUT_OFFSETS = (POOL_WIDTH, POOL_WIDTH + NSA_WIDTH)

kernel_name = 'hybrid_pool_nsa_s5_macaron'


def _rms_norm(x, g):
    xf = x.astype(jnp.float32)
    y = xf * lax.rsqrt(jnp.mean(xf * xf, axis=-1, keepdims=True) + EPS)
    return (y * g.astype(jnp.float32)).astype(x.dtype)


def _modulate(h, shift, scale):
    return h * (1.0 + scale[:, None, :]) + shift[:, None, :]


def _swiglu(h, w_in, w_out):
    gate, up = jnp.split(h @ w_in, 2, axis=-1)
    return (jax.nn.silu(gate) * up) @ w_out


def _rope_partial(x, pos):
    half = ROT_DIM // 2
    inv_freq = jnp.exp(-math.log(ROPE_THETA) * jnp.arange(half, dtype=jnp.float32) * (2.0 / ROT_DIM))
    ang = pos[:, None] * inv_freq[None, :]
    cos, sin = jnp.cos(ang), jnp.sin(ang)
    xf = x.astype(jnp.float32)
    x1, x2 = xf[..., :half], xf[..., half:ROT_DIM]
    out = jnp.concatenate([x1 * cos - x2 * sin, x2 * cos + x1 * sin, xf[..., ROT_DIM:]], axis=-1)
    return out.astype(x.dtype)


def _masked_softmax(s, mask):
    return jax.nn.softmax(jnp.where(mask, s.astype(jnp.float32), NEG_INF), axis=-1)


def _pool_mixer(u, pool_w, pool_b, pool_scale):
    B, S, _ = u.shape
    ug = u.reshape(B, S, POOL_GROUPS, POOL_GC)
    count = jnp.arange(1, S + 1, dtype=jnp.float32)[None, :, None]
    outs = []
    for gi, w in enumerate(POOL_WINDOWS):
        v = ug[:, :, gi].astype(jnp.float32)
        cs = jnp.cumsum(v, axis=1)
        lag = jnp.pad(cs[:, :S - w], ((0, 0), (w, 0), (0, 0)))
        outs.append((cs - lag) / jnp.minimum(count, float(w)) - v)
    pooled = jnp.stack(outs, axis=2).astype(u.dtype)
    y = jnp.einsum('bsgc,gcd->bsgd', pooled, pool_w) + pool_b
    return y.reshape(B, S, POOL_WIDTH) * pool_scale


def _nsa_compress(a, pe, w1, w2):
    B, S, G, HD = a.shape
    ch = a.reshape(B, S // CMP_STRIDE, CMP_STRIDE, G, HD)
    blk = jnp.concatenate([ch[:, :-1], ch[:, 1:]], axis=2) + pe[None, None, :, None, :]
    blk = blk.transpose(0, 1, 3, 2, 4).reshape(B, S // CMP_STRIDE - 1, G, CMP_LEN * HD)
    return (jax.nn.gelu(blk @ w1) @ w2).transpose(0, 2, 1, 3)


def _nsa_mixer(u_q, u_kv, u_gate, q_norm, k_norm, cmp_pe, cmp_k_w1, cmp_k_w2, cmp_v_w1, cmp_v_w2):
    B, S, _ = u_q.shape
    G, R, HD = NSA_KV_HEADS, NSA_REP, HEAD_DIM
    dt = u_q.dtype
    pos = jnp.arange(S, dtype=jnp.float32)
    q = u_q.reshape(B, S, G, R, HD).transpose(0, 2, 3, 1, 4)
    q = _rope_partial(_rms_norm(q, q_norm), pos)
    kc, vc, ks, vs, kw, vw = [a.reshape(B, S, G, HD) for a in jnp.split(u_kv, 6, axis=-1)]

    n_cmp = S // CMP_STRIDE - 1
    cmp_end = jnp.arange(n_cmp) * CMP_STRIDE + CMP_LEN - 1
    k_cmp = _rope_partial(_rms_norm(_nsa_compress(kc, cmp_pe[0], cmp_k_w1, cmp_k_w2), k_norm[0]),
                          cmp_end.astype(jnp.float32))
    v_cmp = _nsa_compress(vc, cmp_pe[1], cmp_v_w1, cmp_v_w2)
    k_slc = _rope_partial(_rms_norm(ks.transpose(0, 2, 1, 3), k_norm[1]), pos)
    v_slc = vs.transpose(0, 2, 1, 3)
    k_win = _rope_partial(_rms_norm(kw.transpose(0, 2, 1, 3), k_norm[2]), pos)
    v_win = vw.transpose(0, 2, 1, 3)
    gates = jax.nn.sigmoid(u_gate.astype(jnp.float32).reshape(B, S, G, R, N_BRANCH)
                           .transpose(0, 2, 3, 1, 4))

    n_slc = S // SLC_BLOCK
    n_top = min(N_SELECT, n_slc)
    k_blocks = k_slc.reshape(B, G, n_slc, SLC_BLOCK, HD)
    v_blocks = v_slc.reshape(B, G, n_slc, SLC_BLOCK, HD)
    c_start = jnp.arange(n_cmp)[:, None] * CMP_STRIDE
    s_start = jnp.arange(n_slc)[None, :] * SLC_BLOCK
    overlap = jnp.clip(jnp.minimum(c_start + CMP_LEN, s_start + SLC_BLOCK) - jnp.maximum(c_start, s_start),
                       0, None).astype(jnp.float32) / CMP_LEN
    k_win_pad = jnp.pad(k_win, ((0, 0), (0, 0), (WINDOW, 0), (0, 0)))
    v_win_pad = jnp.pad(v_win, ((0, 0), (0, 0), (WINDOW, 0), (0, 0)))
    bi = jnp.arange(B)[:, None, None, None]
    gi = jnp.arange(G)[None, :, None, None]
    blk_ids = jnp.arange(n_slc)
    scale = HEAD_DIM ** -0.5

    def one_block(b):
        start = b * Q_BLOCK
        t = start + jnp.arange(Q_BLOCK)
        qb = lax.dynamic_slice_in_dim(q, start, Q_BLOCK, axis=3)
        m_cmp = cmp_end[None, :] <= t[:, None]
        s = jnp.einsum('bgrqd,bgnd->bgrqn', qb, k_cmp) * scale
        p_cmp = _masked_softmax(s, m_cmp) * jnp.any(m_cmp, axis=-1)[:, None].astype(jnp.float32)
        o_cmp = jnp.einsum('bgrqn,bgnd->bgrqd', p_cmp.astype(dt), v_cmp)
        imp = jnp.einsum('bgrqn,nj->bgqj', p_cmp, overlap)
        cur = t // SLC_BLOCK
        forced = (blk_ids[None, :] == 0) | (blk_ids[None, :] == cur[:, None]) | (blk_ids[None, :] == cur[:, None] - 1)
        valid = blk_ids[None, :] * SLC_BLOCK <= t[:, None]
        imp = jnp.where(valid, imp + jnp.where(forced, FORCE_BONUS, 0.0), NEG_INF)
        _, idx = lax.top_k(imp, n_top)
        k_sel = k_blocks[bi, gi, idx].reshape(B, G, Q_BLOCK, n_top * SLC_BLOCK, HD)
        v_sel = v_blocks[bi, gi, idx].reshape(B, G, Q_BLOCK, n_top * SLC_BLOCK, HD)
        tok = (idx[..., None] * SLC_BLOCK + jnp.arange(SLC_BLOCK)).reshape(B, G, Q_BLOCK, n_top * SLC_BLOCK)
        m_slc = (tok <= t[:, None])[:, :, None]
        s = jnp.einsum('bgrqd,bgqkd->bgrqk', qb, k_sel) * scale
        o_slc = jnp.einsum('bgrqk,bgqkd->bgrqd', _masked_softmax(s, m_slc).astype(dt), v_sel)
        k_w = lax.dynamic_slice_in_dim(k_win_pad, start, WINDOW + Q_BLOCK, axis=2)
        v_w = lax.dynamic_slice_in_dim(v_win_pad, start, WINDOW + Q_BLOCK, axis=2)
        kp = start - WINDOW + jnp.arange(WINDOW + Q_BLOCK)
        m_win = (kp[None, :] <= t[:, None]) & (kp[None, :] > t[:, None] - WINDOW) & (kp[None, :] >= 0)
        s = jnp.einsum('bgrqd,bgkd->bgrqk', qb, k_w) * scale
        o_win = jnp.einsum('bgrqk,bgkd->bgrqd', _masked_softmax(s, m_win).astype(dt), v_w)
        gb = lax.dynamic_slice_in_dim(gates, start, Q_BLOCK, axis=3)
        o = gb[..., 0:1] * o_cmp + gb[..., 1:2] * o_slc + gb[..., 2:3] * o_win
        return o.astype(dt)

    o = lax.map(one_block, jnp.arange(S // Q_BLOCK))
    return o.transpose(1, 0, 4, 2, 3, 5).reshape(B, S, NSA_WIDTH)


def _s5_mixer(u, lam_re, lam_im, log_dt, b_re, b_im, c_re, c_im, d_skip, glu_w, glu_b):
    B, S, _ = u.shape
    f32 = jnp.float32
    uf = u.astype(f32).reshape(B, S, S5_GROUPS, S5_H)
    lam = lax.complex(lam_re.astype(f32), lam_im.astype(f32))
    step = jnp.exp(log_dt.astype(f32))[:, None]
    lam_bar = jnp.exp(lam * step)
    b_bar = lax.complex(b_re.astype(f32), b_im.astype(f32)) * ((lam_bar - 1.0) / lam)[..., None]
    bu = jnp.einsum('gph,bsgh->bsgp', b_bar, uf.astype(jnp.complex64))
    a = jnp.broadcast_to(lam_bar, bu.shape)

    def combine(e1, e2):
        return (e2[0] * e1[0], e2[0] * e1[1] + e2[1])

    _, states = lax.associative_scan(combine, (a, bu), axis=1)
    c_mat = lax.complex(c_re.astype(f32), c_im.astype(f32))
    y = jnp.einsum('ghp,bsgp->bsgh', c_mat, states).real + d_skip.astype(f32) * uf
    y = jax.nn.gelu(y.reshape(B, S, S5_WIDTH))
    y = y * jax.nn.sigmoid(y @ glu_w.astype(f32) + glu_b.astype(f32))
    return y.astype(u.dtype)


def _hybrid_layer(x, c, ada_w, ada_b, norm_ffn1, ffn1_w_in, ffn1_w_out, norm_mix, w_in, w_out, out_norm,
                  pool_w, pool_b, pool_scale, q_norm, k_norm, cmp_pe, cmp_k_w1, cmp_k_w2, cmp_v_w1, cmp_v_w2,
                  s5_lam_re, s5_lam_im, s5_log_dt, s5_b_re, s5_b_im, s5_c_re, s5_c_im, s5_d, glu_w, glu_b,
                  norm_ffn2, ffn2_w_in, ffn2_w_out):
    B = x.shape[0]
    mod = (jax.nn.silu(c) @ ada_w + ada_b).reshape(B, N_MOD, D_MODEL)
    sh1, sc1, g1, sh2, sc2, g2, sh3, sc3, g3 = [mod[:, i] for i in range(N_MOD)]
    h = _modulate(_rms_norm(x, norm_ffn1), sh1, sc1)
    x = x + 0.5 * g1[:, None, :] * _swiglu(h, ffn1_w_in, ffn1_w_out)
    h = _modulate(_rms_norm(x, norm_mix), sh2, sc2)
    u_pool, u_q, u_kv, u_gate, u_s5 = jnp.split(h @ w_in, IN_OFFSETS, axis=-1)
    n_pool, n_nsa, n_s5 = jnp.split(out_norm, OUT_OFFSETS)
    y_pool = _rms_norm(_pool_mixer(u_pool, pool_w, pool_b, pool_scale), n_pool)
    y_nsa = _rms_norm(_nsa_mixer(u_q, u_kv, u_gate, q_norm, k_norm, cmp_pe, cmp_k_w1, cmp_k_w2,
                                 cmp_v_w1, cmp_v_w2), n_nsa)
    y_s5 = _rms_norm(_s5_mixer(u_s5, s5_lam_re, s5_lam_im, s5_log_dt, s5_b_re, s5_b_im, s5_c_re, s5_c_im,
                               s5_d, glu_w, glu_b), n_s5)
    y = jnp.concatenate([y_pool, y_nsa, y_s5], axis=-1) @ w_out
    x = x + g2[:, None, :] * y
    h = _modulate(_rms_norm(x, norm_ffn2), sh3, sc3)
    x = x + 0.5 * g3[:, None, :] * _swiglu(h, ffn2_w_in, ffn2_w_out)
    return x


def setup_inputs(seed: int = 0) -> dict:
    key = jax.random.key(seed)
    keys = iter(jax.random.split(key, 64))
    L, D = DEPTH, D_MODEL

    def nrm(shape, std):
        return std * jax.random.normal(next(keys), shape, jnp.float32)

    def gain(shape):
        return 1.0 + nrm(shape, 0.02)

    return {
        'x': nrm((BATCH, SEQ, D), 1.0),
        'c': nrm((BATCH, D), 1.0),
        'ada_w': nrm((L, D, N_MOD * D), 0.5 * D ** -0.5),
        'ada_b': nrm((L, N_MOD * D), 0.01),
        'norm_ffn1': gain((L, D)),
        'ffn1_w_in': nrm((L, D, 2 * D_FF), D ** -0.5),
        'ffn1_w_out': nrm((L, D_FF, D), D_FF ** -0.5),
        'norm_mix': gain((L, D)),
        'w_in': nrm((L, D, N_IN), D ** -0.5),
        'w_out': nrm((L, D_MIX, D), D_MIX ** -0.5),
        'out_norm': gain((L, D_MIX)),
        'pool_w': nrm((L, POOL_GROUPS, POOL_GC, POOL_GC), POOL_GC ** -0.5),
        'pool_b': nrm((L, POOL_GROUPS, POOL_GC), 0.01),
        'pool_scale': gain((L, POOL_WIDTH)),
        'q_norm': gain((L, HEAD_DIM)),
        'k_norm': gain((L, N_BRANCH, HEAD_DIM)),
        'cmp_pe': nrm((L, 2, CMP_LEN, HEAD_DIM), 0.02),
        'cmp_k_w1': nrm((L, CMP_LEN * HEAD_DIM, CMP_HIDDEN), (CMP_LEN * HEAD_DIM) ** -0.5),
        'cmp_k_w2': nrm((L, CMP_HIDDEN, HEAD_DIM), CMP_HIDDEN ** -0.5),
        'cmp_v_w1': nrm((L, CMP_LEN * HEAD_DIM, CMP_HIDDEN), (CMP_LEN * HEAD_DIM) ** -0.5),
        'cmp_v_w2': nrm((L, CMP_HIDDEN, HEAD_DIM), CMP_HIDDEN ** -0.5),
        's5_lam_re': -0.5 + nrm((L, S5_GROUPS, S5_P), 0.01),
        's5_lam_im': math.pi * jnp.arange(S5_P, dtype=jnp.float32) + nrm((L, S5_GROUPS, S5_P), 0.01),
        's5_log_dt': jax.random.uniform(next(keys), (L, S5_GROUPS), jnp.float32,
                                        math.log(1e-3), math.log(1e-1)),
        's5_b_re': nrm((L, S5_GROUPS, S5_P, S5_H), (2 * S5_H) ** -0.5),
        's5_b_im': nrm((L, S5_GROUPS, S5_P, S5_H), (2 * S5_H) ** -0.5),
        's5_c_re': nrm((L, S5_GROUPS, S5_H, S5_P), S5_P ** -0.5),
        's5_c_im': nrm((L, S5_GROUPS, S5_H, S5_P), S5_P ** -0.5),
        's5_d': nrm((L, S5_GROUPS, S5_H), 1.0),
        'glu_w': nrm((L, S5_WIDTH, S5_WIDTH), S5_WIDTH ** -0.5),
        'glu_b': nrm((L, S5_WIDTH), 0.01),
        'norm_ffn2': gain((L, D)),
        'ffn2_w_in': nrm((L, D, 2 * D_FF), D ** -0.5),
        'ffn2_w_out': nrm((L, D_FF, D), D_FF ** -0.5),
    }


def reference(x, c, ada_w, ada_b, norm_ffn1, ffn1_w_in, ffn1_w_out, norm_mix, w_in, w_out, out_norm,
              pool_w, pool_b, pool_scale, q_norm, k_norm, cmp_pe, cmp_k_w1, cmp_k_w2, cmp_v_w1, cmp_v_w2,
              s5_lam_re, s5_lam_im, s5_log_dt, s5_b_re, s5_b_im, s5_c_re, s5_c_im, s5_d, glu_w, glu_b,
              norm_ffn2, ffn2_w_in, ffn2_w_out):
    for l in range(DEPTH):
        x = _hybrid_layer(x, c, ada_w[l], ada_b[l], norm_ffn1[l], ffn1_w_in[l], ffn1_w_out[l], norm_mix[l],
                          w_in[l], w_out[l], out_norm[l], pool_w[l], pool_b[l], pool_scale[l], q_norm[l],
                          k_norm[l], cmp_pe[l], cmp_k_w1[l], cmp_k_w2[l], cmp_v_w1[l], cmp_v_w2[l],
                          s5_lam_re[l], s5_lam_im[l], s5_log_dt[l], s5_b_re[l], s5_b_im[l], s5_c_re[l],
                          s5_c_im[l], s5_d[l], glu_w[l], glu_b[l], norm_ffn2[l], ffn2_w_in[l], ffn2_w_out[l])
    return x
```

```python
import functools
import math

import jax
import jax.numpy as jnp
import numpy as np
from jax import lax
from jax.experimental import pallas as pl
from jax.experimental.pallas import tpu as pltpu

F32 = jnp.float32
BF16 = jnp.bfloat16
HIGHEST = lax.Precision.HIGHEST

LANES = 128
SUBLANES = 8
VMEM_LIMIT_BYTES = 56 * 1024 * 1024

D_MODEL = 1024
D_FF = 2816
N_MOD = 9
POOL_WIDTH = 256
POOL_GC = 64
POOL_WINDOWS = (2, 4, 8, 16)
POOL_HALO = 16
HEAD_DIM = 64
NSA_WIDTH = 512
NSA_HEADS = 8
NSA_KV_HEADS = 2
NSA_REP = 4
N_BRANCH = 3
S5_WIDTH = 256
S5_H = 16
S5_GROUPS = 16
S5_P = 64
CMP_STRIDE = 16
CMP_LEN = 32
CMP_HIDDEN = 128
SLC_BLOCK = 64
SLC_SHIFT = 6
N_SELECT = 16
N_FORCED = 3
WINDOW = 512
Q_BLOCK = 128
ROT_DIM = 16
ROPE_THETA = 500000.0
EPS = 1e-6
NEG_INF = -1e30

OFF_POOL = 0
OFF_Q = OFF_POOL + POOL_WIDTH
OFF_KA = OFF_Q + NSA_HEADS * LANES
OFF_VA = OFF_KA + NSA_KV_HEADS * LANES
OFF_KC = OFF_VA + NSA_KV_HEADS * LANES
OFF_VC = OFF_KC + LANES
OFF_GATE = OFF_VC + LANES
OFF_S5 = OFF_GATE + NSA_KV_HEADS * LANES
N_COLS = OFF_S5 + S5_WIDTH

S5_CHUNK = 8
S5_FOLD = S5_CHUNK * S5_WIDTH
S5_STATE = S5_GROUPS * S5_P
KEY_TILE = 512


def _dot(a, b, precision=None):
    return jnp.dot(a, b, preferred_element_type=F32, precision=precision)


def _dot_nt(a, b, precision=None):
    return lax.dot_general(a, b, (((1,), (1,)), ((), ())), preferred_element_type=F32,
                           precision=precision)


def _sigmoid(x):
    return 1.0 / (1.0 + jnp.exp(-x))


def _gelu_tanh(x):
    return 0.5 * x * (1.0 + jnp.tanh(math.sqrt(2.0 / math.pi) * (x + 0.044715 * (x * x * x))))


def _params(sem):
    return pltpu.CompilerParams(dimension_semantics=sem, vmem_limit_bytes=VMEM_LIMIT_BYTES)


def _const_spec(shape):
    nd = len(shape)
    return pl.BlockSpec(shape, lambda *_: (0,) * nd, pipeline_mode=pl.Buffered(1))


def _mod_kernel(c_ref, w_ref, b_ref, o_ref):
    c = c_ref[...]
    o_ref[...] = _dot(c * _sigmoid(c), w_ref[...], HIGHEST) + b_ref[...]


def _modulation(c, ada_w, ada_b):
    n_layers, d, n = ada_w.shape
    b = c.shape[0]
    tn = 1152
    c_pad = jnp.zeros((SUBLANES, d), F32).at[:b].set(c)
    out = pl.pallas_call(
        _mod_kernel,
        grid=(n_layers, n // tn),
        in_specs=[pl.BlockSpec((SUBLANES, d), lambda l, j: (0, 0)),
                  pl.BlockSpec((None, d, tn), lambda l, j: (l, 0, j)),
                  pl.BlockSpec((None, 1, tn), lambda l, j: (l, 0, j))],
        out_specs=pl.BlockSpec((None, SUBLANES, tn), lambda l, j: (l, 0, j)),
        out_shape=jax.ShapeDtypeStruct((n_layers, SUBLANES, n), F32),
        compiler_params=_params(("arbitrary", "arbitrary")),
        name="adaln_mod",
    )(c_pad, ada_w, ada_b.reshape(n_layers, 1, n))
    return out[:, :b].reshape(n_layers, b, N_MOD, d)


def _norm_modulate(x, gain, mod_ref, first_row):
    ms = jnp.mean(x * x, axis=-1, keepdims=True)
    y = x * lax.rsqrt(ms + EPS) * gain
    return y * (1.0 + mod_ref[first_row + 1:first_row + 2, :]) + mod_ref[first_row:first_row + 1, :]


def _ffn_kernel(x_ref, mod_ref, g_ref, win_ref, wout_ref, o_ref, *, first_row):
    x = x_ref[...]
    h = _norm_modulate(x, g_ref[...], mod_ref, first_row).astype(BF16)
    gu = _dot(h, win_ref[...])
    gate, up = gu[:, :D_FF], gu[:, D_FF:]
    a = (gate * _sigmoid(gate) * up).astype(BF16)
    y = _dot(a, wout_ref[...])
    o_ref[...] = x + 0.5 * mod_ref[first_row + 2:first_row + 3, :] * y


def _ffn(x, mod, gain, w_in, w_out, first_row, tm=256):
    b, s, d = x.shape
    return pl.pallas_call(
        functools.partial(_ffn_kernel, first_row=first_row),
        grid=(b, s // tm),
        in_specs=[pl.BlockSpec((None, tm, d), lambda i, j: (i, j, 0)),
                  pl.BlockSpec((None, N_MOD, d), lambda i, j: (i, 0, 0)),
                  _const_spec((1, d)),
                  _const_spec((d, 2 * D_FF)),
                  _const_spec((D_FF, d))],
        out_specs=pl.BlockSpec((None, tm, d), lambda i, j: (i, j, 0)),
        out_shape=jax.ShapeDtypeStruct((b, s, d), F32),
        compiler_params=_params(("parallel", "parallel")),
        name="ffn_half_step",
    )(x, mod, gain.reshape(1, d), w_in, w_out)


def _rope(v, cos_t, sin_lo, sin_hi):
    return (v * cos_t + pltpu.roll(v, LANES - ROT_DIM // 2, 1) * sin_lo
            + pltpu.roll(v, ROT_DIM // 2, 1) * sin_hi)


def _inproj_kernel(x_ref, mod_ref, g_ref, w_ref, cos_ref, slo_ref, shi_ref, qg_ref, kg_ref,
                   pool_ref, q_ref, ka_ref, kb_ref, va_ref, vb_ref, kc_ref, vc_ref, gate_ref, s5_ref):
    h = _norm_modulate(x_ref[...], g_ref[...], mod_ref, 3).astype(BF16)
    u = _dot(h, w_ref[...])
    tm = u.shape[0]
    cos_t, sin_lo, sin_hi = cos_ref[...], slo_ref[...], shi_ref[...]
    low_half = lax.broadcasted_iota(jnp.int32, (tm, LANES), 1) < HEAD_DIM

    pool_ref[...] = u[:, OFF_POOL:OFF_POOL + POOL_WIDTH]
    for hd in range(NSA_HEADS):
        v = u[:, OFF_Q + hd * LANES:OFF_Q + (hd + 1) * LANES]
        ms = jnp.sum(v * v, axis=-1, keepdims=True) * (1.0 / HEAD_DIM)
        vn = v * lax.rsqrt(ms + EPS) * qg_ref[...]
        q_ref[hd] = _rope(vn, cos_t, sin_lo, sin_hi) * (HEAD_DIM ** -0.5)
    for g in range(NSA_KV_HEADS):
        v = u[:, OFF_KA + g * LANES:OFF_KA + (g + 1) * LANES]
        sq = v * v
        ms_lo = jnp.sum(jnp.where(low_half, sq, 0.0), axis=-1, keepdims=True) * (1.0 / HEAD_DIM)
        ms_hi = jnp.sum(jnp.where(low_half, 0.0, sq), axis=-1, keepdims=True) * (1.0 / HEAD_DIM)
        r = jnp.where(low_half, lax.rsqrt(ms_lo + EPS), lax.rsqrt(ms_hi + EPS))
        kn = _rope(v * r * kg_ref[...], cos_t, sin_lo, sin_hi)
        ka_ref[g] = kn.astype(BF16)
        kb_ref[g] = pltpu.roll(kn, HEAD_DIM, 1).astype(BF16)
        vv = u[:, OFF_VA + g * LANES:OFF_VA + (g + 1) * LANES]
        va_ref[g] = vv.astype(BF16)
        vb_ref[g] = pltpu.roll(vv, HEAD_DIM, 1).astype(BF16)
        gate_ref[g] = _sigmoid(u[:, OFF_GATE + g * LANES:OFF_GATE + (g + 1) * LANES])
    kc_ref[...] = u[:, OFF_KC:OFF_KC + LANES].astype(BF16)
    vc_ref[...] = u[:, OFF_VC:OFF_VC + LANES].astype(BF16)
    s5_ref[...] = u[:, OFF_S5:OFF_S5 + S5_WIDTH]


def _inproj(x, mod, gain, w_mix, rope_tabs, q_gain, k_gain, tm=256):
    b, s, d = x.shape
    g = NSA_KV_HEADS
    tok = lambda width: pl.BlockSpec((None, tm, width), lambda i, j: (i, j, 0))
    grp = lambda n: pl.BlockSpec((None, n, tm, LANES), lambda i, j: (i, 0, j, 0))
    tab = pl.BlockSpec((tm, LANES), lambda i, j: (j, 0))
    sds = jax.ShapeDtypeStruct
    return pl.pallas_call(
        _inproj_kernel,
        grid=(b, s // tm),
        in_specs=[tok(d),
                  pl.BlockSpec((None, N_MOD, d), lambda i, j: (i, 0, 0)),
                  _const_spec((1, d)),
                  _const_spec((d, N_COLS)),
                  tab, tab, tab,
                  _const_spec((1, LANES)), _const_spec((1, LANES))],
        out_specs=[tok(POOL_WIDTH), grp(NSA_HEADS), grp(g), grp(g), grp(g), grp(g),
                   tok(LANES), tok(LANES), grp(g), tok(S5_WIDTH)],
        out_shape=[sds((b, s, POOL_WIDTH), F32), sds((b, NSA_HEADS, s, LANES), F32),
                   sds((b, g, s, LANES), BF16), sds((b, g, s, LANES), BF16),
                   sds((b, g, s, LANES), BF16), sds((b, g, s, LANES), BF16),
                   sds((b, s, LANES), BF16), sds((b, s, LANES), BF16),
                   sds((b, g, s, LANES), F32), sds((b, s, S5_WIDTH), F32)],
        compiler_params=_params(("parallel", "parallel")),
        name="mixer_in_proj",
    )(x, mod, gain.reshape(1, d), w_mix, *rope_tabs, q_gain, k_gain)


def _compress_kernel(kc_ref, vc_ref, w1k_ref, w1v_ref, w1k_raw_ref, w1v_raw_ref, pe_ref, w2k_ref, w2v_ref,
                     kg_ref, cos_ref, slo_ref, shi_ref, ko_ref, vo_ref, *, n_cmp):
    ncp = kc_ref.shape[0]
    real_row = lax.broadcasted_iota(jnp.int32, (ncp, LANES), 0) < n_cmp
    for src_ref, w1_ref, raw_ref, w2_ref, out_ref, pe_row, is_key in (
            (kc_ref, w1k_ref, w1k_raw_ref, w2k_ref, ko_ref, 0, True),
            (vc_ref, w1v_ref, w1v_raw_ref, w2v_ref, vo_ref, 1, False)):
        chunks = src_ref[...]
        pe = jnp.broadcast_to(pe_ref[pe_row:pe_row + 1, :], (SUBLANES, CMP_LEN * HEAD_DIM))
        pe_term = _dot(pe, raw_ref[...], HIGHEST)[0:1, :]
        for g in range(NSA_KV_HEADS):
            a = _dot(chunks, w1_ref[g])
            pre = a[:, :CMP_HIDDEN] + pltpu.roll(a[:, CMP_HIDDEN:], ncp - 1, 0) + pe_term
            out = _dot(_gelu_tanh(pre).astype(BF16), w2_ref[...])
            if is_key:
                ms = jnp.sum(out * out, axis=-1, keepdims=True) * (1.0 / HEAD_DIM)
                out = _rope(out * lax.rsqrt(ms + EPS) * kg_ref[...], cos_ref[...], slo_ref[...], shi_ref[...])
            out_ref[g] = jnp.where(real_row, out, 0.0).astype(BF16)


def _compress(kc, vc, w1k, w1v, w1k_raw, w1v_raw, pe, w2k, w2v, k_gain, cmp_tabs):
    b, s, _ = kc.shape
    ncp = s // CMP_STRIDE
    fold = CMP_STRIDE * LANES
    kc_r = kc.reshape(b, ncp, fold)
    vc_r = vc.reshape(b, ncp, fold)
    src = pl.BlockSpec((None, ncp, fold), lambda i: (i, 0, 0))
    out = pl.BlockSpec((None, NSA_KV_HEADS, ncp, LANES), lambda i: (i, 0, 0, 0))
    raw = _const_spec((CMP_LEN * HEAD_DIM, CMP_HIDDEN))
    w1 = _const_spec((NSA_KV_HEADS, fold, 2 * CMP_HIDDEN))
    w2 = _const_spec((CMP_HIDDEN, LANES))
    tab = _const_spec((ncp, LANES))
    return pl.pallas_call(
        functools.partial(_compress_kernel, n_cmp=ncp - 1),
        grid=(b,),
        in_specs=[src, src, w1, w1, raw, raw, _const_spec((2, CMP_LEN * HEAD_DIM)), w2, w2,
                  _const_spec((1, LANES)), tab, tab, tab],
        out_specs=[out, out],
        out_shape=[jax.ShapeDtypeStruct((b, NSA_KV_HEADS, ncp, LANES), BF16)] * 2,
        compiler_params=_params(("parallel",)),
        name="nsa_compress",
    )(kc_r, vc_r, w1k, w1v, w1k_raw, w1v_raw, pe, w2k, w2v, k_gain, *cmp_tabs)


def _select_blocks(imp_t, qb):
    nsb, nq = imp_t.shape
    j = lax.broadcasted_iota(jnp.int32, (nsb, nq), 0)
    t = qb * Q_BLOCK + lax.broadcasted_iota(jnp.int32, (nsb, nq), 1)
    cur = t >> SLC_SHIFT
    valid = j * SLC_BLOCK <= t
    forced = (j == 0) | (j == cur) | (j == cur - 1)
    sel = jnp.where(forced & valid, 1.0, 0.0)
    vals = jnp.where(valid & jnp.logical_not(forced), imp_t, -1.0)
    for _ in range(N_SELECT - N_FORCED):
        m = jnp.max(vals, axis=0, keepdims=True)
        is_max = (vals == m) & (m >= 0.0)
        idx = jnp.min(jnp.where(is_max, j, nsb), axis=0, keepdims=True)
        pick = j == idx
        sel = jnp.where(pick, 1.0, sel)
        vals = jnp.where(pick, -1.0, vals)
    return sel


def _softmax_rows(s, mask):
    s = jnp.where(mask, s, NEG_INF)
    e = jnp.exp(s - jnp.max(s, axis=-1, keepdims=True))
    return e / jnp.sum(e, axis=-1, keepdims=True)


def _attn_kernel(q_ref, kc_ref, vc_ref, ka_ref, kb_ref, va_ref, vb_ref, gate_ref, ovl_ref, o_ref, *, seq):
    qb = pl.program_id(2)
    rows = NSA_REP * Q_BLOCK
    ncp = seq // CMP_STRIDE
    nsb = seq // SLC_BLOCK
    q = q_ref[...].reshape(rows, LANES).astype(BF16)
    t_row = qb * Q_BLOCK + lax.broadcasted_iota(jnp.int32, (rows, 1), 0) % Q_BLOCK
    t_q = qb * Q_BLOCK + lax.broadcasted_iota(jnp.int32, (Q_BLOCK, 1), 0)

    cmp_end = lax.broadcasted_iota(jnp.int32, (1, ncp), 1) * CMP_STRIDE + (CMP_LEN - 1)
    p_cmp = _softmax_rows(_dot_nt(q, kc_ref[...]), cmp_end <= t_row)
    p_cmp = jnp.where(t_row >= CMP_LEN - 1, p_cmp, 0.0)
    o_cmp = _dot(p_cmp.astype(BF16), vc_ref[...])

    p_grp = (p_cmp[0:Q_BLOCK] + p_cmp[Q_BLOCK:2 * Q_BLOCK]
             + p_cmp[2 * Q_BLOCK:3 * Q_BLOCK] + p_cmp[3 * Q_BLOCK:4 * Q_BLOCK])
    imp_t = _dot_nt(ovl_ref[...], p_grp, HIGHEST)
    sel = _select_blocks(imp_t, qb).T.astype(BF16)

    def slc_tile(kt, carry):
        m_i, l_i, acc = carry
        k0 = pl.multiple_of(kt * KEY_TILE, KEY_TILE)
        s = _dot_nt(q, ka_ref[pl.ds(k0, KEY_TILE), :]).reshape(NSA_REP, Q_BLOCK, KEY_TILE)
        kpos = k0 + lax.broadcasted_iota(jnp.int32, (1, KEY_TILE), 1)
        blk = lax.broadcasted_iota(jnp.int32, (nsb, KEY_TILE), 0)
        expand = jnp.where(blk == ((k0 + lax.broadcasted_iota(jnp.int32, (nsb, KEY_TILE), 1)) >> SLC_SHIFT),
                           1.0, 0.0).astype(BF16)
        ok = (_dot(sel, expand) > 0.5) & (kpos <= t_q)
        s = jnp.where(ok[None], s, NEG_INF)
        m_new = jnp.maximum(m_i, jnp.max(s, axis=-1, keepdims=True))
        p = jnp.where(ok[None], jnp.exp(s - m_new), 0.0)
        alpha = jnp.exp(m_i - m_new)
        l_new = alpha * l_i + jnp.sum(p, axis=-1, keepdims=True)
        pv = _dot(p.reshape(rows, KEY_TILE).astype(BF16), va_ref[pl.ds(k0, KEY_TILE), :])
        return m_new, l_new, alpha * acc + pv.reshape(NSA_REP, Q_BLOCK, LANES)

    n_tiles = ((qb + 1) * Q_BLOCK + KEY_TILE - 1) // KEY_TILE
    init = (jnp.full((NSA_REP, Q_BLOCK, 1), NEG_INF, F32), jnp.zeros((NSA_REP, Q_BLOCK, 1), F32),
            jnp.zeros((NSA_REP, Q_BLOCK, LANES), F32))
    _, l_slc, acc_slc = lax.fori_loop(0, n_tiles, slc_tile, init)
    o_slc = (acc_slc / l_slc).reshape(rows, LANES)

    span = WINDOW + Q_BLOCK
    start = pl.multiple_of(jnp.maximum(qb * Q_BLOCK - WINDOW, 0), Q_BLOCK)
    kpos = start + lax.broadcasted_iota(jnp.int32, (1, span), 1)
    p_win = _softmax_rows(_dot_nt(q, kb_ref[pl.ds(start, span), :]),
                          (kpos <= t_row) & (kpos > t_row - WINDOW))
    o_win = _dot(p_win.astype(BF16), vb_ref[pl.ds(start, span), :])

    gates = gate_ref[...]
    low_half = lax.broadcasted_iota(jnp.int32, (Q_BLOCK, LANES), 1) < HEAD_DIM
    heads = []
    for r in range(NSA_REP):
        rs = slice(r * Q_BLOCK, (r + 1) * Q_BLOCK)
        c = r * N_BRANCH
        heads.append(gates[:, c:c + 1] * o_cmp[rs] + gates[:, c + 1:c + 2] * o_slc[rs]
                     + gates[:, c + 2:c + 3] * o_win[rs])
    for pair in range(NSA_REP // 2):
        o_ref[:, pair * LANES:(pair + 1) * LANES] = jnp.where(
            low_half, heads[2 * pair], pltpu.roll(heads[2 * pair + 1], HEAD_DIM, 1))


def _overlap_matrix(seq):
    ncp, nsb = seq // CMP_STRIDE, seq // SLC_BLOCK
    c_start = np.arange(ncp)[None, :] * CMP_STRIDE
    s_start = np.arange(nsb)[:, None] * SLC_BLOCK
    ovl = np.clip(np.minimum(c_start + CMP_LEN, s_start + SLC_BLOCK) - np.maximum(c_start, s_start), 0, None)
    ovl = ovl.astype(np.float32) / CMP_LEN
    ovl[:, ncp - 1] = 0.0
    return jnp.asarray(ovl)


def _nsa_attention(q, k_cmp, v_cmp, ka, kb, va, vb, gates):
    b, _, s, _ = q.shape
    g = NSA_KV_HEADS
    ncp, nsb = s // CMP_STRIDE, s // SLC_BLOCK
    full = lambda n: pl.BlockSpec((None, None, n, LANES), lambda i, j, k: (i, j, 0, 0))
    return pl.pallas_call(
        functools.partial(_attn_kernel, seq=s),
        grid=(b, g, s // Q_BLOCK),
        in_specs=[pl.BlockSpec((None, NSA_REP, Q_BLOCK, LANES), lambda i, j, k: (i, j, k, 0)),
                  full(ncp), full(ncp), full(s), full(s), full(s), full(s),
                  pl.BlockSpec((None, None, Q_BLOCK, LANES), lambda i, j, k: (i, j, k, 0)),
                  _const_spec((nsb, ncp))],
        out_specs=pl.BlockSpec((None, Q_BLOCK, NSA_REP * HEAD_DIM), lambda i, j, k: (i, k, j)),
        out_shape=jax.ShapeDtypeStruct((b, s, NSA_WIDTH), F32),
        compiler_params=_params(("parallel", "parallel", "arbitrary")),
        name="nsa_attention",
    )(q, k_cmp, v_cmp, ka, kb, va, vb, gates, _overlap_matrix(s))


def _s5_kernel(u_ref, mt_ref, bc_ref, cc_ref, are_ref, aim_ref, d_ref, y_ref, h_ref, g_scr, hp_scr):
    @pl.when(pl.program_id(1) == 0)
    def _():
        h_ref[...] = jnp.zeros_like(h_ref)

    u = u_ref[...]
    ub = u.astype(BF16)
    y_local = _dot(ub, mt_ref[...])
    g_scr[...] = _dot(ub, bc_ref[...])
    a_re, a_im = are_ref[...], aim_ref[...]

    def step(i, carry):
        h_re, h_im = carry
        hp_scr[pl.ds(i, 1), 0:S5_STATE] = h_re
        hp_scr[pl.ds(i, 1), S5_STATE:2 * S5_STATE] = h_im
        g_re = g_scr[pl.ds(i, 1), 0:S5_STATE]
        g_im = g_scr[pl.ds(i, 1), S5_STATE:2 * S5_STATE]
        return a_re * h_re - a_im * h_im + g_re, a_re * h_im + a_im * h_re + g_im

    h_re, h_im = lax.fori_loop(0, u.shape[0], step, (h_ref[0:1, :], h_ref[1:2, :]))
    h_ref[0:1, :] = h_re
    h_ref[1:2, :] = h_im
    y_ref[...] = y_local + _dot(hp_scr[...].astype(BF16), cc_ref[...]) + u * d_ref[...]


def _s5_scan(u, mats):
    b, s, _ = u.shape
    rows = s // S5_CHUNK
    tc = min(128, rows)
    mt, bc, cc, a_re, a_im, d_vec = mats
    tile = pl.BlockSpec((None, tc, S5_FOLD), lambda i, j: (i, j, 0))
    mat = _const_spec((S5_FOLD, S5_FOLD))
    y = pl.pallas_call(
        _s5_kernel,
        grid=(b, rows // tc),
        in_specs=[tile, mat, mat, mat, _const_spec((1, S5_STATE)), _const_spec((1, S5_STATE)),
                  _const_spec((1, S5_FOLD))],
        out_specs=tile,
        out_shape=jax.ShapeDtypeStruct((b, rows, S5_FOLD), F32),
        scratch_shapes=[pltpu.VMEM((2, S5_STATE), F32), pltpu.VMEM((tc, S5_FOLD), F32),
                        pltpu.VMEM((tc, S5_FOLD), F32)],
        compiler_params=_params(("parallel", "arbitrary")),
        name="s5_scan",
    )(u.reshape(b, rows, S5_FOLD), mt, bc, cc, a_re, a_im, d_vec)
    return y.reshape(b, s, S5_WIDTH)


def _s5_matrices(lam_re, lam_im, log_dt, b_re, b_im, c_re, c_im, d_skip):
    t0, ng, nh, npm = S5_CHUNK, S5_GROUPS, S5_H, S5_P
    ein = functools.partial(jnp.einsum, precision=HIGHEST)
    lam = lax.complex(lam_re, lam_im)
    step = jnp.exp(log_dt)[:, None]
    lam_bar = jnp.exp(lam * step)
    b_bar = lax.complex(b_re, b_im) * ((lam_bar - 1.0) / lam)[..., None]
    c_mat = lax.complex(c_re, c_im)
    k = jnp.arange(t0 + 1, dtype=F32)[:, None, None]
    pw = jnp.exp((lam * step)[None] * k)
    eye = jnp.eye(ng, dtype=F32)
    kern = jnp.real(ein('ghp,kgp,gpq->kghq', c_mat, pw[:t0], b_bar))
    lag = jnp.arange(t0)[None, :] - jnp.arange(t0)[:, None]
    k_ji = jnp.where((lag >= 0)[:, :, None, None, None], kern[jnp.clip(lag, 0, t0 - 1)], 0.0)
    mt = ein('jighq,gf->jgqifh', k_ji, eye).reshape(t0 * ng * nh, t0 * ng * nh)
    b_j = pw[t0 - 1 - jnp.arange(t0)][..., None] * b_bar[None]
    bc_re = ein('jgpq,gf->jgqfp', jnp.real(b_j), eye).reshape(t0 * ng * nh, ng * npm)
    bc_im = ein('jgpq,gf->jgqfp', jnp.imag(b_j), eye).reshape(t0 * ng * nh, ng * npm)
    bc = jnp.concatenate([bc_re, bc_im], axis=1)
    c_i = c_mat[None] * pw[1:t0 + 1][:, :, None, :]
    cc_re = ein('ighp,gf->gpifh', jnp.real(c_i), eye).reshape(ng * npm, t0 * ng * nh)
    cc_im = ein('ighp,gf->gpifh', -jnp.imag(c_i), eye).reshape(ng * npm, t0 * ng * nh)
    cc = jnp.concatenate([cc_re, cc_im], axis=0)
    a_chunk = pw[t0].reshape(1, ng * npm)
    d_vec = jnp.tile(d_skip.reshape(1, ng * nh), (1, t0))
    return (mt.astype(BF16), bc.astype(BF16), cc.astype(BF16),
            jnp.real(a_chunk), jnp.imag(a_chunk), d_vec)


def _rms_gain(y, gain):
    return y * lax.rsqrt(jnp.mean(y * y, axis=-1, keepdims=True) + EPS) * gain


def _outproj_kernel(x_ref, mod_ref, pool_ref, halo_ref, nsa_ref, s5_ref, pw_ref, pb_ref, ps_ref, on_ref,
                    gw_ref, gb_ref, wo_ref, o_ref, buf):
    j = pl.program_id(1)
    tm = x_ref.shape[0]
    v = pool_ref[...]
    buf[0:POOL_HALO, :] = jnp.where(j > 0, halo_ref[...], 0.0)
    buf[POOL_HALO:POOL_HALO + tm, :] = v
    lane_group = lax.broadcasted_iota(jnp.int32, (tm, POOL_WIDTH), 1) >> SLC_SHIFT
    t1 = (j * tm + 1 + lax.broadcasted_iota(jnp.int32, (tm, 1), 0)).astype(F32)
    run, k, pooled = v, 1, jnp.zeros_like(v)
    for gi, w in enumerate(POOL_WINDOWS):
        while k < w:
            run = run + buf[POOL_HALO - k:POOL_HALO - k + tm, :]
            k += 1
        pooled = jnp.where(lane_group == gi, run / jnp.minimum(t1, float(w)) - v, pooled)
    y_pool = (_dot(pooled.astype(BF16), pw_ref[...]) + pb_ref[...]) * ps_ref[...]

    y = _gelu_tanh(s5_ref[...])
    y_s5 = y * _sigmoid(_dot(y.astype(BF16), gw_ref[...]) + gb_ref[...])

    cat = jnp.concatenate(
        [_rms_gain(y_pool, on_ref[:, 0:POOL_WIDTH]),
         _rms_gain(nsa_ref[...], on_ref[:, POOL_WIDTH:POOL_WIDTH + NSA_WIDTH]),
         _rms_gain(y_s5, on_ref[:, POOL_WIDTH + NSA_WIDTH:])], axis=-1).astype(BF16)
    o_ref[...] = x_ref[...] + mod_ref[5:6, :] * _dot(cat, wo_ref[...])


def _outproj(x, mod, u_pool, o_nsa, y_s5, pool_w_bd, pool_b, pool_scale, out_norm, glu_w, glu_b, w_out, tm=256):
    b, s, d = x.shape
    tok = lambda width: pl.BlockSpec((None, tm, width), lambda i, j: (i, j, 0))
    halo_blocks = tm // POOL_HALO
    return pl.pallas_call(
        _outproj_kernel,
        grid=(b, s // tm),
        in_specs=[tok(d),
                  pl.BlockSpec((None, N_MOD, d), lambda i, j: (i, 0, 0)),
                  tok(POOL_WIDTH),
                  pl.BlockSpec((None, POOL_HALO, POOL_WIDTH),
                               lambda i, j: (i, jnp.maximum(j * halo_blocks - 1, 0), 0)),
                  tok(NSA_WIDTH), tok(S5_WIDTH),
                  _const_spec((POOL_WIDTH, POOL_WIDTH)), _const_spec((1, POOL_WIDTH)),
                  _const_spec((1, POOL_WIDTH)), _const_spec((1, d)),
                  _const_spec((S5_WIDTH, S5_WIDTH)), _const_spec((1, S5_WIDTH)),
                  _const_spec((d, d))],
        out_specs=tok(d),
        out_shape=jax.ShapeDtypeStruct((b, s, d), F32),
        scratch_shapes=[pltpu.VMEM((POOL_HALO + tm, POOL_WIDTH), F32)],
        compiler_params=_params(("parallel", "arbitrary")),
        name="mixer_out_proj",
    )(x, mod, u_pool, u_pool, o_nsa, y_s5, pool_w_bd, pool_b, pool_scale, out_norm, glu_w, glu_b, w_out)


def _rope_tables(pos):
    half = ROT_DIM // 2
    inv_freq = jnp.exp(-math.log(ROPE_THETA) * jnp.arange(half, dtype=F32) * (2.0 / ROT_DIM))
    ang = pos[:, None] * inv_freq[None, :]
    cos, sin = jnp.cos(ang), jnp.sin(ang)
    n = pos.shape[0]
    rest = HEAD_DIM - ROT_DIM
    cos_t = jnp.concatenate([cos, cos, jnp.ones((n, rest), F32)], axis=1)
    sin_lo = jnp.concatenate([-sin, jnp.zeros((n, half + rest), F32)], axis=1)
    sin_hi = jnp.concatenate([jnp.zeros((n, half), F32), sin, jnp.zeros((n, rest), F32)], axis=1)
    return tuple(jnp.tile(t, (1, 2)) for t in (cos_t, sin_lo, sin_hi))


def _pad_lanes(v):
    return jnp.pad(v, [(0, 0)] * (v.ndim - 1) + [(0, LANES - v.shape[-1])])


def _arrange_w_in(w):
    d = w.shape[0]
    o1, o2, o3 = POOL_WIDTH, POOL_WIDTH + NSA_WIDTH, POOL_WIDTH + NSA_WIDTH + 6 * LANES
    q = _pad_lanes(w[:, o1:o2].reshape(d, NSA_HEADS, HEAD_DIM)).reshape(d, NSA_HEADS * LANES)
    kv = w[:, o2:o3].reshape(d, 6, NSA_KV_HEADS, HEAD_DIM)
    ka = jnp.concatenate([kv[:, 2], kv[:, 4]], axis=-1).reshape(d, NSA_KV_HEADS * LANES)
    va = jnp.concatenate([kv[:, 3], kv[:, 5]], axis=-1).reshape(d, NSA_KV_HEADS * LANES)
    kc = kv[:, 0].reshape(d, LANES)
    vc = kv[:, 1].reshape(d, LANES)
    n_gate = NSA_REP * N_BRANCH
    gate = _pad_lanes(w[:, o3:o3 + NSA_KV_HEADS * n_gate].reshape(d, NSA_KV_HEADS, n_gate))
    s5 = w[:, o3 + NSA_KV_HEADS * n_gate:]
    return jnp.concatenate([w[:, :o1], q, ka, va, kc, vc, gate.reshape(d, NSA_KV_HEADS * LANES), s5],
                           axis=1).astype(BF16)


def _expand_cmp_w1(w1):
    halves = w1.reshape(2, CMP_STRIDE, HEAD_DIM, CMP_HIDDEN)
    both = jnp.concatenate([halves[0], halves[1]], axis=-1)
    out = []
    for g in range(NSA_KV_HEADS):
        z = jnp.zeros((CMP_STRIDE, NSA_KV_HEADS, HEAD_DIM, 2 * CMP_HIDDEN), F32).at[:, g].set(both)
        out.append(z.reshape(CMP_STRIDE * LANES, 2 * CMP_HIDDEN))
    return jnp.stack(out).astype(BF16)


def _block_diag(w):
    g, c, _ = w.shape
    return jnp.einsum('gcd,gf->gcfd', w, jnp.eye(g, dtype=w.dtype)).reshape(g * c, g * c)


def _hybrid_layer(x, mod, tabs, cmp_tabs, norm_ffn1, ffn1_w_in, ffn1_w_out, norm_mix, w_in, w_out, out_norm,
                  pool_w, pool_b, pool_scale, q_norm, k_norm, cmp_pe, cmp_k_w1, cmp_k_w2, cmp_v_w1, cmp_v_w2,
                  s5_lam_re, s5_lam_im, s5_log_dt, s5_b_re, s5_b_im, s5_c_re, s5_c_im, s5_d, glu_w, glu_b,
                  norm_ffn2, ffn2_w_in, ffn2_w_out):
    x = _ffn(x, mod, norm_ffn1, ffn1_w_in.astype(BF16), ffn1_w_out.astype(BF16), 0)

    q_gain = _pad_lanes(q_norm.reshape(1, HEAD_DIM))
    slc_win_gain = jnp.concatenate([k_norm[1], k_norm[2]]).reshape(1, LANES)
    u_pool, q, ka, kb, va, vb, kc, vc, gates, u_s5 = _inproj(
        x, mod, norm_mix, _arrange_w_in(w_in), tabs, q_gain, slc_win_gain)

    k_cmp, v_cmp = _compress(
        kc, vc, _expand_cmp_w1(cmp_k_w1), _expand_cmp_w1(cmp_v_w1), cmp_k_w1, cmp_v_w1,
        cmp_pe.reshape(2, CMP_LEN * HEAD_DIM), _pad_lanes(cmp_k_w2).astype(BF16),
        _pad_lanes(cmp_v_w2).astype(BF16), _pad_lanes(k_norm[0].reshape(1, HEAD_DIM)), cmp_tabs)
    o_nsa = _nsa_attention(q, k_cmp, v_cmp, ka, kb, va, vb, gates)

    y_s5 = _s5_scan(u_s5, _s5_matrices(s5_lam_re, s5_lam_im, s5_log_dt, s5_b_re, s5_b_im,
                                       s5_c_re, s5_c_im, s5_d))

    x = _outproj(x, mod, u_pool, o_nsa, y_s5, _block_diag(pool_w).astype(BF16),
                 pool_b.reshape(1, POOL_WIDTH), pool_scale.reshape(1, POOL_WIDTH),
                 out_norm.reshape(1, D_MODEL), glu_w.astype(BF16), glu_b.reshape(1, S5_WIDTH),
                 w_out.astype(BF16))
    return _ffn(x, mod, norm_ffn2, ffn2_w_in.astype(BF16), ffn2_w_out.astype(BF16), 6)


def kernel(x, c, ada_w, ada_b, norm_ffn1, ffn1_w_in, ffn1_w_out, norm_mix, w_in, w_out, out_norm, pool_w, pool_b, pool_scale, q_norm, k_norm, cmp_pe, cmp_k_w1, cmp_k_w2, cmp_v_w1, cmp_v_w2, s5_lam_re, s5_lam_im, s5_log_dt, s5_b_re, s5_b_im, s5_c_re, s5_c_im, s5_d, glu_w, glu_b, norm_ffn2, ffn2_w_in, ffn2_w_out):
    seq = x.shape[1]
    assert seq % (KEY_TILE * 4) == 0 and seq >= WINDOW + Q_BLOCK
    mod = _modulation(c, ada_w, ada_b)
    tabs = _rope_tables(jnp.arange(seq, dtype=F32))
    n_cmp_pad = seq // CMP_STRIDE
    cmp_tabs = _rope_tables((jnp.arange(n_cmp_pad) * CMP_STRIDE + CMP_LEN - 1).astype(F32))
    per_layer = (norm_ffn1, ffn1_w_in, ffn1_w_out, norm_mix, w_in, w_out, out_norm, pool_w, pool_b, pool_scale,
                 q_norm, k_norm, cmp_pe, cmp_k_w1, cmp_k_w2, cmp_v_w1, cmp_v_w2, s5_lam_re, s5_lam_im,
                 s5_log_dt, s5_b_re, s5_b_im, s5_c_re, s5_c_im, s5_d, glu_w, glu_b, norm_ffn2, ffn2_w_in,
                 ffn2_w_out)
    for l in range(ada_w.shape[0]):
        x = _hybrid_layer(x, mod[l], tabs, cmp_tabs, *[p[l] for p in per_layer])
    return x
```

```python
import functools
import math

import jax
import jax.numpy as jnp
import numpy as np
from jax import lax
from jax.experimental import pallas as pl
from jax.experimental.pallas import tpu as pltpu

F32 = jnp.float32
BF16 = jnp.bfloat16
HIGHEST = lax.Precision.HIGHEST

LANES = 128
SUBLANES = 8
VMEM_LIMIT_BYTES = 56 * 1024 * 1024

D_MODEL = 1024
D_FF = 2816
N_MOD = 9
POOL_WIDTH = 256
POOL_GC = 64
POOL_WINDOWS = (2, 4, 8, 16)
POOL_HALO = 16
HEAD_DIM = 64
NSA_WIDTH = 512
NSA_HEADS = 8
NSA_KV_HEADS = 2
NSA_REP = 4
N_BRANCH = 3
S5_WIDTH = 256
S5_H = 16
S5_GROUPS = 16
S5_P = 64
CMP_STRIDE = 16
CMP_LEN = 32
CMP_HIDDEN = 128
SLC_BLOCK = 64
SLC_SHIFT = 6
N_SELECT = 16
N_FORCED = 3
WINDOW = 512
Q_BLOCK = 128
ROT_DIM = 16
ROPE_THETA = 500000.0
EPS = 1e-6
Q_SCALE = HEAD_DIM ** -0.5 * math.log2(math.e)
NEG_INF = -1e30
MASK_VALUE = -(2.0 ** 100)

OFF_POOL = 0
OFF_Q = OFF_POOL + POOL_WIDTH
OFF_KA = OFF_Q + NSA_HEADS * LANES
OFF_VA = OFF_KA + NSA_KV_HEADS * LANES
OFF_KC = OFF_VA + NSA_KV_HEADS * LANES
OFF_VC = OFF_KC + LANES
OFF_GATE = OFF_VC + LANES
OFF_S5 = OFF_GATE + NSA_KV_HEADS * LANES
N_COLS = OFF_S5 + S5_WIDTH

S5_CHUNK = 8
S5_FOLD = S5_CHUNK * S5_WIDTH
S5_STATE = S5_GROUPS * S5_P
KEY_TILE = 512


def _dot(a, b, precision=None):
    return jnp.dot(a, b, preferred_element_type=F32, precision=precision)


def _dot_nt(a, b, precision=None):
    return lax.dot_general(a, b, (((1,), (1,)), ((), ())), preferred_element_type=F32,
                           precision=precision)


def _sigmoid(x):
    return 1.0 / (1.0 + jnp.exp(-x))


def _gelu_tanh(x):
    return 0.5 * x * (1.0 + jnp.tanh(math.sqrt(2.0 / math.pi) * (x + 0.044715 * (x * x * x))))


def _params(sem):
    return pltpu.CompilerParams(dimension_semantics=sem, vmem_limit_bytes=VMEM_LIMIT_BYTES)


def _const_spec(shape):
    nd = len(shape)
    return pl.BlockSpec(shape, lambda *_: (0,) * nd, pipeline_mode=pl.Buffered(1))


def _mod_kernel(c_ref, w_ref, b_ref, o_ref):
    c = c_ref[...]
    o_ref[...] = _dot(c * _sigmoid(c), w_ref[...], HIGHEST) + b_ref[...]


def _modulation(c, ada_w, ada_b):
    n_layers, d, n = ada_w.shape
    b = c.shape[0]
    tn = 1152
    c_pad = jnp.zeros((SUBLANES, d), F32).at[:b].set(c)
    out = pl.pallas_call(
        _mod_kernel,
        grid=(n_layers, n // tn),
        in_specs=[pl.BlockSpec((SUBLANES, d), lambda l, j: (0, 0)),
                  pl.BlockSpec((None, d, tn), lambda l, j: (l, 0, j)),
                  pl.BlockSpec((None, 1, tn), lambda l, j: (l, 0, j))],
        out_specs=pl.BlockSpec((None, SUBLANES, tn), lambda l, j: (l, 0, j)),
        out_shape=jax.ShapeDtypeStruct((n_layers, SUBLANES, n), F32),
        compiler_params=_params(("arbitrary", "arbitrary")),
        name="adaln_mod",
    )(c_pad, ada_w, ada_b.reshape(n_layers, 1, n))
    return out[:, :b].reshape(n_layers, b, N_MOD, d)


def _norm_modulate(x, gain, mod_ref, first_row):
    ms = jnp.mean(x * x, axis=-1, keepdims=True)
    y = x * lax.rsqrt(ms + EPS) * gain
    return y * (1.0 + mod_ref[first_row + 1:first_row + 2, :]) + mod_ref[first_row:first_row + 1, :]


def _ffn_kernel(x_ref, mod_ref, g_ref, win_ref, wout_ref, o_ref, *, first_row):
    x = x_ref[...]
    h = _norm_modulate(x, g_ref[...], mod_ref, first_row).astype(BF16)
    gu = _dot(h, win_ref[...])
    gate, up = gu[:, :D_FF], gu[:, D_FF:]
    a = (gate * _sigmoid(gate) * up).astype(BF16)
    y = _dot(a, wout_ref[...])
    o_ref[...] = x + 0.5 * mod_ref[first_row + 2:first_row + 3, :] * y


def _ffn(x, mod, gain, w_in, w_out, first_row, tm=256):
    b, s, d = x.shape
    return pl.pallas_call(
        functools.partial(_ffn_kernel, first_row=first_row),
        grid=(b, s // tm),
        in_specs=[pl.BlockSpec((None, tm, d), lambda i, j: (i, j, 0)),
                  pl.BlockSpec((None, N_MOD, d), lambda i, j: (i, 0, 0)),
                  _const_spec((1, d)),
                  _const_spec((d, 2 * D_FF)),
                  _const_spec((D_FF, d))],
        out_specs=pl.BlockSpec((None, tm, d), lambda i, j: (i, j, 0)),
        out_shape=jax.ShapeDtypeStruct((b, s, d), F32),
        compiler_params=_params(("parallel", "parallel")),
        name="ffn_half_step",
    )(x, mod, gain.reshape(1, d), w_in, w_out)


def _rope(v, cos_t, sin_lo, sin_hi):
    return (v * cos_t + pltpu.roll(v, LANES - ROT_DIM // 2, 1) * sin_lo
            + pltpu.roll(v, ROT_DIM // 2, 1) * sin_hi)


def _inproj_kernel(x_ref, mod_ref, g_ref, w_ref, cos_ref, slo_ref, shi_ref, qg_ref, kg_ref,
                   pool_ref, q_ref, ka_ref, kb_ref, va_ref, vb_ref, kc_ref, vc_ref, gate_ref, s5_ref):
    h = _norm_modulate(x_ref[...], g_ref[...], mod_ref, 3).astype(BF16)
    u = _dot(h, w_ref[...])
    tm = u.shape[0]
    cos_t, sin_lo, sin_hi = cos_ref[...], slo_ref[...], shi_ref[...]
    lane = lax.broadcasted_iota(jnp.int32, (tm, LANES), 1)
    low_half = lane < HEAD_DIM
    pos = pl.program_id(1) * tm + lax.broadcasted_iota(jnp.int32, (tm, LANES), 0)
    block_one_hot = jnp.where((pos >> SLC_SHIFT) == lane, 1.0, 0.0).astype(BF16)

    pool_ref[...] = u[:, OFF_POOL:OFF_POOL + POOL_WIDTH]
    for hd in range(NSA_HEADS):
        v = u[:, OFF_Q + hd * LANES:OFF_Q + (hd + 1) * LANES]
        ms = jnp.sum(v * v, axis=-1, keepdims=True) * (1.0 / HEAD_DIM)
        vn = v * lax.rsqrt(ms + EPS) * qg_ref[...]
        q_ref[hd] = (_rope(vn, cos_t, sin_lo, sin_hi) * Q_SCALE).astype(BF16)
    for g in range(NSA_KV_HEADS):
        v = u[:, OFF_KA + g * LANES:OFF_KA + (g + 1) * LANES]
        sq = v * v
        ms_lo = jnp.sum(jnp.where(low_half, sq, 0.0), axis=-1, keepdims=True) * (1.0 / HEAD_DIM)
        ms_hi = jnp.sum(jnp.where(low_half, 0.0, sq), axis=-1, keepdims=True) * (1.0 / HEAD_DIM)
        r = jnp.where(low_half, lax.rsqrt(ms_lo + EPS), lax.rsqrt(ms_hi + EPS))
        kn = _rope(v * r * kg_ref[...], cos_t, sin_lo, sin_hi)
        ka_ref[g, :, 0:LANES] = kn.astype(BF16)
        ka_ref[g, :, LANES:2 * LANES] = block_one_hot
        kb_ref[g] = pltpu.roll(kn, HEAD_DIM, 1).astype(BF16)
        vv = u[:, OFF_VA + g * LANES:OFF_VA + (g + 1) * LANES]
        va_ref[g] = jnp.where(low_half, vv, 1.0).astype(BF16)
        vb_ref[g] = jnp.where(low_half, pltpu.roll(vv, HEAD_DIM, 1), 1.0).astype(BF16)
        gate_ref[g] = _sigmoid(u[:, OFF_GATE + g * LANES:OFF_GATE + (g + 1) * LANES])
    kc_ref[...] = u[:, OFF_KC:OFF_KC + LANES].astype(BF16)
    vc_ref[...] = u[:, OFF_VC:OFF_VC + LANES].astype(BF16)
    s5_ref[...] = u[:, OFF_S5:OFF_S5 + S5_WIDTH]


def _inproj(x, mod, gain, w_mix, rope_tabs, q_gain, k_gain, tm=256):
    b, s, d = x.shape
    g = NSA_KV_HEADS
    tok = lambda width: pl.BlockSpec((None, tm, width), lambda i, j: (i, j, 0))
    grp = lambda n, width=LANES: pl.BlockSpec((None, n, tm, width), lambda i, j: (i, 0, j, 0))
    tab = pl.BlockSpec((tm, LANES), lambda i, j: (j, 0))
    sds = jax.ShapeDtypeStruct
    return pl.pallas_call(
        _inproj_kernel,
        grid=(b, s // tm),
        in_specs=[tok(d),
                  pl.BlockSpec((None, N_MOD, d), lambda i, j: (i, 0, 0)),
                  _const_spec((1, d)),
                  _const_spec((d, N_COLS)),
                  tab, tab, tab,
                  _const_spec((1, LANES)), _const_spec((1, LANES))],
        out_specs=[tok(POOL_WIDTH), grp(NSA_HEADS), grp(g, 2 * LANES), grp(g), grp(g), grp(g),
                   tok(LANES), tok(LANES), grp(g), tok(S5_WIDTH)],
        out_shape=[sds((b, s, POOL_WIDTH), F32), sds((b, NSA_HEADS, s, LANES), BF16),
                   sds((b, g, s, 2 * LANES), BF16), sds((b, g, s, LANES), BF16),
                   sds((b, g, s, LANES), BF16), sds((b, g, s, LANES), BF16),
                   sds((b, s, LANES), BF16), sds((b, s, LANES), BF16),
                   sds((b, g, s, LANES), F32), sds((b, s, S5_WIDTH), F32)],
        compiler_params=_params(("parallel", "parallel")),
        name="mixer_in_proj",
    )(x, mod, gain.reshape(1, d), w_mix, *rope_tabs, q_gain, k_gain)


def _compress_kernel(kc_ref, vc_ref, w1k_ref, w1v_ref, w1k_raw_ref, w1v_raw_ref, pe_ref, w2k_ref, w2v_ref,
                     kg_ref, cos_ref, slo_ref, shi_ref, ko_ref, vo_ref, *, n_cmp):
    ncp = kc_ref.shape[0]
    real_row = lax.broadcasted_iota(jnp.int32, (ncp, LANES), 0) < n_cmp
    for src_ref, w1_ref, raw_ref, w2_ref, out_ref, pe_row, is_key in (
            (kc_ref, w1k_ref, w1k_raw_ref, w2k_ref, ko_ref, 0, True),
            (vc_ref, w1v_ref, w1v_raw_ref, w2v_ref, vo_ref, 1, False)):
        chunks = src_ref[...]
        pe = jnp.broadcast_to(pe_ref[pe_row:pe_row + 1, :], (SUBLANES, CMP_LEN * HEAD_DIM))
        pe_term = _dot(pe, raw_ref[...], HIGHEST)[0:1, :]
        for g in range(NSA_KV_HEADS):
            a = _dot(chunks, w1_ref[g])
            pre = a[:, :CMP_HIDDEN] + pltpu.roll(a[:, CMP_HIDDEN:], ncp - 1, 0) + pe_term
            out = _dot(_gelu_tanh(pre).astype(BF16), w2_ref[...])
            if is_key:
                ms = jnp.sum(out * out, axis=-1, keepdims=True) * (1.0 / HEAD_DIM)
                out = _rope(out * lax.rsqrt(ms + EPS) * kg_ref[...], cos_ref[...], slo_ref[...], shi_ref[...])
            out_ref[g] = jnp.where(real_row, out, 0.0).astype(BF16)


def _compress(kc, vc, w1k, w1v, w1k_raw, w1v_raw, pe, w2k, w2v, k_gain, cmp_tabs):
    b, s, _ = kc.shape
    ncp = s // CMP_STRIDE
    fold = CMP_STRIDE * LANES
    kc_r = kc.reshape(b, ncp, fold)
    vc_r = vc.reshape(b, ncp, fold)
    src = pl.BlockSpec((None, ncp, fold), lambda i: (i, 0, 0))
    out = pl.BlockSpec((None, NSA_KV_HEADS, ncp, LANES), lambda i: (i, 0, 0, 0))
    raw = _const_spec((CMP_LEN * HEAD_DIM, CMP_HIDDEN))
    w1 = _const_spec((NSA_KV_HEADS, fold, 2 * CMP_HIDDEN))
    w2 = _const_spec((CMP_HIDDEN, LANES))
    tab = _const_spec((ncp, LANES))
    return pl.pallas_call(
        functools.partial(_compress_kernel, n_cmp=ncp - 1),
        grid=(b,),
        in_specs=[src, src, w1, w1, raw, raw, _const_spec((2, CMP_LEN * HEAD_DIM)), w2, w2,
                  _const_spec((1, LANES)), tab, tab, tab],
        out_specs=[out, out],
        out_shape=[jax.ShapeDtypeStruct((b, NSA_KV_HEADS, ncp, LANES), BF16)] * 2,
        compiler_params=_params(("parallel",)),
        name="nsa_compress",
    )(kc_r, vc_r, w1k, w1v, w1k_raw, w1v_raw, pe, w2k, w2v, k_gain, *cmp_tabs)


def _selection_bias(imp_t, qb):
    nsb, nq = imp_t.shape
    j = lax.broadcasted_iota(jnp.int32, (nsb, nq), 0)
    t = qb * Q_BLOCK + lax.broadcasted_iota(jnp.int32, (nsb, nq), 1)
    cur = t >> SLC_SHIFT
    valid = j * SLC_BLOCK <= t
    forced = (j == 0) | (j == cur) | (j == cur - 1)
    j_f = j.astype(F32)
    start = jnp.where(valid & jnp.logical_not(forced), imp_t, -1.0)
    vals = start
    for _ in range(N_SELECT - N_FORCED):
        m = jnp.max(vals, axis=0, keepdims=True)
        idx = jnp.min(jnp.where(vals == m, j_f, float(nsb)), axis=0, keepdims=True)
        vals = jnp.where(j_f == idx, -2.0, vals)
    return jnp.where((forced & valid) | (vals != start), 0.0, MASK_VALUE)


def _softmax2_rows(s, mask):
    s = jnp.where(mask, s, NEG_INF)
    e = jnp.exp2(s - jnp.max(s, axis=-1, keepdims=True))
    return e / jnp.sum(e, axis=-1, keepdims=True)


def _attn_kernel(q_ref, kc_ref, vc_ref, ka_ref, kb_ref, va_ref, vb_ref, gate_ref, ovl_ref, o_ref,
                 sa_ref, sb_ref, *, seq):
    qb = pl.program_id(2)
    rows = NSA_REP * Q_BLOCK
    ncp = seq // CMP_STRIDE
    q = q_ref[...].reshape(rows, LANES)
    t_row = qb * Q_BLOCK + lax.broadcasted_iota(jnp.int32, (rows, 1), 0) % Q_BLOCK

    cmp_end = lax.broadcasted_iota(jnp.int32, (1, ncp), 1) * CMP_STRIDE + (CMP_LEN - 1)
    p_cmp = _softmax2_rows(_dot_nt(q, kc_ref[...]), cmp_end <= t_row)
    p_cmp = jnp.where(t_row >= CMP_LEN - 1, p_cmp, 0.0)
    o_cmp = _dot(p_cmp.astype(BF16), vc_ref[...])

    p_grp = (p_cmp[0:Q_BLOCK] + p_cmp[Q_BLOCK:2 * Q_BLOCK]
             + p_cmp[2 * Q_BLOCK:3 * Q_BLOCK] + p_cmp[3 * Q_BLOCK:4 * Q_BLOCK])
    imp_t = _dot_nt(ovl_ref[...], p_grp, HIGHEST)
    bias = _selection_bias(imp_t, qb).T.astype(BF16)

    q_aug = jnp.concatenate([q, jnp.concatenate([bias] * NSA_REP, axis=0)], axis=1)

    span = WINDOW + Q_BLOCK
    start = pl.multiple_of(jnp.maximum(qb * Q_BLOCK - WINDOW, 0), Q_BLOCK)
    kpos = start + lax.broadcasted_iota(jnp.int32, (1, span), 1)
    s = jnp.where((kpos <= t_row) & (kpos > t_row - WINDOW), _dot_nt(q, kb_ref[pl.ds(start, span), :]), NEG_INF)
    e = jnp.exp2(s - jnp.max(s, axis=-1, keepdims=True)).astype(BF16)
    acc_win = _dot(e, vb_ref[pl.ds(start, span), :])
    o_win = acc_win * (1.0 / acc_win[:, HEAD_DIM:HEAD_DIM + 1])

    n_key_tiles = seq // KEY_TILE

    def score_tile(kt, s_ref):
        k0 = pl.multiple_of(jnp.minimum(kt, n_key_tiles - 1) * KEY_TILE, KEY_TILE)
        s_ref[...] = _dot_nt(q_aug, ka_ref[pl.ds(k0, KEY_TILE), :])

    def absorb_tile(kt, s_ref, carry, causal):
        m_i, acc = carry
        k0 = pl.multiple_of(jnp.minimum(kt, n_key_tiles - 1) * KEY_TILE, KEY_TILE)
        s = s_ref[...]
        if causal:
            kpos = kt * KEY_TILE + lax.broadcasted_iota(jnp.int32, (1, KEY_TILE), 1)
            s = jnp.where(kpos <= t_row, s, MASK_VALUE)
        m_new = jnp.maximum(m_i, jnp.max(s, axis=-1, keepdims=True))
        p = jnp.exp2(s - m_new).astype(BF16)
        return m_new, jnp.exp2(m_i - m_new) * acc + _dot(p, va_ref[pl.ds(k0, KEY_TILE), :])

    def slc_trip(j, carry):
        score_tile(2 * j + 1, sb_ref)
        carry = absorb_tile(2 * j, sa_ref, carry, causal=False)
        score_tile(2 * j + 2, sa_ref)
        return absorb_tile(2 * j + 1, sb_ref, carry, causal=False)

    last_pair = qb // (2 * KEY_TILE // Q_BLOCK)
    score_tile(0, sa_ref)
    carry = lax.fori_loop(0, last_pair, slc_trip,
                          (jnp.full((rows, 1), NEG_INF, F32), jnp.zeros((rows, LANES), F32)))
    score_tile(2 * last_pair + 1, sb_ref)
    carry = absorb_tile(2 * last_pair, sa_ref, carry, causal=True)
    _, acc_slc = absorb_tile(2 * last_pair + 1, sb_ref, carry, causal=True)
    o_slc = acc_slc * (1.0 / acc_slc[:, HEAD_DIM:HEAD_DIM + 1])

    gates = gate_ref[...]
    low_half = lax.broadcasted_iota(jnp.int32, (Q_BLOCK, LANES), 1) < HEAD_DIM
    heads = []
    for r in range(NSA_REP):
        rs = slice(r * Q_BLOCK, (r + 1) * Q_BLOCK)
        c = r * N_BRANCH
        heads.append(gates[:, c:c + 1] * o_cmp[rs] + gates[:, c + 1:c + 2] * o_slc[rs]
                     + gates[:, c + 2:c + 3] * o_win[rs])
    for pair in range(NSA_REP // 2):
        o_ref[:, pair * LANES:(pair + 1) * LANES] = jnp.where(
            low_half, heads[2 * pair], pltpu.roll(heads[2 * pair + 1], HEAD_DIM, 1))


def _overlap_matrix(seq):
    ncp, nsb = seq // CMP_STRIDE, seq // SLC_BLOCK
    c_start = np.arange(ncp)[None, :] * CMP_STRIDE
    s_start = np.arange(nsb)[:, None] * SLC_BLOCK
    ovl = np.clip(np.minimum(c_start + CMP_LEN, s_start + SLC_BLOCK) - np.maximum(c_start, s_start), 0, None)
    ovl = ovl.astype(np.float32) / CMP_LEN
    ovl[:, ncp - 1] = 0.0
    return jnp.asarray(np.pad(ovl, ((0, LANES - nsb), (0, 0))))


def _nsa_attention(q, k_cmp, v_cmp, ka, kb, va, vb, gates):
    b, _, s, _ = q.shape
    g = NSA_KV_HEADS
    ncp = s // CMP_STRIDE
    assert s // SLC_BLOCK <= LANES
    full = lambda n, width=LANES: pl.BlockSpec((None, None, n, width), lambda i, j, k: (i, j, 0, 0))
    return pl.pallas_call(
        functools.partial(_attn_kernel, seq=s),
        grid=(b, g, s // Q_BLOCK),
        in_specs=[pl.BlockSpec((None, NSA_REP, Q_BLOCK, LANES), lambda i, j, k: (i, j, k, 0)),
                  full(ncp), full(ncp), full(s, 2 * LANES), full(s), full(s), full(s),
                  pl.BlockSpec((None, None, Q_BLOCK, LANES), lambda i, j, k: (i, j, k, 0)),
                  _const_spec((LANES, ncp))],
        out_specs=pl.BlockSpec((None, Q_BLOCK, NSA_REP * HEAD_DIM), lambda i, j, k: (i, k, j)),
        out_shape=jax.ShapeDtypeStruct((b, s, NSA_WIDTH), F32),
        scratch_shapes=[pltpu.VMEM((NSA_REP * Q_BLOCK, KEY_TILE), F32)] * 2,
        compiler_params=_params(("parallel", "parallel", "arbitrary")),
        name="nsa_attention",
    )(q, k_cmp, v_cmp, ka, kb, va, vb, gates, _overlap_matrix(s))


def _s5_kernel(u_ref, mt_ref, bc_ref, cc_ref, are_ref, aim_ref, d_ref, y_ref, h_ref, g_scr, hp_scr):
    @pl.when(pl.program_id(1) == 0)
    def _():
        h_ref[...] = jnp.zeros_like(h_ref)

    u = u_ref[...]
    ub = u.astype(BF16)
    y_local = _dot(ub, mt_ref[...])
    g_scr[...] = _dot(ub, bc_ref[...])
    a_re, a_im = are_ref[...], aim_ref[...]

    def step(i, carry):
        h_re, h_im = carry
        hp_scr[pl.ds(i, 1), 0:S5_STATE] = h_re
        hp_scr[pl.ds(i, 1), S5_STATE:2 * S5_STATE] = h_im
        g_re = g_scr[pl.ds(i, 1), 0:S5_STATE]
        g_im = g_scr[pl.ds(i, 1), S5_STATE:2 * S5_STATE]
        return a_re * h_re - a_im * h_im + g_re, a_re * h_im + a_im * h_re + g_im

    h_re, h_im = lax.fori_loop(0, u.shape[0], step, (h_ref[0:1, :], h_ref[1:2, :]))
    h_ref[0:1, :] = h_re
    h_ref[1:2, :] = h_im
    y_ref[...] = y_local + _dot(hp_scr[...].astype(BF16), cc_ref[...]) + u * d_ref[...]


def _s5_scan(u, mats):
    b, s, _ = u.shape
    rows = s // S5_CHUNK
    tc = min(128, rows)
    mt, bc, cc, a_re, a_im, d_vec = mats
    tile = pl.BlockSpec((None, tc, S5_FOLD), lambda i, j: (i, j, 0))
    mat = _const_spec((S5_FOLD, S5_FOLD))
    y = pl.pallas_call(
        _s5_kernel,
        grid=(b, rows // tc),
        in_specs=[tile, mat, mat, mat, _const_spec((1, S5_STATE)), _const_spec((1, S5_STATE)),
                  _const_spec((1, S5_FOLD))],
        out_specs=tile,
        out_shape=jax.ShapeDtypeStruct((b, rows, S5_FOLD), F32),
        scratch_shapes=[pltpu.VMEM((2, S5_STATE), F32), pltpu.VMEM((tc, S5_FOLD), F32),
                        pltpu.VMEM((tc, S5_FOLD), F32)],
        compiler_params=_params(("parallel", "arbitrary")),
        name="s5_scan",
    )(u.reshape(b, rows, S5_FOLD), mt, bc, cc, a_re, a_im, d_vec)
    return y.reshape(b, s, S5_WIDTH)


def _s5_matrices(lam_re, lam_im, log_dt, b_re, b_im, c_re, c_im, d_skip):
    t0, ng, nh, npm = S5_CHUNK, S5_GROUPS, S5_H, S5_P
    ein = functools.partial(jnp.einsum, precision=HIGHEST)
    lam = lax.complex(lam_re, lam_im)
    step = jnp.exp(log_dt)[:, None]
    lam_bar = jnp.exp(lam * step)
    b_bar = lax.complex(b_re, b_im) * ((lam_bar - 1.0) / lam)[..., None]
    c_mat = lax.complex(c_re, c_im)
    k = jnp.arange(t0 + 1, dtype=F32)[:, None, None]
    pw = jnp.exp((lam * step)[None] * k)
    eye = jnp.eye(ng, dtype=F32)
    kern = jnp.real(ein('ghp,kgp,gpq->kghq', c_mat, pw[:t0], b_bar))
    lag = jnp.arange(t0)[None, :] - jnp.arange(t0)[:, None]
    k_ji = jnp.where((lag >= 0)[:, :, None, None, None], kern[jnp.clip(lag, 0, t0 - 1)], 0.0)
    mt = ein('jighq,gf->jgqifh', k_ji, eye).reshape(t0 * ng * nh, t0 * ng * nh)
    b_j = pw[t0 - 1 - jnp.arange(t0)][..., None] * b_bar[None]
    bc_re = ein('jgpq,gf->jgqfp', jnp.real(b_j), eye).reshape(t0 * ng * nh, ng * npm)
    bc_im = ein('jgpq,gf->jgqfp', jnp.imag(b_j), eye).reshape(t0 * ng * nh, ng * npm)
    bc = jnp.concatenate([bc_re, bc_im], axis=1)
    c_i = c_mat[None] * pw[1:t0 + 1][:, :, None, :]
    cc_re = ein('ighp,gf->gpifh', jnp.real(c_i), eye).reshape(ng * npm, t0 * ng * nh)
    cc_im = ein('ighp,gf->gpifh', -jnp.imag(c_i), eye).reshape(ng * npm, t0 * ng * nh)
    cc = jnp.concatenate([cc_re, cc_im], axis=0)
    a_chunk = pw[t0].reshape(1, ng * npm)
    d_vec = jnp.tile(d_skip.reshape(1, ng * nh), (1, t0))
    return (mt.astype(BF16), bc.astype(BF16), cc.astype(BF16),
            jnp.real(a_chunk), jnp.imag(a_chunk), d_vec)


def _rms_gain(y, gain):
    return y * lax.rsqrt(jnp.mean(y * y, axis=-1, keepdims=True) + EPS) * gain


def _outproj_kernel(x_ref, mod_ref, pool_ref, halo_ref, nsa_ref, s5_ref, pw_ref, pb_ref, ps_ref, on_ref,
                    gw_ref, gb_ref, wo_ref, o_ref, buf):
    j = pl.program_id(1)
    tm = x_ref.shape[0]
    v = pool_ref[...]
    buf[0:POOL_HALO, :] = jnp.where(j > 0, halo_ref[...], 0.0)
    buf[POOL_HALO:POOL_HALO + tm, :] = v
    lane_group = lax.broadcasted_iota(jnp.int32, (tm, POOL_WIDTH), 1) >> SLC_SHIFT
    t1 = (j * tm + 1 + lax.broadcasted_iota(jnp.int32, (tm, 1), 0)).astype(F32)
    run, k, pooled = v, 1, jnp.zeros_like(v)
    for gi, w in enumerate(POOL_WINDOWS):
        while k < w:
            run = run + buf[POOL_HALO - k:POOL_HALO - k + tm, :]
            k += 1
        pooled = jnp.where(lane_group == gi, run / jnp.minimum(t1, float(w)) - v, pooled)
    y_pool = (_dot(pooled.astype(BF16), pw_ref[...]) + pb_ref[...]) * ps_ref[...]

    y = _gelu_tanh(s5_ref[...])
    y_s5 = y * _sigmoid(_dot(y.astype(BF16), gw_ref[...]) + gb_ref[...])

    cat = jnp.concatenate(
        [_rms_gain(y_pool, on_ref[:, 0:POOL_WIDTH]),
         _rms_gain(nsa_ref[...], on_ref[:, POOL_WIDTH:POOL_WIDTH + NSA_WIDTH]),
         _rms_gain(y_s5, on_ref[:, POOL_WIDTH + NSA_WIDTH:])], axis=-1).astype(BF16)
    o_ref[...] = x_ref[...] + mod_ref[5:6, :] * _dot(cat, wo_ref[...])


def _outproj(x, mod, u_pool, o_nsa, y_s5, pool_w_bd, pool_b, pool_scale, out_norm, glu_w, glu_b, w_out, tm=256):
    b, s, d = x.shape
    tok = lambda width: pl.BlockSpec((None, tm, width), lambda i, j: (i, j, 0))
    halo_blocks = tm // POOL_HALO
    return pl.pallas_call(
        _outproj_kernel,
        grid=(b, s // tm),
        in_specs=[tok(d),
                  pl.BlockSpec((None, N_MOD, d), lambda i, j: (i, 0, 0)),
                  tok(POOL_WIDTH),
                  pl.BlockSpec((None, POOL_HALO, POOL_WIDTH),
                               lambda i, j: (i, jnp.maximum(j * halo_blocks - 1, 0), 0)),
                  tok(NSA_WIDTH), tok(S5_WIDTH),
                  _const_spec((POOL_WIDTH, POOL_WIDTH)), _const_spec((1, POOL_WIDTH)),
                  _const_spec((1, POOL_WIDTH)), _const_spec((1, d)),
                  _const_spec((S5_WIDTH, S5_WIDTH)), _const_spec((1, S5_WIDTH)),
                  _const_spec((d, d))],
        out_specs=tok(d),
        out_shape=jax.ShapeDtypeStruct((b, s, d), F32),
        scratch_shapes=[pltpu.VMEM((POOL_HALO + tm, POOL_WIDTH), F32)],
        compiler_params=_params(("parallel", "arbitrary")),
        name="mixer_out_proj",
    )(x, mod, u_pool, u_pool, o_nsa, y_s5, pool_w_bd, pool_b, pool_scale, out_norm, glu_w, glu_b, w_out)


def _rope_tables(pos):
    half = ROT_DIM // 2
    inv_freq = jnp.exp(-math.log(ROPE_THETA) * jnp.arange(half, dtype=F32) * (2.0 / ROT_DIM))
    ang = pos[:, None] * inv_freq[None, :]
    cos, sin = jnp.cos(ang), jnp.sin(ang)
    n = pos.shape[0]
    rest = HEAD_DIM - ROT_DIM
    cos_t = jnp.concatenate([cos, cos, jnp.ones((n, rest), F32)], axis=1)
    sin_lo = jnp.concatenate([-sin, jnp.zeros((n, half + rest), F32)], axis=1)
    sin_hi = jnp.concatenate([jnp.zeros((n, half), F32), sin, jnp.zeros((n, rest), F32)], axis=1)
    return tuple(jnp.tile(t, (1, 2)) for t in (cos_t, sin_lo, sin_hi))


def _pad_lanes(v):
    return jnp.pad(v, [(0, 0)] * (v.ndim - 1) + [(0, LANES - v.shape[-1])])


def _arrange_w_in(w):
    d = w.shape[0]
    o1, o2, o3 = POOL_WIDTH, POOL_WIDTH + NSA_WIDTH, POOL_WIDTH + NSA_WIDTH + 6 * LANES
    q = _pad_lanes(w[:, o1:o2].reshape(d, NSA_HEADS, HEAD_DIM)).reshape(d, NSA_HEADS * LANES)
    kv = w[:, o2:o3].reshape(d, 6, NSA_KV_HEADS, HEAD_DIM)
    ka = jnp.concatenate([kv[:, 2], kv[:, 4]], axis=-1).reshape(d, NSA_KV_HEADS * LANES)
    va = jnp.concatenate([kv[:, 3], kv[:, 5]], axis=-1).reshape(d, NSA_KV_HEADS * LANES)
    kc = kv[:, 0].reshape(d, LANES)
    vc = kv[:, 1].reshape(d, LANES)
    n_gate = NSA_REP * N_BRANCH
    gate = _pad_lanes(w[:, o3:o3 + NSA_KV_HEADS * n_gate].reshape(d, NSA_KV_HEADS, n_gate))
    s5 = w[:, o3 + NSA_KV_HEADS * n_gate:]
    return jnp.concatenate([w[:, :o1], q, ka, va, kc, vc, gate.reshape(d, NSA_KV_HEADS * LANES), s5],
                           axis=1).astype(BF16)


def _expand_cmp_w1(w1):
    halves = w1.reshape(2, CMP_STRIDE, HEAD_DIM, CMP_HIDDEN)
    both = jnp.concatenate([halves[0], halves[1]], axis=-1)
    out = []
    for g in range(NSA_KV_HEADS):
        z = jnp.zeros((CMP_STRIDE, NSA_KV_HEADS, HEAD_DIM, 2 * CMP_HIDDEN), F32).at[:, g].set(both)
        out.append(z.reshape(CMP_STRIDE * LANES, 2 * CMP_HIDDEN))
    return jnp.stack(out).astype(BF16)


def _block_diag(w):
    g, c, _ = w.shape
    return jnp.einsum('gcd,gf->gcfd', w, jnp.eye(g, dtype=w.dtype)).reshape(g * c, g * c)


def _hybrid_layer(x, mod, tabs, cmp_tabs, norm_ffn1, ffn1_w_in, ffn1_w_out, norm_mix, w_in, w_out, out_norm,
                  pool_w, pool_b, pool_scale, q_norm, k_norm, cmp_pe, cmp_k_w1, cmp_k_w2, cmp_v_w1, cmp_v_w2,
                  s5_lam_re, s5_lam_im, s5_log_dt, s5_b_re, s5_b_im, s5_c_re, s5_c_im, s5_d, glu_w, glu_b,
                  norm_ffn2, ffn2_w_in, ffn2_w_out):
    x = _ffn(x, mod, norm_ffn1, ffn1_w_in.astype(BF16), ffn1_w_out.astype(BF16), 0)

    q_gain = _pad_lanes(q_norm.reshape(1, HEAD_DIM))
    slc_win_gain = jnp.concatenate([k_norm[1], k_norm[2]]).reshape(1, LANES)
    u_pool, q, ka, kb, va, vb, kc, vc, gates, u_s5 = _inproj(
        x, mod, norm_mix, _arrange_w_in(w_in), tabs, q_gain, slc_win_gain)

    k_cmp, v_cmp = _compress(
        kc, vc, _expand_cmp_w1(cmp_k_w1), _expand_cmp_w1(cmp_v_w1), cmp_k_w1, cmp_v_w1,
        cmp_pe.reshape(2, CMP_LEN * HEAD_DIM), _pad_lanes(cmp_k_w2).astype(BF16),
        _pad_lanes(cmp_v_w2).astype(BF16), _pad_lanes(k_norm[0].reshape(1, HEAD_DIM)), cmp_tabs)
    o_nsa = _nsa_attention(q, k_cmp, v_cmp, ka, kb, va, vb, gates)

    y_s5 = _s5_scan(u_s5, _s5_matrices(s5_lam_re, s5_lam_im, s5_log_dt, s5_b_re, s5_b_im,
                                       s5_c_re, s5_c_im, s5_d))

    x = _outproj(x, mod, u_pool, o_nsa, y_s5, _block_diag(pool_w).astype(BF16),
                 pool_b.reshape(1, POOL_WIDTH), pool_scale.reshape(1, POOL_WIDTH),
                 out_norm.reshape(1, D_MODEL), glu_w.astype(BF16), glu_b.reshape(1, S5_WIDTH),
                 w_out.astype(BF16))
    return _ffn(x, mod, norm_ffn2, ffn2_w_in.astype(BF16), ffn2_w_out.astype(BF16), 6)


def kernel(x, c, ada_w, ada_b, norm_ffn1, ffn1_w_in, ffn1_w_out, norm_mix, w_in, w_out, out_norm, pool_w, pool_b, pool_scale, q_norm, k_norm, cmp_pe, cmp_k_w1, cmp_k_w2, cmp_v_w1, cmp_v_w2, s5_lam_re, s5_lam_im, s5_log_dt, s5_b_re, s5_b_im, s5_c_re, s5_c_im, s5_d, glu_w, glu_b, norm_ffn2, ffn2_w_in, ffn2_w_out):
    seq = x.shape[1]
    assert seq % (KEY_TILE * 4) == 0 and seq >= WINDOW + Q_BLOCK
    mod = _modulation(c, ada_w, ada_b)
    tabs = _rope_tables(jnp.arange(seq, dtype=F32))
    n_cmp_pad = seq // CMP_STRIDE
    cmp_tabs = _rope_tables((jnp.arange(n_cmp_pad) * CMP_STRIDE + CMP_LEN - 1).astype(F32))
    per_layer = (norm_ffn1, ffn1_w_in, ffn1_w_out, norm_mix, w_in, w_out, out_norm, pool_w, pool_b, pool_scale,
                 q_norm, k_norm, cmp_pe, cmp_k_w1, cmp_k_w2, cmp_v_w1, cmp_v_w2, s5_lam_re, s5_lam_im,
                 s5_log_dt, s5_b_re, s5_b_im, s5_c_re, s5_c_im, s5_d, glu_w, glu_b, norm_ffn2, ffn2_w_in,
                 ffn2_w_out)
    for l in range(ada_w.shape[0]):
        x = _hybrid_layer(x, mod[l], tabs, cmp_tabs, *[p[l] for p in per_layer])
    return x
```

```python
import functools
import math

import jax
import jax.numpy as jnp
import numpy as np
from jax import lax
from jax.experimental import pallas as pl
from jax.experimental.pallas import tpu as pltpu

F32 = jnp.float32
BF16 = jnp.bfloat16
HIGHEST = lax.Precision.HIGHEST

LANES = 128
SUBLANES = 8
VMEM_LIMIT_BYTES = 56 * 1024 * 1024

D_MODEL = 1024
D_FF = 2816
N_MOD = 9
POOL_WIDTH = 256
POOL_GC = 64
POOL_WINDOWS = (2, 4, 8, 16)
POOL_HALO = 16
HEAD_DIM = 64
NSA_WIDTH = 512
NSA_HEADS = 8
NSA_KV_HEADS = 2
NSA_REP = 4
N_BRANCH = 3
S5_WIDTH = 256
S5_H = 16
S5_GROUPS = 16
S5_P = 64
CMP_STRIDE = 16
CMP_LEN = 32
CMP_HIDDEN = 128
SLC_BLOCK = 64
SLC_SHIFT = 6
N_SELECT = 16
N_FORCED = 3
WINDOW = 512
Q_BLOCK = 128
ROT_DIM = 16
ROPE_THETA = 500000.0
EPS = 1e-6
Q_SCALE = HEAD_DIM ** -0.5 * math.log2(math.e)
NEG_INF = -1e30
MASK_VALUE = -(2.0 ** 100)

OFF_POOL = 0
OFF_Q = OFF_POOL + POOL_WIDTH
OFF_KA = OFF_Q + NSA_HEADS * LANES
OFF_VA = OFF_KA + NSA_KV_HEADS * LANES
OFF_KC = OFF_VA + NSA_KV_HEADS * LANES
OFF_VC = OFF_KC + LANES
OFF_GATE = OFF_VC + LANES
OFF_S5 = OFF_GATE + NSA_KV_HEADS * LANES
N_COLS = OFF_S5 + S5_WIDTH

S5_CHUNK = 8
S5_FOLD = S5_CHUNK * S5_WIDTH
S5_STATE = S5_GROUPS * S5_P
S5_HALVES = S5_WIDTH // LANES
KEY_TILE = 512


def _dot(a, b, precision=None):
    return jnp.dot(a, b, preferred_element_type=F32, precision=precision)


def _dot_nt(a, b, precision=None):
    return lax.dot_general(a, b, (((1,), (1,)), ((), ())), preferred_element_type=F32,
                           precision=precision)


def _sigmoid(x):
    return 1.0 / (1.0 + jnp.exp(-x))


def _gelu_tanh(x):
    return 0.5 * x * (1.0 + jnp.tanh(math.sqrt(2.0 / math.pi) * (x + 0.044715 * (x * x * x))))


def _params(sem):
    return pltpu.CompilerParams(dimension_semantics=sem, vmem_limit_bytes=VMEM_LIMIT_BYTES)


def _const_spec(shape):
    nd = len(shape)
    return pl.BlockSpec(shape, lambda *_: (0,) * nd, pipeline_mode=pl.Buffered(1))


def _mod_kernel(c_ref, w_ref, b_ref, o_ref):
    c = c_ref[...]
    o_ref[...] = _dot(c * _sigmoid(c), w_ref[...], HIGHEST) + b_ref[...]


def _modulation(c, ada_w, ada_b):
    n_layers, d, n = ada_w.shape
    b = c.shape[0]
    tn = 1152
    c_pad = jnp.zeros((SUBLANES, d), F32).at[:b].set(c)
    out = pl.pallas_call(
        _mod_kernel,
        grid=(n_layers, n // tn),
        in_specs=[pl.BlockSpec((SUBLANES, d), lambda l, j: (0, 0)),
                  pl.BlockSpec((None, d, tn), lambda l, j: (l, 0, j)),
                  pl.BlockSpec((None, 1, tn), lambda l, j: (l, 0, j))],
        out_specs=pl.BlockSpec((None, SUBLANES, tn), lambda l, j: (l, 0, j)),
        out_shape=jax.ShapeDtypeStruct((n_layers, SUBLANES, n), F32),
        compiler_params=_params(("arbitrary", "arbitrary")),
        name="adaln_mod",
    )(c_pad, ada_w, ada_b.reshape(n_layers, 1, n))
    return out[:, :b].reshape(n_layers, b, N_MOD, d)


def _norm_modulate(x, gain, mod_ref, first_row):
    ms = jnp.mean(x * x, axis=-1, keepdims=True)
    y = x * lax.rsqrt(ms + EPS) * gain
    return y * (1.0 + mod_ref[first_row + 1:first_row + 2, :]) + mod_ref[first_row:first_row + 1, :]


def _ffn_kernel(x_ref, mod_ref, g_ref, win_ref, wout_ref, o_ref, *, first_row):
    x = x_ref[...]
    h = _norm_modulate(x, g_ref[...], mod_ref, first_row).astype(BF16)
    gu = _dot(h, win_ref[...])
    gate, up = gu[:, :D_FF], gu[:, D_FF:]
    a = (gate * _sigmoid(gate) * up).astype(BF16)
    y = _dot(a, wout_ref[...])
    o_ref[...] = x + 0.5 * mod_ref[first_row + 2:first_row + 3, :] * y


def _ffn(x, mod, gain, w_in, w_out, first_row, tm=256):
    b, s, d = x.shape
    return pl.pallas_call(
        functools.partial(_ffn_kernel, first_row=first_row),
        grid=(b, s // tm),
        in_specs=[pl.BlockSpec((None, tm, d), lambda i, j: (i, j, 0)),
                  pl.BlockSpec((None, N_MOD, d), lambda i, j: (i, 0, 0)),
                  _const_spec((1, d)),
                  _const_spec((d, 2 * D_FF)),
                  _const_spec((D_FF, d))],
        out_specs=pl.BlockSpec((None, tm, d), lambda i, j: (i, j, 0)),
        out_shape=jax.ShapeDtypeStruct((b, s, d), F32),
        compiler_params=_params(("parallel", "parallel")),
        name="ffn_half_step",
    )(x, mod, gain.reshape(1, d), w_in, w_out)


def _rope(v, cos_t, sin_lo, sin_hi):
    return (v * cos_t + pltpu.roll(v, LANES - ROT_DIM // 2, 1) * sin_lo
            + pltpu.roll(v, ROT_DIM // 2, 1) * sin_hi)


def _inproj_kernel(x_ref, mod_ref, g_ref, w_ref, cos_ref, slo_ref, shi_ref, qg_ref, kg_ref,
                   pool_ref, q_ref, ka_ref, kb_ref, va_ref, vb_ref, kc_ref, vc_ref, gate_ref, s5_ref):
    h = _norm_modulate(x_ref[...], g_ref[...], mod_ref, 3).astype(BF16)
    u = _dot(h, w_ref[...])
    tm = u.shape[0]
    cos_t, sin_lo, sin_hi = cos_ref[...], slo_ref[...], shi_ref[...]
    lane = lax.broadcasted_iota(jnp.int32, (tm, LANES), 1)
    low_half = lane < HEAD_DIM
    pos = pl.program_id(1) * tm + lax.broadcasted_iota(jnp.int32, (tm, LANES), 0)
    block_one_hot = jnp.where((pos >> SLC_SHIFT) == lane, 1.0, 0.0).astype(BF16)

    pool_ref[...] = u[:, OFF_POOL:OFF_POOL + POOL_WIDTH]
    for hd in range(NSA_HEADS):
        v = u[:, OFF_Q + hd * LANES:OFF_Q + (hd + 1) * LANES]
        ms = jnp.sum(v * v, axis=-1, keepdims=True) * (1.0 / HEAD_DIM)
        vn = v * lax.rsqrt(ms + EPS) * qg_ref[...]
        q_ref[hd] = (_rope(vn, cos_t, sin_lo, sin_hi) * Q_SCALE).astype(BF16)
    for g in range(NSA_KV_HEADS):
        v = u[:, OFF_KA + g * LANES:OFF_KA + (g + 1) * LANES]
        sq = v * v
        ms_lo = jnp.sum(jnp.where(low_half, sq, 0.0), axis=-1, keepdims=True) * (1.0 / HEAD_DIM)
        ms_hi = jnp.sum(jnp.where(low_half, 0.0, sq), axis=-1, keepdims=True) * (1.0 / HEAD_DIM)
        r = jnp.where(low_half, lax.rsqrt(ms_lo + EPS), lax.rsqrt(ms_hi + EPS))
        kn = _rope(v * r * kg_ref[...], cos_t, sin_lo, sin_hi)
        ka_ref[g, :, 0:LANES] = kn.astype(BF16)
        ka_ref[g, :, LANES:2 * LANES] = block_one_hot
        kb_ref[g] = pltpu.roll(kn, HEAD_DIM, 1).astype(BF16)
        vv = u[:, OFF_VA + g * LANES:OFF_VA + (g + 1) * LANES]
        va_ref[g] = jnp.where(low_half, vv, 1.0).astype(BF16)
        vb_ref[g] = jnp.where(low_half, pltpu.roll(vv, HEAD_DIM, 1), 1.0).astype(BF16)
        gate_ref[g] = _sigmoid(u[:, OFF_GATE + g * LANES:OFF_GATE + (g + 1) * LANES])
    kc_ref[...] = u[:, OFF_KC:OFF_KC + LANES]
    vc_ref[...] = u[:, OFF_VC:OFF_VC + LANES]
    for half in range(S5_HALVES):
        s5_ref[half] = u[:, OFF_S5 + half * LANES:OFF_S5 + (half + 1) * LANES]


def _inproj(x, mod, gain, w_mix, rope_tabs, q_gain, k_gain, tm=256):
    b, s, d = x.shape
    g = NSA_KV_HEADS
    tok = lambda width: pl.BlockSpec((None, tm, width), lambda i, j: (i, j, 0))
    grp = lambda n, width=LANES: pl.BlockSpec((None, n, tm, width), lambda i, j: (i, 0, j, 0))
    tab = pl.BlockSpec((tm, LANES), lambda i, j: (j, 0))
    sds = jax.ShapeDtypeStruct
    return pl.pallas_call(
        _inproj_kernel,
        grid=(b, s // tm),
        in_specs=[tok(d),
                  pl.BlockSpec((None, N_MOD, d), lambda i, j: (i, 0, 0)),
                  _const_spec((1, d)),
                  _const_spec((d, N_COLS)),
                  tab, tab, tab,
                  _const_spec((1, LANES)), _const_spec((1, LANES))],
        out_specs=[tok(POOL_WIDTH), grp(NSA_HEADS), grp(g, 2 * LANES), grp(g), grp(g), grp(g),
                   tok(LANES), tok(LANES), grp(g), grp(S5_HALVES)],
        out_shape=[sds((b, s, POOL_WIDTH), F32), sds((b, NSA_HEADS, s, LANES), BF16),
                   sds((b, g, s, 2 * LANES), BF16), sds((b, g, s, LANES), BF16),
                   sds((b, g, s, LANES), BF16), sds((b, g, s, LANES), BF16),
                   sds((b, s, LANES), F32), sds((b, s, LANES), F32),
                   sds((b, g, s, LANES), F32), sds((b, S5_HALVES, s, LANES), F32)],
        compiler_params=_params(("parallel", "parallel")),
        name="mixer_in_proj",
    )(x, mod, gain.reshape(1, d), w_mix, *rope_tabs, q_gain, k_gain)


def _fold_rows(ref, n):
    rows = ref.shape[0] // n
    return jnp.concatenate([ref[pl.ds(k, rows, stride=n), :] for k in range(n)], axis=1)


def _unfold_rows(ref, value, n):
    rows = ref.shape[0] // n
    for k in range(n):
        ref[pl.ds(k, rows, stride=n), :] = value[:, k * LANES:(k + 1) * LANES]


def _compress_kernel(kc_ref, vc_ref, w1k_ref, w1v_ref, w1k_raw_ref, w1v_raw_ref, pe_ref, w2k_ref, w2v_ref,
                     kg_ref, cos_ref, slo_ref, shi_ref, ko_ref, vo_ref, *, n_cmp):
    ncp = kc_ref.shape[0] // CMP_STRIDE
    real_row = lax.broadcasted_iota(jnp.int32, (ncp, LANES), 0) < n_cmp
    for src_ref, w1_ref, raw_ref, w2_ref, out_ref, pe_row, is_key in (
            (kc_ref, w1k_ref, w1k_raw_ref, w2k_ref, ko_ref, 0, True),
            (vc_ref, w1v_ref, w1v_raw_ref, w2v_ref, vo_ref, 1, False)):
        chunks = _fold_rows(src_ref, CMP_STRIDE).astype(BF16)
        pe = jnp.broadcast_to(pe_ref[pe_row:pe_row + 1, :], (SUBLANES, CMP_LEN * HEAD_DIM))
        pe_term = _dot(pe, raw_ref[...], HIGHEST)[0:1, :]
        for g in range(NSA_KV_HEADS):
            a = _dot(chunks, w1_ref[g])
            pre = a[:, :CMP_HIDDEN] + pltpu.roll(a[:, CMP_HIDDEN:], ncp - 1, 0) + pe_term
            out = _dot(_gelu_tanh(pre).astype(BF16), w2_ref[...])
            if is_key:
                ms = jnp.sum(out * out, axis=-1, keepdims=True) * (1.0 / HEAD_DIM)
                out = _rope(out * lax.rsqrt(ms + EPS) * kg_ref[...], cos_ref[...], slo_ref[...], shi_ref[...])
            out_ref[g] = jnp.where(real_row, out, 0.0).astype(BF16)


def _compress(kc, vc, w1k, w1v, w1k_raw, w1v_raw, pe, w2k, w2v, k_gain, cmp_tabs):
    b, s, _ = kc.shape
    ncp = s // CMP_STRIDE
    fold = CMP_STRIDE * LANES
    src = pl.BlockSpec((None, s, LANES), lambda i: (i, 0, 0))
    out = pl.BlockSpec((None, NSA_KV_HEADS, ncp, LANES), lambda i: (i, 0, 0, 0))
    raw = _const_spec((CMP_LEN * HEAD_DIM, CMP_HIDDEN))
    w1 = _const_spec((NSA_KV_HEADS, fold, 2 * CMP_HIDDEN))
    w2 = _const_spec((CMP_HIDDEN, LANES))
    tab = _const_spec((ncp, LANES))
    return pl.pallas_call(
        functools.partial(_compress_kernel, n_cmp=ncp - 1),
        grid=(b,),
        in_specs=[src, src, w1, w1, raw, raw, _const_spec((2, CMP_LEN * HEAD_DIM)), w2, w2,
                  _const_spec((1, LANES)), tab, tab, tab],
        out_specs=[out, out],
        out_shape=[jax.ShapeDtypeStruct((b, NSA_KV_HEADS, ncp, LANES), BF16)] * 2,
        compiler_params=_params(("parallel",)),
        name="nsa_compress",
    )(kc, vc, w1k, w1v, w1k_raw, w1v_raw, pe, w2k, w2v, k_gain, *cmp_tabs)


def _selection_bias(imp_t, qb):
    nsb, nq = imp_t.shape
    j = lax.broadcasted_iota(jnp.int32, (nsb, nq), 0)
    t = qb * Q_BLOCK + lax.broadcasted_iota(jnp.int32, (nsb, nq), 1)
    cur = t >> SLC_SHIFT
    valid = j * SLC_BLOCK <= t
    forced = (j == 0) | (j == cur) | (j == cur - 1)
    j_f = j.astype(F32)
    start = jnp.where(valid & jnp.logical_not(forced), imp_t, -1.0)
    vals = start
    for _ in range(N_SELECT - N_FORCED):
        m = jnp.max(vals, axis=0, keepdims=True)
        idx = jnp.min(jnp.where(vals == m, j_f, float(nsb)), axis=0, keepdims=True)
        vals = jnp.where(j_f == idx, -2.0, vals)
    return jnp.where((forced & valid) | (vals != start), 0.0, MASK_VALUE)


def _softmax2_rows(s, mask):
    s = jnp.where(mask, s, NEG_INF)
    e = jnp.exp2(s - jnp.max(s, axis=-1, keepdims=True))
    return e / jnp.sum(e, axis=-1, keepdims=True)


def _attn_kernel(q_ref, kc_ref, vc_ref, ka_ref, kb_ref, va_ref, vb_ref, gate_ref, ovl_ref, o_ref,
                 sa_ref, sb_ref, *, seq):
    qb = pl.program_id(2)
    rows = NSA_REP * Q_BLOCK
    ncp = seq // CMP_STRIDE
    q = q_ref[...].reshape(rows, LANES)
    t_row = qb * Q_BLOCK + lax.broadcasted_iota(jnp.int32, (rows, 1), 0) % Q_BLOCK

    cmp_end = lax.broadcasted_iota(jnp.int32, (1, ncp), 1) * CMP_STRIDE + (CMP_LEN - 1)
    p_cmp = _softmax2_rows(_dot_nt(q, kc_ref[...]), cmp_end <= t_row)
    p_cmp = jnp.where(t_row >= CMP_LEN - 1, p_cmp, 0.0)
    o_cmp = _dot(p_cmp.astype(BF16), vc_ref[...])

    p_grp = (p_cmp[0:Q_BLOCK] + p_cmp[Q_BLOCK:2 * Q_BLOCK]
             + p_cmp[2 * Q_BLOCK:3 * Q_BLOCK] + p_cmp[3 * Q_BLOCK:4 * Q_BLOCK])
    imp_t = _dot_nt(ovl_ref[...], p_grp, HIGHEST)
    bias = _selection_bias(imp_t, qb).T.astype(BF16)

    q_aug = jnp.concatenate([q, jnp.concatenate([bias] * NSA_REP, axis=0)], axis=1)

    span = WINDOW + Q_BLOCK
    start = pl.multiple_of(jnp.maximum(qb * Q_BLOCK - WINDOW, 0), Q_BLOCK)
    kpos = start + lax.broadcasted_iota(jnp.int32, (1, span), 1)
    s = jnp.where((kpos <= t_row) & (kpos > t_row - WINDOW), _dot_nt(q, kb_ref[pl.ds(start, span), :]), NEG_INF)
    e = jnp.exp2(s - jnp.max(s, axis=-1, keepdims=True)).astype(BF16)
    acc_win = _dot(e, vb_ref[pl.ds(start, span), :])
    o_win = acc_win * (1.0 / acc_win[:, HEAD_DIM:HEAD_DIM + 1])

    n_key_tiles = seq // KEY_TILE

    def score_tile(kt, s_ref):
        k0 = pl.multiple_of(jnp.minimum(kt, n_key_tiles - 1) * KEY_TILE, KEY_TILE)
        s_ref[...] = _dot_nt(q_aug, ka_ref[pl.ds(k0, KEY_TILE), :])

    def absorb_tile(kt, s_ref, carry, causal):
        m_i, acc = carry
        k0 = pl.multiple_of(jnp.minimum(kt, n_key_tiles - 1) * KEY_TILE, KEY_TILE)
        s = s_ref[...]
        if causal:
            kpos = kt * KEY_TILE + lax.broadcasted_iota(jnp.int32, (1, KEY_TILE), 1)
            s = jnp.where(kpos <= t_row, s, MASK_VALUE)
        m_new = jnp.maximum(m_i, jnp.max(s, axis=-1, keepdims=True))
        p = jnp.exp2(s - m_new).astype(BF16)
        return m_new, jnp.exp2(m_i - m_new) * acc + _dot(p, va_ref[pl.ds(k0, KEY_TILE), :])

    def slc_trip(j, carry):
        score_tile(2 * j + 1, sb_ref)
        carry = absorb_tile(2 * j, sa_ref, carry, causal=False)
        score_tile(2 * j + 2, sa_ref)
        return absorb_tile(2 * j + 1, sb_ref, carry, causal=False)

    last_pair = qb // (2 * KEY_TILE // Q_BLOCK)
    score_tile(0, sa_ref)
    carry = lax.fori_loop(0, last_pair, slc_trip,
                          (jnp.full((rows, 1), NEG_INF, F32), jnp.zeros((rows, LANES), F32)))
    score_tile(2 * last_pair + 1, sb_ref)
    carry = absorb_tile(2 * last_pair, sa_ref, carry, causal=True)
    _, acc_slc = absorb_tile(2 * last_pair + 1, sb_ref, carry, causal=True)
    o_slc = acc_slc * (1.0 / acc_slc[:, HEAD_DIM:HEAD_DIM + 1])

    gates = gate_ref[...]
    low_half = lax.broadcasted_iota(jnp.int32, (Q_BLOCK, LANES), 1) < HEAD_DIM
    heads = []
    for r in range(NSA_REP):
        rs = slice(r * Q_BLOCK, (r + 1) * Q_BLOCK)
        c = r * N_BRANCH
        heads.append(gates[:, c:c + 1] * o_cmp[rs] + gates[:, c + 1:c + 2] * o_slc[rs]
                     + gates[:, c + 2:c + 3] * o_win[rs])
    for pair in range(NSA_REP // 2):
        o_ref[:, pair * LANES:(pair + 1) * LANES] = jnp.where(
            low_half, heads[2 * pair], pltpu.roll(heads[2 * pair + 1], HEAD_DIM, 1))


def _overlap_matrix(seq):
    ncp, nsb = seq // CMP_STRIDE, seq // SLC_BLOCK
    c_start = np.arange(ncp)[None, :] * CMP_STRIDE
    s_start = np.arange(nsb)[:, None] * SLC_BLOCK
    ovl = np.clip(np.minimum(c_start + CMP_LEN, s_start + SLC_BLOCK) - np.maximum(c_start, s_start), 0, None)
    ovl = ovl.astype(np.float32) / CMP_LEN
    ovl[:, ncp - 1] = 0.0
    return jnp.asarray(np.pad(ovl, ((0, LANES - nsb), (0, 0))))


def _nsa_attention(q, k_cmp, v_cmp, ka, kb, va, vb, gates):
    b, _, s, _ = q.shape
    g = NSA_KV_HEADS
    ncp = s // CMP_STRIDE
    assert s // SLC_BLOCK <= LANES
    full = lambda n, width=LANES: pl.BlockSpec((None, None, n, width), lambda i, j, k: (i, j, 0, 0))
    return pl.pallas_call(
        functools.partial(_attn_kernel, seq=s),
        grid=(b, g, s // Q_BLOCK),
        in_specs=[pl.BlockSpec((None, NSA_REP, Q_BLOCK, LANES), lambda i, j, k: (i, j, k, 0)),
                  full(ncp), full(ncp), full(s, 2 * LANES), full(s), full(s), full(s),
                  pl.BlockSpec((None, None, Q_BLOCK, LANES), lambda i, j, k: (i, j, k, 0)),
                  _const_spec((LANES, ncp))],
        out_specs=pl.BlockSpec((None, Q_BLOCK, NSA_REP * HEAD_DIM), lambda i, j, k: (i, k, j)),
        out_shape=jax.ShapeDtypeStruct((b, s, NSA_WIDTH), F32),
        scratch_shapes=[pltpu.VMEM((NSA_REP * Q_BLOCK, KEY_TILE), F32)] * 2,
        compiler_params=_params(("parallel", "parallel", "arbitrary")),
        name="nsa_attention",
    )(q, k_cmp, v_cmp, ka, kb, va, vb, gates, _overlap_matrix(s))


def _s5_kernel(u_ref, mt_ref, bc_ref, cc_ref, are_ref, aim_ref, d_ref, y_ref, h_ref, g_scr, hp_scr):
    @pl.when(pl.program_id(1) == 0)
    def _():
        h_ref[...] = jnp.zeros_like(h_ref)

    u = jnp.concatenate([_fold_rows(u_ref.at[half], S5_CHUNK) for half in range(S5_HALVES)], axis=1)
    ub = u.astype(BF16)
    y_local = _dot(ub, mt_ref[...])
    g_scr[...] = _dot(ub, bc_ref[...])
    a_re, a_im = are_ref[...], aim_ref[...]

    def step(i, carry):
        h_re, h_im = carry
        hp_scr[pl.ds(i, 1), 0:S5_STATE] = h_re
        hp_scr[pl.ds(i, 1), S5_STATE:2 * S5_STATE] = h_im
        g_re = g_scr[pl.ds(i, 1), 0:S5_STATE]
        g_im = g_scr[pl.ds(i, 1), S5_STATE:2 * S5_STATE]
        return a_re * h_re - a_im * h_im + g_re, a_re * h_im + a_im * h_re + g_im

    h_re, h_im = lax.fori_loop(0, u.shape[0], step, (h_ref[0:1, :], h_ref[1:2, :]))
    h_ref[0:1, :] = h_re
    h_ref[1:2, :] = h_im
    y = y_local + _dot(hp_scr[...].astype(BF16), cc_ref[...]) + u * d_ref[...]
    half_fold = S5_FOLD // S5_HALVES
    for half in range(S5_HALVES):
        _unfold_rows(y_ref.at[half], y[:, half * half_fold:(half + 1) * half_fold], S5_CHUNK)


def _s5_scan(u, mats):
    b, _, s, _ = u.shape
    rows = s // S5_CHUNK
    tc = min(128, rows)
    mt, bc, cc, a_re, a_im, d_vec = mats
    tile = pl.BlockSpec((None, S5_HALVES, tc * S5_CHUNK, LANES), lambda i, j: (i, 0, j, 0))
    mat = _const_spec((S5_FOLD, S5_FOLD))
    return pl.pallas_call(
        _s5_kernel,
        grid=(b, rows // tc),
        in_specs=[tile, mat, mat, mat, _const_spec((1, S5_STATE)), _const_spec((1, S5_STATE)),
                  _const_spec((1, S5_FOLD))],
        out_specs=tile,
        out_shape=jax.ShapeDtypeStruct((b, S5_HALVES, s, LANES), F32),
        scratch_shapes=[pltpu.VMEM((2, S5_STATE), F32), pltpu.VMEM((tc, S5_FOLD), F32),
                        pltpu.VMEM((tc, S5_FOLD), F32)],
        compiler_params=_params(("parallel", "arbitrary")),
        name="s5_scan",
    )(u, mt, bc, cc, a_re, a_im, d_vec)


def _s5_matrices(lam_re, lam_im, log_dt, b_re, b_im, c_re, c_im, d_skip):
    t0, ng, nh, npm = S5_CHUNK, S5_GROUPS, S5_H, S5_P
    ein = functools.partial(jnp.einsum, precision=HIGHEST)
    lam = lax.complex(lam_re, lam_im)
    step = jnp.exp(log_dt)[:, None]
    lam_bar = jnp.exp(lam * step)
    b_bar = lax.complex(b_re, b_im) * ((lam_bar - 1.0) / lam)[..., None]
    c_mat = lax.complex(c_re, c_im)
    k = jnp.arange(t0 + 1, dtype=F32)[:, None, None]
    pw = jnp.exp((lam * step)[None] * k)
    eye = jnp.eye(ng, dtype=F32)
    kern = jnp.real(ein('ghp,kgp,gpq->kghq', c_mat, pw[:t0], b_bar))
    lag = jnp.arange(t0)[None, :] - jnp.arange(t0)[:, None]
    k_ji = jnp.where((lag >= 0)[:, :, None, None, None], kern[jnp.clip(lag, 0, t0 - 1)], 0.0)
    mt = ein('jighq,gf->jgqifh', k_ji, eye).reshape(t0 * ng * nh, t0 * ng * nh)
    b_j = pw[t0 - 1 - jnp.arange(t0)][..., None] * b_bar[None]
    bc_re = ein('jgpq,gf->jgqfp', jnp.real(b_j), eye).reshape(t0 * ng * nh, ng * npm)
    bc_im = ein('jgpq,gf->jgqfp', jnp.imag(b_j), eye).reshape(t0 * ng * nh, ng * npm)
    bc = jnp.concatenate([bc_re, bc_im], axis=1)
    c_i = c_mat[None] * pw[1:t0 + 1][:, :, None, :]
    cc_re = ein('ighp,gf->gpifh', jnp.real(c_i), eye).reshape(ng * npm, t0 * ng * nh)
    cc_im = ein('ighp,gf->gpifh', -jnp.imag(c_i), eye).reshape(ng * npm, t0 * ng * nh)
    cc = jnp.concatenate([cc_re, cc_im], axis=0)
    a_chunk = pw[t0].reshape(1, ng * npm)
    d_vec = jnp.tile(d_skip.reshape(1, ng * nh), (1, t0))

    def half_major(x, axis):
        shape = x.shape
        x = x.reshape(shape[:axis] + (t0, S5_HALVES, LANES) + shape[axis + 1:])
        return jnp.swapaxes(x, axis, axis + 1).reshape(shape)

    mt = half_major(half_major(mt.astype(BF16), 0), 1)
    return (mt, half_major(bc.astype(BF16), 0), half_major(cc.astype(BF16), 1),
            jnp.real(a_chunk), jnp.imag(a_chunk), half_major(d_vec, 1))


def _rms_gain(y, gain):
    return y * lax.rsqrt(jnp.mean(y * y, axis=-1, keepdims=True) + EPS) * gain


def _outproj_kernel(x_ref, mod_ref, pool_ref, halo_ref, nsa_ref, s5_ref, pw_ref, pb_ref, ps_ref, on_ref,
                    gw_ref, gb_ref, wo_ref, o_ref, buf):
    j = pl.program_id(1)
    tm = x_ref.shape[0]
    v = pool_ref[...]
    buf[0:POOL_HALO, :] = jnp.where(j > 0, halo_ref[...], 0.0)
    buf[POOL_HALO:POOL_HALO + tm, :] = v
    lane_group = lax.broadcasted_iota(jnp.int32, (tm, POOL_WIDTH), 1) >> SLC_SHIFT
    t1 = (j * tm + 1 + lax.broadcasted_iota(jnp.int32, (tm, 1), 0)).astype(F32)
    run, k, pooled = v, 1, jnp.zeros_like(v)
    for gi, w in enumerate(POOL_WINDOWS):
        while k < w:
            run = run + buf[POOL_HALO - k:POOL_HALO - k + tm, :]
            k += 1
        pooled = jnp.where(lane_group == gi, run / jnp.minimum(t1, float(w)) - v, pooled)
    y_pool = (_dot(pooled.astype(BF16), pw_ref[...]) + pb_ref[...]) * ps_ref[...]

    y = _gelu_tanh(jnp.concatenate([s5_ref[half] for half in range(S5_HALVES)], axis=-1))
    y_s5 = y * _sigmoid(_dot(y.astype(BF16), gw_ref[...]) + gb_ref[...])

    cat = jnp.concatenate(
        [_rms_gain(y_pool, on_ref[:, 0:POOL_WIDTH]),
         _rms_gain(nsa_ref[...], on_ref[:, POOL_WIDTH:POOL_WIDTH + NSA_WIDTH]),
         _rms_gain(y_s5, on_ref[:, POOL_WIDTH + NSA_WIDTH:])], axis=-1).astype(BF16)
    o_ref[...] = x_ref[...] + mod_ref[5:6, :] * _dot(cat, wo_ref[...])


def _outproj(x, mod, u_pool, o_nsa, y_s5, pool_w_bd, pool_b, pool_scale, out_norm, glu_w, glu_b, w_out, tm=256):
    b, s, d = x.shape
    tok = lambda width: pl.BlockSpec((None, tm, width), lambda i, j: (i, j, 0))
    halo_blocks = tm // POOL_HALO
    return pl.pallas_call(
        _outproj_kernel,
        grid=(b, s // tm),
        in_specs=[tok(d),
                  pl.BlockSpec((None, N_MOD, d), lambda i, j: (i, 0, 0)),
                  tok(POOL_WIDTH),
                  pl.BlockSpec((None, POOL_HALO, POOL_WIDTH),
                               lambda i, j: (i, jnp.maximum(j * halo_blocks - 1, 0), 0)),
                  tok(NSA_WIDTH),
                  pl.BlockSpec((None, S5_HALVES, tm, LANES), lambda i, j: (i, 0, j, 0)),
                  _const_spec((POOL_WIDTH, POOL_WIDTH)), _const_spec((1, POOL_WIDTH)),
                  _const_spec((1, POOL_WIDTH)), _const_spec((1, d)),
                  _const_spec((S5_WIDTH, S5_WIDTH)), _const_spec((1, S5_WIDTH)),
                  _const_spec((d, d))],
        out_specs=tok(d),
        out_shape=jax.ShapeDtypeStruct((b, s, d), F32),
        scratch_shapes=[pltpu.VMEM((POOL_HALO + tm, POOL_WIDTH), F32)],
        compiler_params=_params(("parallel", "arbitrary")),
        name="mixer_out_proj",
    )(x, mod, u_pool, u_pool, o_nsa, y_s5, pool_w_bd, pool_b, pool_scale, out_norm, glu_w, glu_b, w_out)


def _rope_tables(pos):
    half = ROT_DIM // 2
    inv_freq = jnp.exp(-math.log(ROPE_THETA) * jnp.arange(half, dtype=F32) * (2.0 / ROT_DIM))
    ang = pos[:, None] * inv_freq[None, :]
    cos, sin = jnp.cos(ang), jnp.sin(ang)
    n = pos.shape[0]
    rest = HEAD_DIM - ROT_DIM
    cos_t = jnp.concatenate([cos, cos, jnp.ones((n, rest), F32)], axis=1)
    sin_lo = jnp.concatenate([-sin, jnp.zeros((n, half + rest), F32)], axis=1)
    sin_hi = jnp.concatenate([jnp.zeros((n, half), F32), sin, jnp.zeros((n, rest), F32)], axis=1)
    return tuple(jnp.tile(t, (1, 2)) for t in (cos_t, sin_lo, sin_hi))


def _pad_lanes(v):
    return jnp.pad(v, [(0, 0)] * (v.ndim - 1) + [(0, LANES - v.shape[-1])])


def _arrange_w_in(w):
    d = w.shape[0]
    o1, o2, o3 = POOL_WIDTH, POOL_WIDTH + NSA_WIDTH, POOL_WIDTH + NSA_WIDTH + 6 * LANES
    q = _pad_lanes(w[:, o1:o2].reshape(d, NSA_HEADS, HEAD_DIM)).reshape(d, NSA_HEADS * LANES)
    kv = w[:, o2:o3].reshape(d, 6, NSA_KV_HEADS, HEAD_DIM)
    ka = jnp.concatenate([kv[:, 2], kv[:, 4]], axis=-1).reshape(d, NSA_KV_HEADS * LANES)
    va = jnp.concatenate([kv[:, 3], kv[:, 5]], axis=-1).reshape(d, NSA_KV_HEADS * LANES)
    kc = kv[:, 0].reshape(d, LANES)
    vc = kv[:, 1].reshape(d, LANES)
    n_gate = NSA_REP * N_BRANCH
    gate = _pad_lanes(w[:, o3:o3 + NSA_KV_HEADS * n_gate].reshape(d, NSA_KV_HEADS, n_gate))
    s5 = w[:, o3 + NSA_KV_HEADS * n_gate:]
    return jnp.concatenate([w[:, :o1], q, ka, va, kc, vc, gate.reshape(d, NSA_KV_HEADS * LANES), s5],
                           axis=1).astype(BF16)


def _expand_cmp_w1(w1):
    halves = w1.reshape(2, CMP_STRIDE, HEAD_DIM, CMP_HIDDEN)
    both = jnp.concatenate([halves[0], halves[1]], axis=-1)
    out = []
    for g in range(NSA_KV_HEADS):
        z = jnp.zeros((CMP_STRIDE, NSA_KV_HEADS, HEAD_DIM, 2 * CMP_HIDDEN), F32).at[:, g].set(both)
        out.append(z.reshape(CMP_STRIDE * LANES, 2 * CMP_HIDDEN))
    return jnp.stack(out).astype(BF16)


def _block_diag(w):
    g, c, _ = w.shape
    return jnp.einsum('gcd,gf->gcfd', w, jnp.eye(g, dtype=w.dtype)).reshape(g * c, g * c)


def _hybrid_layer(x, mod, tabs, cmp_tabs, norm_ffn1, ffn1_w_in, ffn1_w_out, norm_mix, w_in, w_out, out_norm,
                  pool_w, pool_b, pool_scale, q_norm, k_norm, cmp_pe, cmp_k_w1, cmp_k_w2, cmp_v_w1, cmp_v_w2,
                  s5_lam_re, s5_lam_im, s5_log_dt, s5_b_re, s5_b_im, s5_c_re, s5_c_im, s5_d, glu_w, glu_b,
                  norm_ffn2, ffn2_w_in, ffn2_w_out):
    x = _ffn(x, mod, norm_ffn1, ffn1_w_in.astype(BF16), ffn1_w_out.astype(BF16), 0)

    q_gain = _pad_lanes(q_norm.reshape(1, HEAD_DIM))
    slc_win_gain = jnp.concatenate([k_norm[1], k_norm[2]]).reshape(1, LANES)
    u_pool, q, ka, kb, va, vb, kc, vc, gates, u_s5 = _inproj(
        x, mod, norm_mix, _arrange_w_in(w_in), tabs, q_gain, slc_win_gain)

    k_cmp, v_cmp = _compress(
        kc, vc, _expand_cmp_w1(cmp_k_w1), _expand_cmp_w1(cmp_v_w1), cmp_k_w1, cmp_v_w1,
        cmp_pe.reshape(2, CMP_LEN * HEAD_DIM), _pad_lanes(cmp_k_w2).astype(BF16),
        _pad_lanes(cmp_v_w2).astype(BF16), _pad_lanes(k_norm[0].reshape(1, HEAD_DIM)), cmp_tabs)
    o_nsa = _nsa_attention(q, k_cmp, v_cmp, ka, kb, va, vb, gates)

    y_s5 = _s5_scan(u_s5, _s5_matrices(s5_lam_re, s5_lam_im, s5_log_dt, s5_b_re, s5_b_im,
                                       s5_c_re, s5_c_im, s5_d))

    x = _outproj(x, mod, u_pool, o_nsa, y_s5, _block_diag(pool_w).astype(BF16),
                 pool_b.reshape(1, POOL_WIDTH), pool_scale.reshape(1, POOL_WIDTH),
                 out_norm.reshape(1, D_MODEL), glu_w.astype(BF16), glu_b.reshape(1, S5_WIDTH),
                 w_out.astype(BF16))
    return _ffn(x, mod, norm_ffn2, ffn2_w_in.astype(BF16), ffn2_w_out.astype(BF16), 6)


def kernel(x, c, ada_w, ada_b, norm_ffn1, ffn1_w_in, ffn1_w_out, norm_mix, w_in, w_out, out_norm, pool_w, pool_b, pool_scale, q_norm, k_norm, cmp_pe, cmp_k_w1, cmp_k_w2, cmp_v_w1, cmp_v_w2, s5_lam_re, s5_lam_im, s5_log_dt, s5_b_re, s5_b_im, s5_c_re, s5_c_im, s5_d, glu_w, glu_b, norm_ffn2, ffn2_w_in, ffn2_w_out):
    seq = x.shape[1]
    assert seq % (KEY_TILE * 4) == 0 and seq >= WINDOW + Q_BLOCK
    mod = _modulation(c, ada_w, ada_b)
    tabs = _rope_tables(jnp.arange(seq, dtype=F32))
    n_cmp_pad = seq // CMP_STRIDE
    cmp_tabs = _rope_tables((jnp.arange(n_cmp_pad) * CMP_STRIDE + CMP_LEN - 1).astype(F32))
    per_layer = (norm_ffn1, ffn1_w_in, ffn1_w_out, norm_mix, w_in, w_out, out_norm, pool_w, pool_b, pool_scale,
                 q_norm, k_norm, cmp_pe, cmp_k_w1, cmp_k_w2, cmp_v_w1, cmp_v_w2, s5_lam_re, s5_lam_im,
                 s5_log_dt, s5_b_re, s5_b_im, s5_c_re, s5_c_im, s5_d, glu_w, glu_b, norm_ffn2, ffn2_w_in,
                 ffn2_w_out)
    for l in range(ada_w.shape[0]):
        x = _hybrid_layer(x, mod[l], tabs, cmp_tabs, *[p[l] for p in per_layer])
    return x
```

```python
import functools
import math

import jax
import jax.numpy as jnp
import numpy as np
from jax import lax
from jax.experimental import pallas as pl
from jax.experimental.pallas import tpu as pltpu

F32 = jnp.float32
BF16 = jnp.bfloat16
HIGHEST = lax.Precision.HIGHEST

LANES = 128
SUBLANES = 8
VMEM_LIMIT_BYTES = 56 * 1024 * 1024

D_MODEL = 1024
D_FF = 2816
N_MOD = 9
POOL_WIDTH = 256
POOL_GC = 64
POOL_WINDOWS = (2, 4, 8, 16)
POOL_HALO = 16
HEAD_DIM = 64
NSA_WIDTH = 512
NSA_HEADS = 8
NSA_KV_HEADS = 2
NSA_REP = 4
N_BRANCH = 3
S5_WIDTH = 256
S5_H = 16
S5_GROUPS = 16
S5_P = 64
CMP_STRIDE = 16
CMP_LEN = 32
CMP_HIDDEN = 128
SLC_BLOCK = 64
SLC_SHIFT = 6
N_SELECT = 16
N_FORCED = 3
WINDOW = 512
Q_BLOCK = 128
ROT_DIM = 16
ROPE_THETA = 500000.0
EPS = 1e-6
Q_SCALE = HEAD_DIM ** -0.5 * math.log2(math.e)
NEG_INF = -1e30
MASK_VALUE = -(2.0 ** 100)

OFF_POOL = 0
OFF_Q = OFF_POOL + POOL_WIDTH
OFF_KA = OFF_Q + NSA_HEADS * LANES
OFF_VA = OFF_KA + NSA_KV_HEADS * LANES
OFF_KC = OFF_VA + NSA_KV_HEADS * LANES
OFF_VC = OFF_KC + LANES
OFF_GATE = OFF_VC + LANES
OFF_S5 = OFF_GATE + NSA_KV_HEADS * LANES
N_COLS = OFF_S5 + S5_WIDTH

S5_CHUNK = 8
S5_FOLD = S5_CHUNK * S5_WIDTH
S5_STATE = S5_GROUPS * S5_P
S5_HALVES = S5_WIDTH // LANES
S5_HALF_FOLD = S5_FOLD // S5_HALVES
KEY_TILE = 512


def _dot(a, b, precision=None):
    return jnp.dot(a, b, preferred_element_type=F32, precision=precision)


def _dot_nt(a, b, precision=None):
    return lax.dot_general(a, b, (((1,), (1,)), ((), ())), preferred_element_type=F32,
                           precision=precision)


def _sigmoid(x):
    return 1.0 / (1.0 + jnp.exp(-x))


def _gelu_tanh(x):
    return 0.5 * x * (1.0 + jnp.tanh(math.sqrt(2.0 / math.pi) * (x + 0.044715 * (x * x * x))))


def _params(sem):
    return pltpu.CompilerParams(dimension_semantics=sem, vmem_limit_bytes=VMEM_LIMIT_BYTES)


def _const_spec(shape):
    nd = len(shape)
    return pl.BlockSpec(shape, lambda *_: (0,) * nd, pipeline_mode=pl.Buffered(1))


def _mod_kernel(c_ref, w_ref, b_ref, o_ref):
    c = c_ref[...]
    o_ref[...] = _dot(c * _sigmoid(c), w_ref[...], HIGHEST) + b_ref[...]


def _modulation(c, ada_w, ada_b):
    n_layers, d, n = ada_w.shape
    b = c.shape[0]
    tn = 1152
    c_pad = jnp.zeros((SUBLANES, d), F32).at[:b].set(c)
    out = pl.pallas_call(
        _mod_kernel,
        grid=(n_layers, n // tn),
        in_specs=[pl.BlockSpec((SUBLANES, d), lambda l, j: (0, 0)),
                  pl.BlockSpec((None, d, tn), lambda l, j: (l, 0, j)),
                  pl.BlockSpec((None, 1, tn), lambda l, j: (l, 0, j))],
        out_specs=pl.BlockSpec((None, SUBLANES, tn), lambda l, j: (l, 0, j)),
        out_shape=jax.ShapeDtypeStruct((n_layers, SUBLANES, n), F32),
        compiler_params=_params(("arbitrary", "arbitrary")),
        name="adaln_mod",
    )(c_pad, ada_w, ada_b.reshape(n_layers, 1, n))
    return out[:, :b].reshape(n_layers, b, N_MOD, d)


def _norm_modulate(x, gain, mod_ref, first_row):
    ms = jnp.mean(x * x, axis=-1, keepdims=True)
    y = x * lax.rsqrt(ms + EPS) * gain
    return y * (1.0 + mod_ref[first_row + 1:first_row + 2, :]) + mod_ref[first_row:first_row + 1, :]


def _ffn_kernel(x_ref, mod_ref, g_ref, win_ref, wout_ref, o_ref, *, first_row):
    x = x_ref[...]
    h = _norm_modulate(x, g_ref[...], mod_ref, first_row).astype(BF16)
    gu = _dot(h, win_ref[...])
    gate, up = gu[:, :D_FF], gu[:, D_FF:]
    a = (gate * _sigmoid(gate) * up).astype(BF16)
    y = _dot(a, wout_ref[...])
    o_ref[...] = x + 0.5 * mod_ref[first_row + 2:first_row + 3, :] * y


def _ffn(x, mod, gain, w_in, w_out, first_row, tm=256):
    b, s, d = x.shape
    return pl.pallas_call(
        functools.partial(_ffn_kernel, first_row=first_row),
        grid=(b, s // tm),
        in_specs=[pl.BlockSpec((None, tm, d), lambda i, j: (i, j, 0)),
                  pl.BlockSpec((None, N_MOD, d), lambda i, j: (i, 0, 0)),
                  _const_spec((1, d)),
                  _const_spec((d, 2 * D_FF)),
                  _const_spec((D_FF, d))],
        out_specs=pl.BlockSpec((None, tm, d), lambda i, j: (i, j, 0)),
        out_shape=jax.ShapeDtypeStruct((b, s, d), F32),
        compiler_params=_params(("parallel", "parallel")),
        name="ffn_half_step",
    )(x, mod, gain.reshape(1, d), w_in, w_out)


def _rope(v, cos_t, sin_lo, sin_hi):
    return (v * cos_t + pltpu.roll(v, LANES - ROT_DIM // 2, 1) * sin_lo
            + pltpu.roll(v, ROT_DIM // 2, 1) * sin_hi)


def _inproj_kernel(x_ref, mod_ref, g_ref, w_ref, cos_ref, slo_ref, shi_ref, qg_ref, kg_ref,
                   pool_ref, q_ref, ka_ref, kb_ref, va_ref, vb_ref, kc_ref, vc_ref, gate_ref, s5_ref):
    h = _norm_modulate(x_ref[...], g_ref[...], mod_ref, 3).astype(BF16)
    u = _dot(h, w_ref[...])
    tm = u.shape[0]
    cos_t, sin_lo, sin_hi = cos_ref[...], slo_ref[...], shi_ref[...]
    lane = lax.broadcasted_iota(jnp.int32, (tm, LANES), 1)
    low_half = lane < HEAD_DIM
    pos = pl.program_id(1) * tm + lax.broadcasted_iota(jnp.int32, (tm, LANES), 0)
    block_one_hot = jnp.where((pos >> SLC_SHIFT) == lane, 1.0, 0.0).astype(BF16)

    pool_ref[...] = u[:, OFF_POOL:OFF_POOL + POOL_WIDTH]
    for hd in range(NSA_HEADS):
        v = u[:, OFF_Q + hd * LANES:OFF_Q + (hd + 1) * LANES]
        ms = jnp.sum(v * v, axis=-1, keepdims=True) * (1.0 / HEAD_DIM)
        vn = v * lax.rsqrt(ms + EPS) * qg_ref[...]
        q_ref[hd] = (_rope(vn, cos_t, sin_lo, sin_hi) * Q_SCALE).astype(BF16)
    for g in range(NSA_KV_HEADS):
        v = u[:, OFF_KA + g * LANES:OFF_KA + (g + 1) * LANES]
        sq = v * v
        ms_lo = jnp.sum(jnp.where(low_half, sq, 0.0), axis=-1, keepdims=True) * (1.0 / HEAD_DIM)
        ms_hi = jnp.sum(jnp.where(low_half, 0.0, sq), axis=-1, keepdims=True) * (1.0 / HEAD_DIM)
        r = jnp.where(low_half, lax.rsqrt(ms_lo + EPS), lax.rsqrt(ms_hi + EPS))
        kn = _rope(v * r * kg_ref[...], cos_t, sin_lo, sin_hi)
        ka_ref[g, :, 0:LANES] = kn.astype(BF16)
        ka_ref[g, :, LANES:2 * LANES] = block_one_hot
        kb_ref[g] = pltpu.roll(kn, HEAD_DIM, 1).astype(BF16)
        vv = u[:, OFF_VA + g * LANES:OFF_VA + (g + 1) * LANES]
        va_ref[g] = jnp.where(low_half, vv, 1.0).astype(BF16)
        vb_ref[g] = jnp.where(low_half, pltpu.roll(vv, HEAD_DIM, 1), 1.0).astype(BF16)
        gate_ref[g] = _sigmoid(u[:, OFF_GATE + g * LANES:OFF_GATE + (g + 1) * LANES])
    kc_ref[...] = u[:, OFF_KC:OFF_KC + LANES]
    vc_ref[...] = u[:, OFF_VC:OFF_VC + LANES]
    for half in range(S5_HALVES):
        s5_ref[half] = u[:, OFF_S5 + half * LANES:OFF_S5 + (half + 1) * LANES]


def _inproj(x, mod, gain, w_mix, rope_tabs, q_gain, k_gain, tm=256):
    b, s, d = x.shape
    g = NSA_KV_HEADS
    tok = lambda width: pl.BlockSpec((None, tm, width), lambda i, j: (i, j, 0))
    grp = lambda n, width=LANES: pl.BlockSpec((None, n, tm, width), lambda i, j: (i, 0, j, 0))
    tab = pl.BlockSpec((tm, LANES), lambda i, j: (j, 0))
    sds = jax.ShapeDtypeStruct
    return pl.pallas_call(
        _inproj_kernel,
        grid=(b, s // tm),
        in_specs=[tok(d),
                  pl.BlockSpec((None, N_MOD, d), lambda i, j: (i, 0, 0)),
                  _const_spec((1, d)),
                  _const_spec((d, N_COLS)),
                  tab, tab, tab,
                  _const_spec((1, LANES)), _const_spec((1, LANES))],
        out_specs=[tok(POOL_WIDTH), grp(NSA_HEADS), grp(g, 2 * LANES), grp(g), grp(g), grp(g),
                   tok(LANES), tok(LANES), grp(g), grp(S5_HALVES)],
        out_shape=[sds((b, s, POOL_WIDTH), F32), sds((b, NSA_HEADS, s, LANES), BF16),
                   sds((b, g, s, 2 * LANES), BF16), sds((b, g, s, LANES), BF16),
                   sds((b, g, s, LANES), BF16), sds((b, g, s, LANES), BF16),
                   sds((b, s, LANES), F32), sds((b, s, LANES), F32),
                   sds((b, g, s, LANES), F32), sds((b, S5_HALVES, s, LANES), F32)],
        compiler_params=_params(("parallel", "parallel")),
        name="mixer_in_proj",
    )(x, mod, gain.reshape(1, d), w_mix, *rope_tabs, q_gain, k_gain)


def _fold_rows(ref, n):
    rows = ref.shape[0] // n
    return jnp.concatenate([ref[pl.ds(k, rows, stride=n), :] for k in range(n)], axis=1)


def _unfold_rows(ref, value, n):
    rows = ref.shape[0] // n
    for k in range(n):
        ref[pl.ds(k, rows, stride=n), :] = value[:, k * LANES:(k + 1) * LANES]


def _compress_kernel(kc_ref, vc_ref, w1k_ref, w1v_ref, w1k_raw_ref, w1v_raw_ref, pe_ref, w2k_ref, w2v_ref,
                     kg_ref, cos_ref, slo_ref, shi_ref, ko_ref, vo_ref, *, n_cmp):
    ncp = kc_ref.shape[0] // CMP_STRIDE
    real_row = lax.broadcasted_iota(jnp.int32, (ncp, LANES), 0) < n_cmp
    for src_ref, w1_ref, raw_ref, w2_ref, out_ref, pe_row, is_key in (
            (kc_ref, w1k_ref, w1k_raw_ref, w2k_ref, ko_ref, 0, True),
            (vc_ref, w1v_ref, w1v_raw_ref, w2v_ref, vo_ref, 1, False)):
        chunks = _fold_rows(src_ref, CMP_STRIDE).astype(BF16)
        pe = jnp.broadcast_to(pe_ref[pe_row:pe_row + 1, :], (SUBLANES, CMP_LEN * HEAD_DIM))
        pe_term = _dot(pe, raw_ref[...], HIGHEST)[0:1, :]
        for g in range(NSA_KV_HEADS):
            a = _dot(chunks, w1_ref[g])
            pre = a[:, :CMP_HIDDEN] + pltpu.roll(a[:, CMP_HIDDEN:], ncp - 1, 0) + pe_term
            out = _dot(_gelu_tanh(pre).astype(BF16), w2_ref[...])
            if is_key:
                ms = jnp.sum(out * out, axis=-1, keepdims=True) * (1.0 / HEAD_DIM)
                out = _rope(out * lax.rsqrt(ms + EPS) * kg_ref[...], cos_ref[...], slo_ref[...], shi_ref[...])
            out_ref[g] = jnp.where(real_row, out, 0.0).astype(BF16)


def _compress(kc, vc, w1k, w1v, w1k_raw, w1v_raw, pe, w2k, w2v, k_gain, cmp_tabs):
    b, s, _ = kc.shape
    ncp = s // CMP_STRIDE
    fold = CMP_STRIDE * LANES
    src = pl.BlockSpec((None, s, LANES), lambda i: (i, 0, 0))
    out = pl.BlockSpec((None, NSA_KV_HEADS, ncp, LANES), lambda i: (i, 0, 0, 0))
    raw = _const_spec((CMP_LEN * HEAD_DIM, CMP_HIDDEN))
    w1 = _const_spec((NSA_KV_HEADS, fold, 2 * CMP_HIDDEN))
    w2 = _const_spec((CMP_HIDDEN, LANES))
    tab = _const_spec((ncp, LANES))
    return pl.pallas_call(
        functools.partial(_compress_kernel, n_cmp=ncp - 1),
        grid=(b,),
        in_specs=[src, src, w1, w1, raw, raw, _const_spec((2, CMP_LEN * HEAD_DIM)), w2, w2,
                  _const_spec((1, LANES)), tab, tab, tab],
        out_specs=[out, out],
        out_shape=[jax.ShapeDtypeStruct((b, NSA_KV_HEADS, ncp, LANES), BF16)] * 2,
        compiler_params=_params(("parallel",)),
        name="nsa_compress",
    )(kc, vc, w1k, w1v, w1k_raw, w1v_raw, pe, w2k, w2v, k_gain, *cmp_tabs)


def _selection_bias(imp_t, qb):
    nsb, nq = imp_t.shape
    j = lax.broadcasted_iota(jnp.int32, (nsb, nq), 0)
    t = qb * Q_BLOCK + lax.broadcasted_iota(jnp.int32, (nsb, nq), 1)
    cur = t >> SLC_SHIFT
    valid = j * SLC_BLOCK <= t
    forced = (j == 0) | (j == cur) | (j == cur - 1)
    j_f = j.astype(F32)
    start = jnp.where(valid & jnp.logical_not(forced), imp_t, -1.0)
    vals = start
    for _ in range(N_SELECT - N_FORCED):
        m = jnp.max(vals, axis=0, keepdims=True)
        idx = jnp.min(jnp.where(vals == m, j_f, float(nsb)), axis=0, keepdims=True)
        vals = jnp.where(j_f == idx, -2.0, vals)
    return jnp.where((forced & valid) | (vals != start), 0.0, MASK_VALUE)


def _softmax2_rows(s, mask):
    s = jnp.where(mask, s, NEG_INF)
    e = jnp.exp2(s - jnp.max(s, axis=-1, keepdims=True))
    return e / jnp.sum(e, axis=-1, keepdims=True)


def _attn_kernel(q_ref, kc_ref, vc_ref, ka_ref, kb_ref, va_ref, vb_ref, gate_ref, ovl_ref, o_ref,
                 sa_ref, sb_ref, *, seq):
    qb = pl.program_id(2)
    rows = NSA_REP * Q_BLOCK
    ncp = seq // CMP_STRIDE
    q = q_ref[...].reshape(rows, LANES)
    t_row = qb * Q_BLOCK + lax.broadcasted_iota(jnp.int32, (rows, 1), 0) % Q_BLOCK

    cmp_end = lax.broadcasted_iota(jnp.int32, (1, ncp), 1) * CMP_STRIDE + (CMP_LEN - 1)
    p_cmp = _softmax2_rows(_dot_nt(q, kc_ref[...]), cmp_end <= t_row)
    p_cmp = jnp.where(t_row >= CMP_LEN - 1, p_cmp, 0.0)
    o_cmp = _dot(p_cmp.astype(BF16), vc_ref[...])

    p_grp = (p_cmp[0:Q_BLOCK] + p_cmp[Q_BLOCK:2 * Q_BLOCK]
             + p_cmp[2 * Q_BLOCK:3 * Q_BLOCK] + p_cmp[3 * Q_BLOCK:4 * Q_BLOCK])
    imp_t = _dot_nt(ovl_ref[...], p_grp, HIGHEST)
    bias = _selection_bias(imp_t, qb).T.astype(BF16)

    q_aug = jnp.concatenate([q, jnp.concatenate([bias] * NSA_REP, axis=0)], axis=1)

    span = WINDOW + Q_BLOCK
    start = pl.multiple_of(jnp.maximum(qb * Q_BLOCK - WINDOW, 0), Q_BLOCK)
    kpos = start + lax.broadcasted_iota(jnp.int32, (1, span), 1)
    s = jnp.where((kpos <= t_row) & (kpos > t_row - WINDOW), _dot_nt(q, kb_ref[pl.ds(start, span), :]), NEG_INF)
    e = jnp.exp2(s - jnp.max(s, axis=-1, keepdims=True)).astype(BF16)
    acc_win = _dot(e, vb_ref[pl.ds(start, span), :])
    o_win = acc_win * (1.0 / acc_win[:, HEAD_DIM:HEAD_DIM + 1])

    n_key_tiles = seq // KEY_TILE

    def score_tile(kt, s_ref):
        k0 = pl.multiple_of(jnp.minimum(kt, n_key_tiles - 1) * KEY_TILE, KEY_TILE)
        s_ref[...] = _dot_nt(q_aug, ka_ref[pl.ds(k0, KEY_TILE), :])

    def absorb_tile(kt, s_ref, carry, causal):
        m_i, acc = carry
        k0 = pl.multiple_of(jnp.minimum(kt, n_key_tiles - 1) * KEY_TILE, KEY_TILE)
        s = s_ref[...]
        if causal:
            kpos = kt * KEY_TILE + lax.broadcasted_iota(jnp.int32, (1, KEY_TILE), 1)
            s = jnp.where(kpos <= t_row, s, MASK_VALUE)
        m_new = jnp.maximum(m_i, jnp.max(s, axis=-1, keepdims=True))
        p = jnp.exp2(s - m_new).astype(BF16)
        return m_new, jnp.exp2(m_i - m_new) * acc + _dot(p, va_ref[pl.ds(k0, KEY_TILE), :])

    def slc_trip(j, carry):
        score_tile(2 * j + 1, sb_ref)
        carry = absorb_tile(2 * j, sa_ref, carry, causal=False)
        score_tile(2 * j + 2, sa_ref)
        return absorb_tile(2 * j + 1, sb_ref, carry, causal=False)

    last_pair = qb // (2 * KEY_TILE // Q_BLOCK)
    score_tile(0, sa_ref)
    carry = lax.fori_loop(0, last_pair, slc_trip,
                          (jnp.full((rows, 1), NEG_INF, F32), jnp.zeros((rows, LANES), F32)))
    score_tile(2 * last_pair + 1, sb_ref)
    carry = absorb_tile(2 * last_pair, sa_ref, carry, causal=True)
    _, acc_slc = absorb_tile(2 * last_pair + 1, sb_ref, carry, causal=True)
    o_slc = acc_slc * (1.0 / acc_slc[:, HEAD_DIM:HEAD_DIM + 1])

    gates = gate_ref[...]
    low_half = lax.broadcasted_iota(jnp.int32, (Q_BLOCK, LANES), 1) < HEAD_DIM
    heads = []
    for r in range(NSA_REP):
        rs = slice(r * Q_BLOCK, (r + 1) * Q_BLOCK)
        c = r * N_BRANCH
        heads.append(gates[:, c:c + 1] * o_cmp[rs] + gates[:, c + 1:c + 2] * o_slc[rs]
                     + gates[:, c + 2:c + 3] * o_win[rs])
    for pair in range(NSA_REP // 2):
        o_ref[:, pair * LANES:(pair + 1) * LANES] = jnp.where(
            low_half, heads[2 * pair], pltpu.roll(heads[2 * pair + 1], HEAD_DIM, 1))


def _overlap_matrix(seq):
    ncp, nsb = seq // CMP_STRIDE, seq // SLC_BLOCK
    c_start = np.arange(ncp)[None, :] * CMP_STRIDE
    s_start = np.arange(nsb)[:, None] * SLC_BLOCK
    ovl = np.clip(np.minimum(c_start + CMP_LEN, s_start + SLC_BLOCK) - np.maximum(c_start, s_start), 0, None)
    ovl = ovl.astype(np.float32) / CMP_LEN
    ovl[:, ncp - 1] = 0.0
    return jnp.asarray(np.pad(ovl, ((0, LANES - nsb), (0, 0))))


def _nsa_attention(q, k_cmp, v_cmp, ka, kb, va, vb, gates):
    b, _, s, _ = q.shape
    g = NSA_KV_HEADS
    ncp = s // CMP_STRIDE
    assert s // SLC_BLOCK <= LANES
    full = lambda n, width=LANES: pl.BlockSpec((None, None, n, width), lambda i, j, k: (i, j, 0, 0))
    return pl.pallas_call(
        functools.partial(_attn_kernel, seq=s),
        grid=(b, g, s // Q_BLOCK),
        in_specs=[pl.BlockSpec((None, NSA_REP, Q_BLOCK, LANES), lambda i, j, k: (i, j, k, 0)),
                  full(ncp), full(ncp), full(s, 2 * LANES), full(s), full(s), full(s),
                  pl.BlockSpec((None, None, Q_BLOCK, LANES), lambda i, j, k: (i, j, k, 0)),
                  _const_spec((LANES, ncp))],
        out_specs=pl.BlockSpec((None, Q_BLOCK, NSA_REP * HEAD_DIM), lambda i, j, k: (i, k, j)),
        out_shape=jax.ShapeDtypeStruct((b, s, NSA_WIDTH), F32),
        scratch_shapes=[pltpu.VMEM((NSA_REP * Q_BLOCK, KEY_TILE), F32)] * 2,
        compiler_params=_params(("parallel", "parallel", "arbitrary")),
        name="nsa_attention",
    )(q, k_cmp, v_cmp, ka, kb, va, vb, gates, _overlap_matrix(s))


def _s5_kernel(u_ref, mt_ref, bc_ref, cc_ref, a1_ref, a2_ref, d_ref, y_ref, h_ref, g_scr, hp_scr):
    @pl.when(pl.program_id(1) == 0)
    def _():
        h_ref[...] = jnp.zeros_like(h_ref)

    halves = range(S5_HALVES)
    u = [_fold_rows(u_ref.at[a], S5_CHUNK) for a in halves]
    ub = [x.astype(BF16) for x in u]
    y_local = [_dot(ub[a], mt_ref[a]) for a in halves]
    for a in halves:
        g_scr[:, a * S5_HALF_FOLD:(a + 1) * S5_HALF_FOLD] = _dot(ub[a], bc_ref[a])
    a1, a2 = a1_ref[...], a2_ref[...]
    n_state = S5_STATE // S5_HALVES

    def swap_re_im(h):
        parts = [h[:, k * n_state:(k + 1) * n_state] for k in range(2 * S5_HALVES)]
        return jnp.concatenate([parts[k ^ 1] for k in range(2 * S5_HALVES)], axis=1)

    def step(i, h):
        hp_scr[pl.ds(i, 1), :] = h
        return a1 * h + a2 * swap_re_im(h) + g_scr[pl.ds(i, 1), :]

    h_ref[...] = lax.fori_loop(0, g_scr.shape[0], step, h_ref[...])
    for a in halves:
        carried = _dot(hp_scr[:, a * S5_HALF_FOLD:(a + 1) * S5_HALF_FOLD].astype(BF16), cc_ref[a])
        _unfold_rows(y_ref.at[a], y_local[a] + carried + u[a] * d_ref[a:a + 1, :], S5_CHUNK)


def _s5_scan(u, mats):
    b, _, s, _ = u.shape
    rows = s // S5_CHUNK
    tc = min(128, rows)
    tile = pl.BlockSpec((None, S5_HALVES, tc * S5_CHUNK, LANES), lambda i, j: (i, 0, j, 0))
    mat = _const_spec((S5_HALVES, S5_HALF_FOLD, S5_HALF_FOLD))
    row = _const_spec((1, S5_HALVES * S5_HALF_FOLD))
    return pl.pallas_call(
        _s5_kernel,
        grid=(b, rows // tc),
        in_specs=[tile, mat, mat, mat, row, row, _const_spec((S5_HALVES, S5_HALF_FOLD))],
        out_specs=tile,
        out_shape=jax.ShapeDtypeStruct((b, S5_HALVES, s, LANES), F32),
        scratch_shapes=[pltpu.VMEM((1, S5_HALVES * S5_HALF_FOLD), F32),
                        pltpu.VMEM((tc, S5_HALVES * S5_HALF_FOLD), F32),
                        pltpu.VMEM((tc, S5_HALVES * S5_HALF_FOLD), F32)],
        compiler_params=_params(("parallel", "arbitrary")),
        name="s5_scan",
    )(u, *mats)


def _s5_matrices(lam_re, lam_im, log_dt, b_re, b_im, c_re, c_im, d_skip):
    t0, ng, nh, npm = S5_CHUNK, S5_GROUPS, S5_H, S5_P
    gh = ng // S5_HALVES
    ein = functools.partial(jnp.einsum, precision=HIGHEST)
    lam = lax.complex(lam_re, lam_im)
    step = jnp.exp(log_dt)[:, None]
    lam_bar = jnp.exp(lam * step)
    b_bar = lax.complex(b_re, b_im) * ((lam_bar - 1.0) / lam)[..., None]
    c_mat = lax.complex(c_re, c_im)
    k = jnp.arange(t0 + 1, dtype=F32)[:, None, None]
    pw = jnp.exp((lam * step)[None] * k)

    def same_group(rows_per_group, cols_per_group):
        r = np.arange(gh * rows_per_group)[:, None] // rows_per_group
        c = np.arange(gh * cols_per_group)[None, :] // cols_per_group
        return jnp.asarray((r == c).astype(np.float32))

    def per_group_blocks(x, rows_per_group, cols_per_group):
        return jnp.tile(x, (1,) * (x.ndim - 1) + (gh,)) * same_group(rows_per_group, cols_per_group)

    kern = jnp.real(ein('ghp,kgp,gpq->kghq', c_mat, pw[:t0], b_bar))
    d_k = per_group_blocks(kern.transpose(0, 1, 3, 2).reshape(t0, S5_HALVES, gh * nh, nh), nh, nh)
    d_k = jnp.concatenate([d_k, jnp.zeros_like(d_k[:1])], axis=0)
    lag = np.arange(t0)[None, :] - np.arange(t0)[:, None]
    mt = d_k[np.where(lag >= 0, lag, t0)]
    mt = mt.transpose(2, 0, 3, 1, 4).reshape(S5_HALVES, S5_HALF_FOLD, S5_HALF_FOLD)
    b_j = (pw[t0 - 1 - np.arange(t0)][..., None] * b_bar[None]).transpose(0, 1, 3, 2)
    b_j = b_j.reshape(t0, S5_HALVES, gh * nh, npm)
    bc = jnp.concatenate([per_group_blocks(jnp.real(b_j), nh, npm), per_group_blocks(jnp.imag(b_j), nh, npm)],
                         axis=-1)
    bc = bc.transpose(1, 0, 2, 3).reshape(S5_HALVES, S5_HALF_FOLD, S5_HALF_FOLD)
    c_i = (c_mat[None] * pw[1:t0 + 1][:, :, None, :]).transpose(1, 3, 0, 2)
    c_i = c_i.reshape(S5_HALVES, gh * npm, t0, nh)
    mask = same_group(npm, nh)[:, None, :]
    cc = jnp.concatenate([jnp.tile(jnp.real(c_i), (1, 1, 1, gh)) * mask,
                          jnp.tile(-jnp.imag(c_i), (1, 1, 1, gh)) * mask], axis=1)
    cc = cc.reshape(S5_HALVES, S5_HALF_FOLD, S5_HALF_FOLD)
    a_chunk = pw[t0].reshape(S5_HALVES, gh * npm)
    a1 = jnp.concatenate([jnp.real(a_chunk), jnp.real(a_chunk)], axis=1).reshape(1, -1)
    a2 = jnp.concatenate([-jnp.imag(a_chunk), jnp.imag(a_chunk)], axis=1).reshape(1, -1)
    d_vec = jnp.tile(d_skip.reshape(S5_HALVES, gh * nh), (1, t0))
    return mt.astype(BF16), bc.astype(BF16), cc.astype(BF16), a1, a2, d_vec


def _rms_gain(y, gain):
    return y * lax.rsqrt(jnp.mean(y * y, axis=-1, keepdims=True) + EPS) * gain


def _outproj_kernel(x_ref, mod_ref, pool_ref, halo_ref, nsa_ref, s5_ref, pw_ref, pb_ref, ps_ref, on_ref,
                    gw_ref, gb_ref, wo_ref, o_ref, buf):
    j = pl.program_id(1)
    tm = x_ref.shape[0]
    v = pool_ref[...]
    buf[0:POOL_HALO, :] = jnp.where(j > 0, halo_ref[...], 0.0)
    buf[POOL_HALO:POOL_HALO + tm, :] = v
    lane_group = lax.broadcasted_iota(jnp.int32, (tm, POOL_WIDTH), 1) >> SLC_SHIFT
    t1 = (j * tm + 1 + lax.broadcasted_iota(jnp.int32, (tm, 1), 0)).astype(F32)
    run, k, pooled = v, 1, jnp.zeros_like(v)
    for gi, w in enumerate(POOL_WINDOWS):
        while k < w:
            run = run + buf[POOL_HALO - k:POOL_HALO - k + tm, :]
            k += 1
        pooled = jnp.where(lane_group == gi, run / jnp.minimum(t1, float(w)) - v, pooled)
    y_pool = (_dot(pooled.astype(BF16), pw_ref[...]) + pb_ref[...]) * ps_ref[...]

    y = _gelu_tanh(jnp.concatenate([s5_ref[half] for half in range(S5_HALVES)], axis=-1))
    y_s5 = y * _sigmoid(_dot(y.astype(BF16), gw_ref[...]) + gb_ref[...])

    cat = jnp.concatenate(
        [_rms_gain(y_pool, on_ref[:, 0:POOL_WIDTH]),
         _rms_gain(nsa_ref[...], on_ref[:, POOL_WIDTH:POOL_WIDTH + NSA_WIDTH]),
         _rms_gain(y_s5, on_ref[:, POOL_WIDTH + NSA_WIDTH:])], axis=-1).astype(BF16)
    o_ref[...] = x_ref[...] + mod_ref[5:6, :] * _dot(cat, wo_ref[...])


def _outproj(x, mod, u_pool, o_nsa, y_s5, pool_w_bd, pool_b, pool_scale, out_norm, glu_w, glu_b, w_out, tm=256):
    b, s, d = x.shape
    tok = lambda width: pl.BlockSpec((None, tm, width), lambda i, j: (i, j, 0))
    halo_blocks = tm // POOL_HALO
    return pl.pallas_call(
        _outproj_kernel,
        grid=(b, s // tm),
        in_specs=[tok(d),
                  pl.BlockSpec((None, N_MOD, d), lambda i, j: (i, 0, 0)),
                  tok(POOL_WIDTH),
                  pl.BlockSpec((None, POOL_HALO, POOL_WIDTH),
                               lambda i, j: (i, jnp.maximum(j * halo_blocks - 1, 0), 0)),
                  tok(NSA_WIDTH),
                  pl.BlockSpec((None, S5_HALVES, tm, LANES), lambda i, j: (i, 0, j, 0)),
                  _const_spec((POOL_WIDTH, POOL_WIDTH)), _const_spec((1, POOL_WIDTH)),
                  _const_spec((1, POOL_WIDTH)), _const_spec((1, d)),
                  _const_spec((S5_WIDTH, S5_WIDTH)), _const_spec((1, S5_WIDTH)),
                  _const_spec((d, d))],
        out_specs=tok(d),
        out_shape=jax.ShapeDtypeStruct((b, s, d), F32),
        scratch_shapes=[pltpu.VMEM((POOL_HALO + tm, POOL_WIDTH), F32)],
        compiler_params=_params(("parallel", "arbitrary")),
        name="mixer_out_proj",
    )(x, mod, u_pool, u_pool, o_nsa, y_s5, pool_w_bd, pool_b, pool_scale, out_norm, glu_w, glu_b, w_out)


def _rope_tables(pos):
    half = ROT_DIM // 2
    inv_freq = jnp.exp(-math.log(ROPE_THETA) * jnp.arange(half, dtype=F32) * (2.0 / ROT_DIM))
    ang = pos[:, None] * inv_freq[None, :]
    cos, sin = jnp.cos(ang), jnp.sin(ang)
    n = pos.shape[0]
    rest = HEAD_DIM - ROT_DIM
    cos_t = jnp.concatenate([cos, cos, jnp.ones((n, rest), F32)], axis=1)
    sin_lo = jnp.concatenate([-sin, jnp.zeros((n, half + rest), F32)], axis=1)
    sin_hi = jnp.concatenate([jnp.zeros((n, half), F32), sin, jnp.zeros((n, rest), F32)], axis=1)
    return tuple(jnp.tile(t, (1, 2)) for t in (cos_t, sin_lo, sin_hi))


def _pad_lanes(v):
    return jnp.pad(v, [(0, 0)] * (v.ndim - 1) + [(0, LANES - v.shape[-1])])


def _arrange_w_in(w):
    d = w.shape[0]
    o1, o2, o3 = POOL_WIDTH, POOL_WIDTH + NSA_WIDTH, POOL_WIDTH + NSA_WIDTH + 6 * LANES
    q = _pad_lanes(w[:, o1:o2].reshape(d, NSA_HEADS, HEAD_DIM)).reshape(d, NSA_HEADS * LANES)
    kv = w[:, o2:o3].reshape(d, 6, NSA_KV_HEADS, HEAD_DIM)
    ka = jnp.concatenate([kv[:, 2], kv[:, 4]], axis=-1).reshape(d, NSA_KV_HEADS * LANES)
    va = jnp.concatenate([kv[:, 3], kv[:, 5]], axis=-1).reshape(d, NSA_KV_HEADS * LANES)
    kc = kv[:, 0].reshape(d, LANES)
    vc = kv[:, 1].reshape(d, LANES)
    n_gate = NSA_REP * N_BRANCH
    gate = _pad_lanes(w[:, o3:o3 + NSA_KV_HEADS * n_gate].reshape(d, NSA_KV_HEADS, n_gate))
    s5 = w[:, o3 + NSA_KV_HEADS * n_gate:]
    return jnp.concatenate([w[:, :o1], q, ka, va, kc, vc, gate.reshape(d, NSA_KV_HEADS * LANES), s5],
                           axis=1).astype(BF16)


def _expand_cmp_w1(w1):
    halves = w1.reshape(2, CMP_STRIDE, HEAD_DIM, CMP_HIDDEN)
    both = jnp.concatenate([halves[0], halves[1]], axis=-1)
    out = []
    for g in range(NSA_KV_HEADS):
        z = jnp.zeros((CMP_STRIDE, NSA_KV_HEADS, HEAD_DIM, 2 * CMP_HIDDEN), F32).at[:, g].set(both)
        out.append(z.reshape(CMP_STRIDE * LANES, 2 * CMP_HIDDEN))
    return jnp.stack(out).astype(BF16)


def _block_diag(w):
    g, c, _ = w.shape
    return jnp.einsum('gcd,gf->gcfd', w, jnp.eye(g, dtype=w.dtype)).reshape(g * c, g * c)


def _hybrid_layer(x, mod, tabs, cmp_tabs, norm_ffn1, ffn1_w_in, ffn1_w_out, norm_mix, w_in, w_out, out_norm,
                  pool_w, pool_b, pool_scale, q_norm, k_norm, cmp_pe, cmp_k_w1, cmp_k_w2, cmp_v_w1, cmp_v_w2,
                  s5_lam_re, s5_lam_im, s5_log_dt, s5_b_re, s5_b_im, s5_c_re, s5_c_im, s5_d, glu_w, glu_b,
                  norm_ffn2, ffn2_w_in, ffn2_w_out):
    x = _ffn(x, mod, norm_ffn1, ffn1_w_in.astype(BF16), ffn1_w_out.astype(BF16), 0)

    q_gain = _pad_lanes(q_norm.reshape(1, HEAD_DIM))
    slc_win_gain = jnp.concatenate([k_norm[1], k_norm[2]]).reshape(1, LANES)
    u_pool, q, ka, kb, va, vb, kc, vc, gates, u_s5 = _inproj(
        x, mod, norm_mix, _arrange_w_in(w_in), tabs, q_gain, slc_win_gain)

    k_cmp, v_cmp = _compress(
        kc, vc, _expand_cmp_w1(cmp_k_w1), _expand_cmp_w1(cmp_v_w1), cmp_k_w1, cmp_v_w1,
        cmp_pe.reshape(2, CMP_LEN * HEAD_DIM), _pad_lanes(cmp_k_w2).astype(BF16),
        _pad_lanes(cmp_v_w2).astype(BF16), _pad_lanes(k_norm[0].reshape(1, HEAD_DIM)), cmp_tabs)
    o_nsa = _nsa_attention(q, k_cmp, v_cmp, ka, kb, va, vb, gates)

    y_s5 = _s5_scan(u_s5, _s5_matrices(s5_lam_re, s5_lam_im, s5_log_dt, s5_b_re, s5_b_im,
                                       s5_c_re, s5_c_im, s5_d))

    x = _outproj(x, mod, u_pool, o_nsa, y_s5, _block_diag(pool_w).astype(BF16),
                 pool_b.reshape(1, POOL_WIDTH), pool_scale.reshape(1, POOL_WIDTH),
                 out_norm.reshape(1, D_MODEL), glu_w.astype(BF16), glu_b.reshape(1, S5_WIDTH),
                 w_out.astype(BF16))
    return _ffn(x, mod, norm_ffn2, ffn2_w_in.astype(BF16), ffn2_w_out.astype(BF16), 6)


def kernel(x, c, ada_w, ada_b, norm_ffn1, ffn1_w_in, ffn1_w_out, norm_mix, w_in, w_out, out_norm, pool_w, pool_b, pool_scale, q_norm, k_norm, cmp_pe, cmp_k_w1, cmp_k_w2, cmp_v_w1, cmp_v_w2, s5_lam_re, s5_lam_im, s5_log_dt, s5_b_re, s5_b_im, s5_c_re, s5_c_im, s5_d, glu_w, glu_b, norm_ffn2, ffn2_w_in, ffn2_w_out):
    seq = x.shape[1]
    assert seq % (KEY_TILE * 4) == 0 and seq >= WINDOW + Q_BLOCK
    mod = _modulation(c, ada_w, ada_b)
    tabs = _rope_tables(jnp.arange(seq, dtype=F32))
    n_cmp_pad = seq // CMP_STRIDE
    cmp_tabs = _rope_tables((jnp.arange(n_cmp_pad) * CMP_STRIDE + CMP_LEN - 1).astype(F32))
    per_layer = (norm_ffn1, ffn1_w_in, ffn1_w_out, norm_mix, w_in, w_out, out_norm, pool_w, pool_b, pool_scale,
                 q_norm, k_norm, cmp_pe, cmp_k_w1, cmp_k_w2, cmp_v_w1, cmp_v_w2, s5_lam_re, s5_lam_im,
                 s5_log_dt, s5_b_re, s5_b_im, s5_c_re, s5_c_im, s5_d, glu_w, glu_b, norm_ffn2, ffn2_w_in,
                 ffn2_w_out)
    for l in range(ada_w.shape[0]):
        x = _hybrid_layer(x, mod[l], tabs, cmp_tabs, *[p[l] for p in per_layer])
    return x
```

```python
import functools
import math

import jax
import jax.numpy as jnp
import numpy as np
from jax import lax
from jax.experimental import pallas as pl
from jax.experimental.pallas import tpu as pltpu

F32 = jnp.float32
BF16 = jnp.bfloat16
HIGHEST = lax.Precision.HIGHEST

LANES = 128
SUBLANES = 8
VMEM_LIMIT_BYTES = 56 * 1024 * 1024

D_MODEL = 1024
D_FF = 2816
N_MOD = 9
POOL_WIDTH = 256
POOL_GC = 64
POOL_WINDOWS = (2, 4, 8, 16)
POOL_HALO = 16
HEAD_DIM = 64
NSA_WIDTH = 512
NSA_HEADS = 8
NSA_KV_HEADS = 2
NSA_REP = 4
N_BRANCH = 3
S5_WIDTH = 256
S5_H = 16
S5_GROUPS = 16
S5_P = 64
CMP_STRIDE = 16
CMP_LEN = 32
CMP_HIDDEN = 128
SLC_BLOCK = 64
SLC_SHIFT = 6
N_SELECT = 16
N_FORCED = 3
WINDOW = 512
Q_BLOCK = 128
ROT_DIM = 16
ROPE_THETA = 500000.0
EPS = 1e-6
Q_SCALE = HEAD_DIM ** -0.5 * math.log2(math.e)
NEG_INF = -1e30
MASK_VALUE = -(2.0 ** 100)

OFF_POOL = 0
OFF_KA = OFF_POOL + POOL_WIDTH
OFF_KC = OFF_KA + NSA_KV_HEADS * LANES
OFF_VC = OFF_KC + LANES
OFF_S5 = OFF_VC + LANES
N_COLS = OFF_S5 + S5_WIDTH
GATE_ROWS = 16
ROW_Q = 0
ROW_V = ROW_Q + NSA_HEADS * LANES
ROW_GATE = ROW_V + NSA_KV_HEADS * LANES
N_ROWS_T = ROW_GATE + NSA_KV_HEADS * GATE_ROWS

S5_CHUNK = 8
S5_FOLD = S5_CHUNK * S5_WIDTH
S5_STATE = S5_GROUPS * S5_P
S5_HALVES = S5_WIDTH // LANES
S5_HALF_FOLD = S5_FOLD // S5_HALVES
KEY_TILE = 512


def _dot(a, b, precision=None):
    return jnp.dot(a, b, preferred_element_type=F32, precision=precision)


def _dot_nt(a, b, precision=None):
    return lax.dot_general(a, b, (((1,), (1,)), ((), ())), preferred_element_type=F32,
                           precision=precision)


def _sigmoid(x):
    return 1.0 / (1.0 + jnp.exp(-x))


def _gelu_tanh(x):
    return 0.5 * x * (1.0 + jnp.tanh(math.sqrt(2.0 / math.pi) * (x + 0.044715 * (x * x * x))))


def _params(sem):
    return pltpu.CompilerParams(dimension_semantics=sem, vmem_limit_bytes=VMEM_LIMIT_BYTES)


def _const_spec(shape):
    nd = len(shape)
    return pl.BlockSpec(shape, lambda *_: (0,) * nd, pipeline_mode=pl.Buffered(1))


def _mod_kernel(c_ref, w_ref, b_ref, o_ref):
    c = c_ref[...]
    o_ref[...] = _dot(c * _sigmoid(c), w_ref[...], HIGHEST) + b_ref[...]


def _modulation(c, ada_w, ada_b):
    n_layers, d, n = ada_w.shape
    b = c.shape[0]
    tn = 1152
    c_pad = jnp.zeros((SUBLANES, d), F32).at[:b].set(c)
    out = pl.pallas_call(
        _mod_kernel,
        grid=(n_layers, n // tn),
        in_specs=[pl.BlockSpec((SUBLANES, d), lambda l, j: (0, 0)),
                  pl.BlockSpec((None, d, tn), lambda l, j: (l, 0, j)),
                  pl.BlockSpec((None, 1, tn), lambda l, j: (l, 0, j))],
        out_specs=pl.BlockSpec((None, SUBLANES, tn), lambda l, j: (l, 0, j)),
        out_shape=jax.ShapeDtypeStruct((n_layers, SUBLANES, n), F32),
        compiler_params=_params(("arbitrary", "arbitrary")),
        name="adaln_mod",
    )(c_pad, ada_w, ada_b.reshape(n_layers, 1, n))
    return out[:, :b].reshape(n_layers, b, N_MOD, d)


def _norm_modulate(x, gain, mod_ref, first_row):
    ms = jnp.mean(x * x, axis=-1, keepdims=True)
    y = x * lax.rsqrt(ms + EPS) * gain
    return y * (1.0 + mod_ref[first_row + 1:first_row + 2, :]) + mod_ref[first_row:first_row + 1, :]


def _ffn_kernel(x_ref, mod_ref, g_ref, win_ref, wout_ref, o_ref, *, first_row):
    x = x_ref[...]
    h = _norm_modulate(x, g_ref[...], mod_ref, first_row).astype(BF16)
    gu = _dot(h, win_ref[...])
    gate, up = gu[:, :D_FF], gu[:, D_FF:]
    a = (gate * _sigmoid(gate) * up).astype(BF16)
    y = _dot(a, wout_ref[...])
    o_ref[...] = x + 0.5 * mod_ref[first_row + 2:first_row + 3, :] * y


def _ffn(x, mod, gain, w_in, w_out, first_row, tm=256):
    b, s, d = x.shape
    return pl.pallas_call(
        functools.partial(_ffn_kernel, first_row=first_row),
        grid=(b, s // tm),
        in_specs=[pl.BlockSpec((None, tm, d), lambda i, j: (i, j, 0)),
                  pl.BlockSpec((None, N_MOD, d), lambda i, j: (i, 0, 0)),
                  _const_spec((1, d)),
                  _const_spec((d, 2 * D_FF)),
                  _const_spec((D_FF, d))],
        out_specs=pl.BlockSpec((None, tm, d), lambda i, j: (i, j, 0)),
        out_shape=jax.ShapeDtypeStruct((b, s, d), F32),
        compiler_params=_params(("parallel", "parallel")),
        name="ffn_half_step",
    )(x, mod, gain.reshape(1, d), w_in, w_out)


def _rope(v, cos_t, sin_lo, sin_hi):
    return (v * cos_t + pltpu.roll(v, LANES - ROT_DIM // 2, 1) * sin_lo
            + pltpu.roll(v, ROT_DIM // 2, 1) * sin_hi)


def _inproj_kernel(x_ref, mod_ref, g_ref, w_ref, wt_ref, cos_ref, slo_ref, shi_ref, cost_ref, sint_ref, qg_ref, kg_ref,
                   pool_ref, q_ref, ka_ref, kb_ref, va_ref, vb_ref, kc_ref, vc_ref, gate_ref, s5_ref):
    h = _norm_modulate(x_ref[...], g_ref[...], mod_ref, 3).astype(BF16)
    u = _dot(h, w_ref[...])
    ut = _dot_nt(wt_ref[...], h)
    tm = u.shape[0]
    cos_t, sin_lo, sin_hi = cos_ref[...], slo_ref[...], shi_ref[...]
    lane = lax.broadcasted_iota(jnp.int32, (tm, LANES), 1)
    low_half = lane < HEAD_DIM
    pos = pl.program_id(1) * tm + lax.broadcasted_iota(jnp.int32, (tm, LANES), 0)
    block_one_hot = jnp.where((pos >> SLC_SHIFT) == lane, 1.0, 0.0).astype(BF16)

    pool_ref[...] = u[:, OFF_POOL:OFF_POOL + POOL_WIDTH]
    kc_ref[...] = u[:, OFF_KC:OFF_KC + LANES]
    vc_ref[...] = u[:, OFF_VC:OFF_VC + LANES]
    for half in range(S5_HALVES):
        s5_ref[half] = u[:, OFF_S5 + half * LANES:OFF_S5 + (half + 1) * LANES]
    for g in range(NSA_KV_HEADS):
        v = u[:, OFF_KA + g * LANES:OFF_KA + (g + 1) * LANES]
        sq = v * v
        ms_lo = jnp.sum(jnp.where(low_half, sq, 0.0), axis=-1, keepdims=True) * (1.0 / HEAD_DIM)
        ms_hi = jnp.sum(jnp.where(low_half, 0.0, sq), axis=-1, keepdims=True) * (1.0 / HEAD_DIM)
        r = jnp.where(low_half, lax.rsqrt(ms_lo + EPS), lax.rsqrt(ms_hi + EPS))
        kn = _rope(v * r * kg_ref[...], cos_t, sin_lo, sin_hi)
        ka_ref[g, :, 0:LANES] = kn.astype(BF16)
        ka_ref[g, :, LANES:2 * LANES] = block_one_hot
        kb_ref[g] = pltpu.roll(kn, HEAD_DIM, 1).astype(BF16)

    half_rot = ROT_DIM // 2
    cos8, sin8 = cost_ref[...], sint_ref[...]
    for hd in range(NSA_HEADS):
        v = ut[ROW_Q + hd * LANES:ROW_Q + (hd + 1) * LANES, :]
        ms = jnp.sum(v * v, axis=0, keepdims=True) * (1.0 / HEAD_DIM)
        vn = v * lax.rsqrt(ms + EPS) * qg_ref[...]
        x1, x2 = vn[0:half_rot], vn[half_rot:ROT_DIM]
        roped = jnp.concatenate([x1 * cos8 - x2 * sin8, x2 * cos8 + x1 * sin8, vn[ROT_DIM:]], axis=0)
        q_ref[hd] = (roped * Q_SCALE).astype(BF16)
    ones = jnp.ones((HEAD_DIM, tm), F32)
    for g in range(NSA_KV_HEADS):
        vt = ut[ROW_V + g * LANES:ROW_V + (g + 1) * LANES, :]
        va = jnp.concatenate([vt[0:HEAD_DIM], ones], axis=0).astype(BF16)
        vb = jnp.concatenate([vt[HEAD_DIM:], ones], axis=0).astype(BF16)
        for blk in range(tm // LANES):
            va_ref[g, blk] = va[:, blk * LANES:(blk + 1) * LANES]
            vb_ref[g, blk] = vb[:, blk * LANES:(blk + 1) * LANES]
        gate_ref[g] = _sigmoid(ut[ROW_GATE + g * GATE_ROWS:ROW_GATE + (g + 1) * GATE_ROWS, :])


def _inproj(x, mod, gain, w_row, w_t, rope_tabs, rope_tabs_t, q_gain, k_gain, tm=256):
    b, s, d = x.shape
    g = NSA_KV_HEADS
    tok = lambda width: pl.BlockSpec((None, tm, width), lambda i, j: (i, j, 0))
    grp = lambda n, width=LANES: pl.BlockSpec((None, n, tm, width), lambda i, j: (i, 0, j, 0))
    lanes_tok = lambda n, rows: pl.BlockSpec((None, n, rows, tm), lambda i, j: (i, 0, 0, j))
    v_blocks = pl.BlockSpec((None, g, tm // LANES, LANES, LANES), lambda i, j: (i, 0, j, 0, 0))
    tab = pl.BlockSpec((tm, LANES), lambda i, j: (j, 0))
    tab_t = pl.BlockSpec((ROT_DIM // 2, tm), lambda i, j: (0, j))
    sds = jax.ShapeDtypeStruct
    return pl.pallas_call(
        _inproj_kernel,
        grid=(b, s // tm),
        in_specs=[tok(d),
                  pl.BlockSpec((None, N_MOD, d), lambda i, j: (i, 0, 0)),
                  _const_spec((1, d)),
                  _const_spec((d, N_COLS)), _const_spec((N_ROWS_T, d)),
                  tab, tab, tab, tab_t, tab_t,
                  _const_spec((LANES, 1)), _const_spec((1, LANES))],
        out_specs=[tok(POOL_WIDTH), lanes_tok(NSA_HEADS, LANES), grp(g, 2 * LANES), grp(g), v_blocks, v_blocks,
                   tok(LANES), tok(LANES), lanes_tok(g, GATE_ROWS), grp(S5_HALVES)],
        out_shape=[sds((b, s, POOL_WIDTH), F32), sds((b, NSA_HEADS, LANES, s), BF16),
                   sds((b, g, s, 2 * LANES), BF16), sds((b, g, s, LANES), BF16),
                   sds((b, g, s // LANES, LANES, LANES), BF16), sds((b, g, s // LANES, LANES, LANES), BF16),
                   sds((b, s, LANES), F32), sds((b, s, LANES), F32),
                   sds((b, g, GATE_ROWS, s), F32), sds((b, S5_HALVES, s, LANES), F32)],
        compiler_params=_params(("parallel", "parallel")),
        name="mixer_in_proj",
    )(x, mod, gain.reshape(1, d), w_row, w_t, *rope_tabs, *rope_tabs_t, q_gain, k_gain)


def _fold_rows(ref, n):
    rows = ref.shape[0] // n
    return jnp.concatenate([ref[pl.ds(k, rows, stride=n), :] for k in range(n)], axis=1)


def _unfold_rows(ref, value, n):
    rows = ref.shape[0] // n
    for k in range(n):
        ref[pl.ds(k, rows, stride=n), :] = value[:, k * LANES:(k + 1) * LANES]


def _compress_kernel(kc_ref, vc_ref, w1k_ref, w1v_ref, w1k_raw_ref, w1v_raw_ref, pe_ref, w2k_ref, w2vt_ref,
                     kg_ref, cos_ref, slo_ref, shi_ref, ko_ref, vo_ref, *, n_cmp):
    ncp = kc_ref.shape[0] // CMP_STRIDE
    for src_ref, w1_ref, raw_ref, pe_row, is_key in ((kc_ref, w1k_ref, w1k_raw_ref, 0, True),
                                                    (vc_ref, w1v_ref, w1v_raw_ref, 1, False)):
        chunks = _fold_rows(src_ref, CMP_STRIDE).astype(BF16)
        pe = jnp.broadcast_to(pe_ref[pe_row:pe_row + 1, :], (SUBLANES, CMP_LEN * HEAD_DIM))
        pe_term = _dot(pe, raw_ref[...], HIGHEST)[0:1, :]
        for g in range(NSA_KV_HEADS):
            a = _dot(chunks, w1_ref[g])
            pre = a[:, :CMP_HIDDEN] + pltpu.roll(a[:, CMP_HIDDEN:], ncp - 1, 0) + pe_term
            hidden = _gelu_tanh(pre).astype(BF16)
            if is_key:
                out = _dot(hidden, w2k_ref[...])
                ms = jnp.sum(out * out, axis=-1, keepdims=True) * (1.0 / HEAD_DIM)
                out = _rope(out * lax.rsqrt(ms + EPS) * kg_ref[...], cos_ref[...], slo_ref[...], shi_ref[...])
                real_row = lax.broadcasted_iota(jnp.int32, (ncp, LANES), 0) < n_cmp
                ko_ref[g] = jnp.where(real_row, out, 0.0).astype(BF16)
            else:
                out_t = _dot_nt(w2vt_ref[...], hidden)
                real_col = lax.broadcasted_iota(jnp.int32, (LANES, ncp), 1) < n_cmp
                value_row = lax.broadcasted_iota(jnp.int32, (LANES, ncp), 0) < HEAD_DIM
                vo_ref[g] = jnp.where(value_row, jnp.where(real_col, out_t, 0.0), 1.0).astype(BF16)


def _compress(kc, vc, w1k, w1v, w1k_raw, w1v_raw, pe, w2k, w2v_t, k_gain, cmp_tabs):
    b, s, _ = kc.shape
    ncp = s // CMP_STRIDE
    fold = CMP_STRIDE * LANES
    src = pl.BlockSpec((None, s, LANES), lambda i: (i, 0, 0))
    raw = _const_spec((CMP_LEN * HEAD_DIM, CMP_HIDDEN))
    w1 = _const_spec((NSA_KV_HEADS, fold, 2 * CMP_HIDDEN))
    w2 = _const_spec((CMP_HIDDEN, LANES))
    tab = _const_spec((ncp, LANES))
    return pl.pallas_call(
        functools.partial(_compress_kernel, n_cmp=ncp - 1),
        grid=(b,),
        in_specs=[src, src, w1, w1, raw, raw, _const_spec((2, CMP_LEN * HEAD_DIM)), w2, w2,
                  _const_spec((1, LANES)), tab, tab, tab],
        out_specs=[pl.BlockSpec((None, NSA_KV_HEADS, ncp, LANES), lambda i: (i, 0, 0, 0)),
                   pl.BlockSpec((None, NSA_KV_HEADS, LANES, ncp), lambda i: (i, 0, 0, 0))],
        out_shape=[jax.ShapeDtypeStruct((b, NSA_KV_HEADS, ncp, LANES), BF16),
                   jax.ShapeDtypeStruct((b, NSA_KV_HEADS, LANES, ncp), BF16)],
        compiler_params=_params(("parallel",)),
        name="nsa_compress",
    )(kc, vc, w1k, w1v, w1k_raw, w1v_raw, pe, w2k, w2v_t, k_gain, *cmp_tabs)


def _selection_bias(imp_t, qb):
    nsb, nq = imp_t.shape
    j = lax.broadcasted_iota(jnp.int32, (nsb, nq), 0)
    t = qb * Q_BLOCK + lax.broadcasted_iota(jnp.int32, (nsb, nq), 1)
    cur = t >> SLC_SHIFT
    valid = j * SLC_BLOCK <= t
    forced = (j == 0) | (j == cur) | (j == cur - 1)
    j_f = j.astype(F32)
    start = jnp.where(valid & jnp.logical_not(forced), imp_t, -1.0)
    vals = start
    for _ in range(N_SELECT - N_FORCED):
        m = jnp.max(vals, axis=0, keepdims=True)
        idx = jnp.min(jnp.where(vals == m, j_f, float(nsb)), axis=0, keepdims=True)
        vals = jnp.where(j_f == idx, -2.0, vals)
    return jnp.where((forced & valid) | (vals != start), 0.0, MASK_VALUE)


def _attn_kernel(q_ref, kc_ref, vct_ref, ka_ref, kb_ref, va_ref, vb_ref, gate_ref, ovl_ref, o_ref,
                 sa_ref, sb_ref, *, seq):
    qb = pl.program_id(2)
    cols = NSA_REP * Q_BLOCK
    ncp = seq // CMP_STRIDE
    q_t = jnp.concatenate([q_ref[r] for r in range(NSA_REP)], axis=1)
    t_col = qb * Q_BLOCK + lax.broadcasted_iota(jnp.int32, (1, cols), 1) % Q_BLOCK

    cmp_end = lax.broadcasted_iota(jnp.int32, (ncp, cols), 0) * CMP_STRIDE + (CMP_LEN - 1)
    s = jnp.where(cmp_end <= t_col, _dot(kc_ref[...], q_t), NEG_INF)
    e = jnp.exp2(s - jnp.max(s, axis=0, keepdims=True)).astype(BF16)
    oc = _dot(jnp.concatenate([vct_ref[...], ovl_ref[...]], axis=0), e)
    oc = oc * jnp.where(t_col >= CMP_LEN - 1, 1.0 / oc[HEAD_DIM:HEAD_DIM + 1, :], 0.0)
    o_cmp, imp = oc[0:LANES], oc[LANES:2 * LANES]
    imp_t = (imp[:, 0:Q_BLOCK] + imp[:, Q_BLOCK:2 * Q_BLOCK]
             + imp[:, 2 * Q_BLOCK:3 * Q_BLOCK] + imp[:, 3 * Q_BLOCK:4 * Q_BLOCK])
    bias = _selection_bias(imp_t, qb).astype(BF16)

    q_aug = jnp.concatenate([q_t, jnp.concatenate([bias] * NSA_REP, axis=1)], axis=0)

    span = WINDOW + Q_BLOCK
    start_blk = jnp.maximum(qb - WINDOW // Q_BLOCK, 0)
    start = pl.multiple_of(start_blk * Q_BLOCK, Q_BLOCK)
    kpos = start + lax.broadcasted_iota(jnp.int32, (span, cols), 0)
    s = jnp.where((kpos <= t_col) & (kpos > t_col - WINDOW), _dot(kb_ref[pl.ds(start, span), :], q_t), NEG_INF)
    e = jnp.exp2(s - jnp.max(s, axis=0, keepdims=True)).astype(BF16)
    acc_win = _dot(jnp.concatenate([vb_ref[start_blk + i] for i in range(span // LANES)], axis=1), e)
    o_win = acc_win * (1.0 / acc_win[HEAD_DIM:HEAD_DIM + 1, :])

    n_key_tiles = seq // KEY_TILE
    blocks_per_tile = KEY_TILE // LANES

    def score_tile(kt, s_ref):
        k0 = pl.multiple_of(jnp.minimum(kt, n_key_tiles - 1) * KEY_TILE, KEY_TILE)
        s_ref[...] = _dot(ka_ref[pl.ds(k0, KEY_TILE), :], q_aug)

    def absorb_tile(kt, s_ref, carry, causal):
        m_i, acc = carry
        blk0 = jnp.minimum(kt, n_key_tiles - 1) * blocks_per_tile
        s = s_ref[...]
        if causal:
            kpos = kt * KEY_TILE + lax.broadcasted_iota(jnp.int32, (KEY_TILE, cols), 0)
            s = jnp.where(kpos <= t_col, s, MASK_VALUE)
        m_new = jnp.maximum(m_i, jnp.max(s, axis=0, keepdims=True))
        p = jnp.exp2(s - m_new).astype(BF16)
        v_t = jnp.concatenate([va_ref[blk0 + i] for i in range(blocks_per_tile)], axis=1)
        return m_new, jnp.exp2(m_i - m_new) * acc + _dot(v_t, p)

    def slc_trip(j, carry):
        score_tile(2 * j + 1, sb_ref)
        carry = absorb_tile(2 * j, sa_ref, carry, causal=False)
        score_tile(2 * j + 2, sa_ref)
        return absorb_tile(2 * j + 1, sb_ref, carry, causal=False)

    last_pair = qb // (2 * KEY_TILE // Q_BLOCK)
    score_tile(0, sa_ref)
    carry = lax.fori_loop(0, last_pair, slc_trip,
                          (jnp.full((1, cols), NEG_INF, F32), jnp.zeros((LANES, cols), F32)))
    score_tile(2 * last_pair + 1, sb_ref)
    carry = absorb_tile(2 * last_pair, sa_ref, carry, causal=True)
    _, acc_slc = absorb_tile(2 * last_pair + 1, sb_ref, carry, causal=True)
    o_slc = acc_slc * (1.0 / acc_slc[HEAD_DIM:HEAD_DIM + 1, :])

    gates = gate_ref[...]
    heads = []
    for r in range(NSA_REP):
        cs = slice(r * Q_BLOCK, (r + 1) * Q_BLOCK)
        c = r * N_BRANCH
        heads.append(gates[c:c + 1, :] * o_cmp[0:HEAD_DIM, cs] + gates[c + 1:c + 2, :] * o_slc[0:HEAD_DIM, cs]
                     + gates[c + 2:c + 3, :] * o_win[0:HEAD_DIM, cs])
    o_ref[...] = jnp.concatenate(heads, axis=0).T


def _overlap_matrix(seq):
    ncp, nsb = seq // CMP_STRIDE, seq // SLC_BLOCK
    c_start = np.arange(ncp)[None, :] * CMP_STRIDE
    s_start = np.arange(nsb)[:, None] * SLC_BLOCK
    ovl = np.clip(np.minimum(c_start + CMP_LEN, s_start + SLC_BLOCK) - np.maximum(c_start, s_start), 0, None)
    ovl = ovl.astype(np.float32) / CMP_LEN
    ovl[:, ncp - 1] = 0.0
    return jnp.asarray(np.pad(ovl, ((0, LANES - nsb), (0, 0))))


def _nsa_attention(q_t, k_cmp, v_cmp_t, ka, kb, va_t, vb_t, gates_t):
    b, _, _, s = q_t.shape
    g = NSA_KV_HEADS
    ncp = s // CMP_STRIDE
    assert s // SLC_BLOCK <= LANES
    per_group = lambda *shape: pl.BlockSpec((None, None) + shape, lambda i, j, k: (i, j) + (0,) * len(shape))
    return pl.pallas_call(
        functools.partial(_attn_kernel, seq=s),
        grid=(b, g, s // Q_BLOCK),
        in_specs=[pl.BlockSpec((None, NSA_REP, LANES, Q_BLOCK), lambda i, j, k: (i, j, 0, k)),
                  per_group(ncp, LANES), per_group(LANES, ncp), per_group(s, 2 * LANES), per_group(s, LANES),
                  per_group(s // LANES, LANES, LANES), per_group(s // LANES, LANES, LANES),
                  pl.BlockSpec((None, None, GATE_ROWS, Q_BLOCK), lambda i, j, k: (i, j, 0, k)),
                  _const_spec((LANES, ncp))],
        out_specs=pl.BlockSpec((None, Q_BLOCK, NSA_REP * HEAD_DIM), lambda i, j, k: (i, k, j)),
        out_shape=jax.ShapeDtypeStruct((b, s, NSA_WIDTH), F32),
        scratch_shapes=[pltpu.VMEM((KEY_TILE, NSA_REP * Q_BLOCK), F32)] * 2,
        compiler_params=_params(("parallel", "parallel", "arbitrary")),
        name="nsa_attention",
    )(q_t, k_cmp, v_cmp_t, ka, kb, va_t, vb_t, gates_t, _overlap_matrix(s).astype(BF16))


def _s5_kernel(u_ref, mt_ref, bc_ref, cc_ref, a1_ref, a2_ref, d_ref, y_ref, h_ref, g_scr, hp_scr):
    @pl.when(pl.program_id(1) == 0)
    def _():
        h_ref[...] = jnp.zeros_like(h_ref)

    halves = range(S5_HALVES)
    u = [_fold_rows(u_ref.at[a], S5_CHUNK) for a in halves]
    ub = [x.astype(BF16) for x in u]
    y_local = [_dot(ub[a], mt_ref[a]) for a in halves]
    for a in halves:
        g_scr[:, a * S5_HALF_FOLD:(a + 1) * S5_HALF_FOLD] = _dot(ub[a], bc_ref[a])
    a1, a2 = a1_ref[...], a2_ref[...]
    n_state = S5_STATE // S5_HALVES

    def swap_re_im(h):
        parts = [h[:, k * n_state:(k + 1) * n_state] for k in range(2 * S5_HALVES)]
        return jnp.concatenate([parts[k ^ 1] for k in range(2 * S5_HALVES)], axis=1)

    def step(i, h):
        hp_scr[pl.ds(i, 1), :] = h
        return a1 * h + a2 * swap_re_im(h) + g_scr[pl.ds(i, 1), :]

    h_ref[...] = lax.fori_loop(0, g_scr.shape[0], step, h_ref[...])
    for a in halves:
        carried = _dot(hp_scr[:, a * S5_HALF_FOLD:(a + 1) * S5_HALF_FOLD].astype(BF16), cc_ref[a])
        _unfold_rows(y_ref.at[a], y_local[a] + carried + u[a] * d_ref[a:a + 1, :], S5_CHUNK)


def _s5_scan(u, mats):
    b, _, s, _ = u.shape
    rows = s // S5_CHUNK
    tc = min(128, rows)
    tile = pl.BlockSpec((None, S5_HALVES, tc * S5_CHUNK, LANES), lambda i, j: (i, 0, j, 0))
    mat = _const_spec((S5_HALVES, S5_HALF_FOLD, S5_HALF_FOLD))
    row = _const_spec((1, S5_HALVES * S5_HALF_FOLD))
    return pl.pallas_call(
        _s5_kernel,
        grid=(b, rows // tc),
        in_specs=[tile, mat, mat, mat, row, row, _const_spec((S5_HALVES, S5_HALF_FOLD))],
        out_specs=tile,
        out_shape=jax.ShapeDtypeStruct((b, S5_HALVES, s, LANES), F32),
        scratch_shapes=[pltpu.VMEM((1, S5_HALVES * S5_HALF_FOLD), F32),
                        pltpu.VMEM((tc, S5_HALVES * S5_HALF_FOLD), F32),
                        pltpu.VMEM((tc, S5_HALVES * S5_HALF_FOLD), F32)],
        compiler_params=_params(("parallel", "arbitrary")),
        name="s5_scan",
    )(u, *mats)


def _s5_matrices(lam_re, lam_im, log_dt, b_re, b_im, c_re, c_im, d_skip):
    t0, ng, nh, npm = S5_CHUNK, S5_GROUPS, S5_H, S5_P
    gh = ng // S5_HALVES
    ein = functools.partial(jnp.einsum, precision=HIGHEST)
    lam = lax.complex(lam_re, lam_im)
    step = jnp.exp(log_dt)[:, None]
    lam_bar = jnp.exp(lam * step)
    b_bar = lax.complex(b_re, b_im) * ((lam_bar - 1.0) / lam)[..., None]
    c_mat = lax.complex(c_re, c_im)
    k = jnp.arange(t0 + 1, dtype=F32)[:, None, None]
    pw = jnp.exp((lam * step)[None] * k)

    def same_group(rows_per_group, cols_per_group):
        r = np.arange(gh * rows_per_group)[:, None] // rows_per_group
        c = np.arange(gh * cols_per_group)[None, :] // cols_per_group
        return jnp.asarray((r == c).astype(np.float32))

    def per_group_blocks(x, rows_per_group, cols_per_group):
        return jnp.tile(x, (1,) * (x.ndim - 1) + (gh,)) * same_group(rows_per_group, cols_per_group)

    kern = jnp.real(ein('ghp,kgp,gpq->kghq', c_mat, pw[:t0], b_bar))
    d_k = per_group_blocks(kern.transpose(0, 1, 3, 2).reshape(t0, S5_HALVES, gh * nh, nh), nh, nh)
    d_k = jnp.concatenate([d_k, jnp.zeros_like(d_k[:1])], axis=0)
    lag = np.arange(t0)[None, :] - np.arange(t0)[:, None]
    mt = d_k[np.where(lag >= 0, lag, t0)]
    mt = mt.transpose(2, 0, 3, 1, 4).reshape(S5_HALVES, S5_HALF_FOLD, S5_HALF_FOLD)
    b_j = (pw[t0 - 1 - np.arange(t0)][..., None] * b_bar[None]).transpose(0, 1, 3, 2)
    b_j = b_j.reshape(t0, S5_HALVES, gh * nh, npm)
    bc = jnp.concatenate([per_group_blocks(jnp.real(b_j), nh, npm), per_group_blocks(jnp.imag(b_j), nh, npm)],
                         axis=-1)
    bc = bc.transpose(1, 0, 2, 3).reshape(S5_HALVES, S5_HALF_FOLD, S5_HALF_FOLD)
    c_i = (c_mat[None] * pw[1:t0 + 1][:, :, None, :]).transpose(1, 3, 0, 2)
    c_i = c_i.reshape(S5_HALVES, gh * npm, t0, nh)
    mask = same_group(npm, nh)[:, None, :]
    cc = jnp.concatenate([jnp.tile(jnp.real(c_i), (1, 1, 1, gh)) * mask,
                          jnp.tile(-jnp.imag(c_i), (1, 1, 1, gh)) * mask], axis=1)
    cc = cc.reshape(S5_HALVES, S5_HALF_FOLD, S5_HALF_FOLD)
    a_chunk = pw[t0].reshape(S5_HALVES, gh * npm)
    a1 = jnp.concatenate([jnp.real(a_chunk), jnp.real(a_chunk)], axis=1).reshape(1, -1)
    a2 = jnp.concatenate([-jnp.imag(a_chunk), jnp.imag(a_chunk)], axis=1).reshape(1, -1)
    d_vec = jnp.tile(d_skip.reshape(S5_HALVES, gh * nh), (1, t0))
    return mt.astype(BF16), bc.astype(BF16), cc.astype(BF16), a1, a2, d_vec


def _rms_gain(y, gain):
    return y * lax.rsqrt(jnp.mean(y * y, axis=-1, keepdims=True) + EPS) * gain


def _outproj_kernel(x_ref, mod_ref, pool_ref, halo_ref, nsa_ref, s5_ref, pw_ref, pb_ref, ps_ref, on_ref,
                    gw_ref, gb_ref, wo_ref, o_ref, buf):
    j = pl.program_id(1)
    tm = x_ref.shape[0]
    v = pool_ref[...]
    buf[0:POOL_HALO, :] = jnp.where(j > 0, halo_ref[...], 0.0)
    buf[POOL_HALO:POOL_HALO + tm, :] = v
    lane_group = lax.broadcasted_iota(jnp.int32, (tm, POOL_WIDTH), 1) >> SLC_SHIFT
    t1 = (j * tm + 1 + lax.broadcasted_iota(jnp.int32, (tm, 1), 0)).astype(F32)
    run, k, pooled = v, 1, jnp.zeros_like(v)
    for gi, w in enumerate(POOL_WINDOWS):
        while k < w:
            run = run + buf[POOL_HALO - k:POOL_HALO - k + tm, :]
            k += 1
        pooled = jnp.where(lane_group == gi, run / jnp.minimum(t1, float(w)) - v, pooled)
    y_pool = (_dot(pooled.astype(BF16), pw_ref[...]) + pb_ref[...]) * ps_ref[...]

    y = _gelu_tanh(jnp.concatenate([s5_ref[half] for half in range(S5_HALVES)], axis=-1))
    y_s5 = y * _sigmoid(_dot(y.astype(BF16), gw_ref[...]) + gb_ref[...])

    cat = jnp.concatenate(
        [_rms_gain(y_pool, on_ref[:, 0:POOL_WIDTH]),
         _rms_gain(nsa_ref[...], on_ref[:, POOL_WIDTH:POOL_WIDTH + NSA_WIDTH]),
         _rms_gain(y_s5, on_ref[:, POOL_WIDTH + NSA_WIDTH:])], axis=-1).astype(BF16)
    o_ref[...] = x_ref[...] + mod_ref[5:6, :] * _dot(cat, wo_ref[...])


def _outproj(x, mod, u_pool, o_nsa, y_s5, pool_w_bd, pool_b, pool_scale, out_norm, glu_w, glu_b, w_out, tm=256):
    b, s, d = x.shape
    tok = lambda width: pl.BlockSpec((None, tm, width), lambda i, j: (i, j, 0))
    halo_blocks = tm // POOL_HALO
    return pl.pallas_call(
        _outproj_kernel,
        grid=(b, s // tm),
        in_specs=[tok(d),
                  pl.BlockSpec((None, N_MOD, d), lambda i, j: (i, 0, 0)),
                  tok(POOL_WIDTH),
                  pl.BlockSpec((None, POOL_HALO, POOL_WIDTH),
                               lambda i, j: (i, jnp.maximum(j * halo_blocks - 1, 0), 0)),
                  tok(NSA_WIDTH),
                  pl.BlockSpec((None, S5_HALVES, tm, LANES), lambda i, j: (i, 0, j, 0)),
                  _const_spec((POOL_WIDTH, POOL_WIDTH)), _const_spec((1, POOL_WIDTH)),
                  _const_spec((1, POOL_WIDTH)), _const_spec((1, d)),
                  _const_spec((S5_WIDTH, S5_WIDTH)), _const_spec((1, S5_WIDTH)),
                  _const_spec((d, d))],
        out_specs=tok(d),
        out_shape=jax.ShapeDtypeStruct((b, s, d), F32),
        scratch_shapes=[pltpu.VMEM((POOL_HALO + tm, POOL_WIDTH), F32)],
        compiler_params=_params(("parallel", "arbitrary")),
        name="mixer_out_proj",
    )(x, mod, u_pool, u_pool, o_nsa, y_s5, pool_w_bd, pool_b, pool_scale, out_norm, glu_w, glu_b, w_out)


def _rope_tables(pos):
    half = ROT_DIM // 2
    inv_freq = jnp.exp(-math.log(ROPE_THETA) * jnp.arange(half, dtype=F32) * (2.0 / ROT_DIM))
    ang = pos[:, None] * inv_freq[None, :]
    cos, sin = jnp.cos(ang), jnp.sin(ang)
    n = pos.shape[0]
    rest = HEAD_DIM - ROT_DIM
    cos_t = jnp.concatenate([cos, cos, jnp.ones((n, rest), F32)], axis=1)
    sin_lo = jnp.concatenate([-sin, jnp.zeros((n, half + rest), F32)], axis=1)
    sin_hi = jnp.concatenate([jnp.zeros((n, half), F32), sin, jnp.zeros((n, rest), F32)], axis=1)
    return tuple(jnp.tile(t, (1, 2)) for t in (cos_t, sin_lo, sin_hi))


def _pad_lanes(v):
    return jnp.pad(v, [(0, 0)] * (v.ndim - 1) + [(0, LANES - v.shape[-1])])


def _rope_tables_t(pos):
    half = ROT_DIM // 2
    inv_freq = jnp.exp(-math.log(ROPE_THETA) * jnp.arange(half, dtype=F32) * (2.0 / ROT_DIM))
    ang = inv_freq[:, None] * pos[None, :]
    return jnp.cos(ang), jnp.sin(ang)


def _arrange_w_in(w):
    d = w.shape[0]
    o1, o2, o3 = POOL_WIDTH, POOL_WIDTH + NSA_WIDTH, POOL_WIDTH + NSA_WIDTH + 6 * LANES
    kv = w[:, o2:o3].reshape(d, 6, NSA_KV_HEADS, HEAD_DIM)
    ka = jnp.concatenate([kv[:, 2], kv[:, 4]], axis=-1).reshape(d, NSA_KV_HEADS * LANES)
    n_gate = NSA_REP * N_BRANCH
    w_row = jnp.concatenate([w[:, :o1], ka, kv[:, 0].reshape(d, LANES), kv[:, 1].reshape(d, LANES),
                             w[:, o3 + NSA_KV_HEADS * n_gate:]], axis=1)
    q_t = w[:, o1:o2].T.reshape(NSA_HEADS, HEAD_DIM, d)
    q_t = jnp.pad(q_t, ((0, 0), (0, LANES - HEAD_DIM), (0, 0))).reshape(NSA_HEADS * LANES, d)
    v_t = jnp.concatenate([kv[:, 3], kv[:, 5]], axis=-1).reshape(d, NSA_KV_HEADS * LANES).T
    gate_t = w[:, o3:o3 + NSA_KV_HEADS * n_gate].T.reshape(NSA_KV_HEADS, n_gate, d)
    gate_t = jnp.pad(gate_t, ((0, 0), (0, GATE_ROWS - n_gate), (0, 0))).reshape(NSA_KV_HEADS * GATE_ROWS, d)
    return w_row.astype(BF16), jnp.concatenate([q_t, v_t, gate_t], axis=0).astype(BF16)


def _expand_cmp_w1(w1):
    halves = w1.reshape(2, CMP_STRIDE, HEAD_DIM, CMP_HIDDEN)
    both = jnp.concatenate([halves[0], halves[1]], axis=-1)
    out = []
    for g in range(NSA_KV_HEADS):
        z = jnp.zeros((CMP_STRIDE, NSA_KV_HEADS, HEAD_DIM, 2 * CMP_HIDDEN), F32).at[:, g].set(both)
        out.append(z.reshape(CMP_STRIDE * LANES, 2 * CMP_HIDDEN))
    return jnp.stack(out).astype(BF16)


def _block_diag(w):
    g, c, _ = w.shape
    return jnp.einsum('gcd,gf->gcfd', w, jnp.eye(g, dtype=w.dtype)).reshape(g * c, g * c)


def _hybrid_layer(x, mod, tabs, tabs_t, cmp_tabs, norm_ffn1, ffn1_w_in, ffn1_w_out, norm_mix, w_in, w_out, out_norm,
                  pool_w, pool_b, pool_scale, q_norm, k_norm, cmp_pe, cmp_k_w1, cmp_k_w2, cmp_v_w1, cmp_v_w2,
                  s5_lam_re, s5_lam_im, s5_log_dt, s5_b_re, s5_b_im, s5_c_re, s5_c_im, s5_d, glu_w, glu_b,
                  norm_ffn2, ffn2_w_in, ffn2_w_out):
    x = _ffn(x, mod, norm_ffn1, ffn1_w_in.astype(BF16), ffn1_w_out.astype(BF16), 0)

    q_gain = _pad_lanes(q_norm.reshape(1, HEAD_DIM)).reshape(LANES, 1)
    slc_win_gain = jnp.concatenate([k_norm[1], k_norm[2]]).reshape(1, LANES)
    u_pool, q_t, ka, kb, va_t, vb_t, kc, vc, gates_t, u_s5 = _inproj(
        x, mod, norm_mix, *_arrange_w_in(w_in), tabs, tabs_t, q_gain, slc_win_gain)

    k_cmp, v_cmp_t = _compress(
        kc, vc, _expand_cmp_w1(cmp_k_w1), _expand_cmp_w1(cmp_v_w1), cmp_k_w1, cmp_v_w1,
        cmp_pe.reshape(2, CMP_LEN * HEAD_DIM), _pad_lanes(cmp_k_w2).astype(BF16),
        _pad_lanes(cmp_v_w2).T.astype(BF16), _pad_lanes(k_norm[0].reshape(1, HEAD_DIM)), cmp_tabs)
    o_nsa = _nsa_attention(q_t, k_cmp, v_cmp_t, ka, kb, va_t, vb_t, gates_t)

    y_s5 = _s5_scan(u_s5, _s5_matrices(s5_lam_re, s5_lam_im, s5_log_dt, s5_b_re, s5_b_im,
                                       s5_c_re, s5_c_im, s5_d))

    x = _outproj(x, mod, u_pool, o_nsa, y_s5, _block_diag(pool_w).astype(BF16),
                 pool_b.reshape(1, POOL_WIDTH), pool_scale.reshape(1, POOL_WIDTH),
                 out_norm.reshape(1, D_MODEL), glu_w.astype(BF16), glu_b.reshape(1, S5_WIDTH),
                 w_out.astype(BF16))
    return _ffn(x, mod, norm_ffn2, ffn2_w_in.astype(BF16), ffn2_w_out.astype(BF16), 6)


def kernel(x, c, ada_w, ada_b, norm_ffn1, ffn1_w_in, ffn1_w_out, norm_mix, w_in, w_out, out_norm, pool_w, pool_b, pool_scale, q_norm, k_norm, cmp_pe, cmp_k_w1, cmp_k_w2, cmp_v_w1, cmp_v_w2, s5_lam_re, s5_lam_im, s5_log_dt, s5_b_re, s5_b_im, s5_c_re, s5_c_im, s5_d, glu_w, glu_b, norm_ffn2, ffn2_w_in, ffn2_w_out):
    seq = x.shape[1]
    assert seq % (KEY_TILE * 4) == 0 and seq >= WINDOW + Q_BLOCK
    mod = _modulation(c, ada_w, ada_b)
    tabs = _rope_tables(jnp.arange(seq, dtype=F32))
    tabs_t = _rope_tables_t(jnp.arange(seq, dtype=F32))
    n_cmp_pad = seq // CMP_STRIDE
    cmp_tabs = _rope_tables((jnp.arange(n_cmp_pad) * CMP_STRIDE + CMP_LEN - 1).astype(F32))
    per_layer = (norm_ffn1, ffn1_w_in, ffn1_w_out, norm_mix, w_in, w_out, out_norm, pool_w, pool_b, pool_scale,
                 q_norm, k_norm, cmp_pe, cmp_k_w1, cmp_k_w2, cmp_v_w1, cmp_v_w2, s5_lam_re, s5_lam_im,
                 s5_log_dt, s5_b_re, s5_b_im, s5_c_re, s5_c_im, s5_d, glu_w, glu_b, norm_ffn2, ffn2_w_in,
                 ffn2_w_out)
    for l in range(ada_w.shape[0]):
        x = _hybrid_layer(x, mod[l], tabs, tabs_t, cmp_tabs, *[p[l] for p in per_layer])
    return x
```

```python
import functools
import math

import jax
import jax.numpy as jnp
import numpy as np
from jax import lax
from jax.experimental import pallas as pl
from jax.experimental.pallas import tpu as pltpu

F32 = jnp.float32
BF16 = jnp.bfloat16
HIGHEST = lax.Precision.HIGHEST

LANES = 128
SUBLANES = 8
VMEM_LIMIT_BYTES = 56 * 1024 * 1024

D_MODEL = 1024
D_FF = 2816
N_MOD = 9
POOL_WIDTH = 256
POOL_GC = 64
POOL_WINDOWS = (2, 4, 8, 16)
POOL_HALO = 16
HEAD_DIM = 64
NSA_WIDTH = 512
NSA_HEADS = 8
NSA_KV_HEADS = 2
NSA_REP = 4
N_BRANCH = 3
S5_WIDTH = 256
S5_H = 16
S5_GROUPS = 16
S5_P = 64
CMP_STRIDE = 16
CMP_LEN = 32
CMP_HIDDEN = 128
SLC_BLOCK = 64
SLC_SHIFT = 6
N_SELECT = 16
N_FORCED = 3
WINDOW = 512
Q_BLOCK = 256
ROT_DIM = 16
ROPE_THETA = 500000.0
EPS = 1e-6
Q_SCALE = HEAD_DIM ** -0.5 * math.log2(math.e)
NEG_INF = -1e30
MASK_VALUE = -(2.0 ** 100)

OFF_POOL = 0
OFF_KA = OFF_POOL + POOL_WIDTH
OFF_KC = OFF_KA + NSA_KV_HEADS * LANES
OFF_VC = OFF_KC + LANES
OFF_S5 = OFF_VC + LANES
N_COLS = OFF_S5 + S5_WIDTH
GATE_ROWS = 16
ROW_Q = 0
ROW_V = ROW_Q + NSA_HEADS * LANES
ROW_GATE = ROW_V + NSA_KV_HEADS * LANES
N_ROWS_T = ROW_GATE + NSA_KV_HEADS * GATE_ROWS

S5_CHUNK = 8
S5_FOLD = S5_CHUNK * S5_WIDTH
S5_STATE = S5_GROUPS * S5_P
S5_HALVES = S5_WIDTH // LANES
S5_HALF_FOLD = S5_FOLD // S5_HALVES
KEY_TILE = 512


def _dot(a, b, precision=None):
    return jnp.dot(a, b, preferred_element_type=F32, precision=precision)


def _dot_nt(a, b, precision=None):
    return lax.dot_general(a, b, (((1,), (1,)), ((), ())), preferred_element_type=F32,
                           precision=precision)


def _sigmoid(x):
    return 1.0 / (1.0 + jnp.exp(-x))


def _gelu_tanh(x):
    return 0.5 * x * (1.0 + jnp.tanh(math.sqrt(2.0 / math.pi) * (x + 0.044715 * (x * x * x))))


def _params(sem):
    return pltpu.CompilerParams(dimension_semantics=sem, vmem_limit_bytes=VMEM_LIMIT_BYTES)


def _const_spec(shape):
    nd = len(shape)
    return pl.BlockSpec(shape, lambda *_: (0,) * nd, pipeline_mode=pl.Buffered(1))


def _mod_kernel(c_ref, w_ref, b_ref, o_ref):
    c = c_ref[...]
    o_ref[...] = _dot(c * _sigmoid(c), w_ref[...], HIGHEST) + b_ref[...]


def _modulation(c, ada_w, ada_b):
    n_layers, d, n = ada_w.shape
    b = c.shape[0]
    tn = 1152
    c_pad = jnp.zeros((SUBLANES, d), F32).at[:b].set(c)
    out = pl.pallas_call(
        _mod_kernel,
        grid=(n_layers, n // tn),
        in_specs=[pl.BlockSpec((SUBLANES, d), lambda l, j: (0, 0)),
                  pl.BlockSpec((None, d, tn), lambda l, j: (l, 0, j)),
                  pl.BlockSpec((None, 1, tn), lambda l, j: (l, 0, j))],
        out_specs=pl.BlockSpec((None, SUBLANES, tn), lambda l, j: (l, 0, j)),
        out_shape=jax.ShapeDtypeStruct((n_layers, SUBLANES, n), F32),
        compiler_params=_params(("arbitrary", "arbitrary")),
        name="adaln_mod",
    )(c_pad, ada_w, ada_b.reshape(n_layers, 1, n))
    return out[:, :b].reshape(n_layers, b, N_MOD, d)


def _norm_modulate(x, gain, mod_ref, first_row):
    ms = jnp.mean(x * x, axis=-1, keepdims=True)
    y = x * lax.rsqrt(ms + EPS) * gain
    return y * (1.0 + mod_ref[first_row + 1:first_row + 2, :]) + mod_ref[first_row:first_row + 1, :]


def _ffn_kernel(x_ref, mod_ref, g_ref, win_ref, wout_ref, o_ref, *, first_row):
    x = x_ref[...]
    h = _norm_modulate(x, g_ref[...], mod_ref, first_row).astype(BF16)
    gu = _dot(h, win_ref[...])
    gate, up = gu[:, :D_FF], gu[:, D_FF:]
    a = (gate * _sigmoid(gate) * up).astype(BF16)
    y = _dot(a, wout_ref[...])
    o_ref[...] = x + 0.5 * mod_ref[first_row + 2:first_row + 3, :] * y


def _ffn(x, mod, gain, w_in, w_out, first_row, tm=512):
    b, s, d = x.shape
    return pl.pallas_call(
        functools.partial(_ffn_kernel, first_row=first_row),
        grid=(b, s // tm),
        in_specs=[pl.BlockSpec((None, tm, d), lambda i, j: (i, j, 0)),
                  pl.BlockSpec((None, N_MOD, d), lambda i, j: (i, 0, 0)),
                  _const_spec((1, d)),
                  _const_spec((d, 2 * D_FF)),
                  _const_spec((D_FF, d))],
        out_specs=pl.BlockSpec((None, tm, d), lambda i, j: (i, j, 0)),
        out_shape=jax.ShapeDtypeStruct((b, s, d), F32),
        compiler_params=_params(("parallel", "parallel")),
        name="ffn_half_step",
    )(x, mod, gain.reshape(1, d), w_in, w_out)


def _rope(v, cos_t, sin_lo, sin_hi):
    return (v * cos_t + pltpu.roll(v, LANES - ROT_DIM // 2, 1) * sin_lo
            + pltpu.roll(v, ROT_DIM // 2, 1) * sin_hi)


def _inproj_kernel(x_ref, mod_ref, g_ref, w_ref, wt_ref, cos_ref, slo_ref, shi_ref, cost_ref, sint_ref, qg_ref, kg_ref,
                   pool_ref, q_ref, ka_ref, kb_ref, va_ref, vb_ref, kc_ref, vc_ref, gate_ref, s5_ref):
    h = _norm_modulate(x_ref[...], g_ref[...], mod_ref, 3).astype(BF16)
    u = _dot(h, w_ref[...])
    ut = _dot_nt(wt_ref[...], h)
    tm = u.shape[0]
    cos_t, sin_lo, sin_hi = cos_ref[...], slo_ref[...], shi_ref[...]
    lane = lax.broadcasted_iota(jnp.int32, (tm, LANES), 1)
    low_half = lane < HEAD_DIM
    pos = pl.program_id(1) * tm + lax.broadcasted_iota(jnp.int32, (tm, LANES), 0)
    block_one_hot = jnp.where((pos >> SLC_SHIFT) == lane, 1.0, 0.0).astype(BF16)

    pool_ref[...] = u[:, OFF_POOL:OFF_POOL + POOL_WIDTH]
    kc_ref[...] = u[:, OFF_KC:OFF_KC + LANES]
    vc_ref[...] = u[:, OFF_VC:OFF_VC + LANES]
    for half in range(S5_HALVES):
        s5_ref[half] = u[:, OFF_S5 + half * LANES:OFF_S5 + (half + 1) * LANES]
    for g in range(NSA_KV_HEADS):
        v = u[:, OFF_KA + g * LANES:OFF_KA + (g + 1) * LANES]
        sq = v * v
        ms_lo = jnp.sum(jnp.where(low_half, sq, 0.0), axis=-1, keepdims=True) * (1.0 / HEAD_DIM)
        ms_hi = jnp.sum(jnp.where(low_half, 0.0, sq), axis=-1, keepdims=True) * (1.0 / HEAD_DIM)
        r = jnp.where(low_half, lax.rsqrt(ms_lo + EPS), lax.rsqrt(ms_hi + EPS))
        kn = _rope(v * r * kg_ref[...], cos_t, sin_lo, sin_hi)
        ka_ref[g, :, 0:LANES] = kn.astype(BF16)
        ka_ref[g, :, LANES:2 * LANES] = block_one_hot
        kb_ref[g] = pltpu.roll(kn, HEAD_DIM, 1).astype(BF16)

    half_rot = ROT_DIM // 2
    cos8, sin8 = cost_ref[...], sint_ref[...]
    for hd in range(NSA_HEADS):
        v = ut[ROW_Q + hd * LANES:ROW_Q + (hd + 1) * LANES, :]
        ms = jnp.sum(v * v, axis=0, keepdims=True) * (1.0 / HEAD_DIM)
        vn = v * lax.rsqrt(ms + EPS) * qg_ref[...]
        x1, x2 = vn[0:half_rot], vn[half_rot:ROT_DIM]
        roped = jnp.concatenate([x1 * cos8 - x2 * sin8, x2 * cos8 + x1 * sin8, vn[ROT_DIM:]], axis=0)
        q_ref[hd] = (roped * Q_SCALE).astype(BF16)
    ones = jnp.ones((HEAD_DIM, tm), F32)
    for g in range(NSA_KV_HEADS):
        vt = ut[ROW_V + g * LANES:ROW_V + (g + 1) * LANES, :]
        va = jnp.concatenate([vt[0:HEAD_DIM], ones], axis=0).astype(BF16)
        vb = jnp.concatenate([vt[HEAD_DIM:], ones], axis=0).astype(BF16)
        for blk in range(tm // LANES):
            va_ref[g, blk] = va[:, blk * LANES:(blk + 1) * LANES]
            vb_ref[g, blk] = vb[:, blk * LANES:(blk + 1) * LANES]
        gate_ref[g] = _sigmoid(ut[ROW_GATE + g * GATE_ROWS:ROW_GATE + (g + 1) * GATE_ROWS, :])


def _inproj(x, mod, gain, w_row, w_t, rope_tabs, rope_tabs_t, q_gain, k_gain, tm=256):
    b, s, d = x.shape
    g = NSA_KV_HEADS
    tok = lambda width: pl.BlockSpec((None, tm, width), lambda i, j: (i, j, 0))
    grp = lambda n, width=LANES: pl.BlockSpec((None, n, tm, width), lambda i, j: (i, 0, j, 0))
    lanes_tok = lambda n, rows: pl.BlockSpec((None, n, rows, tm), lambda i, j: (i, 0, 0, j))
    v_blocks = pl.BlockSpec((None, g, tm // LANES, LANES, LANES), lambda i, j: (i, 0, j, 0, 0))
    tab = pl.BlockSpec((tm, LANES), lambda i, j: (j, 0))
    tab_t = pl.BlockSpec((ROT_DIM // 2, tm), lambda i, j: (0, j))
    sds = jax.ShapeDtypeStruct
    return pl.pallas_call(
        _inproj_kernel,
        grid=(b, s // tm),
        in_specs=[tok(d),
                  pl.BlockSpec((None, N_MOD, d), lambda i, j: (i, 0, 0)),
                  _const_spec((1, d)),
                  _const_spec((d, N_COLS)), _const_spec((N_ROWS_T, d)),
                  tab, tab, tab, tab_t, tab_t,
                  _const_spec((LANES, 1)), _const_spec((1, LANES))],
        out_specs=[tok(POOL_WIDTH), lanes_tok(NSA_HEADS, LANES), grp(g, 2 * LANES), grp(g), v_blocks, v_blocks,
                   tok(LANES), tok(LANES), lanes_tok(g, GATE_ROWS), grp(S5_HALVES)],
        out_shape=[sds((b, s, POOL_WIDTH), F32), sds((b, NSA_HEADS, LANES, s), BF16),
                   sds((b, g, s, 2 * LANES), BF16), sds((b, g, s, LANES), BF16),
                   sds((b, g, s // LANES, LANES, LANES), BF16), sds((b, g, s // LANES, LANES, LANES), BF16),
                   sds((b, s, LANES), F32), sds((b, s, LANES), F32),
                   sds((b, g, GATE_ROWS, s), F32), sds((b, S5_HALVES, s, LANES), F32)],
        compiler_params=_params(("parallel", "parallel")),
        name="mixer_in_proj",
    )(x, mod, gain.reshape(1, d), w_row, w_t, *rope_tabs, *rope_tabs_t, q_gain, k_gain)


def _fold_rows(ref, n):
    rows = ref.shape[0] // n
    return jnp.concatenate([ref[pl.ds(k, rows, stride=n), :] for k in range(n)], axis=1)


def _unfold_rows(ref, value, n):
    rows = ref.shape[0] // n
    for k in range(n):
        ref[pl.ds(k, rows, stride=n), :] = value[:, k * LANES:(k + 1) * LANES]


def _compress_kernel(kc_ref, vc_ref, w1k_ref, w1v_ref, w1k_raw_ref, w1v_raw_ref, pe_ref, w2k_ref, w2vt_ref,
                     kg_ref, cos_ref, slo_ref, shi_ref, ko_ref, vo_ref, *, n_cmp):
    ncp = kc_ref.shape[0] // CMP_STRIDE
    for src_ref, w1_ref, raw_ref, pe_row, is_key in ((kc_ref, w1k_ref, w1k_raw_ref, 0, True),
                                                    (vc_ref, w1v_ref, w1v_raw_ref, 1, False)):
        chunks = _fold_rows(src_ref, CMP_STRIDE).astype(BF16)
        pe = jnp.broadcast_to(pe_ref[pe_row:pe_row + 1, :], (SUBLANES, CMP_LEN * HEAD_DIM))
        pe_term = _dot(pe, raw_ref[...], HIGHEST)[0:1, :]
        for g in range(NSA_KV_HEADS):
            a = _dot(chunks, w1_ref[g])
            pre = a[:, :CMP_HIDDEN] + pltpu.roll(a[:, CMP_HIDDEN:], ncp - 1, 0) + pe_term
            hidden = _gelu_tanh(pre).astype(BF16)
            if is_key:
                out = _dot(hidden, w2k_ref[...])
                ms = jnp.sum(out * out, axis=-1, keepdims=True) * (1.0 / HEAD_DIM)
                out = _rope(out * lax.rsqrt(ms + EPS) * kg_ref[...], cos_ref[...], slo_ref[...], shi_ref[...])
                real_row = lax.broadcasted_iota(jnp.int32, (ncp, LANES), 0) < n_cmp
                ko_ref[g] = jnp.where(real_row, out, 0.0).astype(BF16)
            else:
                out_t = _dot_nt(w2vt_ref[...], hidden)
                real_col = lax.broadcasted_iota(jnp.int32, (LANES, ncp), 1) < n_cmp
                value_row = lax.broadcasted_iota(jnp.int32, (LANES, ncp), 0) < HEAD_DIM
                vo_ref[g] = jnp.where(value_row, jnp.where(real_col, out_t, 0.0), 1.0).astype(BF16)


def _compress(kc, vc, w1k, w1v, w1k_raw, w1v_raw, pe, w2k, w2v_t, k_gain, cmp_tabs):
    b, s, _ = kc.shape
    ncp = s // CMP_STRIDE
    fold = CMP_STRIDE * LANES
    src = pl.BlockSpec((None, s, LANES), lambda i: (i, 0, 0))
    raw = _const_spec((CMP_LEN * HEAD_DIM, CMP_HIDDEN))
    w1 = _const_spec((NSA_KV_HEADS, fold, 2 * CMP_HIDDEN))
    w2 = _const_spec((CMP_HIDDEN, LANES))
    tab = _const_spec((ncp, LANES))
    return pl.pallas_call(
        functools.partial(_compress_kernel, n_cmp=ncp - 1),
        grid=(b,),
        in_specs=[src, src, w1, w1, raw, raw, _const_spec((2, CMP_LEN * HEAD_DIM)), w2, w2,
                  _const_spec((1, LANES)), tab, tab, tab],
        out_specs=[pl.BlockSpec((None, NSA_KV_HEADS, ncp, LANES), lambda i: (i, 0, 0, 0)),
                   pl.BlockSpec((None, NSA_KV_HEADS, LANES, ncp), lambda i: (i, 0, 0, 0))],
        out_shape=[jax.ShapeDtypeStruct((b, NSA_KV_HEADS, ncp, LANES), BF16),
                   jax.ShapeDtypeStruct((b, NSA_KV_HEADS, LANES, ncp), BF16)],
        compiler_params=_params(("parallel",)),
        name="nsa_compress",
    )(kc, vc, w1k, w1v, w1k_raw, w1v_raw, pe, w2k, w2v_t, k_gain, *cmp_tabs)


def _selection_bias(imp_t, qb):
    nsb, nq = imp_t.shape
    j = lax.broadcasted_iota(jnp.int32, (nsb, nq), 0)
    t = qb * Q_BLOCK + lax.broadcasted_iota(jnp.int32, (nsb, nq), 1)
    cur = t >> SLC_SHIFT
    valid = j * SLC_BLOCK <= t
    forced = (j == 0) | (j == cur) | (j == cur - 1)
    j_f = j.astype(F32)
    start = jnp.where(valid & jnp.logical_not(forced), imp_t, -1.0)
    vals = start
    for _ in range(N_SELECT - N_FORCED):
        m = jnp.max(vals, axis=0, keepdims=True)
        idx = jnp.min(jnp.where(vals == m, j_f, float(nsb)), axis=0, keepdims=True)
        vals = jnp.where(j_f == idx, -2.0, vals)
    return jnp.where((forced & valid) | (vals != start), 0.0, MASK_VALUE)


def _attn_kernel(q_ref, kc_ref, vct_ref, ka_ref, kb_ref, va_ref, vb_ref, gate_ref, ovl_ref, o_ref,
                 sa_ref, sb_ref, *, seq):
    qb = pl.program_id(2)
    cols = NSA_REP * Q_BLOCK
    ncp = seq // CMP_STRIDE
    q_t = jnp.concatenate([q_ref[r] for r in range(NSA_REP)], axis=1)
    t_col = qb * Q_BLOCK + lax.broadcasted_iota(jnp.int32, (1, cols), 1) % Q_BLOCK

    cmp_end = lax.broadcasted_iota(jnp.int32, (ncp, cols), 0) * CMP_STRIDE + (CMP_LEN - 1)
    s = jnp.where(cmp_end <= t_col, _dot(kc_ref[...], q_t), NEG_INF)
    e = jnp.exp2(s - jnp.max(s, axis=0, keepdims=True)).astype(BF16)
    oc = _dot(jnp.concatenate([vct_ref[...], ovl_ref[...]], axis=0), e)
    oc = oc * jnp.where(t_col >= CMP_LEN - 1, 1.0 / oc[HEAD_DIM:HEAD_DIM + 1, :], 0.0)
    o_cmp, imp = oc[0:LANES], oc[LANES:2 * LANES]
    imp_t = (imp[:, 0:Q_BLOCK] + imp[:, Q_BLOCK:2 * Q_BLOCK]
             + imp[:, 2 * Q_BLOCK:3 * Q_BLOCK] + imp[:, 3 * Q_BLOCK:4 * Q_BLOCK])
    bias = _selection_bias(imp_t, qb).astype(BF16)

    q_aug = jnp.concatenate([q_t, jnp.concatenate([bias] * NSA_REP, axis=1)], axis=0)

    span = WINDOW + Q_BLOCK
    start = pl.multiple_of(jnp.maximum(qb * Q_BLOCK - WINDOW, 0), LANES)
    start_blk = start // LANES
    kpos = start + lax.broadcasted_iota(jnp.int32, (span, cols), 0)
    s = jnp.where((kpos <= t_col) & (kpos > t_col - WINDOW), _dot(kb_ref[pl.ds(start, span), :], q_t), NEG_INF)
    e = jnp.exp2(s - jnp.max(s, axis=0, keepdims=True)).astype(BF16)
    acc_win = _dot(jnp.concatenate([vb_ref[start_blk + i] for i in range(span // LANES)], axis=1), e)
    o_win = acc_win * (1.0 / acc_win[HEAD_DIM:HEAD_DIM + 1, :])

    n_key_tiles = seq // KEY_TILE
    blocks_per_tile = KEY_TILE // LANES

    def score_tile(kt, s_ref):
        k0 = pl.multiple_of(jnp.minimum(kt, n_key_tiles - 1) * KEY_TILE, KEY_TILE)
        s_ref[...] = _dot(ka_ref[pl.ds(k0, KEY_TILE), :], q_aug)

    def absorb_tile(kt, s_ref, carry, causal):
        m_i, acc = carry
        blk0 = jnp.minimum(kt, n_key_tiles - 1) * blocks_per_tile
        s = s_ref[...]
        if causal:
            kpos = kt * KEY_TILE + lax.broadcasted_iota(jnp.int32, (KEY_TILE, cols), 0)
            s = jnp.where(kpos <= t_col, s, MASK_VALUE)
        m_new = jnp.maximum(m_i, jnp.max(s, axis=0, keepdims=True))
        p = jnp.exp2(s - m_new).astype(BF16)
        v_t = jnp.concatenate([va_ref[blk0 + i] for i in range(blocks_per_tile)], axis=1)
        return m_new, jnp.exp2(m_i - m_new) * acc + _dot(v_t, p)

    def slc_trip(j, carry):
        score_tile(2 * j + 1, sb_ref)
        carry = absorb_tile(2 * j, sa_ref, carry, causal=False)
        score_tile(2 * j + 2, sa_ref)
        return absorb_tile(2 * j + 1, sb_ref, carry, causal=False)

    last_pair = qb // (2 * KEY_TILE // Q_BLOCK)
    score_tile(0, sa_ref)
    carry = lax.fori_loop(0, last_pair, slc_trip,
                          (jnp.full((1, cols), NEG_INF, F32), jnp.zeros((LANES, cols), F32)))
    score_tile(2 * last_pair + 1, sb_ref)
    carry = absorb_tile(2 * last_pair, sa_ref, carry, causal=True)
    _, acc_slc = absorb_tile(2 * last_pair + 1, sb_ref, carry, causal=True)
    o_slc = acc_slc * (1.0 / acc_slc[HEAD_DIM:HEAD_DIM + 1, :])

    gates = gate_ref[...]
    heads = []
    for r in range(NSA_REP):
        cs = slice(r * Q_BLOCK, (r + 1) * Q_BLOCK)
        c = r * N_BRANCH
        heads.append(gates[c:c + 1, :] * o_cmp[0:HEAD_DIM, cs] + gates[c + 1:c + 2, :] * o_slc[0:HEAD_DIM, cs]
                     + gates[c + 2:c + 3, :] * o_win[0:HEAD_DIM, cs])
    o_ref[...] = jnp.concatenate(heads, axis=0).T


def _overlap_matrix(seq):
    ncp, nsb = seq // CMP_STRIDE, seq // SLC_BLOCK
    c_start = np.arange(ncp)[None, :] * CMP_STRIDE
    s_start = np.arange(nsb)[:, None] * SLC_BLOCK
    ovl = np.clip(np.minimum(c_start + CMP_LEN, s_start + SLC_BLOCK) - np.maximum(c_start, s_start), 0, None)
    ovl = ovl.astype(np.float32) / CMP_LEN
    ovl[:, ncp - 1] = 0.0
    return jnp.asarray(np.pad(ovl, ((0, LANES - nsb), (0, 0))))


def _nsa_attention(q_t, k_cmp, v_cmp_t, ka, kb, va_t, vb_t, gates_t):
    b, _, _, s = q_t.shape
    g = NSA_KV_HEADS
    ncp = s // CMP_STRIDE
    assert s // SLC_BLOCK <= LANES
    per_group = lambda *shape: pl.BlockSpec((None, None) + shape, lambda i, j, k: (i, j) + (0,) * len(shape))
    return pl.pallas_call(
        functools.partial(_attn_kernel, seq=s),
        grid=(b, g, s // Q_BLOCK),
        in_specs=[pl.BlockSpec((None, NSA_REP, LANES, Q_BLOCK), lambda i, j, k: (i, j, 0, k)),
                  per_group(ncp, LANES), per_group(LANES, ncp), per_group(s, 2 * LANES), per_group(s, LANES),
                  per_group(s // LANES, LANES, LANES), per_group(s // LANES, LANES, LANES),
                  pl.BlockSpec((None, None, GATE_ROWS, Q_BLOCK), lambda i, j, k: (i, j, 0, k)),
                  _const_spec((LANES, ncp))],
        out_specs=pl.BlockSpec((None, Q_BLOCK, NSA_REP * HEAD_DIM), lambda i, j, k: (i, k, j)),
        out_shape=jax.ShapeDtypeStruct((b, s, NSA_WIDTH), F32),
        scratch_shapes=[pltpu.VMEM((KEY_TILE, NSA_REP * Q_BLOCK), F32)] * 2,
        compiler_params=_params(("parallel", "parallel", "arbitrary")),
        name="nsa_attention",
    )(q_t, k_cmp, v_cmp_t, ka, kb, va_t, vb_t, gates_t, _overlap_matrix(s).astype(BF16))


def _s5_kernel(u_ref, mt_ref, bc_ref, cc_ref, a1_ref, a2_ref, d_ref, y_ref, h_ref, g_scr, hp_scr):
    @pl.when(pl.program_id(1) == 0)
    def _():
        h_ref[...] = jnp.zeros_like(h_ref)

    halves = range(S5_HALVES)
    u = [_fold_rows(u_ref.at[a], S5_CHUNK) for a in halves]
    ub = [x.astype(BF16) for x in u]
    y_local = [_dot(ub[a], mt_ref[a]) for a in halves]
    for a in halves:
        g_scr[:, a * S5_HALF_FOLD:(a + 1) * S5_HALF_FOLD] = _dot(ub[a], bc_ref[a])
    a1, a2 = a1_ref[...], a2_ref[...]
    n_state = S5_STATE // S5_HALVES

    def swap_re_im(h):
        parts = [h[:, k * n_state:(k + 1) * n_state] for k in range(2 * S5_HALVES)]
        return jnp.concatenate([parts[k ^ 1] for k in range(2 * S5_HALVES)], axis=1)

    def step(i, h):
        hp_scr[pl.ds(i, 1), :] = h
        return a1 * h + a2 * swap_re_im(h) + g_scr[pl.ds(i, 1), :]

    h_ref[...] = lax.fori_loop(0, g_scr.shape[0], step, h_ref[...])
    for a in halves:
        carried = _dot(hp_scr[:, a * S5_HALF_FOLD:(a + 1) * S5_HALF_FOLD].astype(BF16), cc_ref[a])
        _unfold_rows(y_ref.at[a], y_local[a] + carried + u[a] * d_ref[a:a + 1, :], S5_CHUNK)


def _s5_scan(u, mats):
    b, _, s, _ = u.shape
    rows = s // S5_CHUNK
    tc = min(128, rows)
    tile = pl.BlockSpec((None, S5_HALVES, tc * S5_CHUNK, LANES), lambda i, j: (i, 0, j, 0))
    mat = _const_spec((S5_HALVES, S5_HALF_FOLD, S5_HALF_FOLD))
    row = _const_spec((1, S5_HALVES * S5_HALF_FOLD))
    return pl.pallas_call(
        _s5_kernel,
        grid=(b, rows // tc),
        in_specs=[tile, mat, mat, mat, row, row, _const_spec((S5_HALVES, S5_HALF_FOLD))],
        out_specs=tile,
        out_shape=jax.ShapeDtypeStruct((b, S5_HALVES, s, LANES), F32),
        scratch_shapes=[pltpu.VMEM((1, S5_HALVES * S5_HALF_FOLD), F32),
                        pltpu.VMEM((tc, S5_HALVES * S5_HALF_FOLD), F32),
                        pltpu.VMEM((tc, S5_HALVES * S5_HALF_FOLD), F32)],
        compiler_params=_params(("parallel", "arbitrary")),
        name="s5_scan",
    )(u, *mats)


def _s5_matrices(lam_re, lam_im, log_dt, b_re, b_im, c_re, c_im, d_skip):
    t0, ng, nh, npm = S5_CHUNK, S5_GROUPS, S5_H, S5_P
    gh = ng // S5_HALVES
    ein = functools.partial(jnp.einsum, precision=HIGHEST)
    lam = lax.complex(lam_re, lam_im)
    step = jnp.exp(log_dt)[:, None]
    lam_bar = jnp.exp(lam * step)
    b_bar = lax.complex(b_re, b_im) * ((lam_bar - 1.0) / lam)[..., None]
    c_mat = lax.complex(c_re, c_im)
    k = jnp.arange(t0 + 1, dtype=F32)[:, None, None]
    pw = jnp.exp((lam * step)[None] * k)

    def same_group(rows_per_group, cols_per_group):
        r = np.arange(gh * rows_per_group)[:, None] // rows_per_group
        c = np.arange(gh * cols_per_group)[None, :] // cols_per_group
        return jnp.asarray((r == c).astype(np.float32))

    def per_group_blocks(x, rows_per_group, cols_per_group):
        return jnp.tile(x, (1,) * (x.ndim - 1) + (gh,)) * same_group(rows_per_group, cols_per_group)

    kern = jnp.real(ein('ghp,kgp,gpq->kghq', c_mat, pw[:t0], b_bar))
    d_k = per_group_blocks(kern.transpose(0, 1, 3, 2).reshape(t0, S5_HALVES, gh * nh, nh), nh, nh)
    d_k = jnp.concatenate([d_k, jnp.zeros_like(d_k[:1])], axis=0)
    lag = np.arange(t0)[None, :] - np.arange(t0)[:, None]
    mt = d_k[np.where(lag >= 0, lag, t0)]
    mt = mt.transpose(2, 0, 3, 1, 4).reshape(S5_HALVES, S5_HALF_FOLD, S5_HALF_FOLD)
    b_j = (pw[t0 - 1 - np.arange(t0)][..., None] * b_bar[None]).transpose(0, 1, 3, 2)
    b_j = b_j.reshape(t0, S5_HALVES, gh * nh, npm)
    bc = jnp.concatenate([per_group_blocks(jnp.real(b_j), nh, npm), per_group_blocks(jnp.imag(b_j), nh, npm)],
                         axis=-1)
    bc = bc.transpose(1, 0, 2, 3).reshape(S5_HALVES, S5_HALF_FOLD, S5_HALF_FOLD)
    c_i = (c_mat[None] * pw[1:t0 + 1][:, :, None, :]).transpose(1, 3, 0, 2)
    c_i = c_i.reshape(S5_HALVES, gh * npm, t0, nh)
    mask = same_group(npm, nh)[:, None, :]
    cc = jnp.concatenate([jnp.tile(jnp.real(c_i), (1, 1, 1, gh)) * mask,
                          jnp.tile(-jnp.imag(c_i), (1, 1, 1, gh)) * mask], axis=1)
    cc = cc.reshape(S5_HALVES, S5_HALF_FOLD, S5_HALF_FOLD)
    a_chunk = pw[t0].reshape(S5_HALVES, gh * npm)
    a1 = jnp.concatenate([jnp.real(a_chunk), jnp.real(a_chunk)], axis=1).reshape(1, -1)
    a2 = jnp.concatenate([-jnp.imag(a_chunk), jnp.imag(a_chunk)], axis=1).reshape(1, -1)
    d_vec = jnp.tile(d_skip.reshape(S5_HALVES, gh * nh), (1, t0))
    return mt.astype(BF16), bc.astype(BF16), cc.astype(BF16), a1, a2, d_vec


def _rms_gain(y, gain):
    return y * lax.rsqrt(jnp.mean(y * y, axis=-1, keepdims=True) + EPS) * gain


def _outproj_kernel(x_ref, mod_ref, pool_ref, halo_ref, nsa_ref, s5_ref, pw_ref, pb_ref, ps_ref, on_ref,
                    gw_ref, gb_ref, wo_ref, o_ref, buf):
    j = pl.program_id(1)
    tm = x_ref.shape[0]
    v = pool_ref[...]
    buf[0:POOL_HALO, :] = jnp.where(j > 0, halo_ref[...], 0.0)
    buf[POOL_HALO:POOL_HALO + tm, :] = v
    lane_group = lax.broadcasted_iota(jnp.int32, (tm, POOL_WIDTH), 1) >> SLC_SHIFT
    t1 = (j * tm + 1 + lax.broadcasted_iota(jnp.int32, (tm, 1), 0)).astype(F32)
    run, k, pooled = v, 1, jnp.zeros_like(v)
    for gi, w in enumerate(POOL_WINDOWS):
        while k < w:
            run = run + buf[POOL_HALO - k:POOL_HALO - k + tm, :]
            k += 1
        pooled = jnp.where(lane_group == gi, run / jnp.minimum(t1, float(w)) - v, pooled)
    y_pool = (_dot(pooled.astype(BF16), pw_ref[...]) + pb_ref[...]) * ps_ref[...]

    y = _gelu_tanh(jnp.concatenate([s5_ref[half] for half in range(S5_HALVES)], axis=-1))
    y_s5 = y * _sigmoid(_dot(y.astype(BF16), gw_ref[...]) + gb_ref[...])

    cat = jnp.concatenate(
        [_rms_gain(y_pool, on_ref[:, 0:POOL_WIDTH]),
         _rms_gain(nsa_ref[...], on_ref[:, POOL_WIDTH:POOL_WIDTH + NSA_WIDTH]),
         _rms_gain(y_s5, on_ref[:, POOL_WIDTH + NSA_WIDTH:])], axis=-1).astype(BF16)
    o_ref[...] = x_ref[...] + mod_ref[5:6, :] * _dot(cat, wo_ref[...])


def _outproj(x, mod, u_pool, o_nsa, y_s5, pool_w_bd, pool_b, pool_scale, out_norm, glu_w, glu_b, w_out, tm=256):
    b, s, d = x.shape
    tok = lambda width: pl.BlockSpec((None, tm, width), lambda i, j: (i, j, 0))
    halo_blocks = tm // POOL_HALO
    return pl.pallas_call(
        _outproj_kernel,
        grid=(b, s // tm),
        in_specs=[tok(d),
                  pl.BlockSpec((None, N_MOD, d), lambda i, j: (i, 0, 0)),
                  tok(POOL_WIDTH),
                  pl.BlockSpec((None, POOL_HALO, POOL_WIDTH),
                               lambda i, j: (i, jnp.maximum(j * halo_blocks - 1, 0), 0)),
                  tok(NSA_WIDTH),
                  pl.BlockSpec((None, S5_HALVES, tm, LANES), lambda i, j: (i, 0, j, 0)),
                  _const_spec((POOL_WIDTH, POOL_WIDTH)), _const_spec((1, POOL_WIDTH)),
                  _const_spec((1, POOL_WIDTH)), _const_spec((1, d)),
                  _const_spec((S5_WIDTH, S5_WIDTH)), _const_spec((1, S5_WIDTH)),
                  _const_spec((d, d))],
        out_specs=tok(d),
        out_shape=jax.ShapeDtypeStruct((b, s, d), F32),
        scratch_shapes=[pltpu.VMEM((POOL_HALO + tm, POOL_WIDTH), F32)],
        compiler_params=_params(("parallel", "arbitrary")),
        name="mixer_out_proj",
    )(x, mod, u_pool, u_pool, o_nsa, y_s5, pool_w_bd, pool_b, pool_scale, out_norm, glu_w, glu_b, w_out)


def _rope_tables(pos):
    half = ROT_DIM // 2
    inv_freq = jnp.exp(-math.log(ROPE_THETA) * jnp.arange(half, dtype=F32) * (2.0 / ROT_DIM))
    ang = pos[:, None] * inv_freq[None, :]
    cos, sin = jnp.cos(ang), jnp.sin(ang)
    n = pos.shape[0]
    rest = HEAD_DIM - ROT_DIM
    cos_t = jnp.concatenate([cos, cos, jnp.ones((n, rest), F32)], axis=1)
    sin_lo = jnp.concatenate([-sin, jnp.zeros((n, half + rest), F32)], axis=1)
    sin_hi = jnp.concatenate([jnp.zeros((n, half), F32), sin, jnp.zeros((n, rest), F32)], axis=1)
    return tuple(jnp.tile(t, (1, 2)) for t in (cos_t, sin_lo, sin_hi))


def _pad_lanes(v):
    return jnp.pad(v, [(0, 0)] * (v.ndim - 1) + [(0, LANES - v.shape[-1])])


def _rope_tables_t(pos):
    half = ROT_DIM // 2
    inv_freq = jnp.exp(-math.log(ROPE_THETA) * jnp.arange(half, dtype=F32) * (2.0 / ROT_DIM))
    ang = inv_freq[:, None] * pos[None, :]
    return jnp.cos(ang), jnp.sin(ang)


def _arrange_w_in(w):
    d = w.shape[0]
    o1, o2, o3 = POOL_WIDTH, POOL_WIDTH + NSA_WIDTH, POOL_WIDTH + NSA_WIDTH + 6 * LANES
    kv = w[:, o2:o3].reshape(d, 6, NSA_KV_HEADS, HEAD_DIM)
    ka = jnp.concatenate([kv[:, 2], kv[:, 4]], axis=-1).reshape(d, NSA_KV_HEADS * LANES)
    n_gate = NSA_REP * N_BRANCH
    w_row = jnp.concatenate([w[:, :o1], ka, kv[:, 0].reshape(d, LANES), kv[:, 1].reshape(d, LANES),
                             w[:, o3 + NSA_KV_HEADS * n_gate:]], axis=1)
    q_t = w[:, o1:o2].T.reshape(NSA_HEADS, HEAD_DIM, d)
    q_t = jnp.pad(q_t, ((0, 0), (0, LANES - HEAD_DIM), (0, 0))).reshape(NSA_HEADS * LANES, d)
    v_t = jnp.concatenate([kv[:, 3], kv[:, 5]], axis=-1).reshape(d, NSA_KV_HEADS * LANES).T
    gate_t = w[:, o3:o3 + NSA_KV_HEADS * n_gate].T.reshape(NSA_KV_HEADS, n_gate, d)
    gate_t = jnp.pad(gate_t, ((0, 0), (0, GATE_ROWS - n_gate), (0, 0))).reshape(NSA_KV_HEADS * GATE_ROWS, d)
    return w_row.astype(BF16), jnp.concatenate([q_t, v_t, gate_t], axis=0).astype(BF16)


def _expand_cmp_w1(w1):
    halves = w1.reshape(2, CMP_STRIDE, HEAD_DIM, CMP_HIDDEN)
    both = jnp.concatenate([halves[0], halves[1]], axis=-1)
    out = []
    for g in range(NSA_KV_HEADS):
        z = jnp.zeros((CMP_STRIDE, NSA_KV_HEADS, HEAD_DIM, 2 * CMP_HIDDEN), F32).at[:, g].set(both)
        out.append(z.reshape(CMP_STRIDE * LANES, 2 * CMP_HIDDEN))
    return jnp.stack(out).astype(BF16)


def _block_diag(w):
    g, c, _ = w.shape
    return jnp.einsum('gcd,gf->gcfd', w, jnp.eye(g, dtype=w.dtype)).reshape(g * c, g * c)


def _hybrid_layer(x, mod, tabs, tabs_t, cmp_tabs, norm_ffn1, ffn1_w_in, ffn1_w_out, norm_mix, w_in, w_out, out_norm,
                  pool_w, pool_b, pool_scale, q_norm, k_norm, cmp_pe, cmp_k_w1, cmp_k_w2, cmp_v_w1, cmp_v_w2,
                  s5_lam_re, s5_lam_im, s5_log_dt, s5_b_re, s5_b_im, s5_c_re, s5_c_im, s5_d, glu_w, glu_b,
                  norm_ffn2, ffn2_w_in, ffn2_w_out):
    x = _ffn(x, mod, norm_ffn1, ffn1_w_in.astype(BF16), ffn1_w_out.astype(BF16), 0)

    q_gain = _pad_lanes(q_norm.reshape(1, HEAD_DIM)).reshape(LANES, 1)
    slc_win_gain = jnp.concatenate([k_norm[1], k_norm[2]]).reshape(1, LANES)
    u_pool, q_t, ka, kb, va_t, vb_t, kc, vc, gates_t, u_s5 = _inproj(
        x, mod, norm_mix, *_arrange_w_in(w_in), tabs, tabs_t, q_gain, slc_win_gain)

    k_cmp, v_cmp_t = _compress(
        kc, vc, _expand_cmp_w1(cmp_k_w1), _expand_cmp_w1(cmp_v_w1), cmp_k_w1, cmp_v_w1,
        cmp_pe.reshape(2, CMP_LEN * HEAD_DIM), _pad_lanes(cmp_k_w2).astype(BF16),
        _pad_lanes(cmp_v_w2).T.astype(BF16), _pad_lanes(k_norm[0].reshape(1, HEAD_DIM)), cmp_tabs)
    o_nsa = _nsa_attention(q_t, k_cmp, v_cmp_t, ka, kb, va_t, vb_t, gates_t)

    y_s5 = _s5_scan(u_s5, _s5_matrices(s5_lam_re, s5_lam_im, s5_log_dt, s5_b_re, s5_b_im,
                                       s5_c_re, s5_c_im, s5_d))

    x = _outproj(x, mod, u_pool, o_nsa, y_s5, _block_diag(pool_w).astype(BF16),
                 pool_b.reshape(1, POOL_WIDTH), pool_scale.reshape(1, POOL_WIDTH),
                 out_norm.reshape(1, D_MODEL), glu_w.astype(BF16), glu_b.reshape(1, S5_WIDTH),
                 w_out.astype(BF16))
    return _ffn(x, mod, norm_ffn2, ffn2_w_in.astype(BF16), ffn2_w_out.astype(BF16), 6)


def kernel(x, c, ada_w, ada_b, norm_ffn1, ffn1_w_in, ffn1_w_out, norm_mix, w_in, w_out, out_norm, pool_w, pool_b, pool_scale, q_norm, k_norm, cmp_pe, cmp_k_w1, cmp_k_w2, cmp_v_w1, cmp_v_w2, s5_lam_re, s5_lam_im, s5_log_dt, s5_b_re, s5_b_im, s5_c_re, s5_c_im, s5_d, glu_w, glu_b, norm_ffn2, ffn2_w_in, ffn2_w_out):
    seq = x.shape[1]
    assert seq % (KEY_TILE * 4) == 0 and seq >= WINDOW + Q_BLOCK
    mod = _modulation(c, ada_w, ada_b)
    tabs = _rope_tables(jnp.arange(seq, dtype=F32))
    tabs_t = _rope_tables_t(jnp.arange(seq, dtype=F32))
    n_cmp_pad = seq // CMP_STRIDE
    cmp_tabs = _rope_tables((jnp.arange(n_cmp_pad) * CMP_STRIDE + CMP_LEN - 1).astype(F32))
    per_layer = (norm_ffn1, ffn1_w_in, ffn1_w_out, norm_mix, w_in, w_out, out_norm, pool_w, pool_b, pool_scale,
                 q_norm, k_norm, cmp_pe, cmp_k_w1, cmp_k_w2, cmp_v_w1, cmp_v_w2, s5_lam_re, s5_lam_im,
                 s5_log_dt, s5_b_re, s5_b_im, s5_c_re, s5_c_im, s5_d, glu_w, glu_b, norm_ffn2, ffn2_w_in,
                 ffn2_w_out)
    for l in range(ada_w.shape[0]):
        x = _hybrid_layer(x, mod[l], tabs, tabs_t, cmp_tabs, *[p[l] for p in per_layer])
    return x
```

```python
import functools
import math

import jax
import jax.numpy as jnp
import numpy as np
from jax import lax
from jax.experimental import pallas as pl
from jax.experimental.pallas import tpu as pltpu

F32 = jnp.float32
BF16 = jnp.bfloat16
HIGHEST = lax.Precision.HIGHEST

LANES = 128
SUBLANES = 8
VMEM_LIMIT_BYTES = 56 * 1024 * 1024

D_MODEL = 1024
D_FF = 2816
N_MOD = 9
POOL_WIDTH = 256
POOL_GC = 64
POOL_WINDOWS = (2, 4, 8, 16)
POOL_HALO = 16
HEAD_DIM = 64
NSA_WIDTH = 512
NSA_HEADS = 8
NSA_KV_HEADS = 2
NSA_REP = 4
N_BRANCH = 3
S5_WIDTH = 256
S5_H = 16
S5_GROUPS = 16
S5_P = 64
CMP_STRIDE = 16
CMP_LEN = 32
CMP_HIDDEN = 128
SLC_BLOCK = 64
SLC_SHIFT = 6
N_SELECT = 16
N_FORCED = 3
WINDOW = 512
Q_BLOCK = 256
ROT_DIM = 16
ROPE_THETA = 500000.0
EPS = 1e-6
Q_SCALE = HEAD_DIM ** -0.5 * math.log2(math.e)
NEG_INF = -1e30
MASK_VALUE = -(2.0 ** 100)

OFF_POOL = 0
OFF_KA = OFF_POOL + POOL_WIDTH
OFF_KC = OFF_KA + NSA_KV_HEADS * LANES
OFF_VC = OFF_KC + LANES
OFF_S5 = OFF_VC + LANES
N_COLS = OFF_S5 + S5_WIDTH
GATE_ROWS = 16
ROW_Q = 0
ROW_V = ROW_Q + NSA_HEADS * HEAD_DIM
ROW_GATE = ROW_V + NSA_KV_HEADS * LANES
N_ROWS_T = ROW_GATE + NSA_KV_HEADS * GATE_ROWS

S5_CHUNK = 8
S5_FOLD = S5_CHUNK * S5_WIDTH
S5_STATE = S5_GROUPS * S5_P
S5_HALVES = S5_WIDTH // LANES
S5_HALF_FOLD = S5_FOLD // S5_HALVES
KEY_TILE = 512
V_ROWS = HEAD_DIM + 16


def _dot(a, b, precision=None):
    return jnp.dot(a, b, preferred_element_type=F32, precision=precision)


def _dot_nt(a, b, precision=None):
    return lax.dot_general(a, b, (((1,), (1,)), ((), ())), preferred_element_type=F32,
                           precision=precision)


def _sigmoid(x):
    return 1.0 / (1.0 + jnp.exp(-x))


def _gelu_tanh(x):
    return 0.5 * x * (1.0 + jnp.tanh(math.sqrt(2.0 / math.pi) * (x + 0.044715 * (x * x * x))))


def _params(sem):
    return pltpu.CompilerParams(dimension_semantics=sem, vmem_limit_bytes=VMEM_LIMIT_BYTES)


def _const_spec(shape):
    nd = len(shape)
    return pl.BlockSpec(shape, lambda *_: (0,) * nd, pipeline_mode=pl.Buffered(1))


def _layer_spec(shape, layer):
    nd = len(shape)
    return pl.BlockSpec((None,) + shape, lambda *_: (layer,) + (0,) * nd, pipeline_mode=pl.Buffered(1))


def _mod_kernel(c_ref, w_ref, b_ref, o_ref):
    c = c_ref[...]
    o_ref[...] = _dot(c * _sigmoid(c), w_ref[...], HIGHEST) + b_ref[...]


def _modulation(c, ada_w, ada_b):
    n_layers, d, n = ada_w.shape
    b = c.shape[0]
    tn = 1152
    c_pad = jnp.zeros((SUBLANES, d), F32).at[:b].set(c)
    out = pl.pallas_call(
        _mod_kernel,
        grid=(n_layers, n // tn),
        in_specs=[pl.BlockSpec((SUBLANES, d), lambda l, j: (0, 0)),
                  pl.BlockSpec((None, d, tn), lambda l, j: (l, 0, j)),
                  pl.BlockSpec((None, 1, tn), lambda l, j: (l, 0, j))],
        out_specs=pl.BlockSpec((None, SUBLANES, tn), lambda l, j: (l, 0, j)),
        out_shape=jax.ShapeDtypeStruct((n_layers, SUBLANES, n), F32),
        compiler_params=_params(("arbitrary", "arbitrary")),
        name="adaln_mod",
    )(c_pad, ada_w, ada_b.reshape(n_layers, 1, n))
    return out[:, :b].reshape(n_layers, b, N_MOD, d)


def _norm_modulate(x, gain, mod_ref, first_row):
    ms = jnp.mean(x * x, axis=-1, keepdims=True)
    y = x * lax.rsqrt(ms + EPS) * gain
    return y * (1.0 + mod_ref[first_row + 1:first_row + 2, :]) + mod_ref[first_row:first_row + 1, :]


def _ffn_kernel(x_ref, mod_ref, g_ref, win_ref, wout_ref, o_ref, *, first_row):
    x = x_ref[...]
    h = _norm_modulate(x, g_ref[...], mod_ref, first_row).astype(BF16)
    gu = _dot(h, win_ref[...])
    gate, up = gu[:, :D_FF], gu[:, D_FF:]
    a = (gate * _sigmoid(gate) * up).astype(BF16)
    y = _dot(a, wout_ref[...])
    o_ref[...] = x + 0.5 * mod_ref[first_row + 2:first_row + 3, :] * y


def _mod_spec(layer):
    return pl.BlockSpec((None, None, N_MOD, D_MODEL), lambda i, *_: (layer, i, 0, 0))


def _ffn(x, layer, mod, gain, w_in, w_out, first_row, tm=512):
    b, s, d = x.shape
    return pl.pallas_call(
        functools.partial(_ffn_kernel, first_row=first_row),
        grid=(b, s // tm),
        in_specs=[pl.BlockSpec((None, tm, d), lambda i, j: (i, j, 0)),
                  _mod_spec(layer),
                  _layer_spec((1, d), layer),
                  _layer_spec((d, 2 * D_FF), layer),
                  _layer_spec((D_FF, d), layer)],
        out_specs=pl.BlockSpec((None, tm, d), lambda i, j: (i, j, 0)),
        out_shape=jax.ShapeDtypeStruct((b, s, d), F32),
        compiler_params=_params(("parallel", "parallel")),
        name="ffn_half_step",
    )(x, mod, gain, w_in, w_out)


def _rope(v, cos_t, sin_lo, sin_hi):
    return (v * cos_t + pltpu.roll(v, LANES - ROT_DIM // 2, 1) * sin_lo
            + pltpu.roll(v, ROT_DIM // 2, 1) * sin_hi)


def _inproj_kernel(x_ref, mod_ref, g_ref, w_ref, wt_ref, cos_ref, slo_ref, shi_ref, cost_ref, sint_ref, qg_ref, kg_ref,
                   pool_ref, q_ref, ka_ref, kb_ref, va_ref, vb_ref, kc_ref, vc_ref, gate_ref, s5_ref):
    h = _norm_modulate(x_ref[...], g_ref[...], mod_ref, 3).astype(BF16)
    u = _dot(h, w_ref[...])
    ut = _dot_nt(wt_ref[...], h)
    tm = u.shape[0]
    cos_t, sin_lo, sin_hi = cos_ref[...], slo_ref[...], shi_ref[...]
    lane = lax.broadcasted_iota(jnp.int32, (tm, LANES), 1)
    low_half = lane < HEAD_DIM
    pos = pl.program_id(1) * tm + lax.broadcasted_iota(jnp.int32, (tm, LANES), 0)
    block_one_hot = jnp.where((pos >> SLC_SHIFT) == lane, 1.0, 0.0).astype(BF16)

    pool_ref[...] = u[:, OFF_POOL:OFF_POOL + POOL_WIDTH]
    kc_ref[...] = u[:, OFF_KC:OFF_KC + LANES]
    vc_ref[...] = u[:, OFF_VC:OFF_VC + LANES]
    for half in range(S5_HALVES):
        s5_ref[half] = u[:, OFF_S5 + half * LANES:OFF_S5 + (half + 1) * LANES]
    for g in range(NSA_KV_HEADS):
        v = u[:, OFF_KA + g * LANES:OFF_KA + (g + 1) * LANES]
        sq = v * v
        ms_lo = jnp.sum(jnp.where(low_half, sq, 0.0), axis=-1, keepdims=True) * (1.0 / HEAD_DIM)
        ms_hi = jnp.sum(jnp.where(low_half, 0.0, sq), axis=-1, keepdims=True) * (1.0 / HEAD_DIM)
        r = jnp.where(low_half, lax.rsqrt(ms_lo + EPS), lax.rsqrt(ms_hi + EPS))
        kn = _rope(v * r * kg_ref[...], cos_t, sin_lo, sin_hi)
        ka_ref[g, :, 0:LANES] = kn.astype(BF16)
        ka_ref[g, :, LANES:2 * LANES] = block_one_hot
        kb_ref[g] = pltpu.roll(kn, HEAD_DIM, 1).astype(BF16)

    half_rot = ROT_DIM // 2
    cos8, sin8 = cost_ref[...], sint_ref[...]
    zero_rows = jnp.zeros((LANES - HEAD_DIM, tm), F32)
    for hd in range(NSA_HEADS):
        v = ut[ROW_Q + hd * HEAD_DIM:ROW_Q + (hd + 1) * HEAD_DIM, :]
        ms = jnp.sum(v * v, axis=0, keepdims=True) * (1.0 / HEAD_DIM)
        vn = v * lax.rsqrt(ms + EPS) * qg_ref[...]
        x1, x2 = vn[0:half_rot], vn[half_rot:ROT_DIM]
        roped = jnp.concatenate([x1 * cos8 - x2 * sin8, x2 * cos8 + x1 * sin8, vn[ROT_DIM:]], axis=0)
        q_ref[hd] = jnp.concatenate([roped * Q_SCALE, zero_rows], axis=0).astype(BF16)
    ones = jnp.ones((HEAD_DIM, tm), F32)
    for g in range(NSA_KV_HEADS):
        vt = ut[ROW_V + g * LANES:ROW_V + (g + 1) * LANES, :]
        va = jnp.concatenate([vt[0:HEAD_DIM], ones], axis=0).astype(BF16)
        vb = jnp.concatenate([vt[HEAD_DIM:], ones], axis=0).astype(BF16)
        for blk in range(tm // LANES):
            va_ref[g, blk] = va[:, blk * LANES:(blk + 1) * LANES]
            vb_ref[g, blk] = vb[:, blk * LANES:(blk + 1) * LANES]
        gate_ref[g] = _sigmoid(ut[ROW_GATE + g * GATE_ROWS:ROW_GATE + (g + 1) * GATE_ROWS, :])


def _inproj(x, layer, mod, gain, w_row, w_t, rope_tabs, rope_tabs_t, q_gain, k_gain, tm=256):
    b, s, d = x.shape
    g = NSA_KV_HEADS
    tok = lambda width: pl.BlockSpec((None, tm, width), lambda i, j: (i, j, 0))
    grp = lambda n, width=LANES: pl.BlockSpec((None, n, tm, width), lambda i, j: (i, 0, j, 0))
    lanes_tok = lambda n, rows: pl.BlockSpec((None, n, rows, tm), lambda i, j: (i, 0, 0, j))
    v_blocks = pl.BlockSpec((None, g, tm // LANES, LANES, LANES), lambda i, j: (i, 0, j, 0, 0))
    tab = pl.BlockSpec((tm, LANES), lambda i, j: (j, 0))
    tab_t = pl.BlockSpec((ROT_DIM // 2, tm), lambda i, j: (0, j))
    sds = jax.ShapeDtypeStruct
    return pl.pallas_call(
        _inproj_kernel,
        grid=(b, s // tm),
        in_specs=[tok(d),
                  _mod_spec(layer),
                  _layer_spec((1, d), layer),
                  _layer_spec((d, N_COLS), layer), _layer_spec((N_ROWS_T, d), layer),
                  tab, tab, tab, tab_t, tab_t,
                  _layer_spec((HEAD_DIM, 1), layer), _layer_spec((1, LANES), layer)],
        out_specs=[tok(POOL_WIDTH), lanes_tok(NSA_HEADS, LANES), grp(g, 2 * LANES), grp(g), v_blocks, v_blocks,
                   tok(LANES), tok(LANES), lanes_tok(g, GATE_ROWS), grp(S5_HALVES)],
        out_shape=[sds((b, s, POOL_WIDTH), F32), sds((b, NSA_HEADS, LANES, s), BF16),
                   sds((b, g, s, 2 * LANES), BF16), sds((b, g, s, LANES), BF16),
                   sds((b, g, s // LANES, LANES, LANES), BF16), sds((b, g, s // LANES, LANES, LANES), BF16),
                   sds((b, s, LANES), F32), sds((b, s, LANES), F32),
                   sds((b, g, GATE_ROWS, s), F32), sds((b, S5_HALVES, s, LANES), F32)],
        compiler_params=_params(("parallel", "parallel")),
        name="mixer_in_proj",
    )(x, mod, gain, w_row, w_t, *rope_tabs, *rope_tabs_t, q_gain, k_gain)


def _fold_rows(ref, n):
    rows = ref.shape[0] // n
    return jnp.concatenate([ref[pl.ds(k, rows, stride=n), :] for k in range(n)], axis=1)


def _unfold_rows(ref, value, n):
    rows = ref.shape[0] // n
    for k in range(n):
        ref[pl.ds(k, rows, stride=n), :] = value[:, k * LANES:(k + 1) * LANES]


def _compress_kernel(kc_ref, vc_ref, w1k_ref, w1v_ref, w1k_raw_ref, w1v_raw_ref, pe_ref, w2k_ref, w2vt_ref,
                     kg_ref, cos_ref, slo_ref, shi_ref, ko_ref, vo_ref, *, n_cmp):
    ncp = kc_ref.shape[0] // CMP_STRIDE
    for src_ref, w1_ref, raw_ref, pe_row, is_key in ((kc_ref, w1k_ref, w1k_raw_ref, 0, True),
                                                    (vc_ref, w1v_ref, w1v_raw_ref, 1, False)):
        chunks = _fold_rows(src_ref, CMP_STRIDE).astype(BF16)
        pe = jnp.broadcast_to(pe_ref[pe_row:pe_row + 1, :], (SUBLANES, CMP_LEN * HEAD_DIM))
        pe_term = _dot(pe, raw_ref[...], HIGHEST)[0:1, :]
        for g in range(NSA_KV_HEADS):
            a = _dot(chunks, w1_ref[g])
            pre = a[:, :CMP_HIDDEN] + pltpu.roll(a[:, CMP_HIDDEN:], ncp - 1, 0) + pe_term
            hidden = _gelu_tanh(pre).astype(BF16)
            if is_key:
                out = _dot(hidden, w2k_ref[...])
                ms = jnp.sum(out * out, axis=-1, keepdims=True) * (1.0 / HEAD_DIM)
                out = _rope(out * lax.rsqrt(ms + EPS) * kg_ref[...], cos_ref[...], slo_ref[...], shi_ref[...])
                real_row = lax.broadcasted_iota(jnp.int32, (ncp, LANES), 0) < n_cmp
                ko_ref[g] = jnp.where(real_row, out, 0.0).astype(BF16)
            else:
                out_t = _dot_nt(w2vt_ref[...], hidden)
                real_col = lax.broadcasted_iota(jnp.int32, (LANES, ncp), 1) < n_cmp
                value_row = lax.broadcasted_iota(jnp.int32, (LANES, ncp), 0) < HEAD_DIM
                vo_ref[g] = jnp.where(value_row, jnp.where(real_col, out_t, 0.0), 1.0).astype(BF16)


def _compress(kc, vc, layer, w1k, w1v, w1k_raw, w1v_raw, pe, w2k, w2v_t, k_gain, cmp_tabs):
    b, s, _ = kc.shape
    ncp = s // CMP_STRIDE
    fold = CMP_STRIDE * LANES
    src = pl.BlockSpec((None, s, LANES), lambda i: (i, 0, 0))
    raw = _layer_spec((CMP_LEN * HEAD_DIM, CMP_HIDDEN), layer)
    w1 = _layer_spec((NSA_KV_HEADS, fold, 2 * CMP_HIDDEN), layer)
    w2 = _layer_spec((CMP_HIDDEN, LANES), layer)
    tab = _const_spec((ncp, LANES))
    return pl.pallas_call(
        functools.partial(_compress_kernel, n_cmp=ncp - 1),
        grid=(b,),
        in_specs=[src, src, w1, w1, raw, raw, _layer_spec((2, CMP_LEN * HEAD_DIM), layer), w2, w2,
                  _layer_spec((1, LANES), layer), tab, tab, tab],
        out_specs=[pl.BlockSpec((None, NSA_KV_HEADS, ncp, LANES), lambda i: (i, 0, 0, 0)),
                   pl.BlockSpec((None, NSA_KV_HEADS, LANES, ncp), lambda i: (i, 0, 0, 0))],
        out_shape=[jax.ShapeDtypeStruct((b, NSA_KV_HEADS, ncp, LANES), BF16),
                   jax.ShapeDtypeStruct((b, NSA_KV_HEADS, LANES, ncp), BF16)],
        compiler_params=_params(("parallel",)),
        name="nsa_compress",
    )(kc, vc, w1k, w1v, w1k_raw, w1v_raw, pe, w2k, w2v_t, k_gain, *cmp_tabs)


def _selection_bias(imp_t, qb):
    nsb, nq = imp_t.shape
    j = lax.broadcasted_iota(jnp.int32, (nsb, nq), 0)
    t = qb * Q_BLOCK + lax.broadcasted_iota(jnp.int32, (nsb, nq), 1)
    cur = t >> SLC_SHIFT
    valid = j * SLC_BLOCK <= t
    forced = (j == 0) | (j == cur) | (j == cur - 1)
    j_f = j.astype(F32)
    start = jnp.where(valid & jnp.logical_not(forced), imp_t, -1.0)
    vals = start
    for _ in range(N_SELECT - N_FORCED):
        m = jnp.max(vals, axis=0, keepdims=True)
        idx = jnp.min(jnp.where(vals == m, j_f, float(nsb)), axis=0, keepdims=True)
        vals = jnp.where(j_f == idx, -2.0, vals)
    return jnp.where((forced & valid) | (vals != start), 0.0, MASK_VALUE)


def _attn_kernel(q_ref, kc_ref, vct_ref, ka_ref, kb_ref, va_ref, vb_ref, gate_ref, ovl_ref, o_ref,
                 sa_ref, sb_ref, *, seq):
    qb = pl.program_id(2)
    cols = NSA_REP * Q_BLOCK
    ncp = seq // CMP_STRIDE
    q_t = jnp.concatenate([q_ref[r] for r in range(NSA_REP)], axis=1)
    t_col = qb * Q_BLOCK + lax.broadcasted_iota(jnp.int32, (1, cols), 1) % Q_BLOCK

    cmp_end = lax.broadcasted_iota(jnp.int32, (ncp, cols), 0) * CMP_STRIDE + (CMP_LEN - 1)
    s = jnp.where(cmp_end <= t_col, _dot(kc_ref[...], q_t), NEG_INF)
    e = jnp.exp2(s - jnp.max(s, axis=0, keepdims=True)).astype(BF16)
    oc = _dot(jnp.concatenate([vct_ref[0:V_ROWS, :], ovl_ref[...]], axis=0), e)
    oc = oc * jnp.where(t_col >= CMP_LEN - 1, 1.0 / oc[HEAD_DIM:HEAD_DIM + 1, :], 0.0)
    o_cmp, imp = oc[0:V_ROWS], oc[V_ROWS:V_ROWS + LANES]
    imp_t = (imp[:, 0:Q_BLOCK] + imp[:, Q_BLOCK:2 * Q_BLOCK]
             + imp[:, 2 * Q_BLOCK:3 * Q_BLOCK] + imp[:, 3 * Q_BLOCK:4 * Q_BLOCK])
    bias = _selection_bias(imp_t, qb).astype(BF16)

    q_aug = jnp.concatenate([q_t, jnp.concatenate([bias] * NSA_REP, axis=1)], axis=0)

    span = WINDOW + Q_BLOCK
    start = pl.multiple_of(jnp.maximum(qb * Q_BLOCK - WINDOW, 0), LANES)
    start_blk = start // LANES
    kpos = start + lax.broadcasted_iota(jnp.int32, (span, cols), 0)
    s = jnp.where((kpos <= t_col) & (kpos > t_col - WINDOW), _dot(kb_ref[pl.ds(start, span), :], q_t), NEG_INF)
    e = jnp.exp2(s - jnp.max(s, axis=0, keepdims=True)).astype(BF16)
    acc_win = _dot(jnp.concatenate([vb_ref[start_blk + i, 0:V_ROWS, :] for i in range(span // LANES)], axis=1), e)
    o_win = acc_win * (1.0 / acc_win[HEAD_DIM:HEAD_DIM + 1, :])

    n_key_tiles = seq // KEY_TILE
    blocks_per_tile = KEY_TILE // LANES

    def score_tile(kt, s_ref):
        k0 = pl.multiple_of(jnp.minimum(kt, n_key_tiles - 1) * KEY_TILE, KEY_TILE)
        s_ref[...] = _dot(ka_ref[pl.ds(k0, KEY_TILE), :], q_aug)

    def absorb_tile(kt, s_ref, carry, causal):
        m_i, acc = carry
        blk0 = jnp.minimum(kt, n_key_tiles - 1) * blocks_per_tile
        s = s_ref[...]
        if causal:
            kpos = kt * KEY_TILE + lax.broadcasted_iota(jnp.int32, (KEY_TILE, cols), 0)
            s = jnp.where(kpos <= t_col, s, MASK_VALUE)
        m_new = jnp.maximum(m_i, jnp.max(s, axis=0, keepdims=True))
        p = jnp.exp2(s - m_new).astype(BF16)
        v_t = jnp.concatenate([va_ref[blk0 + i, 0:V_ROWS, :] for i in range(blocks_per_tile)], axis=1)
        return m_new, jnp.exp2(m_i - m_new) * acc + _dot(v_t, p)

    def slc_trip(j, carry):
        score_tile(2 * j + 1, sb_ref)
        carry = absorb_tile(2 * j, sa_ref, carry, causal=False)
        score_tile(2 * j + 2, sa_ref)
        return absorb_tile(2 * j + 1, sb_ref, carry, causal=False)

    last_pair = qb // (2 * KEY_TILE // Q_BLOCK)
    score_tile(0, sa_ref)
    carry = lax.fori_loop(0, last_pair, slc_trip,
                          (jnp.full((1, cols), NEG_INF, F32), jnp.zeros((V_ROWS, cols), F32)))
    score_tile(2 * last_pair + 1, sb_ref)
    carry = absorb_tile(2 * last_pair, sa_ref, carry, causal=True)
    _, acc_slc = absorb_tile(2 * last_pair + 1, sb_ref, carry, causal=True)
    o_slc = acc_slc * (1.0 / acc_slc[HEAD_DIM:HEAD_DIM + 1, :])

    gates = gate_ref[...]
    heads = []
    for r in range(NSA_REP):
        cs = slice(r * Q_BLOCK, (r + 1) * Q_BLOCK)
        c = r * N_BRANCH
        heads.append(gates[c:c + 1, :] * o_cmp[0:HEAD_DIM, cs] + gates[c + 1:c + 2, :] * o_slc[0:HEAD_DIM, cs]
                     + gates[c + 2:c + 3, :] * o_win[0:HEAD_DIM, cs])
    o_ref[...] = jnp.concatenate(heads, axis=0).T


def _overlap_matrix(seq):
    ncp, nsb = seq // CMP_STRIDE, seq // SLC_BLOCK
    c_start = np.arange(ncp)[None, :] * CMP_STRIDE
    s_start = np.arange(nsb)[:, None] * SLC_BLOCK
    ovl = np.clip(np.minimum(c_start + CMP_LEN, s_start + SLC_BLOCK) - np.maximum(c_start, s_start), 0, None)
    ovl = ovl.astype(np.float32) / CMP_LEN
    ovl[:, ncp - 1] = 0.0
    return jnp.asarray(np.pad(ovl, ((0, LANES - nsb), (0, 0))))


def _nsa_attention(q_t, k_cmp, v_cmp_t, ka, kb, va_t, vb_t, gates_t):
    b, _, _, s = q_t.shape
    g = NSA_KV_HEADS
    ncp = s // CMP_STRIDE
    assert s // SLC_BLOCK <= LANES
    per_group = lambda *shape: pl.BlockSpec((None, None) + shape, lambda i, j, k: (i, j) + (0,) * len(shape))
    return pl.pallas_call(
        functools.partial(_attn_kernel, seq=s),
        grid=(b, g, s // Q_BLOCK),
        in_specs=[pl.BlockSpec((None, NSA_REP, LANES, Q_BLOCK), lambda i, j, k: (i, j, 0, k)),
                  per_group(ncp, LANES), per_group(LANES, ncp), per_group(s, 2 * LANES), per_group(s, LANES),
                  per_group(s // LANES, LANES, LANES), per_group(s // LANES, LANES, LANES),
                  pl.BlockSpec((None, None, GATE_ROWS, Q_BLOCK), lambda i, j, k: (i, j, 0, k)),
                  _const_spec((LANES, ncp))],
        out_specs=pl.BlockSpec((None, Q_BLOCK, NSA_REP * HEAD_DIM), lambda i, j, k: (i, k, j)),
        out_shape=jax.ShapeDtypeStruct((b, s, NSA_WIDTH), F32),
        scratch_shapes=[pltpu.VMEM((KEY_TILE, NSA_REP * Q_BLOCK), F32)] * 2,
        compiler_params=_params(("parallel", "parallel", "arbitrary")),
        name="nsa_attention",
    )(q_t, k_cmp, v_cmp_t, ka, kb, va_t, vb_t, gates_t, _overlap_matrix(s).astype(BF16))


def _s5_kernel(u_ref, mt_ref, bc_ref, cc_ref, a1_ref, a2_ref, d_ref, y_ref, h_ref, g_scr, hp_scr):
    @pl.when(pl.program_id(1) == 0)
    def _():
        h_ref[...] = jnp.zeros_like(h_ref)

    halves = range(S5_HALVES)
    u = [_fold_rows(u_ref.at[a], S5_CHUNK) for a in halves]
    ub = [x.astype(BF16) for x in u]
    y_local = [_dot(ub[a], mt_ref[a]) for a in halves]
    for a in halves:
        g_scr[:, a * S5_HALF_FOLD:(a + 1) * S5_HALF_FOLD] = _dot(ub[a], bc_ref[a])
    a1, a2 = a1_ref[...], a2_ref[...]
    n_state = S5_STATE // S5_HALVES

    def swap_re_im(h):
        parts = [h[:, k * n_state:(k + 1) * n_state] for k in range(2 * S5_HALVES)]
        return jnp.concatenate([parts[k ^ 1] for k in range(2 * S5_HALVES)], axis=1)

    def step(i, h):
        hp_scr[pl.ds(i, 1), :] = h
        return a1 * h + a2 * swap_re_im(h) + g_scr[pl.ds(i, 1), :]

    h_ref[...] = lax.fori_loop(0, g_scr.shape[0], step, h_ref[...])
    for a in halves:
        carried = _dot(hp_scr[:, a * S5_HALF_FOLD:(a + 1) * S5_HALF_FOLD].astype(BF16), cc_ref[a])
        _unfold_rows(y_ref.at[a], y_local[a] + carried + u[a] * d_ref[a:a + 1, :], S5_CHUNK)


def _s5_scan(u, layer, mats):
    b, _, s, _ = u.shape
    rows = s // S5_CHUNK
    tc = min(128, rows)
    tile = pl.BlockSpec((None, S5_HALVES, tc * S5_CHUNK, LANES), lambda i, j: (i, 0, j, 0))
    mat = _layer_spec((S5_HALVES, S5_HALF_FOLD, S5_HALF_FOLD), layer)
    row = _layer_spec((1, S5_HALVES * S5_HALF_FOLD), layer)
    return pl.pallas_call(
        _s5_kernel,
        grid=(b, rows // tc),
        in_specs=[tile, mat, mat, mat, row, row, _layer_spec((S5_HALVES, S5_HALF_FOLD), layer)],
        out_specs=tile,
        out_shape=jax.ShapeDtypeStruct((b, S5_HALVES, s, LANES), F32),
        scratch_shapes=[pltpu.VMEM((1, S5_HALVES * S5_HALF_FOLD), F32),
                        pltpu.VMEM((tc, S5_HALVES * S5_HALF_FOLD), F32),
                        pltpu.VMEM((tc, S5_HALVES * S5_HALF_FOLD), F32)],
        compiler_params=_params(("parallel", "arbitrary")),
        name="s5_scan",
    )(u, *mats)


def _s5_matrices(lam_re, lam_im, log_dt, b_re, b_im, c_re, c_im, d_skip):
    t0, ng, nh, npm = S5_CHUNK, S5_GROUPS, S5_H, S5_P
    gh = ng // S5_HALVES
    ein = functools.partial(jnp.einsum, precision=HIGHEST)
    lam = lax.complex(lam_re, lam_im)
    step = jnp.exp(log_dt)[:, None]
    lam_bar = jnp.exp(lam * step)
    b_bar = lax.complex(b_re, b_im) * ((lam_bar - 1.0) / lam)[..., None]
    c_mat = lax.complex(c_re, c_im)
    k = jnp.arange(t0 + 1, dtype=F32)[:, None, None]
    pw = jnp.exp((lam * step)[None] * k)

    def same_group(rows_per_group, cols_per_group):
        r = np.arange(gh * rows_per_group)[:, None] // rows_per_group
        c = np.arange(gh * cols_per_group)[None, :] // cols_per_group
        return jnp.asarray((r == c).astype(np.float32))

    def per_group_blocks(x, rows_per_group, cols_per_group):
        return jnp.tile(x, (1,) * (x.ndim - 1) + (gh,)) * same_group(rows_per_group, cols_per_group)

    kern = jnp.real(ein('ghp,kgp,gpq->kghq', c_mat, pw[:t0], b_bar))
    d_k = per_group_blocks(kern.transpose(0, 1, 3, 2).reshape(t0, S5_HALVES, gh * nh, nh), nh, nh)
    d_k = jnp.concatenate([d_k, jnp.zeros_like(d_k[:1])], axis=0)
    lag = np.arange(t0)[None, :] - np.arange(t0)[:, None]
    mt = d_k[np.where(lag >= 0, lag, t0)]
    mt = mt.transpose(2, 0, 3, 1, 4).reshape(S5_HALVES, S5_HALF_FOLD, S5_HALF_FOLD)
    b_j = (pw[t0 - 1 - np.arange(t0)][..., None] * b_bar[None]).transpose(0, 1, 3, 2)
    b_j = b_j.reshape(t0, S5_HALVES, gh * nh, npm)
    bc = jnp.concatenate([per_group_blocks(jnp.real(b_j), nh, npm), per_group_blocks(jnp.imag(b_j), nh, npm)],
                         axis=-1)
    bc = bc.transpose(1, 0, 2, 3).reshape(S5_HALVES, S5_HALF_FOLD, S5_HALF_FOLD)
    c_i = (c_mat[None] * pw[1:t0 + 1][:, :, None, :]).transpose(1, 3, 0, 2)
    c_i = c_i.reshape(S5_HALVES, gh * npm, t0, nh)
    mask = same_group(npm, nh)[:, None, :]
    cc = jnp.concatenate([jnp.tile(jnp.real(c_i), (1, 1, 1, gh)) * mask,
                          jnp.tile(-jnp.imag(c_i), (1, 1, 1, gh)) * mask], axis=1)
    cc = cc.reshape(S5_HALVES, S5_HALF_FOLD, S5_HALF_FOLD)
    a_chunk = pw[t0].reshape(S5_HALVES, gh * npm)
    a1 = jnp.concatenate([jnp.real(a_chunk), jnp.real(a_chunk)], axis=1).reshape(1, -1)
    a2 = jnp.concatenate([-jnp.imag(a_chunk), jnp.imag(a_chunk)], axis=1).reshape(1, -1)
    d_vec = jnp.tile(d_skip.reshape(S5_HALVES, gh * nh), (1, t0))
    return mt.astype(BF16), bc.astype(BF16), cc.astype(BF16), a1, a2, d_vec


def _rms_gain(y, gain):
    return y * lax.rsqrt(jnp.mean(y * y, axis=-1, keepdims=True) + EPS) * gain


def _outproj_kernel(x_ref, mod_ref, pool_ref, halo_ref, nsa_ref, s5_ref, pw_ref, pb_ref, ps_ref, on_ref,
                    gw_ref, gb_ref, wo_ref, o_ref, buf):
    j = pl.program_id(1)
    tm = x_ref.shape[0]
    v = pool_ref[...]
    buf[0:POOL_HALO, :] = jnp.where(j > 0, halo_ref[...], 0.0)
    buf[POOL_HALO:POOL_HALO + tm, :] = v
    lane_group = lax.broadcasted_iota(jnp.int32, (tm, POOL_WIDTH), 1) >> SLC_SHIFT
    t1 = (j * tm + 1 + lax.broadcasted_iota(jnp.int32, (tm, 1), 0)).astype(F32)
    run, k, pooled = v, 1, jnp.zeros_like(v)
    for gi, w in enumerate(POOL_WINDOWS):
        while k < w:
            run = run + buf[POOL_HALO - k:POOL_HALO - k + tm, :]
            k += 1
        pooled = jnp.where(lane_group == gi, run / jnp.minimum(t1, float(w)) - v, pooled)
    y_pool = (_dot(pooled.astype(BF16), pw_ref[...]) + pb_ref[...]) * ps_ref[...]

    y = _gelu_tanh(jnp.concatenate([s5_ref[half] for half in range(S5_HALVES)], axis=-1))
    y_s5 = y * _sigmoid(_dot(y.astype(BF16), gw_ref[...]) + gb_ref[...])

    cat = jnp.concatenate(
        [_rms_gain(y_pool, on_ref[:, 0:POOL_WIDTH]),
         _rms_gain(nsa_ref[...], on_ref[:, POOL_WIDTH:POOL_WIDTH + NSA_WIDTH]),
         _rms_gain(y_s5, on_ref[:, POOL_WIDTH + NSA_WIDTH:])], axis=-1).astype(BF16)
    o_ref[...] = x_ref[...] + mod_ref[5:6, :] * _dot(cat, wo_ref[...])


def _outproj(x, layer, mod, u_pool, o_nsa, y_s5, pool_w_bd, pool_b, pool_scale, out_norm, glu_w, glu_b, w_out,
             tm=256):
    b, s, d = x.shape
    tok = lambda width: pl.BlockSpec((None, tm, width), lambda i, j: (i, j, 0))
    halo_blocks = tm // POOL_HALO
    return pl.pallas_call(
        _outproj_kernel,
        grid=(b, s // tm),
        in_specs=[tok(d),
                  _mod_spec(layer),
                  tok(POOL_WIDTH),
                  pl.BlockSpec((None, POOL_HALO, POOL_WIDTH),
                               lambda i, j: (i, jnp.maximum(j * halo_blocks - 1, 0), 0)),
                  tok(NSA_WIDTH),
                  pl.BlockSpec((None, S5_HALVES, tm, LANES), lambda i, j: (i, 0, j, 0)),
                  _layer_spec((POOL_WIDTH, POOL_WIDTH), layer), _layer_spec((1, POOL_WIDTH), layer),
                  _layer_spec((1, POOL_WIDTH), layer), _layer_spec((1, d), layer),
                  _layer_spec((S5_WIDTH, S5_WIDTH), layer), _layer_spec((1, S5_WIDTH), layer),
                  _layer_spec((d, d), layer)],
        out_specs=tok(d),
        out_shape=jax.ShapeDtypeStruct((b, s, d), F32),
        scratch_shapes=[pltpu.VMEM((POOL_HALO + tm, POOL_WIDTH), F32)],
        compiler_params=_params(("parallel", "arbitrary")),
        name="mixer_out_proj",
    )(x, mod, u_pool, u_pool, o_nsa, y_s5, pool_w_bd, pool_b, pool_scale, out_norm, glu_w, glu_b, w_out)


def _rope_tables(pos):
    half = ROT_DIM // 2
    inv_freq = jnp.exp(-math.log(ROPE_THETA) * jnp.arange(half, dtype=F32) * (2.0 / ROT_DIM))
    ang = pos[:, None] * inv_freq[None, :]
    cos, sin = jnp.cos(ang), jnp.sin(ang)
    n = pos.shape[0]
    rest = HEAD_DIM - ROT_DIM
    cos_t = jnp.concatenate([cos, cos, jnp.ones((n, rest), F32)], axis=1)
    sin_lo = jnp.concatenate([-sin, jnp.zeros((n, half + rest), F32)], axis=1)
    sin_hi = jnp.concatenate([jnp.zeros((n, half), F32), sin, jnp.zeros((n, rest), F32)], axis=1)
    return tuple(jnp.tile(t, (1, 2)) for t in (cos_t, sin_lo, sin_hi))


def _pad_lanes(v):
    return jnp.pad(v, [(0, 0)] * (v.ndim - 1) + [(0, LANES - v.shape[-1])])


def _rope_tables_t(pos):
    half = ROT_DIM // 2
    inv_freq = jnp.exp(-math.log(ROPE_THETA) * jnp.arange(half, dtype=F32) * (2.0 / ROT_DIM))
    ang = inv_freq[:, None] * pos[None, :]
    return jnp.cos(ang), jnp.sin(ang)


def _arrange_w_in(w):
    d = w.shape[0]
    o1, o2, o3 = POOL_WIDTH, POOL_WIDTH + NSA_WIDTH, POOL_WIDTH + NSA_WIDTH + 6 * LANES
    kv = w[:, o2:o3].reshape(d, 6, NSA_KV_HEADS, HEAD_DIM)
    ka = jnp.concatenate([kv[:, 2], kv[:, 4]], axis=-1).reshape(d, NSA_KV_HEADS * LANES)
    n_gate = NSA_REP * N_BRANCH
    w_row = jnp.concatenate([w[:, :o1], ka, kv[:, 0].reshape(d, LANES), kv[:, 1].reshape(d, LANES),
                             w[:, o3 + NSA_KV_HEADS * n_gate:]], axis=1)
    q_t = w[:, o1:o2].T
    v_t = jnp.concatenate([kv[:, 3], kv[:, 5]], axis=-1).reshape(d, NSA_KV_HEADS * LANES).T
    gate_t = w[:, o3:o3 + NSA_KV_HEADS * n_gate].T.reshape(NSA_KV_HEADS, n_gate, d)
    gate_t = jnp.pad(gate_t, ((0, 0), (0, GATE_ROWS - n_gate), (0, 0))).reshape(NSA_KV_HEADS * GATE_ROWS, d)
    return w_row.astype(BF16), jnp.concatenate([q_t, v_t, gate_t], axis=0).astype(BF16)


def _expand_cmp_w1(w1):
    halves = w1.reshape(2, CMP_STRIDE, HEAD_DIM, CMP_HIDDEN)
    both = jnp.concatenate([halves[0], halves[1]], axis=-1)
    out = []
    for g in range(NSA_KV_HEADS):
        z = jnp.zeros((CMP_STRIDE, NSA_KV_HEADS, HEAD_DIM, 2 * CMP_HIDDEN), F32).at[:, g].set(both)
        out.append(z.reshape(CMP_STRIDE * LANES, 2 * CMP_HIDDEN))
    return jnp.stack(out).astype(BF16)


def _block_diag(w):
    g, c, _ = w.shape
    return jnp.einsum('gcd,gf->gcfd', w, jnp.eye(g, dtype=w.dtype)).reshape(g * c, g * c)


def _prepare_parameters(norm_ffn1, ffn1_w_in, ffn1_w_out, norm_mix, w_in, w_out, out_norm, pool_w, pool_b, pool_scale,
                        q_norm, k_norm, cmp_pe, cmp_k_w1, cmp_k_w2, cmp_v_w1, cmp_v_w2, s5_lam_re, s5_lam_im,
                        s5_log_dt, s5_b_re, s5_b_im, s5_c_re, s5_c_im, s5_d, glu_w, glu_b, norm_ffn2, ffn2_w_in,
                        ffn2_w_out):
    n_layers = norm_mix.shape[0]
    row = lambda v: v.reshape(n_layers, 1, -1)
    w_row, w_t = jax.vmap(_arrange_w_in)(w_in)
    return dict(
        ffn1=(row(norm_ffn1), ffn1_w_in.astype(BF16), ffn1_w_out.astype(BF16)),
        ffn2=(row(norm_ffn2), ffn2_w_in.astype(BF16), ffn2_w_out.astype(BF16)),
        inproj=(row(norm_mix), w_row, w_t, q_norm.reshape(n_layers, HEAD_DIM, 1),
                jnp.concatenate([k_norm[:, 1], k_norm[:, 2]], axis=-1).reshape(n_layers, 1, LANES)),
        compress=(jax.vmap(_expand_cmp_w1)(cmp_k_w1), jax.vmap(_expand_cmp_w1)(cmp_v_w1), cmp_k_w1, cmp_v_w1,
                  cmp_pe.reshape(n_layers, 2, CMP_LEN * HEAD_DIM), _pad_lanes(cmp_k_w2).astype(BF16),
                  jnp.swapaxes(_pad_lanes(cmp_v_w2), 1, 2).astype(BF16), row(_pad_lanes(k_norm[:, 0]))),
        s5=jax.vmap(_s5_matrices)(s5_lam_re, s5_lam_im, s5_log_dt, s5_b_re, s5_b_im, s5_c_re, s5_c_im, s5_d),
        outproj=(jax.vmap(_block_diag)(pool_w).astype(BF16), row(pool_b), row(pool_scale), row(out_norm),
                 glu_w.astype(BF16), row(glu_b), w_out.astype(BF16)))


def _hybrid_layer(x, layer, mod, params, tabs, tabs_t, cmp_tabs):
    x = _ffn(x, layer, mod, *params["ffn1"], 0)
    u_pool, q_t, ka, kb, va_t, vb_t, kc, vc, gates_t, u_s5 = _inproj(
        x, layer, mod, *params["inproj"][:3], tabs, tabs_t, *params["inproj"][3:])
    k_cmp, v_cmp_t = _compress(kc, vc, layer, *params["compress"], cmp_tabs)
    o_nsa = _nsa_attention(q_t, k_cmp, v_cmp_t, ka, kb, va_t, vb_t, gates_t)
    y_s5 = _s5_scan(u_s5, layer, params["s5"])
    x = _outproj(x, layer, mod, u_pool, o_nsa, y_s5, *params["outproj"])
    return _ffn(x, layer, mod, *params["ffn2"], 6)


def kernel(x, c, ada_w, ada_b, norm_ffn1, ffn1_w_in, ffn1_w_out, norm_mix, w_in, w_out, out_norm, pool_w, pool_b, pool_scale, q_norm, k_norm, cmp_pe, cmp_k_w1, cmp_k_w2, cmp_v_w1, cmp_v_w2, s5_lam_re, s5_lam_im, s5_log_dt, s5_b_re, s5_b_im, s5_c_re, s5_c_im, s5_d, glu_w, glu_b, norm_ffn2, ffn2_w_in, ffn2_w_out):
    seq = x.shape[1]
    assert seq % (KEY_TILE * 4) == 0 and seq >= WINDOW + Q_BLOCK
    mod = _modulation(c, ada_w, ada_b)
    params = _prepare_parameters(norm_ffn1, ffn1_w_in, ffn1_w_out, norm_mix, w_in, w_out, out_norm, pool_w, pool_b,
                                 pool_scale, q_norm, k_norm, cmp_pe, cmp_k_w1, cmp_k_w2, cmp_v_w1, cmp_v_w2,
                                 s5_lam_re, s5_lam_im, s5_log_dt, s5_b_re, s5_b_im, s5_c_re, s5_c_im, s5_d, glu_w,
                                 glu_b, norm_ffn2, ffn2_w_in, ffn2_w_out)
    tabs = _rope_tables(jnp.arange(seq, dtype=F32))
    tabs_t = _rope_tables_t(jnp.arange(seq, dtype=F32))
    cmp_tabs = _rope_tables((jnp.arange(seq // CMP_STRIDE) * CMP_STRIDE + CMP_LEN - 1).astype(F32))
    for layer in range(ada_w.shape[0]):
        x = _hybrid_layer(x, layer, mod, params, tabs, tabs_t, cmp_tabs)
    return x
```

```python
import functools
import math

import jax
import jax.numpy as jnp
import numpy as np
from jax import lax
from jax.experimental import pallas as pl
from jax.experimental.pallas import tpu as pltpu

F32 = jnp.float32
BF16 = jnp.bfloat16
HIGHEST = lax.Precision.HIGHEST

LANES = 128
SUBLANES = 8
VMEM_LIMIT_BYTES = 56 * 1024 * 1024

D_MODEL = 1024
D_FF = 2816
N_MOD = 9
POOL_WIDTH = 256
POOL_GC = 64
POOL_WINDOWS = (2, 4, 8, 16)
POOL_HALO = 16
HEAD_DIM = 64
NSA_WIDTH = 512
NSA_HEADS = 8
NSA_KV_HEADS = 2
NSA_REP = 4
N_BRANCH = 3
S5_WIDTH = 256
S5_H = 16
S5_GROUPS = 16
S5_P = 64
CMP_STRIDE = 16
CMP_LEN = 32
CMP_HIDDEN = 128
SLC_BLOCK = 64
SLC_SHIFT = 6
N_SELECT = 16
N_FORCED = 3
WINDOW = 512
Q_BLOCK = 256
ROT_DIM = 16
ROPE_THETA = 500000.0
EPS = 1e-6
Q_SCALE = HEAD_DIM ** -0.5 * math.log2(math.e)
NEG_INF = -1e30
MASK_VALUE = -(2.0 ** 100)

OFF_POOL = 0
OFF_KA = OFF_POOL + POOL_WIDTH
OFF_KC = OFF_KA + NSA_KV_HEADS * LANES
OFF_VC = OFF_KC + LANES
OFF_S5 = OFF_VC + LANES
N_COLS = OFF_S5 + S5_WIDTH
GATE_ROWS = 16
ROW_Q = 0
ROW_V = ROW_Q + NSA_HEADS * HEAD_DIM
ROW_GATE = ROW_V + NSA_KV_HEADS * LANES
N_ROWS_T = ROW_GATE + NSA_KV_HEADS * GATE_ROWS

S5_CHUNK = 8
S5_FOLD = S5_CHUNK * S5_WIDTH
S5_STATE = S5_GROUPS * S5_P
S5_HALVES = S5_WIDTH // LANES
S5_HALF_FOLD = S5_FOLD // S5_HALVES
KEY_TILE = 512
V_ROWS = HEAD_DIM + 16


def _dot(a, b, precision=None):
    return jnp.dot(a, b, preferred_element_type=F32, precision=precision)


def _dot_nt(a, b, precision=None):
    return lax.dot_general(a, b, (((1,), (1,)), ((), ())), preferred_element_type=F32,
                           precision=precision)


def _sigmoid(x):
    return 1.0 / (1.0 + jnp.exp(-x))


def _gelu_tanh(x):
    return 0.5 * x * (1.0 + jnp.tanh(math.sqrt(2.0 / math.pi) * (x + 0.044715 * (x * x * x))))


def _params(sem):
    return pltpu.CompilerParams(dimension_semantics=sem, vmem_limit_bytes=VMEM_LIMIT_BYTES)


def _const_spec(shape):
    nd = len(shape)
    return pl.BlockSpec(shape, lambda *_: (0,) * nd, pipeline_mode=pl.Buffered(1))


def _layer_spec(shape, layer):
    nd = len(shape)
    return pl.BlockSpec((None,) + shape, lambda *_: (layer,) + (0,) * nd, pipeline_mode=pl.Buffered(1))


def _mod_kernel(c_ref, w_ref, b_ref, o_ref):
    c = c_ref[...]
    o_ref[...] = _dot(c * _sigmoid(c), w_ref[...], HIGHEST) + b_ref[...]


def _modulation(c, ada_w, ada_b):
    n_layers, d, n = ada_w.shape
    b = c.shape[0]
    tn = 1152
    c_pad = jnp.zeros((SUBLANES, d), F32).at[:b].set(c)
    out = pl.pallas_call(
        _mod_kernel,
        grid=(n_layers, n // tn),
        in_specs=[pl.BlockSpec((SUBLANES, d), lambda l, j: (0, 0)),
                  pl.BlockSpec((None, d, tn), lambda l, j: (l, 0, j)),
                  pl.BlockSpec((None, 1, tn), lambda l, j: (l, 0, j))],
        out_specs=pl.BlockSpec((None, SUBLANES, tn), lambda l, j: (l, 0, j)),
        out_shape=jax.ShapeDtypeStruct((n_layers, SUBLANES, n), F32),
        compiler_params=_params(("arbitrary", "arbitrary")),
        name="adaln_mod",
    )(c_pad, ada_w, ada_b.reshape(n_layers, 1, n))
    return out[:, :b].reshape(n_layers, b, N_MOD, d)


def _norm_modulate(x, gain, mod_ref, first_row):
    ms = jnp.mean(x * x, axis=-1, keepdims=True)
    y = x * lax.rsqrt(ms + EPS) * gain
    return y * (1.0 + mod_ref[first_row + 1:first_row + 2, :]) + mod_ref[first_row:first_row + 1, :]


def _ffn_kernel(x_ref, mod_ref, g_ref, win_ref, wout_ref, o_ref, *, first_row):
    x = x_ref[...]
    h = _norm_modulate(x, g_ref[...], mod_ref, first_row).astype(BF16)
    gu = _dot(h, win_ref[...])
    gate, up = gu[:, :D_FF], gu[:, D_FF:]
    a = (gate * _sigmoid(gate) * up).astype(BF16)
    y = _dot(a, wout_ref[...])
    o_ref[...] = x + 0.5 * mod_ref[first_row + 2:first_row + 3, :] * y


def _mod_spec(layer):
    return pl.BlockSpec((None, None, N_MOD, D_MODEL), lambda i, *_: (layer, i, 0, 0))


def _ffn(x, layer, mod, gain, w_in, w_out, first_row, tm=512):
    b, s, d = x.shape
    return pl.pallas_call(
        functools.partial(_ffn_kernel, first_row=first_row),
        grid=(b, s // tm),
        in_specs=[pl.BlockSpec((None, tm, d), lambda i, j: (i, j, 0)),
                  _mod_spec(layer),
                  _layer_spec((1, d), layer),
                  _layer_spec((d, 2 * D_FF), layer),
                  _layer_spec((D_FF, d), layer)],
        out_specs=pl.BlockSpec((None, tm, d), lambda i, j: (i, j, 0)),
        out_shape=jax.ShapeDtypeStruct((b, s, d), F32),
        compiler_params=_params(("parallel", "parallel")),
        name="ffn_half_step",
    )(x, mod, gain, w_in, w_out)


def _rope(v, cos_t, sin_lo, sin_hi):
    return (v * cos_t + pltpu.roll(v, LANES - ROT_DIM // 2, 1) * sin_lo
            + pltpu.roll(v, ROT_DIM // 2, 1) * sin_hi)


def _inproj_kernel(x_ref, mod_ref, g_ref, w_ref, wt_ref, cos_ref, slo_ref, shi_ref, cost_ref, sint_ref, qg_ref, kg_ref,
                   pool_ref, q_ref, ka_ref, kb_ref, va_ref, vb_ref, kc_ref, vc_ref, gate_ref, s5_ref):
    h = _norm_modulate(x_ref[...], g_ref[...], mod_ref, 3).astype(BF16)
    u = _dot(h, w_ref[...])
    ut = _dot_nt(wt_ref[...], h)
    tm = u.shape[0]
    cos_t, sin_lo, sin_hi = cos_ref[...], slo_ref[...], shi_ref[...]
    lane = lax.broadcasted_iota(jnp.int32, (tm, LANES), 1)
    low_half = lane < HEAD_DIM
    pos = pl.program_id(1) * tm + lax.broadcasted_iota(jnp.int32, (tm, LANES), 0)
    block_one_hot = jnp.where((pos >> SLC_SHIFT) == lane, 1.0, 0.0).astype(BF16)

    pool_ref[...] = u[:, OFF_POOL:OFF_POOL + POOL_WIDTH]
    kc_ref[...] = u[:, OFF_KC:OFF_KC + LANES]
    vc_ref[...] = u[:, OFF_VC:OFF_VC + LANES]
    for half in range(S5_HALVES):
        s5_ref[half] = u[:, OFF_S5 + half * LANES:OFF_S5 + (half + 1) * LANES]
    for g in range(NSA_KV_HEADS):
        v = u[:, OFF_KA + g * LANES:OFF_KA + (g + 1) * LANES]
        sq = v * v
        ms_lo = jnp.sum(jnp.where(low_half, sq, 0.0), axis=-1, keepdims=True) * (1.0 / HEAD_DIM)
        ms_hi = jnp.sum(jnp.where(low_half, 0.0, sq), axis=-1, keepdims=True) * (1.0 / HEAD_DIM)
        r = jnp.where(low_half, lax.rsqrt(ms_lo + EPS), lax.rsqrt(ms_hi + EPS))
        kn = _rope(v * r * kg_ref[...], cos_t, sin_lo, sin_hi)
        ka_ref[g, :, 0:LANES] = kn.astype(BF16)
        ka_ref[g, :, LANES:2 * LANES] = block_one_hot
        kb_ref[g] = pltpu.roll(kn, HEAD_DIM, 1).astype(BF16)

    half_rot = ROT_DIM // 2
    cos8, sin8 = cost_ref[...], sint_ref[...]
    zero_rows = jnp.zeros((LANES - HEAD_DIM, tm), F32)
    for hd in range(NSA_HEADS):
        v = ut[ROW_Q + hd * HEAD_DIM:ROW_Q + (hd + 1) * HEAD_DIM, :]
        ms = jnp.sum(v * v, axis=0, keepdims=True) * (1.0 / HEAD_DIM)
        vn = v * lax.rsqrt(ms + EPS) * qg_ref[...]
        x1, x2 = vn[0:half_rot], vn[half_rot:ROT_DIM]
        roped = jnp.concatenate([x1 * cos8 - x2 * sin8, x2 * cos8 + x1 * sin8, vn[ROT_DIM:]], axis=0)
        q_ref[hd] = jnp.concatenate([roped * Q_SCALE, zero_rows], axis=0).astype(BF16)
    ones = jnp.ones((HEAD_DIM, tm), F32)
    for g in range(NSA_KV_HEADS):
        vt = ut[ROW_V + g * LANES:ROW_V + (g + 1) * LANES, :]
        va = jnp.concatenate([vt[0:HEAD_DIM], ones], axis=0).astype(BF16)
        vb = jnp.concatenate([vt[HEAD_DIM:], ones], axis=0).astype(BF16)
        for blk in range(tm // LANES):
            va_ref[g, blk] = va[:, blk * LANES:(blk + 1) * LANES]
            vb_ref[g, blk] = vb[:, blk * LANES:(blk + 1) * LANES]
        gate_ref[g] = _sigmoid(ut[ROW_GATE + g * GATE_ROWS:ROW_GATE + (g + 1) * GATE_ROWS, :])


def _inproj(x, layer, mod, gain, w_row, w_t, rope_tabs, rope_tabs_t, q_gain, k_gain, tm=256):
    b, s, d = x.shape
    g = NSA_KV_HEADS
    tok = lambda width: pl.BlockSpec((None, tm, width), lambda i, j: (i, j, 0))
    grp = lambda n, width=LANES: pl.BlockSpec((None, n, tm, width), lambda i, j: (i, 0, j, 0))
    lanes_tok = lambda n, rows: pl.BlockSpec((None, n, rows, tm), lambda i, j: (i, 0, 0, j))
    v_blocks = pl.BlockSpec((None, g, tm // LANES, LANES, LANES), lambda i, j: (i, 0, j, 0, 0))
    tab = pl.BlockSpec((tm, LANES), lambda i, j: (j, 0))
    tab_t = pl.BlockSpec((ROT_DIM // 2, tm), lambda i, j: (0, j))
    sds = jax.ShapeDtypeStruct
    return pl.pallas_call(
        _inproj_kernel,
        grid=(b, s // tm),
        in_specs=[tok(d),
                  _mod_spec(layer),
                  _layer_spec((1, d), layer),
                  _layer_spec((d, N_COLS), layer), _layer_spec((N_ROWS_T, d), layer),
                  tab, tab, tab, tab_t, tab_t,
                  _layer_spec((HEAD_DIM, 1), layer), _layer_spec((1, LANES), layer)],
        out_specs=[tok(POOL_WIDTH), lanes_tok(NSA_HEADS, LANES), grp(g, 2 * LANES), grp(g), v_blocks, v_blocks,
                   tok(LANES), tok(LANES), lanes_tok(g, GATE_ROWS), grp(S5_HALVES)],
        out_shape=[sds((b, s, POOL_WIDTH), F32), sds((b, NSA_HEADS, LANES, s), BF16),
                   sds((b, g, s, 2 * LANES), BF16), sds((b, g, s, LANES), BF16),
                   sds((b, g, s // LANES, LANES, LANES), BF16), sds((b, g, s // LANES, LANES, LANES), BF16),
                   sds((b, s, LANES), F32), sds((b, s, LANES), F32),
                   sds((b, g, GATE_ROWS, s), F32), sds((b, S5_HALVES, s, LANES), F32)],
        compiler_params=_params(("parallel", "parallel")),
        name="mixer_in_proj",
    )(x, mod, gain, w_row, w_t, *rope_tabs, *rope_tabs_t, q_gain, k_gain)


def _fold_rows(ref, n):
    rows = ref.shape[0] // n
    return jnp.concatenate([ref[pl.ds(k, rows, stride=n), :] for k in range(n)], axis=1)


def _unfold_rows(ref, value, n):
    rows = ref.shape[0] // n
    for k in range(n):
        ref[pl.ds(k, rows, stride=n), :] = value[:, k * LANES:(k + 1) * LANES]


def _compress_kernel(kc_ref, vc_ref, w1k_ref, w1v_ref, w1k_raw_ref, w1v_raw_ref, pe_ref, w2k_ref, w2vt_ref,
                     kg_ref, cos_ref, slo_ref, shi_ref, ko_ref, vo_ref, *, n_cmp):
    ncp = kc_ref.shape[0] // CMP_STRIDE
    for src_ref, w1_ref, raw_ref, pe_row, is_key in ((kc_ref, w1k_ref, w1k_raw_ref, 0, True),
                                                    (vc_ref, w1v_ref, w1v_raw_ref, 1, False)):
        chunks = _fold_rows(src_ref, CMP_STRIDE).astype(BF16)
        pe = jnp.broadcast_to(pe_ref[pe_row:pe_row + 1, :], (SUBLANES, CMP_LEN * HEAD_DIM))
        pe_term = _dot(pe, raw_ref[...], HIGHEST)[0:1, :]
        for g in range(NSA_KV_HEADS):
            a = _dot(chunks, w1_ref[g])
            pre = a[:, :CMP_HIDDEN] + pltpu.roll(a[:, CMP_HIDDEN:], ncp - 1, 0) + pe_term
            hidden = _gelu_tanh(pre).astype(BF16)
            if is_key:
                out = _dot(hidden, w2k_ref[...])
                ms = jnp.sum(out * out, axis=-1, keepdims=True) * (1.0 / HEAD_DIM)
                out = _rope(out * lax.rsqrt(ms + EPS) * kg_ref[...], cos_ref[...], slo_ref[...], shi_ref[...])
                real_row = lax.broadcasted_iota(jnp.int32, (ncp, LANES), 0) < n_cmp
                ko_ref[g] = jnp.where(real_row, out, 0.0).astype(BF16)
            else:
                out_t = _dot_nt(w2vt_ref[...], hidden)
                real_col = lax.broadcasted_iota(jnp.int32, (LANES, ncp), 1) < n_cmp
                value_row = lax.broadcasted_iota(jnp.int32, (LANES, ncp), 0) < HEAD_DIM
                vo_ref[g] = jnp.where(value_row, jnp.where(real_col, out_t, 0.0), 1.0).astype(BF16)


def _compress(kc, vc, layer, w1k, w1v, w1k_raw, w1v_raw, pe, w2k, w2v_t, k_gain, cmp_tabs):
    b, s, _ = kc.shape
    ncp = s // CMP_STRIDE
    fold = CMP_STRIDE * LANES
    src = pl.BlockSpec((None, s, LANES), lambda i: (i, 0, 0))
    raw = _layer_spec((CMP_LEN * HEAD_DIM, CMP_HIDDEN), layer)
    w1 = _layer_spec((NSA_KV_HEADS, fold, 2 * CMP_HIDDEN), layer)
    w2 = _layer_spec((CMP_HIDDEN, LANES), layer)
    tab = _const_spec((ncp, LANES))
    return pl.pallas_call(
        functools.partial(_compress_kernel, n_cmp=ncp - 1),
        grid=(b,),
        in_specs=[src, src, w1, w1, raw, raw, _layer_spec((2, CMP_LEN * HEAD_DIM), layer), w2, w2,
                  _layer_spec((1, LANES), layer), tab, tab, tab],
        out_specs=[pl.BlockSpec((None, NSA_KV_HEADS, ncp, LANES), lambda i: (i, 0, 0, 0)),
                   pl.BlockSpec((None, NSA_KV_HEADS, LANES, ncp), lambda i: (i, 0, 0, 0))],
        out_shape=[jax.ShapeDtypeStruct((b, NSA_KV_HEADS, ncp, LANES), BF16),
                   jax.ShapeDtypeStruct((b, NSA_KV_HEADS, LANES, ncp), BF16)],
        compiler_params=_params(("parallel",)),
        name="nsa_compress",
    )(kc, vc, w1k, w1v, w1k_raw, w1v_raw, pe, w2k, w2v_t, k_gain, *cmp_tabs)


def _selection_bias(imp_t, qb):
    nsb, nq = imp_t.shape
    j = lax.broadcasted_iota(jnp.int32, (nsb, nq), 0)
    t = qb * Q_BLOCK + lax.broadcasted_iota(jnp.int32, (nsb, nq), 1)
    cur = t >> SLC_SHIFT
    valid = j * SLC_BLOCK <= t
    forced = (j == 0) | (j == cur) | (j == cur - 1)
    j_f = j.astype(F32)
    start = jnp.where(valid & jnp.logical_not(forced), imp_t, -1.0)
    vals = start
    for _ in range(N_SELECT - N_FORCED):
        m = jnp.max(vals, axis=0, keepdims=True)
        idx = jnp.min(jnp.where(vals == m, j_f, float(nsb)), axis=0, keepdims=True)
        vals = jnp.where(j_f == idx, -2.0, vals)
    return jnp.where((forced & valid) | (vals != start), 0.0, MASK_VALUE)


def _attn_kernel(q_ref, kc_ref, vct_ref, ka_ref, kb_ref, va_ref, vb_ref, gate_ref, ovl_ref, o_ref,
                 sa_ref, sb_ref, acc_ref, *, seq):
    qb = pl.program_id(2)
    cols = NSA_REP * Q_BLOCK
    ncp = seq // CMP_STRIDE
    q_t = jnp.concatenate([q_ref[r] for r in range(NSA_REP)], axis=1)
    t_col = qb * Q_BLOCK + lax.broadcasted_iota(jnp.int32, (1, cols), 1) % Q_BLOCK
    t_q = qb * Q_BLOCK + lax.broadcasted_iota(jnp.int32, (1, Q_BLOCK), 1)

    def all_heads(per_query):
        return jnp.concatenate([per_query] * NSA_REP, axis=1)

    cmp_end = lax.broadcasted_iota(jnp.int32, (ncp, Q_BLOCK), 0) * CMP_STRIDE + (CMP_LEN - 1)
    s = _dot(kc_ref[...], q_t) + all_heads(jnp.where(cmp_end <= t_q, 0.0, NEG_INF))
    e = jnp.exp2(s - jnp.max(s, axis=0, keepdims=True)).astype(BF16)
    oc = _dot(jnp.concatenate([vct_ref[0:V_ROWS, :], ovl_ref[...]], axis=0), e)
    oc = oc * jnp.where(t_col >= CMP_LEN - 1, 1.0 / oc[HEAD_DIM:HEAD_DIM + 1, :], 0.0)
    o_cmp, imp = oc[0:V_ROWS], oc[V_ROWS:V_ROWS + LANES]
    imp_t = (imp[:, 0:Q_BLOCK] + imp[:, Q_BLOCK:2 * Q_BLOCK]
             + imp[:, 2 * Q_BLOCK:3 * Q_BLOCK] + imp[:, 3 * Q_BLOCK:4 * Q_BLOCK])
    bias = _selection_bias(imp_t, qb).astype(BF16)

    q_aug = jnp.concatenate([q_t, jnp.concatenate([bias] * NSA_REP, axis=1)], axis=0)

    span = WINDOW + Q_BLOCK
    start = pl.multiple_of(jnp.maximum(qb * Q_BLOCK - WINDOW, 0), LANES)
    start_blk = start // LANES
    kpos = start + lax.broadcasted_iota(jnp.int32, (span, Q_BLOCK), 0)
    in_window = (kpos <= t_q) & (kpos > t_q - WINDOW)
    s = _dot(kb_ref[pl.ds(start, span), :], q_t) + all_heads(jnp.where(in_window, 0.0, NEG_INF))
    e = jnp.exp2(s - jnp.max(s, axis=0, keepdims=True)).astype(BF16)
    acc_win = _dot(jnp.concatenate([vb_ref[start_blk + i, 0:V_ROWS, :] for i in range(span // LANES)], axis=1), e)
    o_win = acc_win * (1.0 / acc_win[HEAD_DIM:HEAD_DIM + 1, :])

    blocks_per_tile = KEY_TILE // LANES

    def score_tile(kt, s_ref):
        k0 = pl.multiple_of(kt * KEY_TILE, KEY_TILE)
        s_ref[...] = _dot(ka_ref[pl.ds(k0, KEY_TILE), :], q_aug)

    def absorb_tile(kt, s_ref, carry, causal):
        m_i, acc = carry
        blk0 = kt * blocks_per_tile
        s = s_ref[...]
        if causal:
            kpos = kt * KEY_TILE + lax.broadcasted_iota(jnp.int32, (KEY_TILE, Q_BLOCK), 0)
            s = s + all_heads(jnp.where(kpos <= t_q, 0.0, MASK_VALUE))
        m_new = jnp.maximum(m_i, jnp.max(s, axis=0, keepdims=True))
        p = jnp.exp2(s - m_new).astype(BF16)
        v_t = jnp.concatenate([va_ref[blk0 + i, 0:V_ROWS, :] for i in range(blocks_per_tile)], axis=1)
        return m_new, jnp.exp2(m_i - m_new) * acc + _dot(v_t, p)

    def slc_trip(j, carry):
        score_tile(2 * j + 1, sb_ref)
        carry = absorb_tile(2 * j, sa_ref, carry, causal=False)
        score_tile(2 * j + 2, sa_ref)
        return absorb_tile(2 * j + 1, sb_ref, carry, causal=False)

    last_pair = qb // (2 * KEY_TILE // Q_BLOCK)
    score_tile(0, sa_ref)
    carry = lax.fori_loop(0, last_pair, slc_trip,
                          (jnp.full((1, cols), NEG_INF, F32), jnp.zeros((V_ROWS, cols), F32)))
    second_tile_live = (qb // (KEY_TILE // Q_BLOCK)) % 2 == 1

    @pl.when(second_tile_live)
    def _():
        score_tile(2 * last_pair + 1, sb_ref)
        both = absorb_tile(2 * last_pair, sa_ref, carry, causal=False)
        acc_ref[...] = absorb_tile(2 * last_pair + 1, sb_ref, both, causal=True)[1]

    @pl.when(jnp.logical_not(second_tile_live))
    def _():
        acc_ref[...] = absorb_tile(2 * last_pair, sa_ref, carry, causal=True)[1]

    acc_slc = acc_ref[...]
    o_slc = acc_slc * (1.0 / acc_slc[HEAD_DIM:HEAD_DIM + 1, :])

    gates = gate_ref[...]
    heads = []
    for r in range(NSA_REP):
        cs = slice(r * Q_BLOCK, (r + 1) * Q_BLOCK)
        c = r * N_BRANCH
        heads.append(gates[c:c + 1, :] * o_cmp[0:HEAD_DIM, cs] + gates[c + 1:c + 2, :] * o_slc[0:HEAD_DIM, cs]
                     + gates[c + 2:c + 3, :] * o_win[0:HEAD_DIM, cs])
    o_ref[...] = jnp.concatenate(heads, axis=0).T


def _overlap_matrix(seq):
    ncp, nsb = seq // CMP_STRIDE, seq // SLC_BLOCK
    c_start = np.arange(ncp)[None, :] * CMP_STRIDE
    s_start = np.arange(nsb)[:, None] * SLC_BLOCK
    ovl = np.clip(np.minimum(c_start + CMP_LEN, s_start + SLC_BLOCK) - np.maximum(c_start, s_start), 0, None)
    ovl = ovl.astype(np.float32) / CMP_LEN
    ovl[:, ncp - 1] = 0.0
    return jnp.asarray(np.pad(ovl, ((0, LANES - nsb), (0, 0))))


def _nsa_attention(q_t, k_cmp, v_cmp_t, ka, kb, va_t, vb_t, gates_t):
    b, _, _, s = q_t.shape
    g = NSA_KV_HEADS
    ncp = s // CMP_STRIDE
    assert s // SLC_BLOCK <= LANES
    per_group = lambda *shape: pl.BlockSpec((None, None) + shape, lambda i, j, k: (i, j) + (0,) * len(shape))
    return pl.pallas_call(
        functools.partial(_attn_kernel, seq=s),
        grid=(b, g, s // Q_BLOCK),
        in_specs=[pl.BlockSpec((None, NSA_REP, LANES, Q_BLOCK), lambda i, j, k: (i, j, 0, k)),
                  per_group(ncp, LANES), per_group(LANES, ncp), per_group(s, 2 * LANES), per_group(s, LANES),
                  per_group(s // LANES, LANES, LANES), per_group(s // LANES, LANES, LANES),
                  pl.BlockSpec((None, None, GATE_ROWS, Q_BLOCK), lambda i, j, k: (i, j, 0, k)),
                  _const_spec((LANES, ncp))],
        out_specs=pl.BlockSpec((None, Q_BLOCK, NSA_REP * HEAD_DIM), lambda i, j, k: (i, k, j)),
        out_shape=jax.ShapeDtypeStruct((b, s, NSA_WIDTH), F32),
        scratch_shapes=[pltpu.VMEM((KEY_TILE, NSA_REP * Q_BLOCK), F32)] * 2
        + [pltpu.VMEM((V_ROWS, NSA_REP * Q_BLOCK), F32)],
        compiler_params=_params(("parallel", "parallel", "arbitrary")),
        name="nsa_attention",
    )(q_t, k_cmp, v_cmp_t, ka, kb, va_t, vb_t, gates_t, _overlap_matrix(s).astype(BF16))


def _s5_kernel(u_ref, mt_ref, bc_ref, cc_ref, a1_ref, a2_ref, d_ref, y_ref, h_ref, g_scr, hp_scr):
    @pl.when(pl.program_id(1) == 0)
    def _():
        h_ref[...] = jnp.zeros_like(h_ref)

    halves = range(S5_HALVES)
    u = [_fold_rows(u_ref.at[a], S5_CHUNK) for a in halves]
    ub = [x.astype(BF16) for x in u]
    y_local = [_dot(ub[a], mt_ref[a]) for a in halves]
    for a in halves:
        g_scr[:, a * S5_HALF_FOLD:(a + 1) * S5_HALF_FOLD] = _dot(ub[a], bc_ref[a])
    a1, a2 = a1_ref[...], a2_ref[...]
    n_state = S5_STATE // S5_HALVES

    def swap_re_im(h):
        parts = [h[:, k * n_state:(k + 1) * n_state] for k in range(2 * S5_HALVES)]
        return jnp.concatenate([parts[k ^ 1] for k in range(2 * S5_HALVES)], axis=1)

    def step(i, h):
        hp_scr[pl.ds(i, 1), :] = h
        return a1 * h + a2 * swap_re_im(h) + g_scr[pl.ds(i, 1), :]

    h_ref[...] = lax.fori_loop(0, g_scr.shape[0], step, h_ref[...], unroll=8)
    for a in halves:
        carried = _dot(hp_scr[:, a * S5_HALF_FOLD:(a + 1) * S5_HALF_FOLD].astype(BF16), cc_ref[a])
        _unfold_rows(y_ref.at[a], y_local[a] + carried + u[a] * d_ref[a:a + 1, :], S5_CHUNK)


def _s5_scan(u, layer, mats):
    b, _, s, _ = u.shape
    rows = s // S5_CHUNK
    tc = min(128, rows)
    tile = pl.BlockSpec((None, S5_HALVES, tc * S5_CHUNK, LANES), lambda i, j: (i, 0, j, 0))
    mat = _layer_spec((S5_HALVES, S5_HALF_FOLD, S5_HALF_FOLD), layer)
    row = _layer_spec((1, S5_HALVES * S5_HALF_FOLD), layer)
    return pl.pallas_call(
        _s5_kernel,
        grid=(b, rows // tc),
        in_specs=[tile, mat, mat, mat, row, row, _layer_spec((S5_HALVES, S5_HALF_FOLD), layer)],
        out_specs=tile,
        out_shape=jax.ShapeDtypeStruct((b, S5_HALVES, s, LANES), F32),
        scratch_shapes=[pltpu.VMEM((1, S5_HALVES * S5_HALF_FOLD), F32),
                        pltpu.VMEM((tc, S5_HALVES * S5_HALF_FOLD), F32),
                        pltpu.VMEM((tc, S5_HALVES * S5_HALF_FOLD), F32)],
        compiler_params=_params(("parallel", "arbitrary")),
        name="s5_scan",
    )(u, *mats)


def _s5_matrices(lam_re, lam_im, log_dt, b_re, b_im, c_re, c_im, d_skip):
    t0, ng, nh, npm = S5_CHUNK, S5_GROUPS, S5_H, S5_P
    gh = ng // S5_HALVES
    ein = functools.partial(jnp.einsum, precision=HIGHEST)
    lam = lax.complex(lam_re, lam_im)
    step = jnp.exp(log_dt)[:, None]
    lam_bar = jnp.exp(lam * step)
    b_bar = lax.complex(b_re, b_im) * ((lam_bar - 1.0) / lam)[..., None]
    c_mat = lax.complex(c_re, c_im)
    k = jnp.arange(t0 + 1, dtype=F32)[:, None, None]
    pw = jnp.exp((lam * step)[None] * k)

    def same_group(rows_per_group, cols_per_group):
        r = np.arange(gh * rows_per_group)[:, None] // rows_per_group
        c = np.arange(gh * cols_per_group)[None, :] // cols_per_group
        return jnp.asarray((r == c).astype(np.float32))

    def per_group_blocks(x, rows_per_group, cols_per_group):
        return jnp.tile(x, (1,) * (x.ndim - 1) + (gh,)) * same_group(rows_per_group, cols_per_group)

    kern = jnp.real(ein('ghp,kgp,gpq->kghq', c_mat, pw[:t0], b_bar))
    d_k = per_group_blocks(kern.transpose(0, 1, 3, 2).reshape(t0, S5_HALVES, gh * nh, nh), nh, nh)
    d_k = jnp.concatenate([d_k, jnp.zeros_like(d_k[:1])], axis=0)
    lag = np.arange(t0)[None, :] - np.arange(t0)[:, None]
    mt = d_k[np.where(lag >= 0, lag, t0)]
    mt = mt.transpose(2, 0, 3, 1, 4).reshape(S5_HALVES, S5_HALF_FOLD, S5_HALF_FOLD)
    b_j = (pw[t0 - 1 - np.arange(t0)][..., None] * b_bar[None]).transpose(0, 1, 3, 2)
    b_j = b_j.reshape(t0, S5_HALVES, gh * nh, npm)
    bc = jnp.concatenate([per_group_blocks(jnp.real(b_j), nh, npm), per_group_blocks(jnp.imag(b_j), nh, npm)],
                         axis=-1)
    bc = bc.transpose(1, 0, 2, 3).reshape(S5_HALVES, S5_HALF_FOLD, S5_HALF_FOLD)
    c_i = (c_mat[None] * pw[1:t0 + 1][:, :, None, :]).transpose(1, 3, 0, 2)
    c_i = c_i.reshape(S5_HALVES, gh * npm, t0, nh)
    mask = same_group(npm, nh)[:, None, :]
    cc = jnp.concatenate([jnp.tile(jnp.real(c_i), (1, 1, 1, gh)) * mask,
                          jnp.tile(-jnp.imag(c_i), (1, 1, 1, gh)) * mask], axis=1)
    cc = cc.reshape(S5_HALVES, S5_HALF_FOLD, S5_HALF_FOLD)
    a_chunk = pw[t0].reshape(S5_HALVES, gh * npm)
    a1 = jnp.concatenate([jnp.real(a_chunk), jnp.real(a_chunk)], axis=1).reshape(1, -1)
    a2 = jnp.concatenate([-jnp.imag(a_chunk), jnp.imag(a_chunk)], axis=1).reshape(1, -1)
    d_vec = jnp.tile(d_skip.reshape(S5_HALVES, gh * nh), (1, t0))
    return mt.astype(BF16), bc.astype(BF16), cc.astype(BF16), a1, a2, d_vec


def _rms_gain(y, gain):
    return y * lax.rsqrt(jnp.mean(y * y, axis=-1, keepdims=True) + EPS) * gain


def _outproj_kernel(x_ref, mod_ref, pool_ref, halo_ref, nsa_ref, s5_ref, pw_ref, pb_ref, ps_ref, on_ref,
                    gw_ref, gb_ref, wo_ref, o_ref, buf):
    j = pl.program_id(1)
    tm = x_ref.shape[0]
    v = pool_ref[...]
    buf[0:POOL_HALO, :] = jnp.where(j > 0, halo_ref[...], 0.0)
    buf[POOL_HALO:POOL_HALO + tm, :] = v
    lane_group = lax.broadcasted_iota(jnp.int32, (tm, POOL_WIDTH), 1) >> SLC_SHIFT
    t1 = (j * tm + 1 + lax.broadcasted_iota(jnp.int32, (tm, 1), 0)).astype(F32)
    run, k, pooled = v, 1, jnp.zeros_like(v)
    for gi, w in enumerate(POOL_WINDOWS):
        while k < w:
            run = run + buf[POOL_HALO - k:POOL_HALO - k + tm, :]
            k += 1
        pooled = jnp.where(lane_group == gi, run / jnp.minimum(t1, float(w)) - v, pooled)
    y_pool = (_dot(pooled.astype(BF16), pw_ref[...]) + pb_ref[...]) * ps_ref[...]

    y = _gelu_tanh(jnp.concatenate([s5_ref[half] for half in range(S5_HALVES)], axis=-1))
    y_s5 = y * _sigmoid(_dot(y.astype(BF16), gw_ref[...]) + gb_ref[...])

    cat = jnp.concatenate(
        [_rms_gain(y_pool, on_ref[:, 0:POOL_WIDTH]),
         _rms_gain(nsa_ref[...], on_ref[:, POOL_WIDTH:POOL_WIDTH + NSA_WIDTH]),
         _rms_gain(y_s5, on_ref[:, POOL_WIDTH + NSA_WIDTH:])], axis=-1).astype(BF16)
    o_ref[...] = x_ref[...] + mod_ref[5:6, :] * _dot(cat, wo_ref[...])


def _outproj(x, layer, mod, u_pool, o_nsa, y_s5, pool_w_bd, pool_b, pool_scale, out_norm, glu_w, glu_b, w_out,
             tm=256):
    b, s, d = x.shape
    tok = lambda width: pl.BlockSpec((None, tm, width), lambda i, j: (i, j, 0))
    halo_blocks = tm // POOL_HALO
    return pl.pallas_call(
        _outproj_kernel,
        grid=(b, s // tm),
        in_specs=[tok(d),
                  _mod_spec(layer),
                  tok(POOL_WIDTH),
                  pl.BlockSpec((None, POOL_HALO, POOL_WIDTH),
                               lambda i, j: (i, jnp.maximum(j * halo_blocks - 1, 0), 0)),
                  tok(NSA_WIDTH),
                  pl.BlockSpec((None, S5_HALVES, tm, LANES), lambda i, j: (i, 0, j, 0)),
                  _layer_spec((POOL_WIDTH, POOL_WIDTH), layer), _layer_spec((1, POOL_WIDTH), layer),
                  _layer_spec((1, POOL_WIDTH), layer), _layer_spec((1, d), layer),
                  _layer_spec((S5_WIDTH, S5_WIDTH), layer), _layer_spec((1, S5_WIDTH), layer),
                  _layer_spec((d, d), layer)],
        out_specs=tok(d),
        out_shape=jax.ShapeDtypeStruct((b, s, d), F32),
        scratch_shapes=[pltpu.VMEM((POOL_HALO + tm, POOL_WIDTH), F32)],
        compiler_params=_params(("parallel", "arbitrary")),
        name="mixer_out_proj",
    )(x, mod, u_pool, u_pool, o_nsa, y_s5, pool_w_bd, pool_b, pool_scale, out_norm, glu_w, glu_b, w_out)


def _rope_tables(pos):
    half = ROT_DIM // 2
    inv_freq = jnp.exp(-math.log(ROPE_THETA) * jnp.arange(half, dtype=F32) * (2.0 / ROT_DIM))
    ang = pos[:, None] * inv_freq[None, :]
    cos, sin = jnp.cos(ang), jnp.sin(ang)
    n = pos.shape[0]
    rest = HEAD_DIM - ROT_DIM
    cos_t = jnp.concatenate([cos, cos, jnp.ones((n, rest), F32)], axis=1)
    sin_lo = jnp.concatenate([-sin, jnp.zeros((n, half + rest), F32)], axis=1)
    sin_hi = jnp.concatenate([jnp.zeros((n, half), F32), sin, jnp.zeros((n, rest), F32)], axis=1)
    return tuple(jnp.tile(t, (1, 2)) for t in (cos_t, sin_lo, sin_hi))


def _pad_lanes(v):
    return jnp.pad(v, [(0, 0)] * (v.ndim - 1) + [(0, LANES - v.shape[-1])])


def _rope_tables_t(pos):
    half = ROT_DIM // 2
    inv_freq = jnp.exp(-math.log(ROPE_THETA) * jnp.arange(half, dtype=F32) * (2.0 / ROT_DIM))
    ang = inv_freq[:, None] * pos[None, :]
    return jnp.cos(ang), jnp.sin(ang)


def _arrange_w_in(w):
    d = w.shape[0]
    o1, o2, o3 = POOL_WIDTH, POOL_WIDTH + NSA_WIDTH, POOL_WIDTH + NSA_WIDTH + 6 * LANES
    kv = w[:, o2:o3].reshape(d, 6, NSA_KV_HEADS, HEAD_DIM)
    ka = jnp.concatenate([kv[:, 2], kv[:, 4]], axis=-1).reshape(d, NSA_KV_HEADS * LANES)
    n_gate = NSA_REP * N_BRANCH
    w_row = jnp.concatenate([w[:, :o1], ka, kv[:, 0].reshape(d, LANES), kv[:, 1].reshape(d, LANES),
                             w[:, o3 + NSA_KV_HEADS * n_gate:]], axis=1)
    q_t = w[:, o1:o2].T
    v_t = jnp.concatenate([kv[:, 3], kv[:, 5]], axis=-1).reshape(d, NSA_KV_HEADS * LANES).T
    gate_t = w[:, o3:o3 + NSA_KV_HEADS * n_gate].T.reshape(NSA_KV_HEADS, n_gate, d)
    gate_t = jnp.pad(gate_t, ((0, 0), (0, GATE_ROWS - n_gate), (0, 0))).reshape(NSA_KV_HEADS * GATE_ROWS, d)
    return w_row.astype(BF16), jnp.concatenate([q_t, v_t, gate_t], axis=0).astype(BF16)


def _expand_cmp_w1(w1):
    halves = w1.reshape(2, CMP_STRIDE, HEAD_DIM, CMP_HIDDEN)
    both = jnp.concatenate([halves[0], halves[1]], axis=-1)
    out = []
    for g in range(NSA_KV_HEADS):
        z = jnp.zeros((CMP_STRIDE, NSA_KV_HEADS, HEAD_DIM, 2 * CMP_HIDDEN), F32).at[:, g].set(both)
        out.append(z.reshape(CMP_STRIDE * LANES, 2 * CMP_HIDDEN))
    return jnp.stack(out).astype(BF16)


def _block_diag(w):
    g, c, _ = w.shape
    return jnp.einsum('gcd,gf->gcfd', w, jnp.eye(g, dtype=w.dtype)).reshape(g * c, g * c)


def _prepare_parameters(norm_ffn1, ffn1_w_in, ffn1_w_out, norm_mix, w_in, w_out, out_norm, pool_w, pool_b, pool_scale,
                        q_norm, k_norm, cmp_pe, cmp_k_w1, cmp_k_w2, cmp_v_w1, cmp_v_w2, s5_lam_re, s5_lam_im,
                        s5_log_dt, s5_b_re, s5_b_im, s5_c_re, s5_c_im, s5_d, glu_w, glu_b, norm_ffn2, ffn2_w_in,
                        ffn2_w_out):
    n_layers = norm_mix.shape[0]
    row = lambda v: v.reshape(n_layers, 1, -1)
    w_row, w_t = jax.vmap(_arrange_w_in)(w_in)
    return dict(
        ffn1=(row(norm_ffn1), ffn1_w_in.astype(BF16), ffn1_w_out.astype(BF16)),
        ffn2=(row(norm_ffn2), ffn2_w_in.astype(BF16), ffn2_w_out.astype(BF16)),
        inproj=(row(norm_mix), w_row, w_t, q_norm.reshape(n_layers, HEAD_DIM, 1),
                jnp.concatenate([k_norm[:, 1], k_norm[:, 2]], axis=-1).reshape(n_layers, 1, LANES)),
        compress=(jax.vmap(_expand_cmp_w1)(cmp_k_w1), jax.vmap(_expand_cmp_w1)(cmp_v_w1), cmp_k_w1, cmp_v_w1,
                  cmp_pe.reshape(n_layers, 2, CMP_LEN * HEAD_DIM), _pad_lanes(cmp_k_w2).astype(BF16),
                  jnp.swapaxes(_pad_lanes(cmp_v_w2), 1, 2).astype(BF16), row(_pad_lanes(k_norm[:, 0]))),
        s5=jax.vmap(_s5_matrices)(s5_lam_re, s5_lam_im, s5_log_dt, s5_b_re, s5_b_im, s5_c_re, s5_c_im, s5_d),
        outproj=(jax.vmap(_block_diag)(pool_w).astype(BF16), row(pool_b), row(pool_scale), row(out_norm),
                 glu_w.astype(BF16), row(glu_b), w_out.astype(BF16)))


def _hybrid_layer(x, layer, mod, params, tabs, tabs_t, cmp_tabs):
    x = _ffn(x, layer, mod, *params["ffn1"], 0)
    u_pool, q_t, ka, kb, va_t, vb_t, kc, vc, gates_t, u_s5 = _inproj(
        x, layer, mod, *params["inproj"][:3], tabs, tabs_t, *params["inproj"][3:])
    k_cmp, v_cmp_t = _compress(kc, vc, layer, *params["compress"], cmp_tabs)
    o_nsa = _nsa_attention(q_t, k_cmp, v_cmp_t, ka, kb, va_t, vb_t, gates_t)
    y_s5 = _s5_scan(u_s5, layer, params["s5"])
    x = _outproj(x, layer, mod, u_pool, o_nsa, y_s5, *params["outproj"])
    return _ffn(x, layer, mod, *params["ffn2"], 6)


def kernel(x, c, ada_w, ada_b, norm_ffn1, ffn1_w_in, ffn1_w_out, norm_mix, w_in, w_out, out_norm, pool_w, pool_b, pool_scale, q_norm, k_norm, cmp_pe, cmp_k_w1, cmp_k_w2, cmp_v_w1, cmp_v_w2, s5_lam_re, s5_lam_im, s5_log_dt, s5_b_re, s5_b_im, s5_c_re, s5_c_im, s5_d, glu_w, glu_b, norm_ffn2, ffn2_w_in, ffn2_w_out):
    seq = x.shape[1]
    assert seq % (KEY_TILE * 4) == 0 and seq >= WINDOW + Q_BLOCK
    mod = _modulation(c, ada_w, ada_b)
    params = _prepare_parameters(norm_ffn1, ffn1_w_in, ffn1_w_out, norm_mix, w_in, w_out, out_norm, pool_w, pool_b,
                                 pool_scale, q_norm, k_norm, cmp_pe, cmp_k_w1, cmp_k_w2, cmp_v_w1, cmp_v_w2,
                                 s5_lam_re, s5_lam_im, s5_log_dt, s5_b_re, s5_b_im, s5_c_re, s5_c_im, s5_d, glu_w,
                                 glu_b, norm_ffn2, ffn2_w_in, ffn2_w_out)
    tabs = _rope_tables(jnp.arange(seq, dtype=F32))
    tabs_t = _rope_tables_t(jnp.arange(seq, dtype=F32))
    cmp_tabs = _rope_tables((jnp.arange(seq // CMP_STRIDE) * CMP_STRIDE + CMP_LEN - 1).astype(F32))
    for layer in range(ada_w.shape[0]):
        x = _hybrid_layer(x, layer, mod, params, tabs, tabs_t, cmp_tabs)
    return x
```

```python
import functools
import math

import jax
import jax.numpy as jnp
import numpy as np
from jax import lax
from jax.experimental import pallas as pl
from jax.experimental.pallas import tpu as pltpu

F32 = jnp.float32
BF16 = jnp.bfloat16
HIGHEST = lax.Precision.HIGHEST

LANES = 128
SUBLANES = 8
VMEM_LIMIT_BYTES = 56 * 1024 * 1024

D_MODEL = 1024
D_FF = 2816
N_MOD = 9
POOL_WIDTH = 256
POOL_GC = 64
POOL_WINDOWS = (2, 4, 8, 16)
POOL_HALO = 16
HEAD_DIM = 64
NSA_WIDTH = 512
NSA_HEADS = 8
NSA_KV_HEADS = 2
NSA_REP = 4
N_BRANCH = 3
S5_WIDTH = 256
S5_H = 16
S5_GROUPS = 16
S5_P = 64
CMP_STRIDE = 16
CMP_LEN = 32
CMP_HIDDEN = 128
SLC_BLOCK = 64
SLC_SHIFT = 6
N_SELECT = 16
N_FORCED = 3
WINDOW = 512
Q_BLOCK = 256
ROT_DIM = 16
ROPE_THETA = 500000.0
EPS = 1e-6
Q_SCALE = HEAD_DIM ** -0.5 * math.log2(math.e)
NEG_INF = -1e30
MASK_VALUE = -(2.0 ** 100)

OFF_POOL = 0
OFF_KA = OFF_POOL + POOL_WIDTH
OFF_KC = OFF_KA + NSA_KV_HEADS * LANES
OFF_VC = OFF_KC + LANES
OFF_S5 = OFF_VC + LANES
N_COLS = OFF_S5 + S5_WIDTH
GATE_ROWS = 16
ROW_Q = 0
ROW_V = ROW_Q + NSA_HEADS * HEAD_DIM
ROW_GATE = ROW_V + NSA_KV_HEADS * LANES
N_ROWS_T = ROW_GATE + NSA_KV_HEADS * GATE_ROWS

S5_CHUNK = 8
S5_FOLD = S5_CHUNK * S5_WIDTH
S5_STATE = S5_GROUPS * S5_P
S5_HALVES = S5_WIDTH // LANES
S5_HALF_FOLD = S5_FOLD // S5_HALVES
KEY_TILE = 512
V_ROWS = HEAD_DIM + 16


def _dot(a, b, precision=None):
    return jnp.dot(a, b, preferred_element_type=F32, precision=precision)


def _dot_nt(a, b, precision=None):
    return lax.dot_general(a, b, (((1,), (1,)), ((), ())), preferred_element_type=F32,
                           precision=precision)


def _sigmoid(x):
    return 1.0 / (1.0 + jnp.exp(-x))


def _gelu_tanh(x):
    return 0.5 * x * (1.0 + jnp.tanh(math.sqrt(2.0 / math.pi) * (x + 0.044715 * (x * x * x))))


def _params(sem):
    return pltpu.CompilerParams(dimension_semantics=sem, vmem_limit_bytes=VMEM_LIMIT_BYTES)


def _const_spec(shape):
    nd = len(shape)
    return pl.BlockSpec(shape, lambda *_: (0,) * nd, pipeline_mode=pl.Buffered(1))


def _layer_spec(shape, layer):
    nd = len(shape)
    return pl.BlockSpec((None,) + shape, lambda *_: (layer,) + (0,) * nd, pipeline_mode=pl.Buffered(1))


def _mod_kernel(c_ref, w_ref, b_ref, o_ref):
    c = c_ref[...]
    o_ref[...] = _dot(c * _sigmoid(c), w_ref[...], HIGHEST) + b_ref[...]


def _modulation(c, ada_w, ada_b):
    n_layers, d, n = ada_w.shape
    b = c.shape[0]
    tn = 1152
    c_pad = jnp.zeros((SUBLANES, d), F32).at[:b].set(c)
    out = pl.pallas_call(
        _mod_kernel,
        grid=(n_layers, n // tn),
        in_specs=[pl.BlockSpec((SUBLANES, d), lambda l, j: (0, 0)),
                  pl.BlockSpec((None, d, tn), lambda l, j: (l, 0, j)),
                  pl.BlockSpec((None, 1, tn), lambda l, j: (l, 0, j))],
        out_specs=pl.BlockSpec((None, SUBLANES, tn), lambda l, j: (l, 0, j)),
        out_shape=jax.ShapeDtypeStruct((n_layers, SUBLANES, n), F32),
        compiler_params=_params(("arbitrary", "arbitrary")),
        name="adaln_mod",
    )(c_pad, ada_w, ada_b.reshape(n_layers, 1, n))
    return out[:, :b].reshape(n_layers, b, N_MOD, d)


def _norm_modulate(x, gain, mod_ref, first_row):
    ms = jnp.mean(x * x, axis=-1, keepdims=True)
    y = x * lax.rsqrt(ms + EPS) * gain
    return y * (1.0 + mod_ref[first_row + 1:first_row + 2, :]) + mod_ref[first_row:first_row + 1, :]


def _ffn_kernel(x_ref, mod_ref, g_ref, win_ref, wout_ref, o_ref, *, first_row):
    x = x_ref[...]
    h = _norm_modulate(x, g_ref[...], mod_ref, first_row).astype(BF16)
    gu = _dot(h, win_ref[...])
    gate, up = gu[:, :D_FF], gu[:, D_FF:]
    a = (gate * _sigmoid(gate) * up).astype(BF16)
    y = _dot(a, wout_ref[...])
    o_ref[...] = x + 0.5 * mod_ref[first_row + 2:first_row + 3, :] * y


def _mod_spec(layer):
    return pl.BlockSpec((None, None, N_MOD, D_MODEL), lambda i, *_: (layer, i, 0, 0))


def _ffn(x, layer, mod, gain, w_in, w_out, first_row, tm=512):
    b, s, d = x.shape
    return pl.pallas_call(
        functools.partial(_ffn_kernel, first_row=first_row),
        grid=(b, s // tm),
        in_specs=[pl.BlockSpec((None, tm, d), lambda i, j: (i, j, 0)),
                  _mod_spec(layer),
                  _layer_spec((1, d), layer),
                  _layer_spec((d, 2 * D_FF), layer),
                  _layer_spec((D_FF, d), layer)],
        out_specs=pl.BlockSpec((None, tm, d), lambda i, j: (i, j, 0)),
        out_shape=jax.ShapeDtypeStruct((b, s, d), F32),
        compiler_params=_params(("parallel", "parallel")),
        name="ffn_half_step",
    )(x, mod, gain, w_in, w_out)


def _rope(v, cos_t, sin_lo, sin_hi):
    return (v * cos_t + pltpu.roll(v, LANES - ROT_DIM // 2, 1) * sin_lo
            + pltpu.roll(v, ROT_DIM // 2, 1) * sin_hi)


def _inproj_kernel(x_ref, mod_ref, g_ref, w_ref, wt_ref, cos_ref, slo_ref, shi_ref, cost_ref, sint_ref, qg_ref, kg_ref,
                   pool_ref, q_ref, ka_ref, kb_ref, va_ref, vb_ref, kc_ref, vc_ref, gate_ref, s5_ref):
    h = _norm_modulate(x_ref[...], g_ref[...], mod_ref, 3).astype(BF16)
    u = _dot(h, w_ref[...])
    ut = _dot_nt(wt_ref[...], h)
    tm = u.shape[0]
    cos_t, sin_lo, sin_hi = cos_ref[...], slo_ref[...], shi_ref[...]
    lane = lax.broadcasted_iota(jnp.int32, (tm, LANES), 1)
    low_half = lane < HEAD_DIM
    pos = pl.program_id(1) * tm + lax.broadcasted_iota(jnp.int32, (tm, LANES), 0)
    block_one_hot = jnp.where((pos >> SLC_SHIFT) == lane, 1.0, 0.0).astype(BF16)

    pool_ref[...] = u[:, OFF_POOL:OFF_POOL + POOL_WIDTH]
    kc_ref[...] = u[:, OFF_KC:OFF_KC + LANES]
    vc_ref[...] = u[:, OFF_VC:OFF_VC + LANES]
    for half in range(S5_HALVES):
        s5_ref[half] = u[:, OFF_S5 + half * LANES:OFF_S5 + (half + 1) * LANES]
    for g in range(NSA_KV_HEADS):
        v = u[:, OFF_KA + g * LANES:OFF_KA + (g + 1) * LANES]
        sq = v * v
        ms_lo = jnp.sum(jnp.where(low_half, sq, 0.0), axis=-1, keepdims=True) * (1.0 / HEAD_DIM)
        ms_hi = jnp.sum(jnp.where(low_half, 0.0, sq), axis=-1, keepdims=True) * (1.0 / HEAD_DIM)
        r = jnp.where(low_half, lax.rsqrt(ms_lo + EPS), lax.rsqrt(ms_hi + EPS))
        kn = _rope(v * r * kg_ref[...], cos_t, sin_lo, sin_hi)
        ka_ref[g, :, 0:LANES] = kn.astype(BF16)
        ka_ref[g, :, LANES:2 * LANES] = block_one_hot
        kb_ref[g] = pltpu.roll(kn, HEAD_DIM, 1).astype(BF16)

    half_rot = ROT_DIM // 2
    cos8, sin8 = cost_ref[...], sint_ref[...]
    zero_rows = jnp.zeros((LANES - HEAD_DIM, tm), F32)
    for hd in range(NSA_HEADS):
        v = ut[ROW_Q + hd * HEAD_DIM:ROW_Q + (hd + 1) * HEAD_DIM, :]
        ms = jnp.sum(v * v, axis=0, keepdims=True) * (1.0 / HEAD_DIM)
        vn = v * lax.rsqrt(ms + EPS) * qg_ref[...]
        x1, x2 = vn[0:half_rot], vn[half_rot:ROT_DIM]
        roped = jnp.concatenate([x1 * cos8 - x2 * sin8, x2 * cos8 + x1 * sin8, vn[ROT_DIM:]], axis=0)
        q_ref[hd] = jnp.concatenate([roped * Q_SCALE, zero_rows], axis=0).astype(BF16)
    ones = jnp.ones((HEAD_DIM, tm), F32)
    for g in range(NSA_KV_HEADS):
        vt = ut[ROW_V + g * LANES:ROW_V + (g + 1) * LANES, :]
        va = jnp.concatenate([vt[0:HEAD_DIM], ones], axis=0).astype(BF16)
        vb = jnp.concatenate([vt[HEAD_DIM:], ones], axis=0).astype(BF16)
        for blk in range(tm // LANES):
            va_ref[g, blk] = va[:, blk * LANES:(blk + 1) * LANES]
            vb_ref[g, blk] = vb[:, blk * LANES:(blk + 1) * LANES]
        gate_ref[g] = _sigmoid(ut[ROW_GATE + g * GATE_ROWS:ROW_GATE + (g + 1) * GATE_ROWS, :])


def _inproj(x, layer, mod, gain, w_row, w_t, rope_tabs, rope_tabs_t, q_gain, k_gain, tm=256):
    b, s, d = x.shape
    g = NSA_KV_HEADS
    tok = lambda width: pl.BlockSpec((None, tm, width), lambda i, j: (i, j, 0))
    grp = lambda n, width=LANES: pl.BlockSpec((None, n, tm, width), lambda i, j: (i, 0, j, 0))
    lanes_tok = lambda n, rows: pl.BlockSpec((None, n, rows, tm), lambda i, j: (i, 0, 0, j))
    v_blocks = pl.BlockSpec((None, g, tm // LANES, LANES, LANES), lambda i, j: (i, 0, j, 0, 0))
    tab = pl.BlockSpec((tm, LANES), lambda i, j: (j, 0))
    tab_t = pl.BlockSpec((ROT_DIM // 2, tm), lambda i, j: (0, j))
    sds = jax.ShapeDtypeStruct
    return pl.pallas_call(
        _inproj_kernel,
        grid=(b, s // tm),
        in_specs=[tok(d),
                  _mod_spec(layer),
                  _layer_spec((1, d), layer),
                  _layer_spec((d, N_COLS), layer), _layer_spec((N_ROWS_T, d), layer),
                  tab, tab, tab, tab_t, tab_t,
                  _layer_spec((HEAD_DIM, 1), layer), _layer_spec((1, LANES), layer)],
        out_specs=[tok(POOL_WIDTH), lanes_tok(NSA_HEADS, LANES), grp(g, 2 * LANES), grp(g), v_blocks, v_blocks,
                   tok(LANES), tok(LANES), lanes_tok(g, GATE_ROWS), grp(S5_HALVES)],
        out_shape=[sds((b, s, POOL_WIDTH), F32), sds((b, NSA_HEADS, LANES, s), BF16),
                   sds((b, g, s, 2 * LANES), BF16), sds((b, g, s, LANES), BF16),
                   sds((b, g, s // LANES, LANES, LANES), BF16), sds((b, g, s // LANES, LANES, LANES), BF16),
                   sds((b, s, LANES), F32), sds((b, s, LANES), F32),
                   sds((b, g, GATE_ROWS, s), F32), sds((b, S5_HALVES, s, LANES), F32)],
        compiler_params=_params(("parallel", "parallel")),
        name="mixer_in_proj",
    )(x, mod, gain, w_row, w_t, *rope_tabs, *rope_tabs_t, q_gain, k_gain)


def _fold_rows(ref, n):
    rows = ref.shape[0] // n
    return jnp.concatenate([ref[pl.ds(k, rows, stride=n), :] for k in range(n)], axis=1)


def _unfold_rows(ref, value, n):
    rows = ref.shape[0] // n
    for k in range(n):
        ref[pl.ds(k, rows, stride=n), :] = value[:, k * LANES:(k + 1) * LANES]


def _compress_kernel(kc_ref, vc_ref, w1k_ref, w1v_ref, w1k_raw_ref, w1v_raw_ref, pe_ref, w2k_ref, w2vt_ref,
                     kg_ref, cos_ref, slo_ref, shi_ref, ko_ref, vo_ref, *, n_cmp):
    ncp = kc_ref.shape[0] // CMP_STRIDE
    for src_ref, w1_ref, raw_ref, pe_row, is_key in ((kc_ref, w1k_ref, w1k_raw_ref, 0, True),
                                                    (vc_ref, w1v_ref, w1v_raw_ref, 1, False)):
        chunks = _fold_rows(src_ref, CMP_STRIDE).astype(BF16)
        pe = jnp.broadcast_to(pe_ref[pe_row:pe_row + 1, :], (SUBLANES, CMP_LEN * HEAD_DIM))
        pe_term = _dot(pe, raw_ref[...], HIGHEST)[0:1, :]
        for g in range(NSA_KV_HEADS):
            a = _dot(chunks, w1_ref[g])
            pre = a[:, :CMP_HIDDEN] + pltpu.roll(a[:, CMP_HIDDEN:], ncp - 1, 0) + pe_term
            hidden = _gelu_tanh(pre).astype(BF16)
            if is_key:
                out = _dot(hidden, w2k_ref[...])
                ms = jnp.sum(out * out, axis=-1, keepdims=True) * (1.0 / HEAD_DIM)
                out = _rope(out * lax.rsqrt(ms + EPS) * kg_ref[...], cos_ref[...], slo_ref[...], shi_ref[...])
                real_row = lax.broadcasted_iota(jnp.int32, (ncp, LANES), 0) < n_cmp
                ko_ref[g] = jnp.where(real_row, out, 0.0).astype(BF16)
            else:
                out_t = _dot_nt(w2vt_ref[...], hidden)
                real_col = lax.broadcasted_iota(jnp.int32, (LANES, ncp), 1) < n_cmp
                value_row = lax.broadcasted_iota(jnp.int32, (LANES, ncp), 0) < HEAD_DIM
                vo_ref[g] = jnp.where(value_row, jnp.where(real_col, out_t, 0.0), 1.0).astype(BF16)


def _compress(kc, vc, layer, w1k, w1v, w1k_raw, w1v_raw, pe, w2k, w2v_t, k_gain, cmp_tabs):
    b, s, _ = kc.shape
    ncp = s // CMP_STRIDE
    fold = CMP_STRIDE * LANES
    src = pl.BlockSpec((None, s, LANES), lambda i: (i, 0, 0))
    raw = _layer_spec((CMP_LEN * HEAD_DIM, CMP_HIDDEN), layer)
    w1 = _layer_spec((NSA_KV_HEADS, fold, 2 * CMP_HIDDEN), layer)
    w2 = _layer_spec((CMP_HIDDEN, LANES), layer)
    tab = _const_spec((ncp, LANES))
    return pl.pallas_call(
        functools.partial(_compress_kernel, n_cmp=ncp - 1),
        grid=(b,),
        in_specs=[src, src, w1, w1, raw, raw, _layer_spec((2, CMP_LEN * HEAD_DIM), layer), w2, w2,
                  _layer_spec((1, LANES), layer), tab, tab, tab],
        out_specs=[pl.BlockSpec((None, NSA_KV_HEADS, ncp, LANES), lambda i: (i, 0, 0, 0)),
                   pl.BlockSpec((None, NSA_KV_HEADS, LANES, ncp), lambda i: (i, 0, 0, 0))],
        out_shape=[jax.ShapeDtypeStruct((b, NSA_KV_HEADS, ncp, LANES), BF16),
                   jax.ShapeDtypeStruct((b, NSA_KV_HEADS, LANES, ncp), BF16)],
        compiler_params=_params(("parallel",)),
        name="nsa_compress",
    )(kc, vc, w1k, w1v, w1k_raw, w1v_raw, pe, w2k, w2v_t, k_gain, *cmp_tabs)


def _selection_bias(imp_t, qb):
    nsb, nq = imp_t.shape
    j = lax.broadcasted_iota(jnp.int32, (nsb, nq), 0)
    t = qb * Q_BLOCK + lax.broadcasted_iota(jnp.int32, (nsb, nq), 1)
    cur = t >> SLC_SHIFT
    valid = j * SLC_BLOCK <= t
    forced = (j == 0) | (j == cur) | (j == cur - 1)
    j_f = j.astype(F32)
    start = jnp.where(valid & jnp.logical_not(forced), imp_t, -1.0)
    vals = start
    for _ in range(N_SELECT - N_FORCED):
        m = jnp.max(vals, axis=0, keepdims=True)
        idx = jnp.min(jnp.where(vals == m, j_f, float(nsb)), axis=0, keepdims=True)
        vals = jnp.where(j_f == idx, -2.0, vals)
    return jnp.where((forced & valid) | (vals != start), 0.0, MASK_VALUE)


def _attn_kernel(q_ref, kc_ref, vct_ref, ka_ref, kb_ref, va_ref, vb_ref, gate_ref, ovl_ref, o_ref,
                 sa_ref, sb_ref, acc_ref, *, seq):
    qb = pl.program_id(2)
    cols = NSA_REP * Q_BLOCK
    ncp = seq // CMP_STRIDE
    q_t = jnp.concatenate([q_ref[r] for r in range(NSA_REP)], axis=1)
    t_col = qb * Q_BLOCK + lax.broadcasted_iota(jnp.int32, (1, cols), 1) % Q_BLOCK
    t_q = qb * Q_BLOCK + lax.broadcasted_iota(jnp.int32, (1, Q_BLOCK), 1)

    def all_heads(per_query):
        return jnp.concatenate([per_query] * NSA_REP, axis=1)

    cmp_end = lax.broadcasted_iota(jnp.int32, (ncp, Q_BLOCK), 0) * CMP_STRIDE + (CMP_LEN - 1)
    s = _dot(kc_ref[...], q_t) + all_heads(jnp.where(cmp_end <= t_q, 0.0, NEG_INF))
    e = jnp.exp2(s - jnp.max(s, axis=0, keepdims=True)).astype(BF16)
    oc = _dot(jnp.concatenate([vct_ref[0:V_ROWS, :], ovl_ref[...]], axis=0), e)
    oc = oc * jnp.where(t_col >= CMP_LEN - 1, 1.0 / oc[HEAD_DIM:HEAD_DIM + 1, :], 0.0)
    o_cmp, imp = oc[0:V_ROWS], oc[V_ROWS:V_ROWS + LANES]
    imp_t = (imp[:, 0:Q_BLOCK] + imp[:, Q_BLOCK:2 * Q_BLOCK]
             + imp[:, 2 * Q_BLOCK:3 * Q_BLOCK] + imp[:, 3 * Q_BLOCK:4 * Q_BLOCK])
    bias = _selection_bias(imp_t, qb).astype(BF16)

    q_aug = jnp.concatenate([q_t, jnp.concatenate([bias] * NSA_REP, axis=1)], axis=0)

    span = WINDOW + Q_BLOCK
    start = pl.multiple_of(jnp.maximum(qb * Q_BLOCK - WINDOW, 0), LANES)
    start_blk = start // LANES
    kpos = start + lax.broadcasted_iota(jnp.int32, (span, Q_BLOCK), 0)
    in_window = (kpos <= t_q) & (kpos > t_q - WINDOW)
    s = _dot(kb_ref[pl.ds(start, span), :], q_t) + all_heads(jnp.where(in_window, 0.0, NEG_INF))
    e = jnp.exp2(s - jnp.max(s, axis=0, keepdims=True)).astype(BF16)
    acc_win = _dot(jnp.concatenate([vb_ref[start_blk + i, 0:V_ROWS, :] for i in range(span // LANES)], axis=1), e)
    o_win = acc_win * (1.0 / acc_win[HEAD_DIM:HEAD_DIM + 1, :])

    blocks_per_tile = KEY_TILE // LANES

    def score_tile(kt, s_ref):
        k0 = pl.multiple_of(kt * KEY_TILE, KEY_TILE)
        s_ref[...] = _dot(ka_ref[pl.ds(k0, KEY_TILE), :], q_aug)

    def absorb_tile(kt, s_ref, carry, causal):
        m_i, acc = carry
        blk0 = kt * blocks_per_tile
        s = s_ref[...]
        if causal:
            kpos = kt * KEY_TILE + lax.broadcasted_iota(jnp.int32, (KEY_TILE, Q_BLOCK), 0)
            s = s + all_heads(jnp.where(kpos <= t_q, 0.0, MASK_VALUE))
        m_new = jnp.maximum(m_i, jnp.max(s, axis=0, keepdims=True))
        p = jnp.exp2(s - m_new).astype(BF16)
        v_t = jnp.concatenate([va_ref[blk0 + i, 0:V_ROWS, :] for i in range(blocks_per_tile)], axis=1)
        return m_new, jnp.exp2(m_i - m_new) * acc + _dot(v_t, p)

    def pair_trip(j, carry):
        score_tile(2 * j + 1, sb_ref)
        carry = absorb_tile(2 * j, sa_ref, carry, causal=False)
        score_tile(2 * j + 2, sa_ref)
        return absorb_tile(2 * j + 1, sb_ref, carry, causal=False)

    def double_trip(j, carry):
        return pair_trip(2 * j + 1, pair_trip(2 * j, carry))

    last_pair = qb // (2 * KEY_TILE // Q_BLOCK)
    score_tile(0, sa_ref)
    carry = (jnp.full((1, cols), NEG_INF, F32), jnp.zeros((V_ROWS, cols), F32))
    carry = lax.fori_loop(0, last_pair // 2, double_trip, carry)
    carry = lax.fori_loop(2 * (last_pair // 2), last_pair, pair_trip, carry)
    second_tile_live = (qb // (KEY_TILE // Q_BLOCK)) % 2 == 1

    @pl.when(second_tile_live)
    def _():
        score_tile(2 * last_pair + 1, sb_ref)
        both = absorb_tile(2 * last_pair, sa_ref, carry, causal=False)
        acc_ref[...] = absorb_tile(2 * last_pair + 1, sb_ref, both, causal=True)[1]

    @pl.when(jnp.logical_not(second_tile_live))
    def _():
        acc_ref[...] = absorb_tile(2 * last_pair, sa_ref, carry, causal=True)[1]

    acc_slc = acc_ref[...]
    o_slc = acc_slc * (1.0 / acc_slc[HEAD_DIM:HEAD_DIM + 1, :])

    gates = gate_ref[...]
    heads = []
    for r in range(NSA_REP):
        cs = slice(r * Q_BLOCK, (r + 1) * Q_BLOCK)
        c = r * N_BRANCH
        heads.append(gates[c:c + 1, :] * o_cmp[0:HEAD_DIM, cs] + gates[c + 1:c + 2, :] * o_slc[0:HEAD_DIM, cs]
                     + gates[c + 2:c + 3, :] * o_win[0:HEAD_DIM, cs])
    o_ref[...] = jnp.concatenate(heads, axis=0).T


def _overlap_matrix(seq):
    ncp, nsb = seq // CMP_STRIDE, seq // SLC_BLOCK
    c_start = np.arange(ncp)[None, :] * CMP_STRIDE
    s_start = np.arange(nsb)[:, None] * SLC_BLOCK
    ovl = np.clip(np.minimum(c_start + CMP_LEN, s_start + SLC_BLOCK) - np.maximum(c_start, s_start), 0, None)
    ovl = ovl.astype(np.float32) / CMP_LEN
    ovl[:, ncp - 1] = 0.0
    return jnp.asarray(np.pad(ovl, ((0, LANES - nsb), (0, 0))))


def _nsa_attention(q_t, k_cmp, v_cmp_t, ka, kb, va_t, vb_t, gates_t):
    b, _, _, s = q_t.shape
    g = NSA_KV_HEADS
    ncp = s // CMP_STRIDE
    assert s // SLC_BLOCK <= LANES
    per_group = lambda *shape: pl.BlockSpec((None, None) + shape, lambda i, j, k: (i, j) + (0,) * len(shape))
    return pl.pallas_call(
        functools.partial(_attn_kernel, seq=s),
        grid=(b, g, s // Q_BLOCK),
        in_specs=[pl.BlockSpec((None, NSA_REP, LANES, Q_BLOCK), lambda i, j, k: (i, j, 0, k)),
                  per_group(ncp, LANES), per_group(LANES, ncp), per_group(s, 2 * LANES), per_group(s, LANES),
                  per_group(s // LANES, LANES, LANES), per_group(s // LANES, LANES, LANES),
                  pl.BlockSpec((None, None, GATE_ROWS, Q_BLOCK), lambda i, j, k: (i, j, 0, k)),
                  _const_spec((LANES, ncp))],
        out_specs=pl.BlockSpec((None, Q_BLOCK, NSA_REP * HEAD_DIM), lambda i, j, k: (i, k, j)),
        out_shape=jax.ShapeDtypeStruct((b, s, NSA_WIDTH), F32),
        scratch_shapes=[pltpu.VMEM((KEY_TILE, NSA_REP * Q_BLOCK), F32)] * 2
        + [pltpu.VMEM((V_ROWS, NSA_REP * Q_BLOCK), F32)],
        compiler_params=_params(("parallel", "parallel", "arbitrary")),
        name="nsa_attention",
    )(q_t, k_cmp, v_cmp_t, ka, kb, va_t, vb_t, gates_t, _overlap_matrix(s).astype(BF16))


def _s5_kernel(u_ref, mt_ref, bc_ref, cc_ref, a1_ref, a2_ref, d_ref, y_ref, h_ref, g_scr, hp_scr):
    @pl.when(pl.program_id(1) == 0)
    def _():
        h_ref[...] = jnp.zeros_like(h_ref)

    halves = range(S5_HALVES)
    u = [_fold_rows(u_ref.at[a], S5_CHUNK) for a in halves]
    ub = [x.astype(BF16) for x in u]
    y_local = [_dot(ub[a], mt_ref[a]) for a in halves]
    for a in halves:
        g_scr[:, a * S5_HALF_FOLD:(a + 1) * S5_HALF_FOLD] = _dot(ub[a], bc_ref[a])
    a1, a2 = a1_ref[...], a2_ref[...]
    n_state = S5_STATE // S5_HALVES

    def swap_re_im(h):
        parts = [h[:, k * n_state:(k + 1) * n_state] for k in range(2 * S5_HALVES)]
        return jnp.concatenate([parts[k ^ 1] for k in range(2 * S5_HALVES)], axis=1)

    def step(i, h):
        hp_scr[pl.ds(i, 1), :] = h
        return a1 * h + a2 * swap_re_im(h) + g_scr[pl.ds(i, 1), :]

    h_ref[...] = lax.fori_loop(0, g_scr.shape[0], step, h_ref[...], unroll=8)
    for a in halves:
        carried = _dot(hp_scr[:, a * S5_HALF_FOLD:(a + 1) * S5_HALF_FOLD].astype(BF16), cc_ref[a])
        _unfold_rows(y_ref.at[a], y_local[a] + carried + u[a] * d_ref[a:a + 1, :], S5_CHUNK)


def _s5_scan(u, layer, mats):
    b, _, s, _ = u.shape
    rows = s // S5_CHUNK
    tc = min(128, rows)
    tile = pl.BlockSpec((None, S5_HALVES, tc * S5_CHUNK, LANES), lambda i, j: (i, 0, j, 0))
    mat = _layer_spec((S5_HALVES, S5_HALF_FOLD, S5_HALF_FOLD), layer)
    row = _layer_spec((1, S5_HALVES * S5_HALF_FOLD), layer)
    return pl.pallas_call(
        _s5_kernel,
        grid=(b, rows // tc),
        in_specs=[tile, mat, mat, mat, row, row, _layer_spec((S5_HALVES, S5_HALF_FOLD), layer)],
        out_specs=tile,
        out_shape=jax.ShapeDtypeStruct((b, S5_HALVES, s, LANES), F32),
        scratch_shapes=[pltpu.VMEM((1, S5_HALVES * S5_HALF_FOLD), F32),
                        pltpu.VMEM((tc, S5_HALVES * S5_HALF_FOLD), F32),
                        pltpu.VMEM((tc, S5_HALVES * S5_HALF_FOLD), F32)],
        compiler_params=_params(("parallel", "arbitrary")),
        name="s5_scan",
    )(u, *mats)


def _s5_matrices(lam_re, lam_im, log_dt, b_re, b_im, c_re, c_im, d_skip):
    t0, ng, nh, npm = S5_CHUNK, S5_GROUPS, S5_H, S5_P
    gh = ng // S5_HALVES
    ein = functools.partial(jnp.einsum, precision=HIGHEST)
    lam = lax.complex(lam_re, lam_im)
    step = jnp.exp(log_dt)[:, None]
    lam_bar = jnp.exp(lam * step)
    b_bar = lax.complex(b_re, b_im) * ((lam_bar - 1.0) / lam)[..., None]
    c_mat = lax.complex(c_re, c_im)
    k = jnp.arange(t0 + 1, dtype=F32)[:, None, None]
    pw = jnp.exp((lam * step)[None] * k)

    def same_group(rows_per_group, cols_per_group):
        r = np.arange(gh * rows_per_group)[:, None] // rows_per_group
        c = np.arange(gh * cols_per_group)[None, :] // cols_per_group
        return jnp.asarray((r == c).astype(np.float32))

    def per_group_blocks(x, rows_per_group, cols_per_group):
        return jnp.tile(x, (1,) * (x.ndim - 1) + (gh,)) * same_group(rows_per_group, cols_per_group)

    kern = jnp.real(ein('ghp,kgp,gpq->kghq', c_mat, pw[:t0], b_bar))
    d_k = per_group_blocks(kern.transpose(0, 1, 3, 2).reshape(t0, S5_HALVES, gh * nh, nh), nh, nh)
    d_k = jnp.concatenate([d_k, jnp.zeros_like(d_k[:1])], axis=0)
    lag = np.arange(t0)[None, :] - np.arange(t0)[:, None]
    mt = d_k[np.where(lag >= 0, lag, t0)]
    mt = mt.transpose(2, 0, 3, 1, 4).reshape(S5_HALVES, S5_HALF_FOLD, S5_HALF_FOLD)
    b_j = (pw[t0 - 1 - np.arange(t0)][..., None] * b_bar[None]).transpose(0, 1, 3, 2)
    b_j = b_j.reshape(t0, S5_HALVES, gh * nh, npm)
    bc = jnp.concatenate([per_group_blocks(jnp.real(b_j), nh, npm), per_group_blocks(jnp.imag(b_j), nh, npm)],
                         axis=-1)
    bc = bc.transpose(1, 0, 2, 3).reshape(S5_HALVES, S5_HALF_FOLD, S5_HALF_FOLD)
    c_i = (c_mat[None] * pw[1:t0 + 1][:, :, None, :]).transpose(1, 3, 0, 2)
    c_i = c_i.reshape(S5_HALVES, gh * npm, t0, nh)
    mask = same_group(npm, nh)[:, None, :]
    cc = jnp.concatenate([jnp.tile(jnp.real(c_i), (1, 1, 1, gh)) * mask,
                          jnp.tile(-jnp.imag(c_i), (1, 1, 1, gh)) * mask], axis=1)
    cc = cc.reshape(S5_HALVES, S5_HALF_FOLD, S5_HALF_FOLD)
    a_chunk = pw[t0].reshape(S5_HALVES, gh * npm)
    a1 = jnp.concatenate([jnp.real(a_chunk), jnp.real(a_chunk)], axis=1).reshape(1, -1)
    a2 = jnp.concatenate([-jnp.imag(a_chunk), jnp.imag(a_chunk)], axis=1).reshape(1, -1)
    d_vec = jnp.tile(d_skip.reshape(S5_HALVES, gh * nh), (1, t0))
    return mt.astype(BF16), bc.astype(BF16), cc.astype(BF16), a1, a2, d_vec


def _rms_gain(y, gain):
    return y * lax.rsqrt(jnp.mean(y * y, axis=-1, keepdims=True) + EPS) * gain


def _outproj_kernel(x_ref, mod_ref, pool_ref, halo_ref, nsa_ref, s5_ref, pw_ref, pb_ref, ps_ref, on_ref,
                    gw_ref, gb_ref, wo_ref, o_ref, buf):
    j = pl.program_id(1)
    tm = x_ref.shape[0]
    v = pool_ref[...]
    buf[0:POOL_HALO, :] = jnp.where(j > 0, halo_ref[...], 0.0)
    buf[POOL_HALO:POOL_HALO + tm, :] = v
    lane_group = lax.broadcasted_iota(jnp.int32, (tm, POOL_WIDTH), 1) >> SLC_SHIFT
    t1 = (j * tm + 1 + lax.broadcasted_iota(jnp.int32, (tm, 1), 0)).astype(F32)
    run, k, pooled = v, 1, jnp.zeros_like(v)
    for gi, w in enumerate(POOL_WINDOWS):
        while k < w:
            run = run + buf[POOL_HALO - k:POOL_HALO - k + tm, :]
            k += 1
        pooled = jnp.where(lane_group == gi, run / jnp.minimum(t1, float(w)) - v, pooled)
    y_pool = (_dot(pooled.astype(BF16), pw_ref[...]) + pb_ref[...]) * ps_ref[...]

    y = _gelu_tanh(jnp.concatenate([s5_ref[half] for half in range(S5_HALVES)], axis=-1))
    y_s5 = y * _sigmoid(_dot(y.astype(BF16), gw_ref[...]) + gb_ref[...])

    cat = jnp.concatenate(
        [_rms_gain(y_pool, on_ref[:, 0:POOL_WIDTH]),
         _rms_gain(nsa_ref[...], on_ref[:, POOL_WIDTH:POOL_WIDTH + NSA_WIDTH]),
         _rms_gain(y_s5, on_ref[:, POOL_WIDTH + NSA_WIDTH:])], axis=-1).astype(BF16)
    o_ref[...] = x_ref[...] + mod_ref[5:6, :] * _dot(cat, wo_ref[...])


def _outproj(x, layer, mod, u_pool, o_nsa, y_s5, pool_w_bd, pool_b, pool_scale, out_norm, glu_w, glu_b, w_out,
             tm=256):
    b, s, d = x.shape
    tok = lambda width: pl.BlockSpec((None, tm, width), lambda i, j: (i, j, 0))
    halo_blocks = tm // POOL_HALO
    return pl.pallas_call(
        _outproj_kernel,
        grid=(b, s // tm),
        in_specs=[tok(d),
                  _mod_spec(layer),
                  tok(POOL_WIDTH),
                  pl.BlockSpec((None, POOL_HALO, POOL_WIDTH),
                               lambda i, j: (i, jnp.maximum(j * halo_blocks - 1, 0), 0)),
                  tok(NSA_WIDTH),
                  pl.BlockSpec((None, S5_HALVES, tm, LANES), lambda i, j: (i, 0, j, 0)),
                  _layer_spec((POOL_WIDTH, POOL_WIDTH), layer), _layer_spec((1, POOL_WIDTH), layer),
                  _layer_spec((1, POOL_WIDTH), layer), _layer_spec((1, d), layer),
                  _layer_spec((S5_WIDTH, S5_WIDTH), layer), _layer_spec((1, S5_WIDTH), layer),
                  _layer_spec((d, d), layer)],
        out_specs=tok(d),
        out_shape=jax.ShapeDtypeStruct((b, s, d), F32),
        scratch_shapes=[pltpu.VMEM((POOL_HALO + tm, POOL_WIDTH), F32)],
        compiler_params=_params(("parallel", "arbitrary")),
        name="mixer_out_proj",
    )(x, mod, u_pool, u_pool, o_nsa, y_s5, pool_w_bd, pool_b, pool_scale, out_norm, glu_w, glu_b, w_out)


def _rope_tables(pos):
    half = ROT_DIM // 2
    inv_freq = jnp.exp(-math.log(ROPE_THETA) * jnp.arange(half, dtype=F32) * (2.0 / ROT_DIM))
    ang = pos[:, None] * inv_freq[None, :]
    cos, sin = jnp.cos(ang), jnp.sin(ang)
    n = pos.shape[0]
    rest = HEAD_DIM - ROT_DIM
    cos_t = jnp.concatenate([cos, cos, jnp.ones((n, rest), F32)], axis=1)
    sin_lo = jnp.concatenate([-sin, jnp.zeros((n, half + rest), F32)], axis=1)
    sin_hi = jnp.concatenate([jnp.zeros((n, half), F32), sin, jnp.zeros((n, rest), F32)], axis=1)
    return tuple(jnp.tile(t, (1, 2)) for t in (cos_t, sin_lo, sin_hi))


def _pad_lanes(v):
    return jnp.pad(v, [(0, 0)] * (v.ndim - 1) + [(0, LANES - v.shape[-1])])


def _rope_tables_t(pos):
    half = ROT_DIM // 2
    inv_freq = jnp.exp(-math.log(ROPE_THETA) * jnp.arange(half, dtype=F32) * (2.0 / ROT_DIM))
    ang = inv_freq[:, None] * pos[None, :]
    return jnp.cos(ang), jnp.sin(ang)


def _arrange_w_in(w):
    d = w.shape[0]
    o1, o2, o3 = POOL_WIDTH, POOL_WIDTH + NSA_WIDTH, POOL_WIDTH + NSA_WIDTH + 6 * LANES
    kv = w[:, o2:o3].reshape(d, 6, NSA_KV_HEADS, HEAD_DIM)
    ka = jnp.concatenate([kv[:, 2], kv[:, 4]], axis=-1).reshape(d, NSA_KV_HEADS * LANES)
    n_gate = NSA_REP * N_BRANCH
    w_row = jnp.concatenate([w[:, :o1], ka, kv[:, 0].reshape(d, LANES), kv[:, 1].reshape(d, LANES),
                             w[:, o3 + NSA_KV_HEADS * n_gate:]], axis=1)
    q_t = w[:, o1:o2].T
    v_t = jnp.concatenate([kv[:, 3], kv[:, 5]], axis=-1).reshape(d, NSA_KV_HEADS * LANES).T
    gate_t = w[:, o3:o3 + NSA_KV_HEADS * n_gate].T.reshape(NSA_KV_HEADS, n_gate, d)
    gate_t = jnp.pad(gate_t, ((0, 0), (0, GATE_ROWS - n_gate), (0, 0))).reshape(NSA_KV_HEADS * GATE_ROWS, d)
    return w_row.astype(BF16), jnp.concatenate([q_t, v_t, gate_t], axis=0).astype(BF16)


def _expand_cmp_w1(w1):
    halves = w1.reshape(2, CMP_STRIDE, HEAD_DIM, CMP_HIDDEN)
    both = jnp.concatenate([halves[0], halves[1]], axis=-1)
    out = []
    for g in range(NSA_KV_HEADS):
        z = jnp.zeros((CMP_STRIDE, NSA_KV_HEADS, HEAD_DIM, 2 * CMP_HIDDEN), F32).at[:, g].set(both)
        out.append(z.reshape(CMP_STRIDE * LANES, 2 * CMP_HIDDEN))
    return jnp.stack(out).astype(BF16)


def _block_diag(w):
    g, c, _ = w.shape
    return jnp.einsum('gcd,gf->gcfd', w, jnp.eye(g, dtype=w.dtype)).reshape(g * c, g * c)


def _prepare_parameters(norm_ffn1, ffn1_w_in, ffn1_w_out, norm_mix, w_in, w_out, out_norm, pool_w, pool_b, pool_scale,
                        q_norm, k_norm, cmp_pe, cmp_k_w1, cmp_k_w2, cmp_v_w1, cmp_v_w2, s5_lam_re, s5_lam_im,
                        s5_log_dt, s5_b_re, s5_b_im, s5_c_re, s5_c_im, s5_d, glu_w, glu_b, norm_ffn2, ffn2_w_in,
                        ffn2_w_out):
    n_layers = norm_mix.shape[0]
    row = lambda v: v.reshape(n_layers, 1, -1)
    w_row, w_t = jax.vmap(_arrange_w_in)(w_in)
    return dict(
        ffn1=(row(norm_ffn1), ffn1_w_in.astype(BF16), ffn1_w_out.astype(BF16)),
        ffn2=(row(norm_ffn2), ffn2_w_in.astype(BF16), ffn2_w_out.astype(BF16)),
        inproj=(row(norm_mix), w_row, w_t, q_norm.reshape(n_layers, HEAD_DIM, 1),
                jnp.concatenate([k_norm[:, 1], k_norm[:, 2]], axis=-1).reshape(n_layers, 1, LANES)),
        compress=(jax.vmap(_expand_cmp_w1)(cmp_k_w1), jax.vmap(_expand_cmp_w1)(cmp_v_w1), cmp_k_w1, cmp_v_w1,
                  cmp_pe.reshape(n_layers, 2, CMP_LEN * HEAD_DIM), _pad_lanes(cmp_k_w2).astype(BF16),
                  jnp.swapaxes(_pad_lanes(cmp_v_w2), 1, 2).astype(BF16), row(_pad_lanes(k_norm[:, 0]))),
        s5=jax.vmap(_s5_matrices)(s5_lam_re, s5_lam_im, s5_log_dt, s5_b_re, s5_b_im, s5_c_re, s5_c_im, s5_d),
        outproj=(jax.vmap(_block_diag)(pool_w).astype(BF16), row(pool_b), row(pool_scale), row(out_norm),
                 glu_w.astype(BF16), row(glu_b), w_out.astype(BF16)))


def _hybrid_layer(x, layer, mod, params, tabs, tabs_t, cmp_tabs):
    x = _ffn(x, layer, mod, *params["ffn1"], 0)
    u_pool, q_t, ka, kb, va_t, vb_t, kc, vc, gates_t, u_s5 = _inproj(
        x, layer, mod, *params["inproj"][:3], tabs, tabs_t, *params["inproj"][3:])
    k_cmp, v_cmp_t = _compress(kc, vc, layer, *params["compress"], cmp_tabs)
    o_nsa = _nsa_attention(q_t, k_cmp, v_cmp_t, ka, kb, va_t, vb_t, gates_t)
    y_s5 = _s5_scan(u_s5, layer, params["s5"])
    x = _outproj(x, layer, mod, u_pool, o_nsa, y_s5, *params["outproj"])
    return _ffn(x, layer, mod, *params["ffn2"], 6)


def kernel(x, c, ada_w, ada_b, norm_ffn1, ffn1_w_in, ffn1_w_out, norm_mix, w_in, w_out, out_norm, pool_w, pool_b, pool_scale, q_norm, k_norm, cmp_pe, cmp_k_w1, cmp_k_w2, cmp_v_w1, cmp_v_w2, s5_lam_re, s5_lam_im, s5_log_dt, s5_b_re, s5_b_im, s5_c_re, s5_c_im, s5_d, glu_w, glu_b, norm_ffn2, ffn2_w_in, ffn2_w_out):
    seq = x.shape[1]
    assert seq % (KEY_TILE * 4) == 0 and seq >= WINDOW + Q_BLOCK
    mod = _modulation(c, ada_w, ada_b)
    params = _prepare_parameters(norm_ffn1, ffn1_w_in, ffn1_w_out, norm_mix, w_in, w_out, out_norm, pool_w, pool_b,
                                 pool_scale, q_norm, k_norm, cmp_pe, cmp_k_w1, cmp_k_w2, cmp_v_w1, cmp_v_w2,
                                 s5_lam_re, s5_lam_im, s5_log_dt, s5_b_re, s5_b_im, s5_c_re, s5_c_im, s5_d, glu_w,
                                 glu_b, norm_ffn2, ffn2_w_in, ffn2_w_out)
    tabs = _rope_tables(jnp.arange(seq, dtype=F32))
    tabs_t = _rope_tables_t(jnp.arange(seq, dtype=F32))
    cmp_tabs = _rope_tables((jnp.arange(seq // CMP_STRIDE) * CMP_STRIDE + CMP_LEN - 1).astype(F32))
    for layer in range(ada_w.shape[0]):
        x = _hybrid_layer(x, layer, mod, params, tabs, tabs_t, cmp_tabs)
    return x
```

```python
import functools
import math

import jax
import jax.numpy as jnp
import numpy as np
from jax import lax
from jax.experimental import pallas as pl
from jax.experimental.pallas import tpu as pltpu

F32 = jnp.float32
BF16 = jnp.bfloat16
HIGHEST = lax.Precision.HIGHEST

LANES = 128
SUBLANES = 8
VMEM_LIMIT_BYTES = 56 * 1024 * 1024

D_MODEL = 1024
D_FF = 2816
N_MOD = 9
POOL_WIDTH = 256
POOL_GC = 64
POOL_WINDOWS = (2, 4, 8, 16)
POOL_HALO = 16
HEAD_DIM = 64
NSA_WIDTH = 512
NSA_HEADS = 8
NSA_KV_HEADS = 2
NSA_REP = 4
N_BRANCH = 3
S5_WIDTH = 256
S5_H = 16
S5_GROUPS = 16
S5_P = 64
CMP_STRIDE = 16
CMP_LEN = 32
CMP_HIDDEN = 128
SLC_BLOCK = 64
SLC_SHIFT = 6
N_SELECT = 16
N_FORCED = 3
WINDOW = 512
Q_BLOCK = 256
ROT_DIM = 16
ROPE_THETA = 500000.0
EPS = 1e-6
Q_SCALE = HEAD_DIM ** -0.5 * math.log2(math.e)
NEG_INF = -1e30
MASK_VALUE = -(2.0 ** 100)

OFF_POOL = 0
OFF_KA = OFF_POOL + POOL_WIDTH
OFF_KC = OFF_KA + NSA_KV_HEADS * LANES
OFF_VC = OFF_KC + LANES
OFF_S5 = OFF_VC + LANES
N_COLS = OFF_S5 + S5_WIDTH
GATE_ROWS = 16
ROW_Q = 0
ROW_V = ROW_Q + NSA_HEADS * HEAD_DIM
ROW_GATE = ROW_V + NSA_KV_HEADS * LANES
N_ROWS_T = ROW_GATE + NSA_KV_HEADS * GATE_ROWS

S5_CHUNK = 8
S5_FOLD = S5_CHUNK * S5_WIDTH
S5_STATE = S5_GROUPS * S5_P
S5_HALVES = S5_WIDTH // LANES
S5_HALF_FOLD = S5_FOLD // S5_HALVES
KEY_TILE = 512
V_ROWS = HEAD_DIM + 16


def _dot(a, b, precision=None):
    return jnp.dot(a, b, preferred_element_type=F32, precision=precision)


def _dot_nt(a, b, precision=None):
    return lax.dot_general(a, b, (((1,), (1,)), ((), ())), preferred_element_type=F32,
                           precision=precision)


def _sigmoid(x):
    return 1.0 / (1.0 + jnp.exp(-x))


def _gelu_tanh(x):
    return 0.5 * x * (1.0 + jnp.tanh(math.sqrt(2.0 / math.pi) * (x + 0.044715 * (x * x * x))))


def _params(sem):
    return pltpu.CompilerParams(dimension_semantics=sem, vmem_limit_bytes=VMEM_LIMIT_BYTES)


def _const_spec(shape):
    nd = len(shape)
    return pl.BlockSpec(shape, lambda *_: (0,) * nd, pipeline_mode=pl.Buffered(1))


def _layer_spec(shape, layer):
    nd = len(shape)
    return pl.BlockSpec((None,) + shape, lambda *_: (layer,) + (0,) * nd, pipeline_mode=pl.Buffered(1))


def _mod_kernel(c_ref, w_ref, b_ref, o_ref):
    c = c_ref[...]
    o_ref[...] = _dot(c * _sigmoid(c), w_ref[...], HIGHEST) + b_ref[...]


def _modulation(c, ada_w, ada_b):
    n_layers, d, n = ada_w.shape
    b = c.shape[0]
    tn = 1152
    c_pad = jnp.zeros((SUBLANES, d), F32).at[:b].set(c)
    out = pl.pallas_call(
        _mod_kernel,
        grid=(n_layers, n // tn),
        in_specs=[pl.BlockSpec((SUBLANES, d), lambda l, j: (0, 0)),
                  pl.BlockSpec((None, d, tn), lambda l, j: (l, 0, j)),
                  pl.BlockSpec((None, 1, tn), lambda l, j: (l, 0, j))],
        out_specs=pl.BlockSpec((None, SUBLANES, tn), lambda l, j: (l, 0, j)),
        out_shape=jax.ShapeDtypeStruct((n_layers, SUBLANES, n), F32),
        compiler_params=_params(("arbitrary", "arbitrary")),
        name="adaln_mod",
    )(c_pad, ada_w, ada_b.reshape(n_layers, 1, n))
    return out[:, :b].reshape(n_layers, b, N_MOD, d)


def _norm_modulate(x, gain, mod_ref, first_row):
    ms = jnp.mean(x * x, axis=-1, keepdims=True)
    y = x * lax.rsqrt(ms + EPS) * gain
    return y * (1.0 + mod_ref[first_row + 1:first_row + 2, :]) + mod_ref[first_row:first_row + 1, :]


def _ffn_kernel(x_ref, mod_ref, g_ref, win_ref, wout_ref, o_ref, *, first_row):
    x = x_ref[...]
    h = _norm_modulate(x, g_ref[...], mod_ref, first_row).astype(BF16)
    gu = _dot(h, win_ref[...])
    gate, up = gu[:, :D_FF], gu[:, D_FF:]
    a = (gate * _sigmoid(gate) * up).astype(BF16)
    y = _dot(a, wout_ref[...])
    o_ref[...] = x + 0.5 * mod_ref[first_row + 2:first_row + 3, :] * y


def _mod_spec(layer):
    return pl.BlockSpec((None, None, N_MOD, D_MODEL), lambda i, *_: (layer, i, 0, 0))


def _ffn(x, layer, mod, gain, w_in, w_out, first_row, tm=512):
    b, s, d = x.shape
    return pl.pallas_call(
        functools.partial(_ffn_kernel, first_row=first_row),
        grid=(b, s // tm),
        in_specs=[pl.BlockSpec((None, tm, d), lambda i, j: (i, j, 0)),
                  _mod_spec(layer),
                  _layer_spec((1, d), layer),
                  _layer_spec((d, 2 * D_FF), layer),
                  _layer_spec((D_FF, d), layer)],
        out_specs=pl.BlockSpec((None, tm, d), lambda i, j: (i, j, 0)),
        out_shape=jax.ShapeDtypeStruct((b, s, d), F32),
        compiler_params=_params(("parallel", "parallel")),
        name="ffn_half_step",
    )(x, mod, gain, w_in, w_out)


def _rope(v, cos_t, sin_lo, sin_hi):
    return (v * cos_t + pltpu.roll(v, LANES - ROT_DIM // 2, 1) * sin_lo
            + pltpu.roll(v, ROT_DIM // 2, 1) * sin_hi)


def _inproj_kernel(x_ref, mod_ref, g_ref, w_ref, wt_ref, cos_ref, slo_ref, shi_ref, cost_ref, sint_ref, qg_ref, kg_ref,
                   pool_ref, q_ref, ka_ref, kb_ref, va_ref, vb_ref, kc_ref, vc_ref, gate_ref, s5_ref):
    h = _norm_modulate(x_ref[...], g_ref[...], mod_ref, 3).astype(BF16)
    u = _dot(h, w_ref[...])
    ut = _dot_nt(wt_ref[...], h)
    tm = u.shape[0]
    cos_t, sin_lo, sin_hi = cos_ref[...], slo_ref[...], shi_ref[...]
    lane = lax.broadcasted_iota(jnp.int32, (tm, LANES), 1)
    low_half = lane < HEAD_DIM
    pos = pl.program_id(1) * tm + lax.broadcasted_iota(jnp.int32, (tm, LANES), 0)
    block_one_hot = jnp.where((pos >> SLC_SHIFT) == lane, 1.0, 0.0).astype(BF16)

    pool_ref[...] = u[:, OFF_POOL:OFF_POOL + POOL_WIDTH]
    kc_ref[...] = u[:, OFF_KC:OFF_KC + LANES]
    vc_ref[...] = u[:, OFF_VC:OFF_VC + LANES]
    for half in range(S5_HALVES):
        s5_ref[half] = u[:, OFF_S5 + half * LANES:OFF_S5 + (half + 1) * LANES]
    for g in range(NSA_KV_HEADS):
        v = u[:, OFF_KA + g * LANES:OFF_KA + (g + 1) * LANES]
        sq = v * v
        ms_lo = jnp.sum(jnp.where(low_half, sq, 0.0), axis=-1, keepdims=True) * (1.0 / HEAD_DIM)
        ms_hi = jnp.sum(jnp.where(low_half, 0.0, sq), axis=-1, keepdims=True) * (1.0 / HEAD_DIM)
        r = jnp.where(low_half, lax.rsqrt(ms_lo + EPS), lax.rsqrt(ms_hi + EPS))
        kn = _rope(v * r * kg_ref[...], cos_t, sin_lo, sin_hi)
        ka_ref[g, :, 0:LANES] = kn.astype(BF16)
        ka_ref[g, :, LANES:2 * LANES] = block_one_hot
        kb_ref[g] = pltpu.roll(kn, HEAD_DIM, 1).astype(BF16)

    half_rot = ROT_DIM // 2
    cos8, sin8 = cost_ref[...], sint_ref[...]
    zero_rows = jnp.zeros((LANES - HEAD_DIM, tm), F32)
    for hd in range(NSA_HEADS):
        v = ut[ROW_Q + hd * HEAD_DIM:ROW_Q + (hd + 1) * HEAD_DIM, :]
        ms = jnp.sum(v * v, axis=0, keepdims=True) * (1.0 / HEAD_DIM)
        vn = v * lax.rsqrt(ms + EPS) * qg_ref[...]
        x1, x2 = vn[0:half_rot], vn[half_rot:ROT_DIM]
        roped = jnp.concatenate([x1 * cos8 - x2 * sin8, x2 * cos8 + x1 * sin8, vn[ROT_DIM:]], axis=0)
        q_ref[hd] = jnp.concatenate([roped * Q_SCALE, zero_rows], axis=0).astype(BF16)
    ones = jnp.ones((HEAD_DIM, tm), F32)
    for g in range(NSA_KV_HEADS):
        vt = ut[ROW_V + g * LANES:ROW_V + (g + 1) * LANES, :]
        va = jnp.concatenate([vt[0:HEAD_DIM], ones], axis=0).astype(BF16)
        vb = jnp.concatenate([vt[HEAD_DIM:], ones], axis=0).astype(BF16)
        for blk in range(tm // LANES):
            va_ref[g, blk] = va[:, blk * LANES:(blk + 1) * LANES]
            vb_ref[g, blk] = vb[:, blk * LANES:(blk + 1) * LANES]
        gate_ref[g] = _sigmoid(ut[ROW_GATE + g * GATE_ROWS:ROW_GATE + (g + 1) * GATE_ROWS, :])


def _inproj(x, layer, mod, gain, w_row, w_t, rope_tabs, rope_tabs_t, q_gain, k_gain, tm=512):
    b, s, d = x.shape
    g = NSA_KV_HEADS
    tok = lambda width: pl.BlockSpec((None, tm, width), lambda i, j: (i, j, 0))
    grp = lambda n, width=LANES: pl.BlockSpec((None, n, tm, width), lambda i, j: (i, 0, j, 0))
    lanes_tok = lambda n, rows: pl.BlockSpec((None, n, rows, tm), lambda i, j: (i, 0, 0, j))
    v_blocks = pl.BlockSpec((None, g, tm // LANES, LANES, LANES), lambda i, j: (i, 0, j, 0, 0))
    tab = pl.BlockSpec((tm, LANES), lambda i, j: (j, 0))
    tab_t = pl.BlockSpec((ROT_DIM // 2, tm), lambda i, j: (0, j))
    sds = jax.ShapeDtypeStruct
    return pl.pallas_call(
        _inproj_kernel,
        grid=(b, s // tm),
        in_specs=[tok(d),
                  _mod_spec(layer),
                  _layer_spec((1, d), layer),
                  _layer_spec((d, N_COLS), layer), _layer_spec((N_ROWS_T, d), layer),
                  tab, tab, tab, tab_t, tab_t,
                  _layer_spec((HEAD_DIM, 1), layer), _layer_spec((1, LANES), layer)],
        out_specs=[tok(POOL_WIDTH), lanes_tok(NSA_HEADS, LANES), grp(g, 2 * LANES), grp(g), v_blocks, v_blocks,
                   tok(LANES), tok(LANES), lanes_tok(g, GATE_ROWS), grp(S5_HALVES)],
        out_shape=[sds((b, s, POOL_WIDTH), F32), sds((b, NSA_HEADS, LANES, s), BF16),
                   sds((b, g, s, 2 * LANES), BF16), sds((b, g, s, LANES), BF16),
                   sds((b, g, s // LANES, LANES, LANES), BF16), sds((b, g, s // LANES, LANES, LANES), BF16),
                   sds((b, s, LANES), F32), sds((b, s, LANES), F32),
                   sds((b, g, GATE_ROWS, s), F32), sds((b, S5_HALVES, s, LANES), F32)],
        compiler_params=_params(("parallel", "parallel")),
        name="mixer_in_proj",
    )(x, mod, gain, w_row, w_t, *rope_tabs, *rope_tabs_t, q_gain, k_gain)


def _fold_rows(ref, n):
    rows = ref.shape[0] // n
    return jnp.concatenate([ref[pl.ds(k, rows, stride=n), :] for k in range(n)], axis=1)


def _unfold_rows(ref, value, n):
    rows = ref.shape[0] // n
    for k in range(n):
        ref[pl.ds(k, rows, stride=n), :] = value[:, k * LANES:(k + 1) * LANES]


def _compress_kernel(kc_ref, vc_ref, w1k_ref, w1v_ref, w1k_raw_ref, w1v_raw_ref, pe_ref, w2k_ref, w2vt_ref,
                     kg_ref, cos_ref, slo_ref, shi_ref, ko_ref, vo_ref, *, n_cmp):
    ncp = kc_ref.shape[0] // CMP_STRIDE
    for src_ref, w1_ref, raw_ref, pe_row, is_key in ((kc_ref, w1k_ref, w1k_raw_ref, 0, True),
                                                    (vc_ref, w1v_ref, w1v_raw_ref, 1, False)):
        chunks = _fold_rows(src_ref, CMP_STRIDE).astype(BF16)
        pe = jnp.broadcast_to(pe_ref[pe_row:pe_row + 1, :], (SUBLANES, CMP_LEN * HEAD_DIM))
        pe_term = _dot(pe, raw_ref[...], HIGHEST)[0:1, :]
        for g in range(NSA_KV_HEADS):
            a = _dot(chunks, w1_ref[g])
            pre = a[:, :CMP_HIDDEN] + pltpu.roll(a[:, CMP_HIDDEN:], ncp - 1, 0) + pe_term
            hidden = _gelu_tanh(pre).astype(BF16)
            if is_key:
                out = _dot(hidden, w2k_ref[...])
                ms = jnp.sum(out * out, axis=-1, keepdims=True) * (1.0 / HEAD_DIM)
                out = _rope(out * lax.rsqrt(ms + EPS) * kg_ref[...], cos_ref[...], slo_ref[...], shi_ref[...])
                real_row = lax.broadcasted_iota(jnp.int32, (ncp, LANES), 0) < n_cmp
                ko_ref[g] = jnp.where(real_row, out, 0.0).astype(BF16)
            else:
                out_t = _dot_nt(w2vt_ref[...], hidden)
                real_col = lax.broadcasted_iota(jnp.int32, (LANES, ncp), 1) < n_cmp
                value_row = lax.broadcasted_iota(jnp.int32, (LANES, ncp), 0) < HEAD_DIM
                vo_ref[g] = jnp.where(value_row, jnp.where(real_col, out_t, 0.0), 1.0).astype(BF16)


def _compress(kc, vc, layer, w1k, w1v, w1k_raw, w1v_raw, pe, w2k, w2v_t, k_gain, cmp_tabs):
    b, s, _ = kc.shape
    ncp = s // CMP_STRIDE
    fold = CMP_STRIDE * LANES
    src = pl.BlockSpec((None, s, LANES), lambda i: (i, 0, 0))
    raw = _layer_spec((CMP_LEN * HEAD_DIM, CMP_HIDDEN), layer)
    w1 = _layer_spec((NSA_KV_HEADS, fold, 2 * CMP_HIDDEN), layer)
    w2 = _layer_spec((CMP_HIDDEN, LANES), layer)
    tab = _const_spec((ncp, LANES))
    return pl.pallas_call(
        functools.partial(_compress_kernel, n_cmp=ncp - 1),
        grid=(b,),
        in_specs=[src, src, w1, w1, raw, raw, _layer_spec((2, CMP_LEN * HEAD_DIM), layer), w2, w2,
                  _layer_spec((1, LANES), layer), tab, tab, tab],
        out_specs=[pl.BlockSpec((None, NSA_KV_HEADS, ncp, LANES), lambda i: (i, 0, 0, 0)),
                   pl.BlockSpec((None, NSA_KV_HEADS, LANES, ncp), lambda i: (i, 0, 0, 0))],
        out_shape=[jax.ShapeDtypeStruct((b, NSA_KV_HEADS, ncp, LANES), BF16),
                   jax.ShapeDtypeStruct((b, NSA_KV_HEADS, LANES, ncp), BF16)],
        compiler_params=_params(("parallel",)),
        name="nsa_compress",
    )(kc, vc, w1k, w1v, w1k_raw, w1v_raw, pe, w2k, w2v_t, k_gain, *cmp_tabs)


def _selection_bias(imp_t, qb):
    nsb, nq = imp_t.shape
    j = lax.broadcasted_iota(jnp.int32, (nsb, nq), 0)
    t = qb * Q_BLOCK + lax.broadcasted_iota(jnp.int32, (nsb, nq), 1)
    cur = t >> SLC_SHIFT
    valid = j * SLC_BLOCK <= t
    forced = (j == 0) | (j == cur) | (j == cur - 1)
    j_f = j.astype(F32)
    start = jnp.where(valid & jnp.logical_not(forced), imp_t, -1.0)
    vals = start
    for _ in range(N_SELECT - N_FORCED):
        m = jnp.max(vals, axis=0, keepdims=True)
        idx = jnp.min(jnp.where(vals == m, j_f, float(nsb)), axis=0, keepdims=True)
        vals = jnp.where(j_f == idx, -2.0, vals)
    return jnp.where((forced & valid) | (vals != start), 0.0, MASK_VALUE)


def _attn_kernel(q_ref, kc_ref, vct_ref, ka_ref, kb_ref, va_ref, vb_ref, gate_ref, ovl_ref, o_ref,
                 sa_ref, sb_ref, ma_ref, mb_ref, acc_ref, *, seq):
    qb = pl.program_id(2)
    cols = NSA_REP * Q_BLOCK
    ncp = seq // CMP_STRIDE
    q_t = jnp.concatenate([q_ref[r] for r in range(NSA_REP)], axis=1)
    t_col = qb * Q_BLOCK + lax.broadcasted_iota(jnp.int32, (1, cols), 1) % Q_BLOCK
    t_q = qb * Q_BLOCK + lax.broadcasted_iota(jnp.int32, (1, Q_BLOCK), 1)

    def all_heads(per_query):
        return jnp.concatenate([per_query] * NSA_REP, axis=1)

    cmp_end = lax.broadcasted_iota(jnp.int32, (ncp, Q_BLOCK), 0) * CMP_STRIDE + (CMP_LEN - 1)
    s = _dot(kc_ref[...], q_t) + all_heads(jnp.where(cmp_end <= t_q, 0.0, NEG_INF))
    e = jnp.exp2(s - jnp.max(s, axis=0, keepdims=True)).astype(BF16)
    oc = _dot(jnp.concatenate([vct_ref[0:V_ROWS, :], ovl_ref[...]], axis=0), e)
    oc = oc * jnp.where(t_col >= CMP_LEN - 1, 1.0 / oc[HEAD_DIM:HEAD_DIM + 1, :], 0.0)
    o_cmp, imp = oc[0:V_ROWS], oc[V_ROWS:V_ROWS + LANES]
    imp_t = (imp[:, 0:Q_BLOCK] + imp[:, Q_BLOCK:2 * Q_BLOCK]
             + imp[:, 2 * Q_BLOCK:3 * Q_BLOCK] + imp[:, 3 * Q_BLOCK:4 * Q_BLOCK])
    bias = _selection_bias(imp_t, qb).astype(BF16)

    q_aug = jnp.concatenate([q_t, jnp.concatenate([bias] * NSA_REP, axis=1)], axis=0)

    span = WINDOW + Q_BLOCK
    start = pl.multiple_of(jnp.maximum(qb * Q_BLOCK - WINDOW, 0), LANES)
    start_blk = start // LANES
    kpos = start + lax.broadcasted_iota(jnp.int32, (span, Q_BLOCK), 0)
    in_window = (kpos <= t_q) & (kpos > t_q - WINDOW)
    s = _dot(kb_ref[pl.ds(start, span), :], q_t) + all_heads(jnp.where(in_window, 0.0, NEG_INF))
    e = jnp.exp2(s - jnp.max(s, axis=0, keepdims=True)).astype(BF16)
    acc_win = _dot(jnp.concatenate([vb_ref[start_blk + i, 0:V_ROWS, :] for i in range(span // LANES)], axis=1), e)
    o_win = acc_win * (1.0 / acc_win[HEAD_DIM:HEAD_DIM + 1, :])

    blocks_per_tile = KEY_TILE // LANES
    buf_a, buf_b = (sa_ref, ma_ref), (sb_ref, mb_ref)

    def score_tile(kt, buf):
        k0 = pl.multiple_of(kt * KEY_TILE, KEY_TILE)
        s = _dot(ka_ref[pl.ds(k0, KEY_TILE), :], q_aug)
        buf[0][...] = s
        buf[1][...] = jnp.max(s, axis=0, keepdims=True)

    def absorb_tile(kt, buf, carry, causal):
        m_i, acc = carry
        blk0 = kt * blocks_per_tile
        s, s_max = buf[0][...], buf[1][...]
        if causal:
            kpos = kt * KEY_TILE + lax.broadcasted_iota(jnp.int32, (KEY_TILE, Q_BLOCK), 0)
            s = s + all_heads(jnp.where(kpos <= t_q, 0.0, MASK_VALUE))
            s_max = jnp.max(s, axis=0, keepdims=True)
        m_new = jnp.maximum(m_i, s_max)
        p = jnp.exp2(s - m_new).astype(BF16)
        v_t = jnp.concatenate([va_ref[blk0 + i, 0:V_ROWS, :] for i in range(blocks_per_tile)], axis=1)
        return m_new, jnp.exp2(m_i - m_new) * acc + _dot(v_t, p)

    def pair_trip(j, carry):
        score_tile(2 * j + 1, buf_b)
        carry = absorb_tile(2 * j, buf_a, carry, causal=False)
        score_tile(2 * j + 2, buf_a)
        return absorb_tile(2 * j + 1, buf_b, carry, causal=False)

    def double_trip(j, carry):
        return pair_trip(2 * j + 1, pair_trip(2 * j, carry))

    last_pair = qb // (2 * KEY_TILE // Q_BLOCK)
    score_tile(0, buf_a)
    carry = (jnp.full((1, cols), NEG_INF, F32), jnp.zeros((V_ROWS, cols), F32))
    carry = lax.fori_loop(0, last_pair // 2, double_trip, carry)
    carry = lax.fori_loop(2 * (last_pair // 2), last_pair, pair_trip, carry)
    second_tile_live = (qb // (KEY_TILE // Q_BLOCK)) % 2 == 1

    @pl.when(second_tile_live)
    def _():
        score_tile(2 * last_pair + 1, buf_b)
        both = absorb_tile(2 * last_pair, buf_a, carry, causal=False)
        acc_ref[...] = absorb_tile(2 * last_pair + 1, buf_b, both, causal=True)[1]

    @pl.when(jnp.logical_not(second_tile_live))
    def _():
        acc_ref[...] = absorb_tile(2 * last_pair, buf_a, carry, causal=True)[1]

    acc_slc = acc_ref[...]
    o_slc = acc_slc * (1.0 / acc_slc[HEAD_DIM:HEAD_DIM + 1, :])

    gates = gate_ref[...]
    heads = []
    for r in range(NSA_REP):
        cs = slice(r * Q_BLOCK, (r + 1) * Q_BLOCK)
        c = r * N_BRANCH
        heads.append(gates[c:c + 1, :] * o_cmp[0:HEAD_DIM, cs] + gates[c + 1:c + 2, :] * o_slc[0:HEAD_DIM, cs]
                     + gates[c + 2:c + 3, :] * o_win[0:HEAD_DIM, cs])
    o_ref[...] = jnp.concatenate(heads, axis=0).T


def _overlap_matrix(seq):
    ncp, nsb = seq // CMP_STRIDE, seq // SLC_BLOCK
    c_start = np.arange(ncp)[None, :] * CMP_STRIDE
    s_start = np.arange(nsb)[:, None] * SLC_BLOCK
    ovl = np.clip(np.minimum(c_start + CMP_LEN, s_start + SLC_BLOCK) - np.maximum(c_start, s_start), 0, None)
    ovl = ovl.astype(np.float32) / CMP_LEN
    ovl[:, ncp - 1] = 0.0
    return jnp.asarray(np.pad(ovl, ((0, LANES - nsb), (0, 0))))


def _nsa_attention(q_t, k_cmp, v_cmp_t, ka, kb, va_t, vb_t, gates_t):
    b, _, _, s = q_t.shape
    g = NSA_KV_HEADS
    ncp = s // CMP_STRIDE
    assert s // SLC_BLOCK <= LANES
    per_group = lambda *shape: pl.BlockSpec((None, None) + shape, lambda i, j, k: (i, j) + (0,) * len(shape))
    return pl.pallas_call(
        functools.partial(_attn_kernel, seq=s),
        grid=(b, g, s // Q_BLOCK),
        in_specs=[pl.BlockSpec((None, NSA_REP, LANES, Q_BLOCK), lambda i, j, k: (i, j, 0, k)),
                  per_group(ncp, LANES), per_group(LANES, ncp), per_group(s, 2 * LANES), per_group(s, LANES),
                  per_group(s // LANES, LANES, LANES), per_group(s // LANES, LANES, LANES),
                  pl.BlockSpec((None, None, GATE_ROWS, Q_BLOCK), lambda i, j, k: (i, j, 0, k)),
                  _const_spec((LANES, ncp))],
        out_specs=pl.BlockSpec((None, Q_BLOCK, NSA_REP * HEAD_DIM), lambda i, j, k: (i, k, j)),
        out_shape=jax.ShapeDtypeStruct((b, s, NSA_WIDTH), F32),
        scratch_shapes=[pltpu.VMEM((KEY_TILE, NSA_REP * Q_BLOCK), F32)] * 2
        + [pltpu.VMEM((1, NSA_REP * Q_BLOCK), F32)] * 2 + [pltpu.VMEM((V_ROWS, NSA_REP * Q_BLOCK), F32)],
        compiler_params=_params(("parallel", "parallel", "arbitrary")),
        name="nsa_attention",
    )(q_t, k_cmp, v_cmp_t, ka, kb, va_t, vb_t, gates_t, _overlap_matrix(s).astype(BF16))


def _s5_kernel(u_ref, mt_ref, bc_ref, cc_ref, a1_ref, a2_ref, d_ref, y_ref, h_ref, g_scr, hp_scr):
    @pl.when(pl.program_id(1) == 0)
    def _():
        h_ref[...] = jnp.zeros_like(h_ref)

    halves = range(S5_HALVES)
    u = [_fold_rows(u_ref.at[a], S5_CHUNK) for a in halves]
    ub = [x.astype(BF16) for x in u]
    y_local = [_dot(ub[a], mt_ref[a]) for a in halves]
    for a in halves:
        g_scr[:, a * S5_HALF_FOLD:(a + 1) * S5_HALF_FOLD] = _dot(ub[a], bc_ref[a])
    a1, a2 = a1_ref[...], a2_ref[...]
    n_state = S5_STATE // S5_HALVES

    def swap_re_im(h):
        parts = [h[:, k * n_state:(k + 1) * n_state] for k in range(2 * S5_HALVES)]
        return jnp.concatenate([parts[k ^ 1] for k in range(2 * S5_HALVES)], axis=1)

    def step(i, h):
        hp_scr[pl.ds(i, 1), :] = h
        return a1 * h + a2 * swap_re_im(h) + g_scr[pl.ds(i, 1), :]

    h_ref[...] = lax.fori_loop(0, g_scr.shape[0], step, h_ref[...], unroll=8)
    for a in halves:
        carried = _dot(hp_scr[:, a * S5_HALF_FOLD:(a + 1) * S5_HALF_FOLD].astype(BF16), cc_ref[a])
        _unfold_rows(y_ref.at[a], y_local[a] + carried + u[a] * d_ref[a:a + 1, :], S5_CHUNK)


def _s5_scan(u, layer, mats):
    b, _, s, _ = u.shape
    rows = s // S5_CHUNK
    tc = min(128, rows)
    tile = pl.BlockSpec((None, S5_HALVES, tc * S5_CHUNK, LANES), lambda i, j: (i, 0, j, 0))
    mat = _layer_spec((S5_HALVES, S5_HALF_FOLD, S5_HALF_FOLD), layer)
    row = _layer_spec((1, S5_HALVES * S5_HALF_FOLD), layer)
    return pl.pallas_call(
        _s5_kernel,
        grid=(b, rows // tc),
        in_specs=[tile, mat, mat, mat, row, row, _layer_spec((S5_HALVES, S5_HALF_FOLD), layer)],
        out_specs=tile,
        out_shape=jax.ShapeDtypeStruct((b, S5_HALVES, s, LANES), F32),
        scratch_shapes=[pltpu.VMEM((1, S5_HALVES * S5_HALF_FOLD), F32),
                        pltpu.VMEM((tc, S5_HALVES * S5_HALF_FOLD), F32),
                        pltpu.VMEM((tc, S5_HALVES * S5_HALF_FOLD), F32)],
        compiler_params=_params(("parallel", "arbitrary")),
        name="s5_scan",
    )(u, *mats)


def _s5_matrices(lam_re, lam_im, log_dt, b_re, b_im, c_re, c_im, d_skip):
    t0, ng, nh, npm = S5_CHUNK, S5_GROUPS, S5_H, S5_P
    gh = ng // S5_HALVES
    ein = functools.partial(jnp.einsum, precision=HIGHEST)
    lam = lax.complex(lam_re, lam_im)
    step = jnp.exp(log_dt)[:, None]
    lam_bar = jnp.exp(lam * step)
    b_bar = lax.complex(b_re, b_im) * ((lam_bar - 1.0) / lam)[..., None]
    c_mat = lax.complex(c_re, c_im)
    k = jnp.arange(t0 + 1, dtype=F32)[:, None, None]
    pw = jnp.exp((lam * step)[None] * k)

    def same_group(rows_per_group, cols_per_group):
        r = np.arange(gh * rows_per_group)[:, None] // rows_per_group
        c = np.arange(gh * cols_per_group)[None, :] // cols_per_group
        return jnp.asarray((r == c).astype(np.float32))

    def per_group_blocks(x, rows_per_group, cols_per_group):
        return jnp.tile(x, (1,) * (x.ndim - 1) + (gh,)) * same_group(rows_per_group, cols_per_group)

    kern = jnp.real(ein('ghp,kgp,gpq->kghq', c_mat, pw[:t0], b_bar))
    d_k = per_group_blocks(kern.transpose(0, 1, 3, 2).reshape(t0, S5_HALVES, gh * nh, nh), nh, nh)
    d_k = jnp.concatenate([d_k, jnp.zeros_like(d_k[:1])], axis=0)
    lag = np.arange(t0)[None, :] - np.arange(t0)[:, None]
    mt = d_k[np.where(lag >= 0, lag, t0)]
    mt = mt.transpose(2, 0, 3, 1, 4).reshape(S5_HALVES, S5_HALF_FOLD, S5_HALF_FOLD)
    b_j = (pw[t0 - 1 - np.arange(t0)][..., None] * b_bar[None]).transpose(0, 1, 3, 2)
    b_j = b_j.reshape(t0, S5_HALVES, gh * nh, npm)
    bc = jnp.concatenate([per_group_blocks(jnp.real(b_j), nh, npm), per_group_blocks(jnp.imag(b_j), nh, npm)],
                         axis=-1)
    bc = bc.transpose(1, 0, 2, 3).reshape(S5_HALVES, S5_HALF_FOLD, S5_HALF_FOLD)
    c_i = (c_mat[None] * pw[1:t0 + 1][:, :, None, :]).transpose(1, 3, 0, 2)
    c_i = c_i.reshape(S5_HALVES, gh * npm, t0, nh)
    mask = same_group(npm, nh)[:, None, :]
    cc = jnp.concatenate([jnp.tile(jnp.real(c_i), (1, 1, 1, gh)) * mask,
                          jnp.tile(-jnp.imag(c_i), (1, 1, 1, gh)) * mask], axis=1)
    cc = cc.reshape(S5_HALVES, S5_HALF_FOLD, S5_HALF_FOLD)
    a_chunk = pw[t0].reshape(S5_HALVES, gh * npm)
    a1 = jnp.concatenate([jnp.real(a_chunk), jnp.real(a_chunk)], axis=1).reshape(1, -1)
    a2 = jnp.concatenate([-jnp.imag(a_chunk), jnp.imag(a_chunk)], axis=1).reshape(1, -1)
    d_vec = jnp.tile(d_skip.reshape(S5_HALVES, gh * nh), (1, t0))
    return mt.astype(BF16), bc.astype(BF16), cc.astype(BF16), a1, a2, d_vec


def _rms_gain(y, gain):
    return y * lax.rsqrt(jnp.mean(y * y, axis=-1, keepdims=True) + EPS) * gain


def _outproj_kernel(x_ref, mod_ref, pool_ref, halo_ref, nsa_ref, s5_ref, pw_ref, pb_ref, ps_ref, on_ref,
                    gw_ref, gb_ref, wo_ref, o_ref, buf):
    j = pl.program_id(1)
    tm = x_ref.shape[0]
    v = pool_ref[...]
    buf[0:POOL_HALO, :] = jnp.where(j > 0, halo_ref[...], 0.0)
    buf[POOL_HALO:POOL_HALO + tm, :] = v
    lane_group = lax.broadcasted_iota(jnp.int32, (tm, POOL_WIDTH), 1) >> SLC_SHIFT
    t1 = (j * tm + 1 + lax.broadcasted_iota(jnp.int32, (tm, 1), 0)).astype(F32)
    run, k, pooled = v, 1, jnp.zeros_like(v)
    for gi, w in enumerate(POOL_WINDOWS):
        while k < w:
            run = run + buf[POOL_HALO - k:POOL_HALO - k + tm, :]
            k += 1
        pooled = jnp.where(lane_group == gi, run / jnp.minimum(t1, float(w)) - v, pooled)
    y_pool = (_dot(pooled.astype(BF16), pw_ref[...]) + pb_ref[...]) * ps_ref[...]

    y = _gelu_tanh(jnp.concatenate([s5_ref[half] for half in range(S5_HALVES)], axis=-1))
    y_s5 = y * _sigmoid(_dot(y.astype(BF16), gw_ref[...]) + gb_ref[...])

    cat = jnp.concatenate(
        [_rms_gain(y_pool, on_ref[:, 0:POOL_WIDTH]),
         _rms_gain(nsa_ref[...], on_ref[:, POOL_WIDTH:POOL_WIDTH + NSA_WIDTH]),
         _rms_gain(y_s5, on_ref[:, POOL_WIDTH + NSA_WIDTH:])], axis=-1).astype(BF16)
    o_ref[...] = x_ref[...] + mod_ref[5:6, :] * _dot(cat, wo_ref[...])


def _outproj(x, layer, mod, u_pool, o_nsa, y_s5, pool_w_bd, pool_b, pool_scale, out_norm, glu_w, glu_b, w_out,
             tm=512):
    b, s, d = x.shape
    tok = lambda width: pl.BlockSpec((None, tm, width), lambda i, j: (i, j, 0))
    halo_blocks = tm // POOL_HALO
    return pl.pallas_call(
        _outproj_kernel,
        grid=(b, s // tm),
        in_specs=[tok(d),
                  _mod_spec(layer),
                  tok(POOL_WIDTH),
                  pl.BlockSpec((None, POOL_HALO, POOL_WIDTH),
                               lambda i, j: (i, jnp.maximum(j * halo_blocks - 1, 0), 0)),
                  tok(NSA_WIDTH),
                  pl.BlockSpec((None, S5_HALVES, tm, LANES), lambda i, j: (i, 0, j, 0)),
                  _layer_spec((POOL_WIDTH, POOL_WIDTH), layer), _layer_spec((1, POOL_WIDTH), layer),
                  _layer_spec((1, POOL_WIDTH), layer), _layer_spec((1, d), layer),
                  _layer_spec((S5_WIDTH, S5_WIDTH), layer), _layer_spec((1, S5_WIDTH), layer),
                  _layer_spec((d, d), layer)],
        out_specs=tok(d),
        out_shape=jax.ShapeDtypeStruct((b, s, d), F32),
        scratch_shapes=[pltpu.VMEM((POOL_HALO + tm, POOL_WIDTH), F32)],
        compiler_params=_params(("parallel", "arbitrary")),
        name="mixer_out_proj",
    )(x, mod, u_pool, u_pool, o_nsa, y_s5, pool_w_bd, pool_b, pool_scale, out_norm, glu_w, glu_b, w_out)


def _rope_tables(pos):
    half = ROT_DIM // 2
    inv_freq = jnp.exp(-math.log(ROPE_THETA) * jnp.arange(half, dtype=F32) * (2.0 / ROT_DIM))
    ang = pos[:, None] * inv_freq[None, :]
    cos, sin = jnp.cos(ang), jnp.sin(ang)
    n = pos.shape[0]
    rest = HEAD_DIM - ROT_DIM
    cos_t = jnp.concatenate([cos, cos, jnp.ones((n, rest), F32)], axis=1)
    sin_lo = jnp.concatenate([-sin, jnp.zeros((n, half + rest), F32)], axis=1)
    sin_hi = jnp.concatenate([jnp.zeros((n, half), F32), sin, jnp.zeros((n, rest), F32)], axis=1)
    return tuple(jnp.tile(t, (1, 2)) for t in (cos_t, sin_lo, sin_hi))


def _pad_lanes(v):
    return jnp.pad(v, [(0, 0)] * (v.ndim - 1) + [(0, LANES - v.shape[-1])])


def _rope_tables_t(pos):
    half = ROT_DIM // 2
    inv_freq = jnp.exp(-math.log(ROPE_THETA) * jnp.arange(half, dtype=F32) * (2.0 / ROT_DIM))
    ang = inv_freq[:, None] * pos[None, :]
    return jnp.cos(ang), jnp.sin(ang)


def _arrange_w_in(w):
    d = w.shape[0]
    o1, o2, o3 = POOL_WIDTH, POOL_WIDTH + NSA_WIDTH, POOL_WIDTH + NSA_WIDTH + 6 * LANES
    kv = w[:, o2:o3].reshape(d, 6, NSA_KV_HEADS, HEAD_DIM)
    ka = jnp.concatenate([kv[:, 2], kv[:, 4]], axis=-1).reshape(d, NSA_KV_HEADS * LANES)
    n_gate = NSA_REP * N_BRANCH
    w_row = jnp.concatenate([w[:, :o1], ka, kv[:, 0].reshape(d, LANES), kv[:, 1].reshape(d, LANES),
                             w[:, o3 + NSA_KV_HEADS * n_gate:]], axis=1)
    q_t = w[:, o1:o2].T
    v_t = jnp.concatenate([kv[:, 3], kv[:, 5]], axis=-1).reshape(d, NSA_KV_HEADS * LANES).T
    gate_t = w[:, o3:o3 + NSA_KV_HEADS * n_gate].T.reshape(NSA_KV_HEADS, n_gate, d)
    gate_t = jnp.pad(gate_t, ((0, 0), (0, GATE_ROWS - n_gate), (0, 0))).reshape(NSA_KV_HEADS * GATE_ROWS, d)
    return w_row.astype(BF16), jnp.concatenate([q_t, v_t, gate_t], axis=0).astype(BF16)


def _expand_cmp_w1(w1):
    halves = w1.reshape(2, CMP_STRIDE, HEAD_DIM, CMP_HIDDEN)
    both = jnp.concatenate([halves[0], halves[1]], axis=-1)
    out = []
    for g in range(NSA_KV_HEADS):
        z = jnp.zeros((CMP_STRIDE, NSA_KV_HEADS, HEAD_DIM, 2 * CMP_HIDDEN), F32).at[:, g].set(both)
        out.append(z.reshape(CMP_STRIDE * LANES, 2 * CMP_HIDDEN))
    return jnp.stack(out).astype(BF16)


def _block_diag(w):
    g, c, _ = w.shape
    return jnp.einsum('gcd,gf->gcfd', w, jnp.eye(g, dtype=w.dtype)).reshape(g * c, g * c)


def _prepare_parameters(norm_ffn1, ffn1_w_in, ffn1_w_out, norm_mix, w_in, w_out, out_norm, pool_w, pool_b, pool_scale,
                        q_norm, k_norm, cmp_pe, cmp_k_w1, cmp_k_w2, cmp_v_w1, cmp_v_w2, s5_lam_re, s5_lam_im,
                        s5_log_dt, s5_b_re, s5_b_im, s5_c_re, s5_c_im, s5_d, glu_w, glu_b, norm_ffn2, ffn2_w_in,
                        ffn2_w_out):
    n_layers = norm_mix.shape[0]
    row = lambda v: v.reshape(n_layers, 1, -1)
    w_row, w_t = jax.vmap(_arrange_w_in)(w_in)
    return dict(
        ffn1=(row(norm_ffn1), ffn1_w_in.astype(BF16), ffn1_w_out.astype(BF16)),
        ffn2=(row(norm_ffn2), ffn2_w_in.astype(BF16), ffn2_w_out.astype(BF16)),
        inproj=(row(norm_mix), w_row, w_t, q_norm.reshape(n_layers, HEAD_DIM, 1),
                jnp.concatenate([k_norm[:, 1], k_norm[:, 2]], axis=-1).reshape(n_layers, 1, LANES)),
        compress=(jax.vmap(_expand_cmp_w1)(cmp_k_w1), jax.vmap(_expand_cmp_w1)(cmp_v_w1), cmp_k_w1, cmp_v_w1,
                  cmp_pe.reshape(n_layers, 2, CMP_LEN * HEAD_DIM), _pad_lanes(cmp_k_w2).astype(BF16),
                  jnp.swapaxes(_pad_lanes(cmp_v_w2), 1, 2).astype(BF16), row(_pad_lanes(k_norm[:, 0]))),
        s5=jax.vmap(_s5_matrices)(s5_lam_re, s5_lam_im, s5_log_dt, s5_b_re, s5_b_im, s5_c_re, s5_c_im, s5_d),
        outproj=(jax.vmap(_block_diag)(pool_w).astype(BF16), row(pool_b), row(pool_scale), row(out_norm),
                 glu_w.astype(BF16), row(glu_b), w_out.astype(BF16)))


def _hybrid_layer(x, layer, mod, params, tabs, tabs_t, cmp_tabs):
    x = _ffn(x, layer, mod, *params["ffn1"], 0)
    u_pool, q_t, ka, kb, va_t, vb_t, kc, vc, gates_t, u_s5 = _inproj(
        x, layer, mod, *params["inproj"][:3], tabs, tabs_t, *params["inproj"][3:])
    k_cmp, v_cmp_t = _compress(kc, vc, layer, *params["compress"], cmp_tabs)
    o_nsa = _nsa_attention(q_t, k_cmp, v_cmp_t, ka, kb, va_t, vb_t, gates_t)
    y_s5 = _s5_scan(u_s5, layer, params["s5"])
    x = _outproj(x, layer, mod, u_pool, o_nsa, y_s5, *params["outproj"])
    return _ffn(x, layer, mod, *params["ffn2"], 6)


def kernel(x, c, ada_w, ada_b, norm_ffn1, ffn1_w_in, ffn1_w_out, norm_mix, w_in, w_out, out_norm, pool_w, pool_b, pool_scale, q_norm, k_norm, cmp_pe, cmp_k_w1, cmp_k_w2, cmp_v_w1, cmp_v_w2, s5_lam_re, s5_lam_im, s5_log_dt, s5_b_re, s5_b_im, s5_c_re, s5_c_im, s5_d, glu_w, glu_b, norm_ffn2, ffn2_w_in, ffn2_w_out):
    seq = x.shape[1]
    assert seq % (KEY_TILE * 4) == 0 and seq >= WINDOW + Q_BLOCK
    mod = _modulation(c, ada_w, ada_b)
    params = _prepare_parameters(norm_ffn1, ffn1_w_in, ffn1_w_out, norm_mix, w_in, w_out, out_norm, pool_w, pool_b,
                                 pool_scale, q_norm, k_norm, cmp_pe, cmp_k_w1, cmp_k_w2, cmp_v_w1, cmp_v_w2,
                                 s5_lam_re, s5_lam_im, s5_log_dt, s5_b_re, s5_b_im, s5_c_re, s5_c_im, s5_d, glu_w,
                                 glu_b, norm_ffn2, ffn2_w_in, ffn2_w_out)
    tabs = _rope_tables(jnp.arange(seq, dtype=F32))
    tabs_t = _rope_tables_t(jnp.arange(seq, dtype=F32))
    cmp_tabs = _rope_tables((jnp.arange(seq // CMP_STRIDE) * CMP_STRIDE + CMP_LEN - 1).astype(F32))
    for layer in range(ada_w.shape[0]):
        x = _hybrid_layer(x, layer, mod, params, tabs, tabs_t, cmp_tabs)
    return x
```

```python
import functools
import math

import jax
import jax.numpy as jnp
import numpy as np
from jax import lax
from jax.experimental import pallas as pl
from jax.experimental.pallas import tpu as pltpu

F32 = jnp.float32
BF16 = jnp.bfloat16
HIGHEST = lax.Precision.HIGHEST

LANES = 128
SUBLANES = 8
VMEM_LIMIT_BYTES = 56 * 1024 * 1024

D_MODEL = 1024
D_FF = 2816
N_MOD = 9
POOL_WIDTH = 256
POOL_GC = 64
POOL_WINDOWS = (2, 4, 8, 16)
POOL_HALO = 16
HEAD_DIM = 64
NSA_WIDTH = 512
NSA_HEADS = 8
NSA_KV_HEADS = 2
NSA_REP = 4
N_BRANCH = 3
S5_WIDTH = 256
S5_H = 16
S5_GROUPS = 16
S5_P = 64
CMP_STRIDE = 16
CMP_LEN = 32
CMP_HIDDEN = 128
SLC_BLOCK = 64
SLC_SHIFT = 6
N_SELECT = 16
N_FORCED = 3
WINDOW = 512
Q_BLOCK = 256
ROT_DIM = 16
ROPE_THETA = 500000.0
EPS = 1e-6
Q_SCALE = HEAD_DIM ** -0.5 * math.log2(math.e)
NEG_INF = -1e30
MASK_VALUE = -(2.0 ** 100)

OFF_POOL = 0
OFF_KA = OFF_POOL + POOL_WIDTH
OFF_KC = OFF_KA + NSA_KV_HEADS * LANES
OFF_VC = OFF_KC + LANES
OFF_S5 = OFF_VC + LANES
N_COLS = OFF_S5 + S5_WIDTH
GATE_ROWS = 16
ROW_Q = 0
ROW_V = ROW_Q + NSA_HEADS * HEAD_DIM
ROW_GATE = ROW_V + NSA_KV_HEADS * LANES
N_ROWS_T = ROW_GATE + NSA_KV_HEADS * GATE_ROWS

S5_CHUNK = 8
S5_FOLD = S5_CHUNK * S5_WIDTH
S5_STATE = S5_GROUPS * S5_P
S5_HALVES = S5_WIDTH // LANES
S5_HALF_FOLD = S5_FOLD // S5_HALVES
KEY_TILE = 512
V_ROWS = HEAD_DIM + 16


def _dot(a, b, precision=None):
    return jnp.dot(a, b, preferred_element_type=F32, precision=precision)


def _dot_nt(a, b, precision=None):
    return lax.dot_general(a, b, (((1,), (1,)), ((), ())), preferred_element_type=F32,
                           precision=precision)


def _sigmoid(x):
    return 1.0 / (1.0 + jnp.exp(-x))


def _gelu_tanh(x):
    return 0.5 * x * (1.0 + jnp.tanh(math.sqrt(2.0 / math.pi) * (x + 0.044715 * (x * x * x))))


def _params(sem):
    return pltpu.CompilerParams(dimension_semantics=sem, vmem_limit_bytes=VMEM_LIMIT_BYTES)


def _const_spec(shape):
    nd = len(shape)
    return pl.BlockSpec(shape, lambda *_: (0,) * nd, pipeline_mode=pl.Buffered(1))


def _layer_spec(shape, layer):
    nd = len(shape)
    return pl.BlockSpec((None,) + shape, lambda *_: (layer,) + (0,) * nd, pipeline_mode=pl.Buffered(1))


def _mod_kernel(ct_ref, w_ref, b_ref, o_ref, *, batch):
    w = w_ref[...]
    rows = []
    for b in range(batch):
        col = ct_ref[:, b:b + 1]
        rows.append(jnp.sum(w * (col * _sigmoid(col)), axis=0, keepdims=True) + b_ref[...])
    o_ref[...] = jnp.concatenate(rows + [jnp.zeros((SUBLANES - batch, w.shape[1]), F32)], axis=0)


def _modulation(c, ada_w, ada_b):
    n_layers, d, n = ada_w.shape
    b = c.shape[0]
    tn = 1152
    out = pl.pallas_call(
        functools.partial(_mod_kernel, batch=b),
        grid=(n_layers, n // tn),
        in_specs=[pl.BlockSpec((d, b), lambda l, j: (0, 0)),
                  pl.BlockSpec((None, d, tn), lambda l, j: (l, 0, j)),
                  pl.BlockSpec((None, 1, tn), lambda l, j: (l, 0, j))],
        out_specs=pl.BlockSpec((None, SUBLANES, tn), lambda l, j: (l, 0, j)),
        out_shape=jax.ShapeDtypeStruct((n_layers, SUBLANES, n), F32),
        compiler_params=_params(("arbitrary", "arbitrary")),
        name="adaln_mod",
    )(c.T, ada_w, ada_b.reshape(n_layers, 1, n))
    return out[:, :b].reshape(n_layers, b, N_MOD, d)


def _norm_modulate(x, gain, mod_ref, first_row):
    ms = jnp.mean(x * x, axis=-1, keepdims=True)
    y = x * lax.rsqrt(ms + EPS) * gain
    return y * (1.0 + mod_ref[first_row + 1:first_row + 2, :]) + mod_ref[first_row:first_row + 1, :]


def _ffn_kernel(x_ref, mod_ref, g_ref, win_ref, wout_ref, o_ref, *, first_row):
    x = x_ref[...]
    h = _norm_modulate(x, g_ref[...], mod_ref, first_row).astype(BF16)
    gu = _dot(h, win_ref[...])
    gate, up = gu[:, :D_FF], gu[:, D_FF:]
    a = (gate * _sigmoid(gate) * up).astype(BF16)
    y = _dot(a, wout_ref[...])
    o_ref[...] = x + 0.5 * mod_ref[first_row + 2:first_row + 3, :] * y


def _mod_spec(layer):
    return pl.BlockSpec((None, None, N_MOD, D_MODEL), lambda i, *_: (layer, i, 0, 0))


def _ffn(x, layer, mod, gain, w_in, w_out, first_row, tm=512):
    b, s, d = x.shape
    return pl.pallas_call(
        functools.partial(_ffn_kernel, first_row=first_row),
        grid=(b, s // tm),
        in_specs=[pl.BlockSpec((None, tm, d), lambda i, j: (i, j, 0)),
                  _mod_spec(layer),
                  _layer_spec((1, d), layer),
                  _layer_spec((d, 2 * D_FF), layer),
                  _layer_spec((D_FF, d), layer)],
        out_specs=pl.BlockSpec((None, tm, d), lambda i, j: (i, j, 0)),
        out_shape=jax.ShapeDtypeStruct((b, s, d), F32),
        compiler_params=_params(("parallel", "parallel")),
        name="ffn_half_step",
    )(x, mod, gain, w_in, w_out)


def _rope(v, cos_t, sin_lo, sin_hi):
    return (v * cos_t + pltpu.roll(v, LANES - ROT_DIM // 2, 1) * sin_lo
            + pltpu.roll(v, ROT_DIM // 2, 1) * sin_hi)


def _inproj_kernel(x_ref, mod_ref, g_ref, w_ref, wt_ref, cos_ref, slo_ref, shi_ref, cost_ref, sint_ref, qg_ref, kg_ref,
                   pool_ref, q_ref, ka_ref, kb_ref, va_ref, vb_ref, kc_ref, vc_ref, gate_ref, s5_ref):
    h = _norm_modulate(x_ref[...], g_ref[...], mod_ref, 3).astype(BF16)
    u = _dot(h, w_ref[...])
    ut = _dot_nt(wt_ref[...], h)
    tm = u.shape[0]
    cos_t, sin_lo, sin_hi = cos_ref[...], slo_ref[...], shi_ref[...]
    lane = lax.broadcasted_iota(jnp.int32, (tm, LANES), 1)
    low_half = lane < HEAD_DIM
    pos = pl.program_id(1) * tm + lax.broadcasted_iota(jnp.int32, (tm, LANES), 0)
    block_one_hot = jnp.where((pos >> SLC_SHIFT) == lane, 1.0, 0.0).astype(BF16)

    pool_ref[...] = u[:, OFF_POOL:OFF_POOL + POOL_WIDTH]
    kc_ref[...] = u[:, OFF_KC:OFF_KC + LANES]
    vc_ref[...] = u[:, OFF_VC:OFF_VC + LANES]
    for half in range(S5_HALVES):
        s5_ref[half] = u[:, OFF_S5 + half * LANES:OFF_S5 + (half + 1) * LANES]
    for g in range(NSA_KV_HEADS):
        v = u[:, OFF_KA + g * LANES:OFF_KA + (g + 1) * LANES]
        sq = v * v
        ms_lo = jnp.sum(jnp.where(low_half, sq, 0.0), axis=-1, keepdims=True) * (1.0 / HEAD_DIM)
        ms_hi = jnp.sum(jnp.where(low_half, 0.0, sq), axis=-1, keepdims=True) * (1.0 / HEAD_DIM)
        r = jnp.where(low_half, lax.rsqrt(ms_lo + EPS), lax.rsqrt(ms_hi + EPS))
        kn = _rope(v * r * kg_ref[...], cos_t, sin_lo, sin_hi)
        ka_ref[g, :, 0:LANES] = kn.astype(BF16)
        ka_ref[g, :, LANES:2 * LANES] = block_one_hot
        kb_ref[g] = pltpu.roll(kn, HEAD_DIM, 1).astype(BF16)

    half_rot = ROT_DIM // 2
    cos8, sin8 = cost_ref[...], sint_ref[...]
    zero_rows = jnp.zeros((LANES - HEAD_DIM, tm), F32)
    for hd in range(NSA_HEADS):
        v = ut[ROW_Q + hd * HEAD_DIM:ROW_Q + (hd + 1) * HEAD_DIM, :]
        ms = jnp.sum(v * v, axis=0, keepdims=True) * (1.0 / HEAD_DIM)
        vn = v * lax.rsqrt(ms + EPS) * qg_ref[...]
        x1, x2 = vn[0:half_rot], vn[half_rot:ROT_DIM]
        roped = jnp.concatenate([x1 * cos8 - x2 * sin8, x2 * cos8 + x1 * sin8, vn[ROT_DIM:]], axis=0)
        q_ref[hd] = jnp.concatenate([roped * Q_SCALE, zero_rows], axis=0).astype(BF16)
    ones = jnp.ones((HEAD_DIM, tm), F32)
    for g in range(NSA_KV_HEADS):
        vt = ut[ROW_V + g * LANES:ROW_V + (g + 1) * LANES, :]
        va = jnp.concatenate([vt[0:HEAD_DIM], ones], axis=0).astype(BF16)
        vb = jnp.concatenate([vt[HEAD_DIM:], ones], axis=0).astype(BF16)
        for blk in range(tm // LANES):
            va_ref[g, blk] = va[:, blk * LANES:(blk + 1) * LANES]
            vb_ref[g, blk] = vb[:, blk * LANES:(blk + 1) * LANES]
        gate_ref[g] = _sigmoid(ut[ROW_GATE + g * GATE_ROWS:ROW_GATE + (g + 1) * GATE_ROWS, :])


def _inproj(x, layer, mod, gain, w_row, w_t, rope_tabs, rope_tabs_t, q_gain, k_gain, tm=1024):
    b, s, d = x.shape
    g = NSA_KV_HEADS
    tok = lambda width: pl.BlockSpec((None, tm, width), lambda i, j: (i, j, 0))
    grp = lambda n, width=LANES: pl.BlockSpec((None, n, tm, width), lambda i, j: (i, 0, j, 0))
    lanes_tok = lambda n, rows: pl.BlockSpec((None, n, rows, tm), lambda i, j: (i, 0, 0, j))
    v_blocks = pl.BlockSpec((None, g, tm // LANES, LANES, LANES), lambda i, j: (i, 0, j, 0, 0))
    tab = pl.BlockSpec((tm, LANES), lambda i, j: (j, 0))
    tab_t = pl.BlockSpec((ROT_DIM // 2, tm), lambda i, j: (0, j))
    sds = jax.ShapeDtypeStruct
    return pl.pallas_call(
        _inproj_kernel,
        grid=(b, s // tm),
        in_specs=[tok(d),
                  _mod_spec(layer),
                  _layer_spec((1, d), layer),
                  _layer_spec((d, N_COLS), layer), _layer_spec((N_ROWS_T, d), layer),
                  tab, tab, tab, tab_t, tab_t,
                  _layer_spec((HEAD_DIM, 1), layer), _layer_spec((1, LANES), layer)],
        out_specs=[tok(POOL_WIDTH), lanes_tok(NSA_HEADS, LANES), grp(g, 2 * LANES), grp(g), v_blocks, v_blocks,
                   tok(LANES), tok(LANES), lanes_tok(g, GATE_ROWS), grp(S5_HALVES)],
        out_shape=[sds((b, s, POOL_WIDTH), F32), sds((b, NSA_HEADS, LANES, s), BF16),
                   sds((b, g, s, 2 * LANES), BF16), sds((b, g, s, LANES), BF16),
                   sds((b, g, s // LANES, LANES, LANES), BF16), sds((b, g, s // LANES, LANES, LANES), BF16),
                   sds((b, s, LANES), F32), sds((b, s, LANES), F32),
                   sds((b, g, GATE_ROWS, s), F32), sds((b, S5_HALVES, s, LANES), F32)],
        compiler_params=_params(("parallel", "parallel")),
        name="mixer_in_proj",
    )(x, mod, gain, w_row, w_t, *rope_tabs, *rope_tabs_t, q_gain, k_gain)


def _fold_rows(ref, n):
    rows = ref.shape[0] // n
    return jnp.concatenate([ref[pl.ds(k, rows, stride=n), :] for k in range(n)], axis=1)


def _unfold_rows(ref, value, n):
    rows = ref.shape[0] // n
    for k in range(n):
        ref[pl.ds(k, rows, stride=n), :] = value[:, k * LANES:(k + 1) * LANES]


def _compress_kernel(kc_ref, vc_ref, w1k_ref, w1v_ref, w1k_raw_ref, w1v_raw_ref, pe_ref, w2k_ref, w2vt_ref,
                     kg_ref, cos_ref, slo_ref, shi_ref, ko_ref, vo_ref, *, n_cmp):
    ncp = kc_ref.shape[0] // CMP_STRIDE
    for src_ref, w1_ref, raw_ref, pe_row, is_key in ((kc_ref, w1k_ref, w1k_raw_ref, 0, True),
                                                    (vc_ref, w1v_ref, w1v_raw_ref, 1, False)):
        chunks = _fold_rows(src_ref, CMP_STRIDE).astype(BF16)
        pe = jnp.broadcast_to(pe_ref[pe_row:pe_row + 1, :], (SUBLANES, CMP_LEN * HEAD_DIM))
        pe_term = _dot(pe, raw_ref[...], HIGHEST)[0:1, :]
        for g in range(NSA_KV_HEADS):
            a = _dot(chunks, w1_ref[g])
            pre = a[:, :CMP_HIDDEN] + pltpu.roll(a[:, CMP_HIDDEN:], ncp - 1, 0) + pe_term
            hidden = _gelu_tanh(pre).astype(BF16)
            if is_key:
                out = _dot(hidden, w2k_ref[...])
                ms = jnp.sum(out * out, axis=-1, keepdims=True) * (1.0 / HEAD_DIM)
                out = _rope(out * lax.rsqrt(ms + EPS) * kg_ref[...], cos_ref[...], slo_ref[...], shi_ref[...])
                real_row = lax.broadcasted_iota(jnp.int32, (ncp, LANES), 0) < n_cmp
                ko_ref[g] = jnp.where(real_row, out, 0.0).astype(BF16)
            else:
                out_t = _dot_nt(w2vt_ref[...], hidden)
                real_col = lax.broadcasted_iota(jnp.int32, (LANES, ncp), 1) < n_cmp
                value_row = lax.broadcasted_iota(jnp.int32, (LANES, ncp), 0) < HEAD_DIM
                vo_ref[g] = jnp.where(value_row, jnp.where(real_col, out_t, 0.0), 1.0).astype(BF16)


def _compress(kc, vc, layer, w1k, w1v, w1k_raw, w1v_raw, pe, w2k, w2v_t, k_gain, cmp_tabs):
    b, s, _ = kc.shape
    ncp = s // CMP_STRIDE
    fold = CMP_STRIDE * LANES
    src = pl.BlockSpec((None, s, LANES), lambda i: (i, 0, 0))
    raw = _layer_spec((CMP_LEN * HEAD_DIM, CMP_HIDDEN), layer)
    w1 = _layer_spec((NSA_KV_HEADS, fold, 2 * CMP_HIDDEN), layer)
    w2 = _layer_spec((CMP_HIDDEN, LANES), layer)
    tab = _const_spec((ncp, LANES))
    return pl.pallas_call(
        functools.partial(_compress_kernel, n_cmp=ncp - 1),
        grid=(b,),
        in_specs=[src, src, w1, w1, raw, raw, _layer_spec((2, CMP_LEN * HEAD_DIM), layer), w2, w2,
                  _layer_spec((1, LANES), layer), tab, tab, tab],
        out_specs=[pl.BlockSpec((None, NSA_KV_HEADS, ncp, LANES), lambda i: (i, 0, 0, 0)),
                   pl.BlockSpec((None, NSA_KV_HEADS, LANES, ncp), lambda i: (i, 0, 0, 0))],
        out_shape=[jax.ShapeDtypeStruct((b, NSA_KV_HEADS, ncp, LANES), BF16),
                   jax.ShapeDtypeStruct((b, NSA_KV_HEADS, LANES, ncp), BF16)],
        compiler_params=_params(("parallel",)),
        name="nsa_compress",
    )(kc, vc, w1k, w1v, w1k_raw, w1v_raw, pe, w2k, w2v_t, k_gain, *cmp_tabs)


def _selection_bias(imp_t, qb):
    nsb, nq = imp_t.shape
    j = lax.broadcasted_iota(jnp.int32, (nsb, nq), 0)
    t = qb * Q_BLOCK + lax.broadcasted_iota(jnp.int32, (nsb, nq), 1)
    cur = t >> SLC_SHIFT
    valid = j * SLC_BLOCK <= t
    forced = (j == 0) | (j == cur) | (j == cur - 1)
    j_f = j.astype(F32)
    start = jnp.where(valid & jnp.logical_not(forced), imp_t, -1.0)
    vals = start
    for _ in range(N_SELECT - N_FORCED):
        m = jnp.max(vals, axis=0, keepdims=True)
        idx = jnp.min(jnp.where(vals == m, j_f, float(nsb)), axis=0, keepdims=True)
        vals = jnp.where(j_f == idx, -2.0, vals)
    return jnp.where((forced & valid) | (vals != start), 0.0, MASK_VALUE)


def _attn_kernel(q_ref, kc_ref, vct_ref, ka_ref, kb_ref, va_ref, vb_ref, gate_ref, ovl_ref, o_ref,
                 sa_ref, sb_ref, acc_ref, *, seq):
    qb = pl.program_id(2)
    cols = NSA_REP * Q_BLOCK
    ncp = seq // CMP_STRIDE
    q_t = jnp.concatenate([q_ref[r] for r in range(NSA_REP)], axis=1)
    t_col = qb * Q_BLOCK + lax.broadcasted_iota(jnp.int32, (1, cols), 1) % Q_BLOCK
    t_q = qb * Q_BLOCK + lax.broadcasted_iota(jnp.int32, (1, Q_BLOCK), 1)

    def all_heads(per_query):
        return jnp.concatenate([per_query] * NSA_REP, axis=1)

    cmp_end = lax.broadcasted_iota(jnp.int32, (ncp, Q_BLOCK), 0) * CMP_STRIDE + (CMP_LEN - 1)
    s = _dot(kc_ref[...], q_t) + all_heads(jnp.where(cmp_end <= t_q, 0.0, NEG_INF))
    e = jnp.exp2(s - jnp.max(s, axis=0, keepdims=True)).astype(BF16)
    oc = _dot(jnp.concatenate([vct_ref[0:V_ROWS, :], ovl_ref[...]], axis=0), e)
    oc = oc * jnp.where(t_col >= CMP_LEN - 1, 1.0 / oc[HEAD_DIM:HEAD_DIM + 1, :], 0.0)
    o_cmp, imp = oc[0:V_ROWS], oc[V_ROWS:V_ROWS + LANES]
    imp_t = (imp[:, 0:Q_BLOCK] + imp[:, Q_BLOCK:2 * Q_BLOCK]
             + imp[:, 2 * Q_BLOCK:3 * Q_BLOCK] + imp[:, 3 * Q_BLOCK:4 * Q_BLOCK])
    bias = _selection_bias(imp_t, qb).astype(BF16)

    q_aug = jnp.concatenate([q_t, jnp.concatenate([bias] * NSA_REP, axis=1)], axis=0)

    span = WINDOW + Q_BLOCK
    start = pl.multiple_of(jnp.maximum(qb * Q_BLOCK - WINDOW, 0), LANES)
    start_blk = start // LANES
    kpos = start + lax.broadcasted_iota(jnp.int32, (span, Q_BLOCK), 0)
    in_window = (kpos <= t_q) & (kpos > t_q - WINDOW)
    s = _dot(kb_ref[pl.ds(start, span), :], q_t) + all_heads(jnp.where(in_window, 0.0, NEG_INF))
    e = jnp.exp2(s - jnp.max(s, axis=0, keepdims=True)).astype(BF16)
    acc_win = _dot(jnp.concatenate([vb_ref[start_blk + i, 0:V_ROWS, :] for i in range(span // LANES)], axis=1), e)
    o_win = acc_win * (1.0 / acc_win[HEAD_DIM:HEAD_DIM + 1, :])

    blocks_per_tile = KEY_TILE // LANES

    def score_tile(kt, s_ref):
        k0 = pl.multiple_of(kt * KEY_TILE, KEY_TILE)
        s_ref[...] = _dot(ka_ref[pl.ds(k0, KEY_TILE), :], q_aug)

    def absorb_tile(kt, s_ref, carry, causal):
        m_i, acc = carry
        blk0 = kt * blocks_per_tile
        s = s_ref[...]
        if causal:
            kpos = kt * KEY_TILE + lax.broadcasted_iota(jnp.int32, (KEY_TILE, Q_BLOCK), 0)
            s = s + all_heads(jnp.where(kpos <= t_q, 0.0, MASK_VALUE))
        m_new = jnp.maximum(m_i, jnp.max(s, axis=0, keepdims=True))
        p = jnp.exp2(s - m_new).astype(BF16)
        v_t = jnp.concatenate([va_ref[blk0 + i, 0:V_ROWS, :] for i in range(blocks_per_tile)], axis=1)
        return m_new, jnp.exp2(m_i - m_new) * acc + _dot(v_t, p)

    def pair_trip(j, carry):
        score_tile(2 * j + 1, sb_ref)
        carry = absorb_tile(2 * j, sa_ref, carry, causal=False)
        score_tile(2 * j + 2, sa_ref)
        return absorb_tile(2 * j + 1, sb_ref, carry, causal=False)

    def double_trip(j, carry):
        return pair_trip(2 * j + 1, pair_trip(2 * j, carry))

    last_pair = qb // (2 * KEY_TILE // Q_BLOCK)
    score_tile(0, sa_ref)
    carry = (jnp.full((1, cols), NEG_INF, F32), jnp.zeros((V_ROWS, cols), F32))
    carry = lax.fori_loop(0, last_pair // 2, double_trip, carry)
    carry = lax.fori_loop(2 * (last_pair // 2), last_pair, pair_trip, carry)
    second_tile_live = (qb // (KEY_TILE // Q_BLOCK)) % 2 == 1

    @pl.when(second_tile_live)
    def _():
        score_tile(2 * last_pair + 1, sb_ref)
        both = absorb_tile(2 * last_pair, sa_ref, carry, causal=False)
        acc_ref[...] = absorb_tile(2 * last_pair + 1, sb_ref, both, causal=True)[1]

    @pl.when(jnp.logical_not(second_tile_live))
    def _():
        acc_ref[...] = absorb_tile(2 * last_pair, sa_ref, carry, causal=True)[1]

    acc_slc = acc_ref[...]
    o_slc = acc_slc * (1.0 / acc_slc[HEAD_DIM:HEAD_DIM + 1, :])

    gates = gate_ref[...]
    heads = []
    for r in range(NSA_REP):
        cs = slice(r * Q_BLOCK, (r + 1) * Q_BLOCK)
        c = r * N_BRANCH
        heads.append(gates[c:c + 1, :] * o_cmp[0:HEAD_DIM, cs] + gates[c + 1:c + 2, :] * o_slc[0:HEAD_DIM, cs]
                     + gates[c + 2:c + 3, :] * o_win[0:HEAD_DIM, cs])
    o_ref[...] = jnp.concatenate(heads, axis=0).T


def _overlap_matrix(seq):
    ncp, nsb = seq // CMP_STRIDE, seq // SLC_BLOCK
    c_start = np.arange(ncp)[None, :] * CMP_STRIDE
    s_start = np.arange(nsb)[:, None] * SLC_BLOCK
    ovl = np.clip(np.minimum(c_start + CMP_LEN, s_start + SLC_BLOCK) - np.maximum(c_start, s_start), 0, None)
    ovl = ovl.astype(np.float32) / CMP_LEN
    ovl[:, ncp - 1] = 0.0
    return jnp.asarray(np.pad(ovl, ((0, LANES - nsb), (0, 0))))


def _nsa_attention(q_t, k_cmp, v_cmp_t, ka, kb, va_t, vb_t, gates_t):
    b, _, _, s = q_t.shape
    g = NSA_KV_HEADS
    ncp = s // CMP_STRIDE
    assert s // SLC_BLOCK <= LANES
    per_group = lambda *shape: pl.BlockSpec((None, None) + shape, lambda i, j, k: (i, j) + (0,) * len(shape))
    return pl.pallas_call(
        functools.partial(_attn_kernel, seq=s),
        grid=(b, g, s // Q_BLOCK),
        in_specs=[pl.BlockSpec((None, NSA_REP, LANES, Q_BLOCK), lambda i, j, k: (i, j, 0, k)),
                  per_group(ncp, LANES), per_group(LANES, ncp), per_group(s, 2 * LANES), per_group(s, LANES),
                  per_group(s // LANES, LANES, LANES), per_group(s // LANES, LANES, LANES),
                  pl.BlockSpec((None, None, GATE_ROWS, Q_BLOCK), lambda i, j, k: (i, j, 0, k)),
                  _const_spec((LANES, ncp))],
        out_specs=pl.BlockSpec((None, Q_BLOCK, NSA_REP * HEAD_DIM), lambda i, j, k: (i, k, j)),
        out_shape=jax.ShapeDtypeStruct((b, s, NSA_WIDTH), F32),
        scratch_shapes=[pltpu.VMEM((KEY_TILE, NSA_REP * Q_BLOCK), F32)] * 2
        + [pltpu.VMEM((V_ROWS, NSA_REP * Q_BLOCK), F32)],
        compiler_params=_params(("parallel", "parallel", "arbitrary")),
        name="nsa_attention",
    )(q_t, k_cmp, v_cmp_t, ka, kb, va_t, vb_t, gates_t, _overlap_matrix(s).astype(BF16))


def _s5_kernel(u_ref, mt_ref, bc_ref, cc_ref, a1_ref, a2_ref, d_ref, y_ref, h_ref, g_scr, hp_scr):
    @pl.when(pl.program_id(1) == 0)
    def _():
        h_ref[...] = jnp.zeros_like(h_ref)

    halves = range(S5_HALVES)
    u = [_fold_rows(u_ref.at[a], S5_CHUNK) for a in halves]
    ub = [x.astype(BF16) for x in u]
    y_local = [_dot(ub[a], mt_ref[a]) for a in halves]
    for a in halves:
        g_scr[:, a * S5_HALF_FOLD:(a + 1) * S5_HALF_FOLD] = _dot(ub[a], bc_ref[a])
    a1, a2 = a1_ref[...], a2_ref[...]
    n_state = S5_STATE // S5_HALVES

    def swap_re_im(h):
        parts = [h[:, k * n_state:(k + 1) * n_state] for k in range(2 * S5_HALVES)]
        return jnp.concatenate([parts[k ^ 1] for k in range(2 * S5_HALVES)], axis=1)

    def step(i, h):
        hp_scr[pl.ds(i, 1), :] = h
        return a1 * h + a2 * swap_re_im(h) + g_scr[pl.ds(i, 1), :]

    h_ref[...] = lax.fori_loop(0, g_scr.shape[0], step, h_ref[...], unroll=8)
    for a in halves:
        carried = _dot(hp_scr[:, a * S5_HALF_FOLD:(a + 1) * S5_HALF_FOLD].astype(BF16), cc_ref[a])
        _unfold_rows(y_ref.at[a], y_local[a] + carried + u[a] * d_ref[a:a + 1, :], S5_CHUNK)


def _s5_scan(u, layer, mats):
    b, _, s, _ = u.shape
    rows = s // S5_CHUNK
    tc = min(128, rows)
    tile = pl.BlockSpec((None, S5_HALVES, tc * S5_CHUNK, LANES), lambda i, j: (i, 0, j, 0))
    mat = _layer_spec((S5_HALVES, S5_HALF_FOLD, S5_HALF_FOLD), layer)
    row = _layer_spec((1, S5_HALVES * S5_HALF_FOLD), layer)
    return pl.pallas_call(
        _s5_kernel,
        grid=(b, rows // tc),
        in_specs=[tile, mat, mat, mat, row, row, _layer_spec((S5_HALVES, S5_HALF_FOLD), layer)],
        out_specs=tile,
        out_shape=jax.ShapeDtypeStruct((b, S5_HALVES, s, LANES), F32),
        scratch_shapes=[pltpu.VMEM((1, S5_HALVES * S5_HALF_FOLD), F32),
                        pltpu.VMEM((tc, S5_HALVES * S5_HALF_FOLD), F32),
                        pltpu.VMEM((tc, S5_HALVES * S5_HALF_FOLD), F32)],
        compiler_params=_params(("parallel", "arbitrary")),
        name="s5_scan",
    )(u, *mats)


def _s5_matrices(lam_re, lam_im, log_dt, b_re, b_im, c_re, c_im, d_skip):
    t0, ng, nh, npm = S5_CHUNK, S5_GROUPS, S5_H, S5_P
    gh = ng // S5_HALVES
    ein = functools.partial(jnp.einsum, precision=HIGHEST)
    lam = lax.complex(lam_re, lam_im)
    step = jnp.exp(log_dt)[:, None]
    lam_bar = jnp.exp(lam * step)
    b_bar = lax.complex(b_re, b_im) * ((lam_bar - 1.0) / lam)[..., None]
    c_mat = lax.complex(c_re, c_im)
    k = jnp.arange(t0 + 1, dtype=F32)[:, None, None]
    pw = jnp.exp((lam * step)[None] * k)

    def same_group(rows_per_group, cols_per_group):
        r = np.arange(gh * rows_per_group)[:, None] // rows_per_group
        c = np.arange(gh * cols_per_group)[None, :] // cols_per_group
        return jnp.asarray((r == c).astype(np.float32))

    def per_group_blocks(x, rows_per_group, cols_per_group):
        return jnp.tile(x, (1,) * (x.ndim - 1) + (gh,)) * same_group(rows_per_group, cols_per_group)

    kern = jnp.real(ein('ghp,kgp,gpq->kghq', c_mat, pw[:t0], b_bar))
    d_k = per_group_blocks(kern.transpose(0, 1, 3, 2).reshape(t0, S5_HALVES, gh * nh, nh), nh, nh)
    d_k = jnp.concatenate([d_k, jnp.zeros_like(d_k[:1])], axis=0)
    lag = np.arange(t0)[None, :] - np.arange(t0)[:, None]
    mt = d_k[np.where(lag >= 0, lag, t0)]
    mt = mt.transpose(2, 0, 3, 1, 4).reshape(S5_HALVES, S5_HALF_FOLD, S5_HALF_FOLD)
    b_j = (pw[t0 - 1 - np.arange(t0)][..., None] * b_bar[None]).transpose(0, 1, 3, 2)
    b_j = b_j.reshape(t0, S5_HALVES, gh * nh, npm)
    bc = jnp.concatenate([per_group_blocks(jnp.real(b_j), nh, npm), per_group_blocks(jnp.imag(b_j), nh, npm)],
                         axis=-1)
    bc = bc.transpose(1, 0, 2, 3).reshape(S5_HALVES, S5_HALF_FOLD, S5_HALF_FOLD)
    c_i = (c_mat[None] * pw[1:t0 + 1][:, :, None, :]).transpose(1, 3, 0, 2)
    c_i = c_i.reshape(S5_HALVES, gh * npm, t0, nh)
    mask = same_group(npm, nh)[:, None, :]
    cc = jnp.concatenate([jnp.tile(jnp.real(c_i), (1, 1, 1, gh)) * mask,
                          jnp.tile(-jnp.imag(c_i), (1, 1, 1, gh)) * mask], axis=1)
    cc = cc.reshape(S5_HALVES, S5_HALF_FOLD, S5_HALF_FOLD)
    a_chunk = pw[t0].reshape(S5_HALVES, gh * npm)
    a1 = jnp.concatenate([jnp.real(a_chunk), jnp.real(a_chunk)], axis=1).reshape(1, -1)
    a2 = jnp.concatenate([-jnp.imag(a_chunk), jnp.imag(a_chunk)], axis=1).reshape(1, -1)
    d_vec = jnp.tile(d_skip.reshape(S5_HALVES, gh * nh), (1, t0))
    return mt.astype(BF16), bc.astype(BF16), cc.astype(BF16), a1, a2, d_vec


def _rms_gain(y, gain):
    return y * lax.rsqrt(jnp.mean(y * y, axis=-1, keepdims=True) + EPS) * gain


def _outproj_kernel(x_ref, mod_ref, pool_ref, halo_ref, nsa_ref, s5_ref, pw_ref, pb_ref, ps_ref, on_ref,
                    gw_ref, gb_ref, wo_ref, o_ref, buf):
    j = pl.program_id(1)
    tm = x_ref.shape[0]
    v = pool_ref[...]
    buf[0:POOL_HALO, :] = jnp.where(j > 0, halo_ref[...], 0.0)
    buf[POOL_HALO:POOL_HALO + tm, :] = v
    lane_group = lax.broadcasted_iota(jnp.int32, (tm, POOL_WIDTH), 1) >> SLC_SHIFT
    t1 = (j * tm + 1 + lax.broadcasted_iota(jnp.int32, (tm, 1), 0)).astype(F32)
    run, k, pooled = v, 1, jnp.zeros_like(v)
    for gi, w in enumerate(POOL_WINDOWS):
        while k < w:
            run = run + buf[POOL_HALO - k:POOL_HALO - k + tm, :]
            k += 1
        pooled = jnp.where(lane_group == gi, run / jnp.minimum(t1, float(w)) - v, pooled)
    y_pool = (_dot(pooled.astype(BF16), pw_ref[...]) + pb_ref[...]) * ps_ref[...]

    y = _gelu_tanh(jnp.concatenate([s5_ref[half] for half in range(S5_HALVES)], axis=-1))
    y_s5 = y * _sigmoid(_dot(y.astype(BF16), gw_ref[...]) + gb_ref[...])

    cat = jnp.concatenate(
        [_rms_gain(y_pool, on_ref[:, 0:POOL_WIDTH]),
         _rms_gain(nsa_ref[...], on_ref[:, POOL_WIDTH:POOL_WIDTH + NSA_WIDTH]),
         _rms_gain(y_s5, on_ref[:, POOL_WIDTH + NSA_WIDTH:])], axis=-1).astype(BF16)
    o_ref[...] = x_ref[...] + mod_ref[5:6, :] * _dot(cat, wo_ref[...])


def _outproj(x, layer, mod, u_pool, o_nsa, y_s5, pool_w_bd, pool_b, pool_scale, out_norm, glu_w, glu_b, w_out,
             tm=1024):
    b, s, d = x.shape
    tok = lambda width: pl.BlockSpec((None, tm, width), lambda i, j: (i, j, 0))
    halo_blocks = tm // POOL_HALO
    return pl.pallas_call(
        _outproj_kernel,
        grid=(b, s // tm),
        in_specs=[tok(d),
                  _mod_spec(layer),
                  tok(POOL_WIDTH),
                  pl.BlockSpec((None, POOL_HALO, POOL_WIDTH),
                               lambda i, j: (i, jnp.maximum(j * halo_blocks - 1, 0), 0)),
                  tok(NSA_WIDTH),
                  pl.BlockSpec((None, S5_HALVES, tm, LANES), lambda i, j: (i, 0, j, 0)),
                  _layer_spec((POOL_WIDTH, POOL_WIDTH), layer), _layer_spec((1, POOL_WIDTH), layer),
                  _layer_spec((1, POOL_WIDTH), layer), _layer_spec((1, d), layer),
                  _layer_spec((S5_WIDTH, S5_WIDTH), layer), _layer_spec((1, S5_WIDTH), layer),
                  _layer_spec((d, d), layer)],
        out_specs=tok(d),
        out_shape=jax.ShapeDtypeStruct((b, s, d), F32),
        scratch_shapes=[pltpu.VMEM((POOL_HALO + tm, POOL_WIDTH), F32)],
        compiler_params=_params(("parallel", "arbitrary")),
        name="mixer_out_proj",
    )(x, mod, u_pool, u_pool, o_nsa, y_s5, pool_w_bd, pool_b, pool_scale, out_norm, glu_w, glu_b, w_out)


def _rope_tables(pos):
    half = ROT_DIM // 2
    inv_freq = jnp.exp(-math.log(ROPE_THETA) * jnp.arange(half, dtype=F32) * (2.0 / ROT_DIM))
    ang = pos[:, None] * inv_freq[None, :]
    cos, sin = jnp.cos(ang), jnp.sin(ang)
    n = pos.shape[0]
    rest = HEAD_DIM - ROT_DIM
    cos_t = jnp.concatenate([cos, cos, jnp.ones((n, rest), F32)], axis=1)
    sin_lo = jnp.concatenate([-sin, jnp.zeros((n, half + rest), F32)], axis=1)
    sin_hi = jnp.concatenate([jnp.zeros((n, half), F32), sin, jnp.zeros((n, rest), F32)], axis=1)
    return tuple(jnp.tile(t, (1, 2)) for t in (cos_t, sin_lo, sin_hi))


def _pad_lanes(v):
    return jnp.pad(v, [(0, 0)] * (v.ndim - 1) + [(0, LANES - v.shape[-1])])


def _rope_tables_t(pos):
    half = ROT_DIM // 2
    inv_freq = jnp.exp(-math.log(ROPE_THETA) * jnp.arange(half, dtype=F32) * (2.0 / ROT_DIM))
    ang = inv_freq[:, None] * pos[None, :]
    return jnp.cos(ang), jnp.sin(ang)


def _arrange_w_in(w):
    d = w.shape[0]
    o1, o2, o3 = POOL_WIDTH, POOL_WIDTH + NSA_WIDTH, POOL_WIDTH + NSA_WIDTH + 6 * LANES
    kv = w[:, o2:o3].reshape(d, 6, NSA_KV_HEADS, HEAD_DIM)
    ka = jnp.concatenate([kv[:, 2], kv[:, 4]], axis=-1).reshape(d, NSA_KV_HEADS * LANES)
    n_gate = NSA_REP * N_BRANCH
    w_row = jnp.concatenate([w[:, :o1], ka, kv[:, 0].reshape(d, LANES), kv[:, 1].reshape(d, LANES),
                             w[:, o3 + NSA_KV_HEADS * n_gate:]], axis=1)
    q_t = w[:, o1:o2].T
    v_t = jnp.concatenate([kv[:, 3], kv[:, 5]], axis=-1).reshape(d, NSA_KV_HEADS * LANES).T
    gate_t = w[:, o3:o3 + NSA_KV_HEADS * n_gate].T.reshape(NSA_KV_HEADS, n_gate, d)
    gate_t = jnp.pad(gate_t, ((0, 0), (0, GATE_ROWS - n_gate), (0, 0))).reshape(NSA_KV_HEADS * GATE_ROWS, d)
    return w_row.astype(BF16), jnp.concatenate([q_t, v_t, gate_t], axis=0).astype(BF16)


def _expand_cmp_w1(w1):
    halves = w1.reshape(2, CMP_STRIDE, HEAD_DIM, CMP_HIDDEN)
    both = jnp.concatenate([halves[0], halves[1]], axis=-1)
    out = []
    for g in range(NSA_KV_HEADS):
        z = jnp.zeros((CMP_STRIDE, NSA_KV_HEADS, HEAD_DIM, 2 * CMP_HIDDEN), F32).at[:, g].set(both)
        out.append(z.reshape(CMP_STRIDE * LANES, 2 * CMP_HIDDEN))
    return jnp.stack(out).astype(BF16)


def _block_diag(w):
    g, c, _ = w.shape
    return jnp.einsum('gcd,gf->gcfd', w, jnp.eye(g, dtype=w.dtype)).reshape(g * c, g * c)


def _prepare_parameters(norm_ffn1, ffn1_w_in, ffn1_w_out, norm_mix, w_in, w_out, out_norm, pool_w, pool_b, pool_scale,
                        q_norm, k_norm, cmp_pe, cmp_k_w1, cmp_k_w2, cmp_v_w1, cmp_v_w2, s5_lam_re, s5_lam_im,
                        s5_log_dt, s5_b_re, s5_b_im, s5_c_re, s5_c_im, s5_d, glu_w, glu_b, norm_ffn2, ffn2_w_in,
                        ffn2_w_out):
    n_layers = norm_mix.shape[0]
    row = lambda v: v.reshape(n_layers, 1, -1)
    w_row, w_t = jax.vmap(_arrange_w_in)(w_in)
    return dict(
        ffn1=(row(norm_ffn1), ffn1_w_in.astype(BF16), ffn1_w_out.astype(BF16)),
        ffn2=(row(norm_ffn2), ffn2_w_in.astype(BF16), ffn2_w_out.astype(BF16)),
        inproj=(row(norm_mix), w_row, w_t, q_norm.reshape(n_layers, HEAD_DIM, 1),
                jnp.concatenate([k_norm[:, 1], k_norm[:, 2]], axis=-1).reshape(n_layers, 1, LANES)),
        compress=(jax.vmap(_expand_cmp_w1)(cmp_k_w1), jax.vmap(_expand_cmp_w1)(cmp_v_w1), cmp_k_w1, cmp_v_w1,
                  cmp_pe.reshape(n_layers, 2, CMP_LEN * HEAD_DIM), _pad_lanes(cmp_k_w2).astype(BF16),
                  jnp.swapaxes(_pad_lanes(cmp_v_w2), 1, 2).astype(BF16), row(_pad_lanes(k_norm[:, 0]))),
        s5=jax.vmap(_s5_matrices)(s5_lam_re, s5_lam_im, s5_log_dt, s5_b_re, s5_b_im, s5_c_re, s5_c_im, s5_d),
        outproj=(jax.vmap(_block_diag)(pool_w).astype(BF16), row(pool_b), row(pool_scale), row(out_norm),
                 glu_w.astype(BF16), row(glu_b), w_out.astype(BF16)))


def _hybrid_layer(x, layer, mod, params, tabs, tabs_t, cmp_tabs):
    x = _ffn(x, layer, mod, *params["ffn1"], 0)
    u_pool, q_t, ka, kb, va_t, vb_t, kc, vc, gates_t, u_s5 = _inproj(
        x, layer, mod, *params["inproj"][:3], tabs, tabs_t, *params["inproj"][3:])
    k_cmp, v_cmp_t = _compress(kc, vc, layer, *params["compress"], cmp_tabs)
    o_nsa = _nsa_attention(q_t, k_cmp, v_cmp_t, ka, kb, va_t, vb_t, gates_t)
    y_s5 = _s5_scan(u_s5, layer, params["s5"])
    x = _outproj(x, layer, mod, u_pool, o_nsa, y_s5, *params["outproj"])
    return _ffn(x, layer, mod, *params["ffn2"], 6)


def kernel(x, c, ada_w, ada_b, norm_ffn1, ffn1_w_in, ffn1_w_out, norm_mix, w_in, w_out, out_norm, pool_w, pool_b, pool_scale, q_norm, k_norm, cmp_pe, cmp_k_w1, cmp_k_w2, cmp_v_w1, cmp_v_w2, s5_lam_re, s5_lam_im, s5_log_dt, s5_b_re, s5_b_im, s5_c_re, s5_c_im, s5_d, glu_w, glu_b, norm_ffn2, ffn2_w_in, ffn2_w_out):
    seq = x.shape[1]
    assert seq % (KEY_TILE * 4) == 0 and seq >= WINDOW + Q_BLOCK
    mod = _modulation(c, ada_w, ada_b)
    params = _prepare_parameters(norm_ffn1, ffn1_w_in, ffn1_w_out, norm_mix, w_in, w_out, out_norm, pool_w, pool_b,
                                 pool_scale, q_norm, k_norm, cmp_pe, cmp_k_w1, cmp_k_w2, cmp_v_w1, cmp_v_w2,
                                 s5_lam_re, s5_lam_im, s5_log_dt, s5_b_re, s5_b_im, s5_c_re, s5_c_im, s5_d, glu_w,
                                 glu_b, norm_ffn2, ffn2_w_in, ffn2_w_out)
    tabs = _rope_tables(jnp.arange(seq, dtype=F32))
    tabs_t = _rope_tables_t(jnp.arange(seq, dtype=F32))
    cmp_tabs = _rope_tables((jnp.arange(seq // CMP_STRIDE) * CMP_STRIDE + CMP_LEN - 1).astype(F32))
    for layer in range(ada_w.shape[0]):
        x = _hybrid_layer(x, layer, mod, params, tabs, tabs_t, cmp_tabs)
    return x
```

```python
import functools
import math

import jax
import jax.numpy as jnp
import numpy as np
from jax import lax
from jax.experimental import pallas as pl
from jax.experimental.pallas import tpu as pltpu

F32 = jnp.float32
BF16 = jnp.bfloat16
HIGHEST = lax.Precision.HIGHEST

LANES = 128
SUBLANES = 8
VMEM_LIMIT_BYTES = 56 * 1024 * 1024

D_MODEL = 1024
D_FF = 2816
N_MOD = 9
POOL_WIDTH = 256
POOL_GC = 64
POOL_WINDOWS = (2, 4, 8, 16)
POOL_HALO = 16
HEAD_DIM = 64
NSA_WIDTH = 512
NSA_HEADS = 8
NSA_KV_HEADS = 2
NSA_REP = 4
N_BRANCH = 3
S5_WIDTH = 256
S5_H = 16
S5_GROUPS = 16
S5_P = 64
CMP_STRIDE = 16
CMP_LEN = 32
CMP_HIDDEN = 128
SLC_BLOCK = 64
SLC_SHIFT = 6
N_SELECT = 16
N_FORCED = 3
WINDOW = 512
Q_BLOCK = 256
ROT_DIM = 16
ROPE_THETA = 500000.0
EPS = 1e-6
Q_SCALE = HEAD_DIM ** -0.5 * math.log2(math.e)
NEG_INF = -1e30
MASK_VALUE = -(2.0 ** 100)

OFF_POOL = 0
OFF_KA = OFF_POOL + POOL_WIDTH
OFF_KC = OFF_KA + NSA_KV_HEADS * LANES
OFF_VC = OFF_KC + LANES
OFF_S5 = OFF_VC + LANES
N_COLS = OFF_S5 + S5_WIDTH
GATE_ROWS = 16
ROW_Q = 0
ROW_V = ROW_Q + NSA_HEADS * HEAD_DIM
ROW_GATE = ROW_V + NSA_KV_HEADS * LANES
N_ROWS_T = ROW_GATE + NSA_KV_HEADS * GATE_ROWS

S5_CHUNK = 8
S5_FOLD = S5_CHUNK * S5_WIDTH
S5_STATE = S5_GROUPS * S5_P
S5_HALVES = S5_WIDTH // LANES
S5_HALF_FOLD = S5_FOLD // S5_HALVES
KEY_TILE = 512
V_ROWS = HEAD_DIM + 16


def _dot(a, b):
    return jnp.dot(a, b, preferred_element_type=F32)


def _dot_nt(a, b):
    return lax.dot_general(a, b, (((1,), (1,)), ((), ())), preferred_element_type=F32)


def _sigmoid(x):
    return 1.0 / (1.0 + jnp.exp(-x))


def _gelu_tanh(x):
    return 0.5 * x * (1.0 + jnp.tanh(math.sqrt(2.0 / math.pi) * (x + 0.044715 * (x * x * x))))


def _params(sem):
    return pltpu.CompilerParams(dimension_semantics=sem, vmem_limit_bytes=VMEM_LIMIT_BYTES)


def _const_spec(shape):
    nd = len(shape)
    return pl.BlockSpec(shape, lambda *_: (0,) * nd, pipeline_mode=pl.Buffered(1))


def _layer_spec(shape, layer):
    nd = len(shape)
    return pl.BlockSpec((None,) + shape, lambda *_: (layer,) + (0,) * nd, pipeline_mode=pl.Buffered(1))


def _mod_kernel(ct_ref, w_ref, b_ref, o_ref, *, batch):
    w = w_ref[...]
    rows = []
    for b in range(batch):
        col = ct_ref[:, b:b + 1]
        rows.append(jnp.sum(w * (col * _sigmoid(col)), axis=0, keepdims=True) + b_ref[...])
    o_ref[...] = jnp.concatenate(rows + [jnp.zeros((SUBLANES - batch, w.shape[1]), F32)], axis=0)


def _modulation(c, ada_w, ada_b):
    n_layers, d, n = ada_w.shape
    b = c.shape[0]
    tn = 1152
    out = pl.pallas_call(
        functools.partial(_mod_kernel, batch=b),
        grid=(n_layers, n // tn),
        in_specs=[pl.BlockSpec((d, b), lambda l, j: (0, 0)),
                  pl.BlockSpec((None, d, tn), lambda l, j: (l, 0, j)),
                  pl.BlockSpec((None, 1, tn), lambda l, j: (l, 0, j))],
        out_specs=pl.BlockSpec((None, SUBLANES, tn), lambda l, j: (l, 0, j)),
        out_shape=jax.ShapeDtypeStruct((n_layers, SUBLANES, n), F32),
        compiler_params=_params(("arbitrary", "arbitrary")),
        name="adaln_mod",
    )(c.T, ada_w, ada_b.reshape(n_layers, 1, n))
    return out[:, :b].reshape(n_layers, b, N_MOD, d)


def _norm_modulate(x, gain, mod_ref, first_row):
    ms = jnp.mean(x * x, axis=-1, keepdims=True)
    y = x * lax.rsqrt(ms + EPS) * gain
    return y * (1.0 + mod_ref[first_row + 1:first_row + 2, :]) + mod_ref[first_row:first_row + 1, :]


def _ffn_kernel(x_ref, mod_ref, g_ref, win_ref, wout_ref, o_ref, *, first_row):
    x = x_ref[...]
    h = _norm_modulate(x, g_ref[...], mod_ref, first_row).astype(BF16)
    gu = _dot(h, win_ref[...])
    gate, up = gu[:, :D_FF], gu[:, D_FF:]
    a = (gate * _sigmoid(gate) * up).astype(BF16)
    y = _dot(a, wout_ref[...])
    o_ref[...] = x + 0.5 * mod_ref[first_row + 2:first_row + 3, :] * y


def _mod_spec(layer):
    return pl.BlockSpec((None, None, N_MOD, D_MODEL), lambda i, *_: (layer, i, 0, 0))


def _ffn(x, layer, mod, gain, w_in, w_out, first_row, tm=512):
    b, s, d = x.shape
    return pl.pallas_call(
        functools.partial(_ffn_kernel, first_row=first_row),
        grid=(b, s // tm),
        in_specs=[pl.BlockSpec((None, tm, d), lambda i, j: (i, j, 0)),
                  _mod_spec(layer),
                  _layer_spec((1, d), layer),
                  _layer_spec((d, 2 * D_FF), layer),
                  _layer_spec((D_FF, d), layer)],
        out_specs=pl.BlockSpec((None, tm, d), lambda i, j: (i, j, 0)),
        out_shape=jax.ShapeDtypeStruct((b, s, d), F32),
        compiler_params=_params(("parallel", "parallel")),
        name="ffn_half_step",
    )(x, mod, gain, w_in, w_out)


def _rope(v, cos_t, sin_lo, sin_hi):
    return (v * cos_t + pltpu.roll(v, LANES - ROT_DIM // 2, 1) * sin_lo
            + pltpu.roll(v, ROT_DIM // 2, 1) * sin_hi)


def _inproj_kernel(x_ref, mod_ref, g_ref, w_ref, wt_ref, cos_ref, slo_ref, shi_ref, cost_ref, sint_ref, qg_ref, kg_ref,
                   pool_ref, q_ref, ka_ref, kb_ref, va_ref, vb_ref, kc_ref, vc_ref, gate_ref, s5_ref):
    h = _norm_modulate(x_ref[...], g_ref[...], mod_ref, 3).astype(BF16)
    u = _dot(h, w_ref[...])
    ut = _dot_nt(wt_ref[...], h)
    tm = u.shape[0]
    cos_t, sin_lo, sin_hi = cos_ref[...], slo_ref[...], shi_ref[...]
    lane = lax.broadcasted_iota(jnp.int32, (tm, LANES), 1)
    low_half = lane < HEAD_DIM
    pos = pl.program_id(1) * tm + lax.broadcasted_iota(jnp.int32, (tm, LANES), 0)
    block_one_hot = jnp.where((pos >> SLC_SHIFT) == lane, 1.0, 0.0).astype(BF16)

    pool_ref[...] = u[:, OFF_POOL:OFF_POOL + POOL_WIDTH]
    kc_ref[...] = u[:, OFF_KC:OFF_KC + LANES]
    vc_ref[...] = u[:, OFF_VC:OFF_VC + LANES]
    for half in range(S5_HALVES):
        s5_ref[half] = u[:, OFF_S5 + half * LANES:OFF_S5 + (half + 1) * LANES]
    for g in range(NSA_KV_HEADS):
        v = u[:, OFF_KA + g * LANES:OFF_KA + (g + 1) * LANES]
        sq = v * v
        ms_lo = jnp.sum(jnp.where(low_half, sq, 0.0), axis=-1, keepdims=True) * (1.0 / HEAD_DIM)
        ms_hi = jnp.sum(jnp.where(low_half, 0.0, sq), axis=-1, keepdims=True) * (1.0 / HEAD_DIM)
        r = jnp.where(low_half, lax.rsqrt(ms_lo + EPS), lax.rsqrt(ms_hi + EPS))
        kn = _rope(v * r * kg_ref[...], cos_t, sin_lo, sin_hi)
        ka_ref[g, :, 0:LANES] = kn.astype(BF16)
        ka_ref[g, :, LANES:2 * LANES] = block_one_hot
        kb_ref[g] = pltpu.roll(kn, HEAD_DIM, 1).astype(BF16)

    half_rot = ROT_DIM // 2
    cos8, sin8 = cost_ref[...], sint_ref[...]
    zero_rows = jnp.zeros((LANES - HEAD_DIM, tm), F32)
    for hd in range(NSA_HEADS):
        v = ut[ROW_Q + hd * HEAD_DIM:ROW_Q + (hd + 1) * HEAD_DIM, :]
        ms = jnp.sum(v * v, axis=0, keepdims=True) * (1.0 / HEAD_DIM)
        vn = v * lax.rsqrt(ms + EPS) * qg_ref[...]
        x1, x2 = vn[0:half_rot], vn[half_rot:ROT_DIM]
        roped = jnp.concatenate([x1 * cos8 - x2 * sin8, x2 * cos8 + x1 * sin8, vn[ROT_DIM:]], axis=0)
        q_ref[hd] = jnp.concatenate([roped * Q_SCALE, zero_rows], axis=0).astype(BF16)
    ones = jnp.ones((HEAD_DIM, tm), F32)
    for g in range(NSA_KV_HEADS):
        vt = ut[ROW_V + g * LANES:ROW_V + (g + 1) * LANES, :]
        va = jnp.concatenate([vt[0:HEAD_DIM], ones], axis=0).astype(BF16)
        vb = jnp.concatenate([vt[HEAD_DIM:], ones], axis=0).astype(BF16)
        for blk in range(tm // LANES):
            va_ref[g, blk] = va[:, blk * LANES:(blk + 1) * LANES]
            vb_ref[g, blk] = vb[:, blk * LANES:(blk + 1) * LANES]
        gate_ref[g] = _sigmoid(ut[ROW_GATE + g * GATE_ROWS:ROW_GATE + (g + 1) * GATE_ROWS, :])


def _inproj(x, layer, mod, gain, w_row, w_t, rope_tabs, rope_tabs_t, q_gain, k_gain, tm=1024):
    b, s, d = x.shape
    g = NSA_KV_HEADS
    tok = lambda width: pl.BlockSpec((None, tm, width), lambda i, j: (i, j, 0))
    grp = lambda n, width=LANES: pl.BlockSpec((None, n, tm, width), lambda i, j: (i, 0, j, 0))
    lanes_tok = lambda n, rows: pl.BlockSpec((None, n, rows, tm), lambda i, j: (i, 0, 0, j))
    v_blocks = pl.BlockSpec((None, g, tm // LANES, LANES, LANES), lambda i, j: (i, 0, j, 0, 0))
    tab = pl.BlockSpec((tm, LANES), lambda i, j: (j, 0))
    tab_t = pl.BlockSpec((ROT_DIM // 2, tm), lambda i, j: (0, j))
    sds = jax.ShapeDtypeStruct
    return pl.pallas_call(
        _inproj_kernel,
        grid=(b, s // tm),
        in_specs=[tok(d),
                  _mod_spec(layer),
                  _layer_spec((1, d), layer),
                  _layer_spec((d, N_COLS), layer), _layer_spec((N_ROWS_T, d), layer),
                  tab, tab, tab, tab_t, tab_t,
                  _layer_spec((HEAD_DIM, 1), layer), _layer_spec((1, LANES), layer)],
        out_specs=[tok(POOL_WIDTH), lanes_tok(NSA_HEADS, LANES), grp(g, 2 * LANES), grp(g), v_blocks, v_blocks,
                   tok(LANES), tok(LANES), lanes_tok(g, GATE_ROWS), grp(S5_HALVES)],
        out_shape=[sds((b, s, POOL_WIDTH), F32), sds((b, NSA_HEADS, LANES, s), BF16),
                   sds((b, g, s, 2 * LANES), BF16), sds((b, g, s, LANES), BF16),
                   sds((b, g, s // LANES, LANES, LANES), BF16), sds((b, g, s // LANES, LANES, LANES), BF16),
                   sds((b, s, LANES), F32), sds((b, s, LANES), F32),
                   sds((b, g, GATE_ROWS, s), F32), sds((b, S5_HALVES, s, LANES), F32)],
        compiler_params=_params(("parallel", "parallel")),
        name="mixer_in_proj",
    )(x, mod, gain, w_row, w_t, *rope_tabs, *rope_tabs_t, q_gain, k_gain)


def _fold_rows(ref, n):
    rows = ref.shape[0] // n
    return jnp.concatenate([ref[pl.ds(k, rows, stride=n), :] for k in range(n)], axis=1)


def _unfold_rows(ref, value, n):
    rows = ref.shape[0] // n
    for k in range(n):
        ref[pl.ds(k, rows, stride=n), :] = value[:, k * LANES:(k + 1) * LANES]


def _compress_kernel(kc_ref, vc_ref, w1k_ref, w1v_ref, w1k_raw_ref, w1v_raw_ref, pe_ref, w2k_ref, w2vt_ref,
                     kg_ref, cos_ref, slo_ref, shi_ref, ko_ref, vo_ref, *, n_cmp):
    ncp = kc_ref.shape[0] // CMP_STRIDE
    for src_ref, w1_ref, raw_ref, pe_row, is_key in ((kc_ref, w1k_ref, w1k_raw_ref, 0, True),
                                                    (vc_ref, w1v_ref, w1v_raw_ref, 1, False)):
        chunks = _fold_rows(src_ref, CMP_STRIDE).astype(BF16)
        pe_term = jnp.sum(raw_ref[...] * pe_ref[:, pe_row:pe_row + 1], axis=0, keepdims=True)
        for g in range(NSA_KV_HEADS):
            a = _dot(chunks, w1_ref[g])
            pre = a[:, :CMP_HIDDEN] + pltpu.roll(a[:, CMP_HIDDEN:], ncp - 1, 0) + pe_term
            hidden = _gelu_tanh(pre).astype(BF16)
            if is_key:
                out = _dot(hidden, w2k_ref[...])
                ms = jnp.sum(out * out, axis=-1, keepdims=True) * (1.0 / HEAD_DIM)
                out = _rope(out * lax.rsqrt(ms + EPS) * kg_ref[...], cos_ref[...], slo_ref[...], shi_ref[...])
                real_row = lax.broadcasted_iota(jnp.int32, (ncp, LANES), 0) < n_cmp
                ko_ref[g] = jnp.where(real_row, out, 0.0).astype(BF16)
            else:
                out_t = _dot_nt(w2vt_ref[...], hidden)
                real_col = lax.broadcasted_iota(jnp.int32, (LANES, ncp), 1) < n_cmp
                value_row = lax.broadcasted_iota(jnp.int32, (LANES, ncp), 0) < HEAD_DIM
                vo_ref[g] = jnp.where(value_row, jnp.where(real_col, out_t, 0.0), 1.0).astype(BF16)


def _compress(kc, vc, layer, w1k, w1v, w1k_raw, w1v_raw, pe, w2k, w2v_t, k_gain, cmp_tabs):
    b, s, _ = kc.shape
    ncp = s // CMP_STRIDE
    fold = CMP_STRIDE * LANES
    src = pl.BlockSpec((None, s, LANES), lambda i: (i, 0, 0))
    raw = _layer_spec((CMP_LEN * HEAD_DIM, CMP_HIDDEN), layer)
    w1 = _layer_spec((NSA_KV_HEADS, fold, 2 * CMP_HIDDEN), layer)
    w2 = _layer_spec((CMP_HIDDEN, LANES), layer)
    tab = _const_spec((ncp, LANES))
    return pl.pallas_call(
        functools.partial(_compress_kernel, n_cmp=ncp - 1),
        grid=(b,),
        in_specs=[src, src, w1, w1, raw, raw, _layer_spec((CMP_LEN * HEAD_DIM, 2), layer), w2, w2,
                  _layer_spec((1, LANES), layer), tab, tab, tab],
        out_specs=[pl.BlockSpec((None, NSA_KV_HEADS, ncp, LANES), lambda i: (i, 0, 0, 0)),
                   pl.BlockSpec((None, NSA_KV_HEADS, LANES, ncp), lambda i: (i, 0, 0, 0))],
        out_shape=[jax.ShapeDtypeStruct((b, NSA_KV_HEADS, ncp, LANES), BF16),
                   jax.ShapeDtypeStruct((b, NSA_KV_HEADS, LANES, ncp), BF16)],
        compiler_params=_params(("parallel",)),
        name="nsa_compress",
    )(kc, vc, w1k, w1v, w1k_raw, w1v_raw, pe, w2k, w2v_t, k_gain, *cmp_tabs)


def _selection_bias(imp_t, qb):
    nsb, nq = imp_t.shape
    j = lax.broadcasted_iota(jnp.int32, (nsb, nq), 0)
    t = qb * Q_BLOCK + lax.broadcasted_iota(jnp.int32, (nsb, nq), 1)
    cur = t >> SLC_SHIFT
    valid = j * SLC_BLOCK <= t
    forced = (j == 0) | (j == cur) | (j == cur - 1)
    j_f = j.astype(F32)
    start = jnp.where(valid & jnp.logical_not(forced), imp_t, -1.0)
    vals = start
    for _ in range(N_SELECT - N_FORCED):
        m = jnp.max(vals, axis=0, keepdims=True)
        idx = jnp.min(jnp.where(vals == m, j_f, float(nsb)), axis=0, keepdims=True)
        vals = jnp.where(j_f == idx, -2.0, vals)
    return jnp.where((forced & valid) | (vals != start), 0.0, MASK_VALUE)


def _attn_kernel(q_ref, kc_ref, vct_ref, ka_ref, kb_ref, va_ref, vb_ref, gate_ref, ovl_ref, o_ref,
                 sa_ref, sb_ref, acc_ref, *, seq):
    qb = pl.program_id(2)
    cols = NSA_REP * Q_BLOCK
    ncp = seq // CMP_STRIDE
    q_t = jnp.concatenate([q_ref[r] for r in range(NSA_REP)], axis=1)
    t_col = qb * Q_BLOCK + lax.broadcasted_iota(jnp.int32, (1, cols), 1) % Q_BLOCK
    t_q = qb * Q_BLOCK + lax.broadcasted_iota(jnp.int32, (1, Q_BLOCK), 1)

    def all_heads(per_query):
        return jnp.concatenate([per_query] * NSA_REP, axis=1)

    cmp_end = lax.broadcasted_iota(jnp.int32, (ncp, Q_BLOCK), 0) * CMP_STRIDE + (CMP_LEN - 1)
    s = _dot(kc_ref[...], q_t) + all_heads(jnp.where(cmp_end <= t_q, 0.0, NEG_INF))
    e = jnp.exp2(s - jnp.max(s, axis=0, keepdims=True)).astype(BF16)
    oc = _dot(jnp.concatenate([vct_ref[0:V_ROWS, :], ovl_ref[...]], axis=0), e)
    oc = oc * jnp.where(t_col >= CMP_LEN - 1, 1.0 / oc[HEAD_DIM:HEAD_DIM + 1, :], 0.0)
    o_cmp, imp = oc[0:V_ROWS], oc[V_ROWS:V_ROWS + LANES]
    imp_t = (imp[:, 0:Q_BLOCK] + imp[:, Q_BLOCK:2 * Q_BLOCK]
             + imp[:, 2 * Q_BLOCK:3 * Q_BLOCK] + imp[:, 3 * Q_BLOCK:4 * Q_BLOCK])
    bias = _selection_bias(imp_t, qb).astype(BF16)

    q_aug = jnp.concatenate([q_t, jnp.concatenate([bias] * NSA_REP, axis=1)], axis=0)

    span = WINDOW + Q_BLOCK
    start = pl.multiple_of(jnp.maximum(qb * Q_BLOCK - WINDOW, 0), LANES)
    start_blk = start // LANES
    kpos = start + lax.broadcasted_iota(jnp.int32, (span, Q_BLOCK), 0)
    in_window = (kpos <= t_q) & (kpos > t_q - WINDOW)
    s = _dot(kb_ref[pl.ds(start, span), :], q_t) + all_heads(jnp.where(in_window, 0.0, NEG_INF))
    e = jnp.exp2(s - jnp.max(s, axis=0, keepdims=True)).astype(BF16)
    acc_win = _dot(jnp.concatenate([vb_ref[start_blk + i, 0:V_ROWS, :] for i in range(span // LANES)], axis=1), e)
    o_win = acc_win * (1.0 / acc_win[HEAD_DIM:HEAD_DIM + 1, :])

    blocks_per_tile = KEY_TILE // LANES

    def score_tile(kt, s_ref):
        k0 = pl.multiple_of(kt * KEY_TILE, KEY_TILE)
        s_ref[...] = _dot(ka_ref[pl.ds(k0, KEY_TILE), :], q_aug)

    def absorb_tile(kt, s_ref, carry, causal):
        m_i, acc = carry
        blk0 = kt * blocks_per_tile
        s = s_ref[...]
        if causal:
            kpos = kt * KEY_TILE + lax.broadcasted_iota(jnp.int32, (KEY_TILE, Q_BLOCK), 0)
            s = s + all_heads(jnp.where(kpos <= t_q, 0.0, MASK_VALUE))
        m_new = jnp.maximum(m_i, jnp.max(s, axis=0, keepdims=True))
        p = jnp.exp2(s - m_new).astype(BF16)
        v_t = jnp.concatenate([va_ref[blk0 + i, 0:V_ROWS, :] for i in range(blocks_per_tile)], axis=1)
        return m_new, jnp.exp2(m_i - m_new) * acc + _dot(v_t, p)

    def pair_trip(j, carry):
        score_tile(2 * j + 1, sb_ref)
        carry = absorb_tile(2 * j, sa_ref, carry, causal=False)
        score_tile(2 * j + 2, sa_ref)
        return absorb_tile(2 * j + 1, sb_ref, carry, causal=False)

    def double_trip(j, carry):
        return pair_trip(2 * j + 1, pair_trip(2 * j, carry))

    last_pair = qb // (2 * KEY_TILE // Q_BLOCK)
    score_tile(0, sa_ref)
    carry = (jnp.full((1, cols), NEG_INF, F32), jnp.zeros((V_ROWS, cols), F32))
    carry = lax.fori_loop(0, last_pair // 2, double_trip, carry)
    carry = lax.fori_loop(2 * (last_pair // 2), last_pair, pair_trip, carry)
    second_tile_live = (qb // (KEY_TILE // Q_BLOCK)) % 2 == 1

    @pl.when(second_tile_live)
    def _():
        score_tile(2 * last_pair + 1, sb_ref)
        both = absorb_tile(2 * last_pair, sa_ref, carry, causal=False)
        acc_ref[...] = absorb_tile(2 * last_pair + 1, sb_ref, both, causal=True)[1]

    @pl.when(jnp.logical_not(second_tile_live))
    def _():
        acc_ref[...] = absorb_tile(2 * last_pair, sa_ref, carry, causal=True)[1]

    acc_slc = acc_ref[...]
    o_slc = acc_slc * (1.0 / acc_slc[HEAD_DIM:HEAD_DIM + 1, :])

    gates = gate_ref[...]
    heads = []
    for r in range(NSA_REP):
        cs = slice(r * Q_BLOCK, (r + 1) * Q_BLOCK)
        c = r * N_BRANCH
        heads.append(gates[c:c + 1, :] * o_cmp[0:HEAD_DIM, cs] + gates[c + 1:c + 2, :] * o_slc[0:HEAD_DIM, cs]
                     + gates[c + 2:c + 3, :] * o_win[0:HEAD_DIM, cs])
    o_ref[...] = jnp.concatenate(heads, axis=0).T


def _overlap_matrix(seq):
    ncp, nsb = seq // CMP_STRIDE, seq // SLC_BLOCK
    c_start = np.arange(ncp)[None, :] * CMP_STRIDE
    s_start = np.arange(nsb)[:, None] * SLC_BLOCK
    ovl = np.clip(np.minimum(c_start + CMP_LEN, s_start + SLC_BLOCK) - np.maximum(c_start, s_start), 0, None)
    ovl = ovl.astype(np.float32) / CMP_LEN
    ovl[:, ncp - 1] = 0.0
    return jnp.asarray(np.pad(ovl, ((0, LANES - nsb), (0, 0))))


def _nsa_attention(q_t, k_cmp, v_cmp_t, ka, kb, va_t, vb_t, gates_t):
    b, _, _, s = q_t.shape
    g = NSA_KV_HEADS
    ncp = s // CMP_STRIDE
    assert s // SLC_BLOCK <= LANES
    per_group = lambda *shape: pl.BlockSpec((None, None) + shape, lambda i, j, k: (i, j) + (0,) * len(shape))
    return pl.pallas_call(
        functools.partial(_attn_kernel, seq=s),
        grid=(b, g, s // Q_BLOCK),
        in_specs=[pl.BlockSpec((None, NSA_REP, LANES, Q_BLOCK), lambda i, j, k: (i, j, 0, k)),
                  per_group(ncp, LANES), per_group(LANES, ncp), per_group(s, 2 * LANES), per_group(s, LANES),
                  per_group(s // LANES, LANES, LANES), per_group(s // LANES, LANES, LANES),
                  pl.BlockSpec((None, None, GATE_ROWS, Q_BLOCK), lambda i, j, k: (i, j, 0, k)),
                  _const_spec((LANES, ncp))],
        out_specs=pl.BlockSpec((None, Q_BLOCK, NSA_REP * HEAD_DIM), lambda i, j, k: (i, k, j)),
        out_shape=jax.ShapeDtypeStruct((b, s, NSA_WIDTH), F32),
        scratch_shapes=[pltpu.VMEM((KEY_TILE, NSA_REP * Q_BLOCK), F32)] * 2
        + [pltpu.VMEM((V_ROWS, NSA_REP * Q_BLOCK), F32)],
        compiler_params=_params(("parallel", "parallel", "arbitrary")),
        name="nsa_attention",
    )(q_t, k_cmp, v_cmp_t, ka, kb, va_t, vb_t, gates_t, _overlap_matrix(s).astype(BF16))


def _s5_kernel(u_ref, mt_ref, bc_ref, cc_ref, a1_ref, a2_ref, d_ref, y_ref, h_ref, g_scr, hp_scr):
    @pl.when(pl.program_id(1) == 0)
    def _():
        h_ref[...] = jnp.zeros_like(h_ref)

    halves = range(S5_HALVES)
    u = [_fold_rows(u_ref.at[a], S5_CHUNK) for a in halves]
    ub = [x.astype(BF16) for x in u]
    y_local = [_dot(ub[a], mt_ref[a]) for a in halves]
    for a in halves:
        g_scr[:, a * S5_HALF_FOLD:(a + 1) * S5_HALF_FOLD] = _dot(ub[a], bc_ref[a])
    a1, a2 = a1_ref[...], a2_ref[...]
    n_state = S5_STATE // S5_HALVES

    def swap_re_im(h):
        parts = [h[:, k * n_state:(k + 1) * n_state] for k in range(2 * S5_HALVES)]
        return jnp.concatenate([parts[k ^ 1] for k in range(2 * S5_HALVES)], axis=1)

    def step(i, h):
        hp_scr[pl.ds(i, 1), :] = h
        return a1 * h + a2 * swap_re_im(h) + g_scr[pl.ds(i, 1), :]

    h_ref[...] = lax.fori_loop(0, g_scr.shape[0], step, h_ref[...], unroll=8)
    for a in halves:
        carried = _dot(hp_scr[:, a * S5_HALF_FOLD:(a + 1) * S5_HALF_FOLD].astype(BF16), cc_ref[a])
        _unfold_rows(y_ref.at[a], y_local[a] + carried + u[a] * d_ref[a:a + 1, :], S5_CHUNK)


def _s5_scan(u, layer, mats):
    b, _, s, _ = u.shape
    rows = s // S5_CHUNK
    tc = min(128, rows)
    tile = pl.BlockSpec((None, S5_HALVES, tc * S5_CHUNK, LANES), lambda i, j: (i, 0, j, 0))
    mat = _layer_spec((S5_HALVES, S5_HALF_FOLD, S5_HALF_FOLD), layer)
    row = _layer_spec((1, S5_HALVES * S5_HALF_FOLD), layer)
    return pl.pallas_call(
        _s5_kernel,
        grid=(b, rows // tc),
        in_specs=[tile, mat, mat, mat, row, row, _layer_spec((S5_HALVES, S5_HALF_FOLD), layer)],
        out_specs=tile,
        out_shape=jax.ShapeDtypeStruct((b, S5_HALVES, s, LANES), F32),
        scratch_shapes=[pltpu.VMEM((1, S5_HALVES * S5_HALF_FOLD), F32),
                        pltpu.VMEM((tc, S5_HALVES * S5_HALF_FOLD), F32),
                        pltpu.VMEM((tc, S5_HALVES * S5_HALF_FOLD), F32)],
        compiler_params=_params(("parallel", "arbitrary")),
        name="s5_scan",
    )(u, *mats)


def _s5_matrices(lam_re, lam_im, log_dt, b_re, b_im, c_re, c_im, d_skip):
    t0, ng, nh, npm = S5_CHUNK, S5_GROUPS, S5_H, S5_P
    gh = ng // S5_HALVES
    ein = functools.partial(jnp.einsum, precision=HIGHEST)
    lam = lax.complex(lam_re, lam_im)
    step = jnp.exp(log_dt)[:, None]
    lam_bar = jnp.exp(lam * step)
    b_bar = lax.complex(b_re, b_im) * ((lam_bar - 1.0) / lam)[..., None]
    c_mat = lax.complex(c_re, c_im)
    k = jnp.arange(t0 + 1, dtype=F32)[:, None, None]
    pw = jnp.exp((lam * step)[None] * k)

    def same_group(rows_per_group, cols_per_group):
        r = np.arange(gh * rows_per_group)[:, None] // rows_per_group
        c = np.arange(gh * cols_per_group)[None, :] // cols_per_group
        return jnp.asarray((r == c).astype(np.float32))

    def per_group_blocks(x, rows_per_group, cols_per_group):
        return jnp.tile(x, (1,) * (x.ndim - 1) + (gh,)) * same_group(rows_per_group, cols_per_group)

    kern = jnp.real(ein('ghp,kgp,gpq->kghq', c_mat, pw[:t0], b_bar))
    d_k = per_group_blocks(kern.transpose(0, 1, 3, 2).reshape(t0, S5_HALVES, gh * nh, nh), nh, nh)
    d_k = jnp.concatenate([d_k, jnp.zeros_like(d_k[:1])], axis=0)
    lag = np.arange(t0)[None, :] - np.arange(t0)[:, None]
    mt = d_k[np.where(lag >= 0, lag, t0)]
    mt = mt.transpose(2, 0, 3, 1, 4).reshape(S5_HALVES, S5_HALF_FOLD, S5_HALF_FOLD)
    b_j = (pw[t0 - 1 - np.arange(t0)][..., None] * b_bar[None]).transpose(0, 1, 3, 2)
    b_j = b_j.reshape(t0, S5_HALVES, gh * nh, npm)
    bc = jnp.concatenate([per_group_blocks(jnp.real(b_j), nh, npm), per_group_blocks(jnp.imag(b_j), nh, npm)],
                         axis=-1)
    bc = bc.transpose(1, 0, 2, 3).reshape(S5_HALVES, S5_HALF_FOLD, S5_HALF_FOLD)
    c_i = (c_mat[None] * pw[1:t0 + 1][:, :, None, :]).transpose(1, 3, 0, 2)
    c_i = c_i.reshape(S5_HALVES, gh * npm, t0, nh)
    mask = same_group(npm, nh)[:, None, :]
    cc = jnp.concatenate([jnp.tile(jnp.real(c_i), (1, 1, 1, gh)) * mask,
                          jnp.tile(-jnp.imag(c_i), (1, 1, 1, gh)) * mask], axis=1)
    cc = cc.reshape(S5_HALVES, S5_HALF_FOLD, S5_HALF_FOLD)
    a_chunk = pw[t0].reshape(S5_HALVES, gh * npm)
    a1 = jnp.concatenate([jnp.real(a_chunk), jnp.real(a_chunk)], axis=1).reshape(1, -1)
    a2 = jnp.concatenate([-jnp.imag(a_chunk), jnp.imag(a_chunk)], axis=1).reshape(1, -1)
    d_vec = jnp.tile(d_skip.reshape(S5_HALVES, gh * nh), (1, t0))
    return mt.astype(BF16), bc.astype(BF16), cc.astype(BF16), a1, a2, d_vec


def _rms_gain(y, gain):
    return y * lax.rsqrt(jnp.mean(y * y, axis=-1, keepdims=True) + EPS) * gain


def _outproj_kernel(x_ref, mod_ref, pool_ref, halo_ref, nsa_ref, s5_ref, pw_ref, pb_ref, ps_ref, on_ref,
                    gw_ref, gb_ref, wo_ref, o_ref, buf):
    j = pl.program_id(1)
    tm = x_ref.shape[0]
    v = pool_ref[...]
    buf[0:POOL_HALO, :] = jnp.where(j > 0, halo_ref[...], 0.0)
    buf[POOL_HALO:POOL_HALO + tm, :] = v
    second_group = (lax.broadcasted_iota(jnp.int32, (tm, LANES), 1) >> SLC_SHIFT) == 1
    t1 = (j * tm + 1 + lax.broadcasted_iota(jnp.int32, (tm, 1), 0)).astype(F32)
    pooled = []
    for half in range(POOL_WIDTH // LANES):
        cols = slice(half * LANES, (half + 1) * LANES)
        v_half = v[:, cols]
        run, k, means = v_half, 1, []
        for w in POOL_WINDOWS[2 * half:2 * half + 2]:
            while k < w:
                run = run + buf[POOL_HALO - k:POOL_HALO - k + tm, cols]
                k += 1
            means.append(run / jnp.minimum(t1, float(w)))
        pooled.append(jnp.where(second_group, means[1], means[0]) - v_half)
    y_pool = (_dot(jnp.concatenate(pooled, axis=1).astype(BF16), pw_ref[...]) + pb_ref[...]) * ps_ref[...]

    y = _gelu_tanh(jnp.concatenate([s5_ref[half] for half in range(S5_HALVES)], axis=-1))
    y_s5 = y * _sigmoid(_dot(y.astype(BF16), gw_ref[...]) + gb_ref[...])

    cat = jnp.concatenate(
        [_rms_gain(y_pool, on_ref[:, 0:POOL_WIDTH]),
         _rms_gain(nsa_ref[...], on_ref[:, POOL_WIDTH:POOL_WIDTH + NSA_WIDTH]),
         _rms_gain(y_s5, on_ref[:, POOL_WIDTH + NSA_WIDTH:])], axis=-1).astype(BF16)
    o_ref[...] = x_ref[...] + mod_ref[5:6, :] * _dot(cat, wo_ref[...])


def _outproj(x, layer, mod, u_pool, o_nsa, y_s5, pool_w_bd, pool_b, pool_scale, out_norm, glu_w, glu_b, w_out,
             tm=1024):
    b, s, d = x.shape
    tok = lambda width: pl.BlockSpec((None, tm, width), lambda i, j: (i, j, 0))
    halo_blocks = tm // POOL_HALO
    return pl.pallas_call(
        _outproj_kernel,
        grid=(b, s // tm),
        in_specs=[tok(d),
                  _mod_spec(layer),
                  tok(POOL_WIDTH),
                  pl.BlockSpec((None, POOL_HALO, POOL_WIDTH),
                               lambda i, j: (i, jnp.maximum(j * halo_blocks - 1, 0), 0)),
                  tok(NSA_WIDTH),
                  pl.BlockSpec((None, S5_HALVES, tm, LANES), lambda i, j: (i, 0, j, 0)),
                  _layer_spec((POOL_WIDTH, POOL_WIDTH), layer), _layer_spec((1, POOL_WIDTH), layer),
                  _layer_spec((1, POOL_WIDTH), layer), _layer_spec((1, d), layer),
                  _layer_spec((S5_WIDTH, S5_WIDTH), layer), _layer_spec((1, S5_WIDTH), layer),
                  _layer_spec((d, d), layer)],
        out_specs=tok(d),
        out_shape=jax.ShapeDtypeStruct((b, s, d), F32),
        scratch_shapes=[pltpu.VMEM((POOL_HALO + tm, POOL_WIDTH), F32)],
        compiler_params=_params(("parallel", "arbitrary")),
        name="mixer_out_proj",
    )(x, mod, u_pool, u_pool, o_nsa, y_s5, pool_w_bd, pool_b, pool_scale, out_norm, glu_w, glu_b, w_out)


def _rope_tables(pos):
    half = ROT_DIM // 2
    inv_freq = jnp.exp(-math.log(ROPE_THETA) * jnp.arange(half, dtype=F32) * (2.0 / ROT_DIM))
    ang = pos[:, None] * inv_freq[None, :]
    cos, sin = jnp.cos(ang), jnp.sin(ang)
    n = pos.shape[0]
    rest = HEAD_DIM - ROT_DIM
    cos_t = jnp.concatenate([cos, cos, jnp.ones((n, rest), F32)], axis=1)
    sin_lo = jnp.concatenate([-sin, jnp.zeros((n, half + rest), F32)], axis=1)
    sin_hi = jnp.concatenate([jnp.zeros((n, half), F32), sin, jnp.zeros((n, rest), F32)], axis=1)
    return tuple(jnp.tile(t, (1, 2)) for t in (cos_t, sin_lo, sin_hi))


def _pad_lanes(v):
    return jnp.pad(v, [(0, 0)] * (v.ndim - 1) + [(0, LANES - v.shape[-1])])


def _rope_tables_t(pos):
    half = ROT_DIM // 2
    inv_freq = jnp.exp(-math.log(ROPE_THETA) * jnp.arange(half, dtype=F32) * (2.0 / ROT_DIM))
    ang = inv_freq[:, None] * pos[None, :]
    return jnp.cos(ang), jnp.sin(ang)


def _arrange_w_in(w):
    d = w.shape[0]
    o1, o2, o3 = POOL_WIDTH, POOL_WIDTH + NSA_WIDTH, POOL_WIDTH + NSA_WIDTH + 6 * LANES
    kv = w[:, o2:o3].reshape(d, 6, NSA_KV_HEADS, HEAD_DIM)
    ka = jnp.concatenate([kv[:, 2], kv[:, 4]], axis=-1).reshape(d, NSA_KV_HEADS * LANES)
    n_gate = NSA_REP * N_BRANCH
    w_row = jnp.concatenate([w[:, :o1], ka, kv[:, 0].reshape(d, LANES), kv[:, 1].reshape(d, LANES),
                             w[:, o3 + NSA_KV_HEADS * n_gate:]], axis=1)
    q_t = w[:, o1:o2].T
    v_t = jnp.concatenate([kv[:, 3], kv[:, 5]], axis=-1).reshape(d, NSA_KV_HEADS * LANES).T
    gate_t = w[:, o3:o3 + NSA_KV_HEADS * n_gate].T.reshape(NSA_KV_HEADS, n_gate, d)
    gate_t = jnp.pad(gate_t, ((0, 0), (0, GATE_ROWS - n_gate), (0, 0))).reshape(NSA_KV_HEADS * GATE_ROWS, d)
    return w_row.astype(BF16), jnp.concatenate([q_t, v_t, gate_t], axis=0).astype(BF16)


def _expand_cmp_w1(w1):
    halves = w1.reshape(2, CMP_STRIDE, HEAD_DIM, CMP_HIDDEN)
    both = jnp.concatenate([halves[0], halves[1]], axis=-1)
    out = []
    for g in range(NSA_KV_HEADS):
        z = jnp.zeros((CMP_STRIDE, NSA_KV_HEADS, HEAD_DIM, 2 * CMP_HIDDEN), F32).at[:, g].set(both)
        out.append(z.reshape(CMP_STRIDE * LANES, 2 * CMP_HIDDEN))
    return jnp.stack(out).astype(BF16)


def _block_diag(w):
    g, c, _ = w.shape
    return jnp.einsum('gcd,gf->gcfd', w, jnp.eye(g, dtype=w.dtype)).reshape(g * c, g * c)


def _prepare_parameters(norm_ffn1, ffn1_w_in, ffn1_w_out, norm_mix, w_in, w_out, out_norm, pool_w, pool_b, pool_scale,
                        q_norm, k_norm, cmp_pe, cmp_k_w1, cmp_k_w2, cmp_v_w1, cmp_v_w2, s5_lam_re, s5_lam_im,
                        s5_log_dt, s5_b_re, s5_b_im, s5_c_re, s5_c_im, s5_d, glu_w, glu_b, norm_ffn2, ffn2_w_in,
                        ffn2_w_out):
    n_layers = norm_mix.shape[0]
    row = lambda v: v.reshape(n_layers, 1, -1)
    w_row, w_t = jax.vmap(_arrange_w_in)(w_in)
    return dict(
        ffn1=(row(norm_ffn1), ffn1_w_in.astype(BF16), ffn1_w_out.astype(BF16)),
        ffn2=(row(norm_ffn2), ffn2_w_in.astype(BF16), ffn2_w_out.astype(BF16)),
        inproj=(row(norm_mix), w_row, w_t, q_norm.reshape(n_layers, HEAD_DIM, 1),
                jnp.concatenate([k_norm[:, 1], k_norm[:, 2]], axis=-1).reshape(n_layers, 1, LANES)),
        compress=(jax.vmap(_expand_cmp_w1)(cmp_k_w1), jax.vmap(_expand_cmp_w1)(cmp_v_w1), cmp_k_w1, cmp_v_w1,
                  jnp.swapaxes(cmp_pe.reshape(n_layers, 2, CMP_LEN * HEAD_DIM), 1, 2),
                  _pad_lanes(cmp_k_w2).astype(BF16),
                  jnp.swapaxes(_pad_lanes(cmp_v_w2), 1, 2).astype(BF16), row(_pad_lanes(k_norm[:, 0]))),
        s5=jax.vmap(_s5_matrices)(s5_lam_re, s5_lam_im, s5_log_dt, s5_b_re, s5_b_im, s5_c_re, s5_c_im, s5_d),
        outproj=(jax.vmap(_block_diag)(pool_w).astype(BF16), row(pool_b), row(pool_scale), row(out_norm),
                 glu_w.astype(BF16), row(glu_b), w_out.astype(BF16)))


def _hybrid_layer(x, layer, mod, params, tabs, tabs_t, cmp_tabs):
    x = _ffn(x, layer, mod, *params["ffn1"], 0)
    u_pool, q_t, ka, kb, va_t, vb_t, kc, vc, gates_t, u_s5 = _inproj(
        x, layer, mod, *params["inproj"][:3], tabs, tabs_t, *params["inproj"][3:])
    k_cmp, v_cmp_t = _compress(kc, vc, layer, *params["compress"], cmp_tabs)
    o_nsa = _nsa_attention(q_t, k_cmp, v_cmp_t, ka, kb, va_t, vb_t, gates_t)
    y_s5 = _s5_scan(u_s5, layer, params["s5"])
    x = _outproj(x, layer, mod, u_pool, o_nsa, y_s5, *params["outproj"])
    return _ffn(x, layer, mod, *params["ffn2"], 6)


def kernel(x, c, ada_w, ada_b, norm_ffn1, ffn1_w_in, ffn1_w_out, norm_mix, w_in, w_out, out_norm, pool_w, pool_b, pool_scale, q_norm, k_norm, cmp_pe, cmp_k_w1, cmp_k_w2, cmp_v_w1, cmp_v_w2, s5_lam_re, s5_lam_im, s5_log_dt, s5_b_re, s5_b_im, s5_c_re, s5_c_im, s5_d, glu_w, glu_b, norm_ffn2, ffn2_w_in, ffn2_w_out):
    seq = x.shape[1]
    assert seq % (KEY_TILE * 4) == 0 and seq >= WINDOW + Q_BLOCK
    mod = _modulation(c, ada_w, ada_b)
    params = _prepare_parameters(norm_ffn1, ffn1_w_in, ffn1_w_out, norm_mix, w_in, w_out, out_norm, pool_w, pool_b,
                                 pool_scale, q_norm, k_norm, cmp_pe, cmp_k_w1, cmp_k_w2, cmp_v_w1, cmp_v_w2,
                                 s5_lam_re, s5_lam_im, s5_log_dt, s5_b_re, s5_b_im, s5_c_re, s5_c_im, s5_d, glu_w,
                                 glu_b, norm_ffn2, ffn2_w_in, ffn2_w_out)
    tabs = _rope_tables(jnp.arange(seq, dtype=F32))
    tabs_t = _rope_tables_t(jnp.arange(seq, dtype=F32))
    cmp_tabs = _rope_tables((jnp.arange(seq // CMP_STRIDE) * CMP_STRIDE + CMP_LEN - 1).astype(F32))
    for layer in range(ada_w.shape[0]):
        x = _hybrid_layer(x, layer, mod, params, tabs, tabs_t, cmp_tabs)
    return x
```

```python
import functools
import math

import jax
import jax.numpy as jnp
import numpy as np
from jax import lax
from jax.experimental import pallas as pl
from jax.experimental.pallas import tpu as pltpu

F32 = jnp.float32
BF16 = jnp.bfloat16
HIGHEST = lax.Precision.HIGHEST

LANES = 128
SUBLANES = 8
VMEM_LIMIT_BYTES = 56 * 1024 * 1024

D_MODEL = 1024
D_FF = 2816
N_MOD = 9
POOL_WIDTH = 256
POOL_GC = 64
POOL_WINDOWS = (2, 4, 8, 16)
POOL_HALO = 16
HEAD_DIM = 64
NSA_WIDTH = 512
NSA_HEADS = 8
NSA_KV_HEADS = 2
NSA_REP = 4
N_BRANCH = 3
S5_WIDTH = 256
S5_H = 16
S5_GROUPS = 16
S5_P = 64
CMP_STRIDE = 16
CMP_LEN = 32
CMP_HIDDEN = 128
SLC_BLOCK = 64
SLC_SHIFT = 6
N_SELECT = 16
N_FORCED = 3
WINDOW = 512
Q_BLOCK = 256
ROT_DIM = 16
ROPE_THETA = 500000.0
EPS = 1e-6
Q_SCALE = HEAD_DIM ** -0.5 * math.log2(math.e)
NEG_INF = -1e30
MASK_VALUE = -(2.0 ** 100)

OFF_POOL = 0
OFF_KA = OFF_POOL + POOL_WIDTH
OFF_KC = OFF_KA + NSA_KV_HEADS * LANES
OFF_VC = OFF_KC + LANES
OFF_S5 = OFF_VC + LANES
N_COLS = OFF_S5 + S5_WIDTH
GATE_ROWS = 16
ROW_Q = 0
ROW_V = ROW_Q + NSA_HEADS * HEAD_DIM
ROW_GATE = ROW_V + NSA_KV_HEADS * LANES
N_ROWS_T = ROW_GATE + NSA_KV_HEADS * GATE_ROWS

S5_CHUNK = 8
S5_FOLD = S5_CHUNK * S5_WIDTH
S5_STATE = S5_GROUPS * S5_P
S5_HALVES = S5_WIDTH // LANES
S5_HALF_FOLD = S5_FOLD // S5_HALVES
KEY_TILE = 512
V_ROWS = HEAD_DIM + 16


def _dot(a, b):
    return jnp.dot(a, b, preferred_element_type=F32)


def _dot_nt(a, b):
    return lax.dot_general(a, b, (((1,), (1,)), ((), ())), preferred_element_type=F32)


def _sigmoid(x):
    return 1.0 / (1.0 + jnp.exp(-x))


def _gelu_tanh(x):
    return 0.5 * x * (1.0 + jnp.tanh(math.sqrt(2.0 / math.pi) * (x + 0.044715 * (x * x * x))))


def _params(sem):
    return pltpu.CompilerParams(dimension_semantics=sem, vmem_limit_bytes=VMEM_LIMIT_BYTES)


def _const_spec(shape):
    nd = len(shape)
    return pl.BlockSpec(shape, lambda *_: (0,) * nd, pipeline_mode=pl.Buffered(1))


def _layer_spec(shape, layer):
    nd = len(shape)
    return pl.BlockSpec((None,) + shape, lambda *_: (layer,) + (0,) * nd, pipeline_mode=pl.Buffered(1))


def _mod_kernel(ct_ref, w_ref, b_ref, o_ref, *, batch):
    w = w_ref[...]
    rows = []
    for b in range(batch):
        col = ct_ref[:, b:b + 1]
        rows.append(jnp.sum(w * (col * _sigmoid(col)), axis=0, keepdims=True) + b_ref[...])
    o_ref[...] = jnp.concatenate(rows + [jnp.zeros((SUBLANES - batch, w.shape[1]), F32)], axis=0)


def _modulation(c, ada_w, ada_b):
    n_layers, d, n = ada_w.shape
    b = c.shape[0]
    tn = 2304
    out = pl.pallas_call(
        functools.partial(_mod_kernel, batch=b),
        grid=(n_layers, n // tn),
        in_specs=[pl.BlockSpec((d, b), lambda l, j: (0, 0)),
                  pl.BlockSpec((None, d, tn), lambda l, j: (l, 0, j)),
                  pl.BlockSpec((None, 1, tn), lambda l, j: (l, 0, j))],
        out_specs=pl.BlockSpec((None, SUBLANES, tn), lambda l, j: (l, 0, j)),
        out_shape=jax.ShapeDtypeStruct((n_layers, SUBLANES, n), F32),
        compiler_params=_params(("arbitrary", "arbitrary")),
        name="adaln_mod",
    )(c.T, ada_w, ada_b.reshape(n_layers, 1, n))
    return out[:, :b].reshape(n_layers, b, N_MOD, d)


def _norm_modulate(x, gain, mod_ref, first_row):
    ms = jnp.mean(x * x, axis=-1, keepdims=True)
    y = x * lax.rsqrt(ms + EPS) * gain
    return y * (1.0 + mod_ref[first_row + 1:first_row + 2, :]) + mod_ref[first_row:first_row + 1, :]


def _ffn_kernel(x_ref, mod_ref, g_ref, win_ref, wout_ref, o_ref, *, first_row):
    x = x_ref[...]
    h = _norm_modulate(x, g_ref[...], mod_ref, first_row).astype(BF16)
    gu = _dot(h, win_ref[...])
    gate, up = gu[:, :D_FF], gu[:, D_FF:]
    a = (gate * _sigmoid(gate) * up).astype(BF16)
    y = _dot(a, wout_ref[...])
    o_ref[...] = x + 0.5 * mod_ref[first_row + 2:first_row + 3, :] * y


def _mod_spec(layer):
    return pl.BlockSpec((None, None, N_MOD, D_MODEL), lambda i, *_: (layer, i, 0, 0))


def _ffn(x, layer, mod, gain, w_in, w_out, first_row, tm=512):
    b, s, d = x.shape
    return pl.pallas_call(
        functools.partial(_ffn_kernel, first_row=first_row),
        grid=(b, s // tm),
        in_specs=[pl.BlockSpec((None, tm, d), lambda i, j: (i, j, 0)),
                  _mod_spec(layer),
                  _layer_spec((1, d), layer),
                  _layer_spec((d, 2 * D_FF), layer),
                  _layer_spec((D_FF, d), layer)],
        out_specs=pl.BlockSpec((None, tm, d), lambda i, j: (i, j, 0)),
        out_shape=jax.ShapeDtypeStruct((b, s, d), F32),
        compiler_params=_params(("parallel", "parallel")),
        name="ffn_half_step",
    )(x, mod, gain, w_in, w_out)


def _rope(v, cos_t, sin_lo, sin_hi):
    return (v * cos_t + pltpu.roll(v, LANES - ROT_DIM // 2, 1) * sin_lo
            + pltpu.roll(v, ROT_DIM // 2, 1) * sin_hi)


def _inproj_kernel(x_ref, mod_ref, g_ref, w_ref, wt_ref, cos_ref, slo_ref, shi_ref, cost_ref, sint_ref, qg_ref, kg_ref,
                   pool_ref, q_ref, ka_ref, kb_ref, va_ref, vb_ref, kc_ref, vc_ref, gate_ref, s5_ref):
    h = _norm_modulate(x_ref[...], g_ref[...], mod_ref, 3).astype(BF16)
    u = _dot(h, w_ref[...])
    ut = _dot_nt(wt_ref[...], h)
    tm = u.shape[0]
    cos_t, sin_lo, sin_hi = cos_ref[...], slo_ref[...], shi_ref[...]
    lane = lax.broadcasted_iota(jnp.int32, (tm, LANES), 1)
    low_half = lane < HEAD_DIM
    pos = pl.program_id(1) * tm + lax.broadcasted_iota(jnp.int32, (tm, LANES), 0)
    block_one_hot = jnp.where((pos >> SLC_SHIFT) == lane, 1.0, 0.0).astype(BF16)

    pool_ref[...] = u[:, OFF_POOL:OFF_POOL + POOL_WIDTH]
    kc_ref[...] = u[:, OFF_KC:OFF_KC + LANES]
    vc_ref[...] = u[:, OFF_VC:OFF_VC + LANES]
    for half in range(S5_HALVES):
        s5_ref[half] = u[:, OFF_S5 + half * LANES:OFF_S5 + (half + 1) * LANES]
    for g in range(NSA_KV_HEADS):
        v = u[:, OFF_KA + g * LANES:OFF_KA + (g + 1) * LANES]
        sq = v * v
        ms_lo = jnp.sum(jnp.where(low_half, sq, 0.0), axis=-1, keepdims=True) * (1.0 / HEAD_DIM)
        ms_hi = jnp.sum(jnp.where(low_half, 0.0, sq), axis=-1, keepdims=True) * (1.0 / HEAD_DIM)
        r = jnp.where(low_half, lax.rsqrt(ms_lo + EPS), lax.rsqrt(ms_hi + EPS))
        kn = _rope(v * r * kg_ref[...], cos_t, sin_lo, sin_hi)
        ka_ref[g, :, 0:LANES] = kn.astype(BF16)
        ka_ref[g, :, LANES:2 * LANES] = block_one_hot
        kb_ref[g] = pltpu.roll(kn, HEAD_DIM, 1).astype(BF16)

    half_rot = ROT_DIM // 2
    cos8, sin8 = cost_ref[...], sint_ref[...]
    zero_rows = jnp.zeros((LANES - HEAD_DIM, tm), F32)
    for hd in range(NSA_HEADS):
        v = ut[ROW_Q + hd * HEAD_DIM:ROW_Q + (hd + 1) * HEAD_DIM, :]
        ms = jnp.sum(v * v, axis=0, keepdims=True) * (1.0 / HEAD_DIM)
        vn = v * lax.rsqrt(ms + EPS) * qg_ref[...]
        x1, x2 = vn[0:half_rot], vn[half_rot:ROT_DIM]
        roped = jnp.concatenate([x1 * cos8 - x2 * sin8, x2 * cos8 + x1 * sin8, vn[ROT_DIM:]], axis=0)
        q_ref[hd] = jnp.concatenate([roped * Q_SCALE, zero_rows], axis=0).astype(BF16)
    ones = jnp.ones((HEAD_DIM, tm), F32)
    for g in range(NSA_KV_HEADS):
        vt = ut[ROW_V + g * LANES:ROW_V + (g + 1) * LANES, :]
        va = jnp.concatenate([vt[0:HEAD_DIM], ones], axis=0).astype(BF16)
        vb = jnp.concatenate([vt[HEAD_DIM:], ones], axis=0).astype(BF16)
        for blk in range(tm // LANES):
            va_ref[g, blk] = va[:, blk * LANES:(blk + 1) * LANES]
            vb_ref[g, blk] = vb[:, blk * LANES:(blk + 1) * LANES]
        gate_ref[g] = _sigmoid(ut[ROW_GATE + g * GATE_ROWS:ROW_GATE + (g + 1) * GATE_ROWS, :])


def _inproj(x, layer, mod, gain, w_row, w_t, rope_tabs, rope_tabs_t, q_gain, k_gain, tm=1024):
    b, s, d = x.shape
    g = NSA_KV_HEADS
    tok = lambda width: pl.BlockSpec((None, tm, width), lambda i, j: (i, j, 0))
    grp = lambda n, width=LANES: pl.BlockSpec((None, n, tm, width), lambda i, j: (i, 0, j, 0))
    lanes_tok = lambda n, rows: pl.BlockSpec((None, n, rows, tm), lambda i, j: (i, 0, 0, j))
    v_blocks = pl.BlockSpec((None, g, tm // LANES, LANES, LANES), lambda i, j: (i, 0, j, 0, 0))
    tab = pl.BlockSpec((tm, LANES), lambda i, j: (j, 0))
    tab_t = pl.BlockSpec((ROT_DIM // 2, tm), lambda i, j: (0, j))
    sds = jax.ShapeDtypeStruct
    return pl.pallas_call(
        _inproj_kernel,
        grid=(b, s // tm),
        in_specs=[tok(d),
                  _mod_spec(layer),
                  _layer_spec((1, d), layer),
                  _layer_spec((d, N_COLS), layer), _layer_spec((N_ROWS_T, d), layer),
                  tab, tab, tab, tab_t, tab_t,
                  _layer_spec((HEAD_DIM, 1), layer), _layer_spec((1, LANES), layer)],
        out_specs=[tok(POOL_WIDTH), lanes_tok(NSA_HEADS, LANES), grp(g, 2 * LANES), grp(g), v_blocks, v_blocks,
                   tok(LANES), tok(LANES), lanes_tok(g, GATE_ROWS), grp(S5_HALVES)],
        out_shape=[sds((b, s, POOL_WIDTH), F32), sds((b, NSA_HEADS, LANES, s), BF16),
                   sds((b, g, s, 2 * LANES), BF16), sds((b, g, s, LANES), BF16),
                   sds((b, g, s // LANES, LANES, LANES), BF16), sds((b, g, s // LANES, LANES, LANES), BF16),
                   sds((b, s, LANES), F32), sds((b, s, LANES), F32),
                   sds((b, g, GATE_ROWS, s), F32), sds((b, S5_HALVES, s, LANES), F32)],
        compiler_params=_params(("parallel", "parallel")),
        name="mixer_in_proj",
    )(x, mod, gain, w_row, w_t, *rope_tabs, *rope_tabs_t, q_gain, k_gain)


def _fold_rows(ref, n):
    rows = ref.shape[0] // n
    return jnp.concatenate([ref[pl.ds(k, rows, stride=n), :] for k in range(n)], axis=1)


def _unfold_rows(ref, value, n):
    rows = ref.shape[0] // n
    for k in range(n):
        ref[pl.ds(k, rows, stride=n), :] = value[:, k * LANES:(k + 1) * LANES]


def _compress_kernel(kc_ref, vc_ref, w1k_ref, w1v_ref, w1k_raw_ref, w1v_raw_ref, pe_ref, w2k_ref, w2vt_ref,
                     kg_ref, cos_ref, slo_ref, shi_ref, ko_ref, vo_ref, *, n_cmp):
    ncp = kc_ref.shape[0] // CMP_STRIDE
    for src_ref, w1_ref, raw_ref, pe_row, is_key in ((kc_ref, w1k_ref, w1k_raw_ref, 0, True),
                                                    (vc_ref, w1v_ref, w1v_raw_ref, 1, False)):
        chunks = _fold_rows(src_ref, CMP_STRIDE).astype(BF16)
        pe_term = jnp.sum(raw_ref[...] * pe_ref[:, pe_row:pe_row + 1], axis=0, keepdims=True)
        for g in range(NSA_KV_HEADS):
            a = _dot(chunks, w1_ref[g])
            pre = a[:, :CMP_HIDDEN] + pltpu.roll(a[:, CMP_HIDDEN:], ncp - 1, 0) + pe_term
            hidden = _gelu_tanh(pre).astype(BF16)
            if is_key:
                out = _dot(hidden, w2k_ref[...])
                ms = jnp.sum(out * out, axis=-1, keepdims=True) * (1.0 / HEAD_DIM)
                out = _rope(out * lax.rsqrt(ms + EPS) * kg_ref[...], cos_ref[...], slo_ref[...], shi_ref[...])
                real_row = lax.broadcasted_iota(jnp.int32, (ncp, LANES), 0) < n_cmp
                ko_ref[g] = jnp.where(real_row, out, 0.0).astype(BF16)
            else:
                out_t = _dot_nt(w2vt_ref[...], hidden)
                real_col = lax.broadcasted_iota(jnp.int32, (LANES, ncp), 1) < n_cmp
                value_row = lax.broadcasted_iota(jnp.int32, (LANES, ncp), 0) < HEAD_DIM
                vo_ref[g] = jnp.where(value_row, jnp.where(real_col, out_t, 0.0), 1.0).astype(BF16)


def _compress(kc, vc, layer, w1k, w1v, w1k_raw, w1v_raw, pe, w2k, w2v_t, k_gain, cmp_tabs):
    b, s, _ = kc.shape
    ncp = s // CMP_STRIDE
    fold = CMP_STRIDE * LANES
    src = pl.BlockSpec((None, s, LANES), lambda i: (i, 0, 0))
    raw = _layer_spec((CMP_LEN * HEAD_DIM, CMP_HIDDEN), layer)
    w1 = _layer_spec((NSA_KV_HEADS, fold, 2 * CMP_HIDDEN), layer)
    w2 = _layer_spec((CMP_HIDDEN, LANES), layer)
    tab = _const_spec((ncp, LANES))
    return pl.pallas_call(
        functools.partial(_compress_kernel, n_cmp=ncp - 1),
        grid=(b,),
        in_specs=[src, src, w1, w1, raw, raw, _layer_spec((CMP_LEN * HEAD_DIM, 2), layer), w2, w2,
                  _layer_spec((1, LANES), layer), tab, tab, tab],
        out_specs=[pl.BlockSpec((None, NSA_KV_HEADS, ncp, LANES), lambda i: (i, 0, 0, 0)),
                   pl.BlockSpec((None, NSA_KV_HEADS, LANES, ncp), lambda i: (i, 0, 0, 0))],
        out_shape=[jax.ShapeDtypeStruct((b, NSA_KV_HEADS, ncp, LANES), BF16),
                   jax.ShapeDtypeStruct((b, NSA_KV_HEADS, LANES, ncp), BF16)],
        compiler_params=_params(("parallel",)),
        name="nsa_compress",
    )(kc, vc, w1k, w1v, w1k_raw, w1v_raw, pe, w2k, w2v_t, k_gain, *cmp_tabs)


def _selection_bias(imp_t, qb):
    nsb, nq = imp_t.shape
    j = lax.broadcasted_iota(jnp.int32, (nsb, nq), 0)
    t = qb * Q_BLOCK + lax.broadcasted_iota(jnp.int32, (nsb, nq), 1)
    cur = t >> SLC_SHIFT
    valid = j * SLC_BLOCK <= t
    forced = (j == 0) | (j == cur) | (j == cur - 1)
    j_f = j.astype(F32)
    start = jnp.where(valid & jnp.logical_not(forced), imp_t, -1.0)
    vals = start
    for _ in range(N_SELECT - N_FORCED):
        m = jnp.max(vals, axis=0, keepdims=True)
        idx = jnp.min(jnp.where(vals == m, j_f, float(nsb)), axis=0, keepdims=True)
        vals = jnp.where(j_f == idx, -2.0, vals)
    return jnp.where((forced & valid) | (vals != start), 0.0, MASK_VALUE)


def _attn_kernel(q_ref, kc_ref, vct_ref, ka_ref, kb_ref, va_ref, vb_ref, gate_ref, ovl_ref, o_ref,
                 sa_ref, sb_ref, acc_ref, *, seq):
    qb = pl.program_id(2)
    cols = NSA_REP * Q_BLOCK
    ncp = seq // CMP_STRIDE
    q_t = jnp.concatenate([q_ref[r] for r in range(NSA_REP)], axis=1)
    t_col = qb * Q_BLOCK + lax.broadcasted_iota(jnp.int32, (1, cols), 1) % Q_BLOCK
    t_q = qb * Q_BLOCK + lax.broadcasted_iota(jnp.int32, (1, Q_BLOCK), 1)

    def all_heads(per_query):
        return jnp.concatenate([per_query] * NSA_REP, axis=1)

    cmp_end = lax.broadcasted_iota(jnp.int32, (ncp, Q_BLOCK), 0) * CMP_STRIDE + (CMP_LEN - 1)
    s = _dot(kc_ref[...], q_t) + all_heads(jnp.where(cmp_end <= t_q, 0.0, NEG_INF))
    e = jnp.exp2(s - jnp.max(s, axis=0, keepdims=True)).astype(BF16)
    oc = _dot(jnp.concatenate([vct_ref[0:V_ROWS, :], ovl_ref[...]], axis=0), e)
    oc = oc * jnp.where(t_col >= CMP_LEN - 1, 1.0 / oc[HEAD_DIM:HEAD_DIM + 1, :], 0.0)
    o_cmp, imp = oc[0:V_ROWS], oc[V_ROWS:V_ROWS + LANES]
    imp_t = (imp[:, 0:Q_BLOCK] + imp[:, Q_BLOCK:2 * Q_BLOCK]
             + imp[:, 2 * Q_BLOCK:3 * Q_BLOCK] + imp[:, 3 * Q_BLOCK:4 * Q_BLOCK])
    bias = _selection_bias(imp_t, qb).astype(BF16)

    q_aug = jnp.concatenate([q_t, jnp.concatenate([bias] * NSA_REP, axis=1)], axis=0)

    span = WINDOW + Q_BLOCK
    start = pl.multiple_of(jnp.maximum(qb * Q_BLOCK - WINDOW, 0), LANES)
    start_blk = start // LANES
    kpos = start + lax.broadcasted_iota(jnp.int32, (span, Q_BLOCK), 0)
    in_window = (kpos <= t_q) & (kpos > t_q - WINDOW)
    s = _dot(kb_ref[pl.ds(start, span), :], q_t) + all_heads(jnp.where(in_window, 0.0, NEG_INF))
    e = jnp.exp2(s - jnp.max(s, axis=0, keepdims=True)).astype(BF16)
    acc_win = _dot(jnp.concatenate([vb_ref[start_blk + i, 0:V_ROWS, :] for i in range(span // LANES)], axis=1), e)
    o_win = acc_win * (1.0 / acc_win[HEAD_DIM:HEAD_DIM + 1, :])

    blocks_per_tile = KEY_TILE // LANES

    def score_tile(kt, s_ref):
        k0 = pl.multiple_of(kt * KEY_TILE, KEY_TILE)
        s_ref[...] = _dot(ka_ref[pl.ds(k0, KEY_TILE), :], q_aug)

    def absorb_tile(kt, s_ref, carry, causal):
        m_i, acc = carry
        blk0 = kt * blocks_per_tile
        s = s_ref[...]
        if causal:
            kpos = kt * KEY_TILE + lax.broadcasted_iota(jnp.int32, (KEY_TILE, Q_BLOCK), 0)
            s = s + all_heads(jnp.where(kpos <= t_q, 0.0, MASK_VALUE))
        m_new = jnp.maximum(m_i, jnp.max(s, axis=0, keepdims=True))
        p = jnp.exp2(s - m_new).astype(BF16)
        v_t = jnp.concatenate([va_ref[blk0 + i, 0:V_ROWS, :] for i in range(blocks_per_tile)], axis=1)
        return m_new, jnp.exp2(m_i - m_new) * acc + _dot(v_t, p)

    def pair_trip(j, carry):
        score_tile(2 * j + 1, sb_ref)
        carry = absorb_tile(2 * j, sa_ref, carry, causal=False)
        score_tile(2 * j + 2, sa_ref)
        return absorb_tile(2 * j + 1, sb_ref, carry, causal=False)

    def double_trip(j, carry):
        return pair_trip(2 * j + 1, pair_trip(2 * j, carry))

    last_pair = qb // (2 * KEY_TILE // Q_BLOCK)
    score_tile(0, sa_ref)
    carry = (jnp.full((1, cols), NEG_INF, F32), jnp.zeros((V_ROWS, cols), F32))
    carry = lax.fori_loop(0, last_pair // 2, double_trip, carry)
    carry = lax.fori_loop(2 * (last_pair // 2), last_pair, pair_trip, carry)
    second_tile_live = (qb // (KEY_TILE // Q_BLOCK)) % 2 == 1

    @pl.when(second_tile_live)
    def _():
        score_tile(2 * last_pair + 1, sb_ref)
        both = absorb_tile(2 * last_pair, sa_ref, carry, causal=False)
        acc_ref[...] = absorb_tile(2 * last_pair + 1, sb_ref, both, causal=True)[1]

    @pl.when(jnp.logical_not(second_tile_live))
    def _():
        acc_ref[...] = absorb_tile(2 * last_pair, sa_ref, carry, causal=True)[1]

    acc_slc = acc_ref[...]
    o_slc = acc_slc * (1.0 / acc_slc[HEAD_DIM:HEAD_DIM + 1, :])

    gates = gate_ref[...]
    heads = []
    for r in range(NSA_REP):
        cs = slice(r * Q_BLOCK, (r + 1) * Q_BLOCK)
        c = r * N_BRANCH
        heads.append(gates[c:c + 1, :] * o_cmp[0:HEAD_DIM, cs] + gates[c + 1:c + 2, :] * o_slc[0:HEAD_DIM, cs]
                     + gates[c + 2:c + 3, :] * o_win[0:HEAD_DIM, cs])
    o_ref[...] = jnp.concatenate(heads, axis=0).T


def _overlap_matrix(seq):
    ncp, nsb = seq // CMP_STRIDE, seq // SLC_BLOCK
    c_start = np.arange(ncp)[None, :] * CMP_STRIDE
    s_start = np.arange(nsb)[:, None] * SLC_BLOCK
    ovl = np.clip(np.minimum(c_start + CMP_LEN, s_start + SLC_BLOCK) - np.maximum(c_start, s_start), 0, None)
    ovl = ovl.astype(np.float32) / CMP_LEN
    ovl[:, ncp - 1] = 0.0
    return jnp.asarray(np.pad(ovl, ((0, LANES - nsb), (0, 0))))


def _nsa_attention(q_t, k_cmp, v_cmp_t, ka, kb, va_t, vb_t, gates_t):
    b, _, _, s = q_t.shape
    g = NSA_KV_HEADS
    ncp = s // CMP_STRIDE
    assert s // SLC_BLOCK <= LANES
    per_group = lambda *shape: pl.BlockSpec((None, None) + shape, lambda i, j, k: (i, j) + (0,) * len(shape))
    return pl.pallas_call(
        functools.partial(_attn_kernel, seq=s),
        grid=(b, g, s // Q_BLOCK),
        in_specs=[pl.BlockSpec((None, NSA_REP, LANES, Q_BLOCK), lambda i, j, k: (i, j, 0, k)),
                  per_group(ncp, LANES), per_group(LANES, ncp), per_group(s, 2 * LANES), per_group(s, LANES),
                  per_group(s // LANES, LANES, LANES), per_group(s // LANES, LANES, LANES),
                  pl.BlockSpec((None, None, GATE_ROWS, Q_BLOCK), lambda i, j, k: (i, j, 0, k)),
                  _const_spec((LANES, ncp))],
        out_specs=pl.BlockSpec((None, Q_BLOCK, NSA_REP * HEAD_DIM), lambda i, j, k: (i, k, j)),
        out_shape=jax.ShapeDtypeStruct((b, s, NSA_WIDTH), F32),
        scratch_shapes=[pltpu.VMEM((KEY_TILE, NSA_REP * Q_BLOCK), F32)] * 2
        + [pltpu.VMEM((V_ROWS, NSA_REP * Q_BLOCK), F32)],
        compiler_params=_params(("parallel", "parallel", "arbitrary")),
        name="nsa_attention",
    )(q_t, k_cmp, v_cmp_t, ka, kb, va_t, vb_t, gates_t, _overlap_matrix(s).astype(BF16))


def _s5_kernel(u_ref, mt_ref, bc_ref, cc_ref, a1_ref, a2_ref, d_ref, y_ref, h_ref, g_scr, hp_scr):
    @pl.when(pl.program_id(1) == 0)
    def _():
        h_ref[...] = jnp.zeros_like(h_ref)

    halves = range(S5_HALVES)
    u = [_fold_rows(u_ref.at[a], S5_CHUNK) for a in halves]
    ub = [x.astype(BF16) for x in u]
    y_local = [_dot(ub[a], mt_ref[a]) for a in halves]
    for a in halves:
        g_scr[:, a * S5_HALF_FOLD:(a + 1) * S5_HALF_FOLD] = _dot(ub[a], bc_ref[a])
    a1, a2 = a1_ref[...], a2_ref[...]
    n_state = S5_STATE // S5_HALVES

    def swap_re_im(h):
        parts = [h[:, k * n_state:(k + 1) * n_state] for k in range(2 * S5_HALVES)]
        return jnp.concatenate([parts[k ^ 1] for k in range(2 * S5_HALVES)], axis=1)

    def step(i, h):
        hp_scr[pl.ds(i, 1), :] = h
        return a1 * h + a2 * swap_re_im(h) + g_scr[pl.ds(i, 1), :]

    h_ref[...] = lax.fori_loop(0, g_scr.shape[0], step, h_ref[...], unroll=8)
    for a in halves:
        carried = _dot(hp_scr[:, a * S5_HALF_FOLD:(a + 1) * S5_HALF_FOLD].astype(BF16), cc_ref[a])
        _unfold_rows(y_ref.at[a], y_local[a] + carried + u[a] * d_ref[a:a + 1, :], S5_CHUNK)


def _s5_scan(u, layer, mats):
    b, _, s, _ = u.shape
    rows = s // S5_CHUNK
    tc = min(256, rows)
    tile = pl.BlockSpec((None, S5_HALVES, tc * S5_CHUNK, LANES), lambda i, j: (i, 0, j, 0))
    mat = _layer_spec((S5_HALVES, S5_HALF_FOLD, S5_HALF_FOLD), layer)
    row = _layer_spec((1, S5_HALVES * S5_HALF_FOLD), layer)
    return pl.pallas_call(
        _s5_kernel,
        grid=(b, rows // tc),
        in_specs=[tile, mat, mat, mat, row, row, _layer_spec((S5_HALVES, S5_HALF_FOLD), layer)],
        out_specs=tile,
        out_shape=jax.ShapeDtypeStruct((b, S5_HALVES, s, LANES), F32),
        scratch_shapes=[pltpu.VMEM((1, S5_HALVES * S5_HALF_FOLD), F32),
                        pltpu.VMEM((tc, S5_HALVES * S5_HALF_FOLD), F32),
                        pltpu.VMEM((tc, S5_HALVES * S5_HALF_FOLD), F32)],
        compiler_params=_params(("parallel", "arbitrary")),
        name="s5_scan",
    )(u, *mats)


def _s5_matrices(lam_re, lam_im, log_dt, b_re, b_im, c_re, c_im, d_skip):
    t0, ng, nh, npm = S5_CHUNK, S5_GROUPS, S5_H, S5_P
    gh = ng // S5_HALVES
    ein = functools.partial(jnp.einsum, precision=HIGHEST)
    lam = lax.complex(lam_re, lam_im)
    step = jnp.exp(log_dt)[:, None]
    lam_bar = jnp.exp(lam * step)
    b_bar = lax.complex(b_re, b_im) * ((lam_bar - 1.0) / lam)[..., None]
    c_mat = lax.complex(c_re, c_im)
    k = jnp.arange(t0 + 1, dtype=F32)[:, None, None]
    pw = jnp.exp((lam * step)[None] * k)

    def same_group(rows_per_group, cols_per_group):
        r = np.arange(gh * rows_per_group)[:, None] // rows_per_group
        c = np.arange(gh * cols_per_group)[None, :] // cols_per_group
        return jnp.asarray((r == c).astype(np.float32))

    def per_group_blocks(x, rows_per_group, cols_per_group):
        return jnp.tile(x, (1,) * (x.ndim - 1) + (gh,)) * same_group(rows_per_group, cols_per_group)

    kern = jnp.real(ein('ghp,kgp,gpq->kghq', c_mat, pw[:t0], b_bar))
    d_k = per_group_blocks(kern.transpose(0, 1, 3, 2).reshape(t0, S5_HALVES, gh * nh, nh), nh, nh)
    d_k = jnp.concatenate([d_k, jnp.zeros_like(d_k[:1])], axis=0)
    lag = np.arange(t0)[None, :] - np.arange(t0)[:, None]
    mt = d_k[np.where(lag >= 0, lag, t0)]
    mt = mt.transpose(2, 0, 3, 1, 4).reshape(S5_HALVES, S5_HALF_FOLD, S5_HALF_FOLD)
    b_j = (pw[t0 - 1 - np.arange(t0)][..., None] * b_bar[None]).transpose(0, 1, 3, 2)
    b_j = b_j.reshape(t0, S5_HALVES, gh * nh, npm)
    bc = jnp.concatenate([per_group_blocks(jnp.real(b_j), nh, npm), per_group_blocks(jnp.imag(b_j), nh, npm)],
                         axis=-1)
    bc = bc.transpose(1, 0, 2, 3).reshape(S5_HALVES, S5_HALF_FOLD, S5_HALF_FOLD)
    c_i = (c_mat[None] * pw[1:t0 + 1][:, :, None, :]).transpose(1, 3, 0, 2)
    c_i = c_i.reshape(S5_HALVES, gh * npm, t0, nh)
    mask = same_group(npm, nh)[:, None, :]
    cc = jnp.concatenate([jnp.tile(jnp.real(c_i), (1, 1, 1, gh)) * mask,
                          jnp.tile(-jnp.imag(c_i), (1, 1, 1, gh)) * mask], axis=1)
    cc = cc.reshape(S5_HALVES, S5_HALF_FOLD, S5_HALF_FOLD)
    a_chunk = pw[t0].reshape(S5_HALVES, gh * npm)
    a1 = jnp.concatenate([jnp.real(a_chunk), jnp.real(a_chunk)], axis=1).reshape(1, -1)
    a2 = jnp.concatenate([-jnp.imag(a_chunk), jnp.imag(a_chunk)], axis=1).reshape(1, -1)
    d_vec = jnp.tile(d_skip.reshape(S5_HALVES, gh * nh), (1, t0))
    return mt.astype(BF16), bc.astype(BF16), cc.astype(BF16), a1, a2, d_vec


def _rms_gain(y, gain):
    return y * lax.rsqrt(jnp.mean(y * y, axis=-1, keepdims=True) + EPS) * gain


def _outproj_kernel(x_ref, mod_ref, pool_ref, halo_ref, nsa_ref, s5_ref, pw_ref, pb_ref, ps_ref, on_ref,
                    gw_ref, gb_ref, wo_ref, o_ref, buf):
    j = pl.program_id(1)
    tm = x_ref.shape[0]
    v = pool_ref[...]
    buf[0:POOL_HALO, :] = jnp.where(j > 0, halo_ref[...], 0.0)
    buf[POOL_HALO:POOL_HALO + tm, :] = v
    second_group = (lax.broadcasted_iota(jnp.int32, (tm, LANES), 1) >> SLC_SHIFT) == 1
    t1 = (j * tm + 1 + lax.broadcasted_iota(jnp.int32, (tm, 1), 0)).astype(F32)
    pooled = []
    for half in range(POOL_WIDTH // LANES):
        cols = slice(half * LANES, (half + 1) * LANES)
        v_half = v[:, cols]
        run, k, means = v_half, 1, []
        for w in POOL_WINDOWS[2 * half:2 * half + 2]:
            while k < w:
                run = run + buf[POOL_HALO - k:POOL_HALO - k + tm, cols]
                k += 1
            means.append(run / jnp.minimum(t1, float(w)))
        pooled.append(jnp.where(second_group, means[1], means[0]) - v_half)
    y_pool = (_dot(jnp.concatenate(pooled, axis=1).astype(BF16), pw_ref[...]) + pb_ref[...]) * ps_ref[...]

    y = _gelu_tanh(jnp.concatenate([s5_ref[half] for half in range(S5_HALVES)], axis=-1))
    y_s5 = y * _sigmoid(_dot(y.astype(BF16), gw_ref[...]) + gb_ref[...])

    cat = jnp.concatenate(
        [_rms_gain(y_pool, on_ref[:, 0:POOL_WIDTH]),
         _rms_gain(nsa_ref[...], on_ref[:, POOL_WIDTH:POOL_WIDTH + NSA_WIDTH]),
         _rms_gain(y_s5, on_ref[:, POOL_WIDTH + NSA_WIDTH:])], axis=-1).astype(BF16)
    o_ref[...] = x_ref[...] + mod_ref[5:6, :] * _dot(cat, wo_ref[...])


def _outproj(x, layer, mod, u_pool, o_nsa, y_s5, pool_w_bd, pool_b, pool_scale, out_norm, glu_w, glu_b, w_out,
             tm=1024):
    b, s, d = x.shape
    tok = lambda width: pl.BlockSpec((None, tm, width), lambda i, j: (i, j, 0))
    halo_blocks = tm // POOL_HALO
    return pl.pallas_call(
        _outproj_kernel,
        grid=(b, s // tm),
        in_specs=[tok(d),
                  _mod_spec(layer),
                  tok(POOL_WIDTH),
                  pl.BlockSpec((None, POOL_HALO, POOL_WIDTH),
                               lambda i, j: (i, jnp.maximum(j * halo_blocks - 1, 0), 0)),
                  tok(NSA_WIDTH),
                  pl.BlockSpec((None, S5_HALVES, tm, LANES), lambda i, j: (i, 0, j, 0)),
                  _layer_spec((POOL_WIDTH, POOL_WIDTH), layer), _layer_spec((1, POOL_WIDTH), layer),
                  _layer_spec((1, POOL_WIDTH), layer), _layer_spec((1, d), layer),
                  _layer_spec((S5_WIDTH, S5_WIDTH), layer), _layer_spec((1, S5_WIDTH), layer),
                  _layer_spec((d, d), layer)],
        out_specs=tok(d),
        out_shape=jax.ShapeDtypeStruct((b, s, d), F32),
        scratch_shapes=[pltpu.VMEM((POOL_HALO + tm, POOL_WIDTH), F32)],
        compiler_params=_params(("parallel", "arbitrary")),
        name="mixer_out_proj",
    )(x, mod, u_pool, u_pool, o_nsa, y_s5, pool_w_bd, pool_b, pool_scale, out_norm, glu_w, glu_b, w_out)


def _rope_angles(pos):
    half = ROT_DIM // 2
    inv_freq = np.exp(-math.log(ROPE_THETA) * np.arange(half, dtype=np.float64) * (2.0 / ROT_DIM))
    return np.asarray(pos, np.float64)[:, None] * inv_freq[None, :]


def _rope_tables(pos):
    ang = _rope_angles(pos)
    cos, sin = np.cos(ang), np.sin(ang)
    n, half = ang.shape
    rest = HEAD_DIM - ROT_DIM
    cos_t = np.concatenate([cos, cos, np.ones((n, rest))], axis=1)
    sin_lo = np.concatenate([-sin, np.zeros((n, half + rest))], axis=1)
    sin_hi = np.concatenate([np.zeros((n, half)), sin, np.zeros((n, rest))], axis=1)
    return tuple(jnp.asarray(np.tile(t, (1, 2)), F32) for t in (cos_t, sin_lo, sin_hi))


def _pad_lanes(v):
    return jnp.pad(v, [(0, 0)] * (v.ndim - 1) + [(0, LANES - v.shape[-1])])


def _rope_tables_t(pos):
    ang = _rope_angles(pos).T
    return jnp.asarray(np.cos(ang), F32), jnp.asarray(np.sin(ang), F32)


def _arrange_w_in(w):
    d = w.shape[0]
    o1, o2, o3 = POOL_WIDTH, POOL_WIDTH + NSA_WIDTH, POOL_WIDTH + NSA_WIDTH + 6 * LANES
    kv = w[:, o2:o3].reshape(d, 6, NSA_KV_HEADS, HEAD_DIM)
    ka = jnp.concatenate([kv[:, 2], kv[:, 4]], axis=-1).reshape(d, NSA_KV_HEADS * LANES)
    n_gate = NSA_REP * N_BRANCH
    w_row = jnp.concatenate([w[:, :o1], ka, kv[:, 0].reshape(d, LANES), kv[:, 1].reshape(d, LANES),
                             w[:, o3 + NSA_KV_HEADS * n_gate:]], axis=1)
    q_t = w[:, o1:o2].T
    v_t = jnp.concatenate([kv[:, 3], kv[:, 5]], axis=-1).reshape(d, NSA_KV_HEADS * LANES).T
    gate_t = w[:, o3:o3 + NSA_KV_HEADS * n_gate].T.reshape(NSA_KV_HEADS, n_gate, d)
    gate_t = jnp.pad(gate_t, ((0, 0), (0, GATE_ROWS - n_gate), (0, 0))).reshape(NSA_KV_HEADS * GATE_ROWS, d)
    return w_row.astype(BF16), jnp.concatenate([q_t, v_t, gate_t], axis=0).astype(BF16)


def _expand_cmp_w1(w1):
    halves = w1.reshape(2, CMP_STRIDE, HEAD_DIM, CMP_HIDDEN)
    both = jnp.concatenate([halves[0], halves[1]], axis=-1)
    out = []
    for g in range(NSA_KV_HEADS):
        z = jnp.zeros((CMP_STRIDE, NSA_KV_HEADS, HEAD_DIM, 2 * CMP_HIDDEN), F32).at[:, g].set(both)
        out.append(z.reshape(CMP_STRIDE * LANES, 2 * CMP_HIDDEN))
    return jnp.stack(out).astype(BF16)


def _block_diag(w):
    g, c, _ = w.shape
    return jnp.einsum('gcd,gf->gcfd', w, jnp.eye(g, dtype=w.dtype)).reshape(g * c, g * c)


def _prepare_parameters(norm_ffn1, ffn1_w_in, ffn1_w_out, norm_mix, w_in, w_out, out_norm, pool_w, pool_b, pool_scale,
                        q_norm, k_norm, cmp_pe, cmp_k_w1, cmp_k_w2, cmp_v_w1, cmp_v_w2, s5_lam_re, s5_lam_im,
                        s5_log_dt, s5_b_re, s5_b_im, s5_c_re, s5_c_im, s5_d, glu_w, glu_b, norm_ffn2, ffn2_w_in,
                        ffn2_w_out):
    n_layers = norm_mix.shape[0]
    row = lambda v: v.reshape(n_layers, 1, -1)
    w_row, w_t = jax.vmap(_arrange_w_in)(w_in)
    return dict(
        ffn1=(row(norm_ffn1), ffn1_w_in.astype(BF16), ffn1_w_out.astype(BF16)),
        ffn2=(row(norm_ffn2), ffn2_w_in.astype(BF16), ffn2_w_out.astype(BF16)),
        inproj=(row(norm_mix), w_row, w_t, q_norm.reshape(n_layers, HEAD_DIM, 1),
                jnp.concatenate([k_norm[:, 1], k_norm[:, 2]], axis=-1).reshape(n_layers, 1, LANES)),
        compress=(jax.vmap(_expand_cmp_w1)(cmp_k_w1), jax.vmap(_expand_cmp_w1)(cmp_v_w1), cmp_k_w1, cmp_v_w1,
                  jnp.swapaxes(cmp_pe.reshape(n_layers, 2, CMP_LEN * HEAD_DIM), 1, 2),
                  _pad_lanes(cmp_k_w2).astype(BF16),
                  jnp.swapaxes(_pad_lanes(cmp_v_w2), 1, 2).astype(BF16), row(_pad_lanes(k_norm[:, 0]))),
        s5=jax.vmap(_s5_matrices)(s5_lam_re, s5_lam_im, s5_log_dt, s5_b_re, s5_b_im, s5_c_re, s5_c_im, s5_d),
        outproj=(jax.vmap(_block_diag)(pool_w).astype(BF16), row(pool_b), row(pool_scale), row(out_norm),
                 glu_w.astype(BF16), row(glu_b), w_out.astype(BF16)))


def _hybrid_layer(x, layer, mod, params, tabs, tabs_t, cmp_tabs):
    x = _ffn(x, layer, mod, *params["ffn1"], 0)
    u_pool, q_t, ka, kb, va_t, vb_t, kc, vc, gates_t, u_s5 = _inproj(
        x, layer, mod, *params["inproj"][:3], tabs, tabs_t, *params["inproj"][3:])
    k_cmp, v_cmp_t = _compress(kc, vc, layer, *params["compress"], cmp_tabs)
    o_nsa = _nsa_attention(q_t, k_cmp, v_cmp_t, ka, kb, va_t, vb_t, gates_t)
    y_s5 = _s5_scan(u_s5, layer, params["s5"])
    x = _outproj(x, layer, mod, u_pool, o_nsa, y_s5, *params["outproj"])
    return _ffn(x, layer, mod, *params["ffn2"], 6)


def kernel(x, c, ada_w, ada_b, norm_ffn1, ffn1_w_in, ffn1_w_out, norm_mix, w_in, w_out, out_norm, pool_w, pool_b, pool_scale, q_norm, k_norm, cmp_pe, cmp_k_w1, cmp_k_w2, cmp_v_w1, cmp_v_w2, s5_lam_re, s5_lam_im, s5_log_dt, s5_b_re, s5_b_im, s5_c_re, s5_c_im, s5_d, glu_w, glu_b, norm_ffn2, ffn2_w_in, ffn2_w_out):
    seq = x.shape[1]
    assert seq % (KEY_TILE * 4) == 0 and seq >= WINDOW + Q_BLOCK
    mod = _modulation(c, ada_w, ada_b)
    params = _prepare_parameters(norm_ffn1, ffn1_w_in, ffn1_w_out, norm_mix, w_in, w_out, out_norm, pool_w, pool_b,
                                 pool_scale, q_norm, k_norm, cmp_pe, cmp_k_w1, cmp_k_w2, cmp_v_w1, cmp_v_w2,
                                 s5_lam_re, s5_lam_im, s5_log_dt, s5_b_re, s5_b_im, s5_c_re, s5_c_im, s5_d, glu_w,
                                 glu_b, norm_ffn2, ffn2_w_in, ffn2_w_out)
    tabs = _rope_tables(np.arange(seq))
    tabs_t = _rope_tables_t(np.arange(seq))
    cmp_tabs = _rope_tables(np.arange(seq // CMP_STRIDE) * CMP_STRIDE + CMP_LEN - 1)
    for layer in range(ada_w.shape[0]):
        x = _hybrid_layer(x, layer, mod, params, tabs, tabs_t, cmp_tabs)
    return x
```

```python
import functools
import math

import jax
import jax.numpy as jnp
import numpy as np
from jax import lax
from jax.experimental import pallas as pl
from jax.experimental.pallas import tpu as pltpu

F32 = jnp.float32
BF16 = jnp.bfloat16
HIGHEST = lax.Precision.HIGHEST

LANES = 128
SUBLANES = 8
VMEM_LIMIT_BYTES = 40 * 1024 * 1024

D_MODEL = 1024
D_FF = 2816
N_MOD = 9
POOL_WIDTH = 256
POOL_GC = 64
POOL_WINDOWS = (2, 4, 8, 16)
POOL_HALO = 16
HEAD_DIM = 64
NSA_WIDTH = 512
NSA_HEADS = 8
NSA_KV_HEADS = 2
NSA_REP = 4
N_BRANCH = 3
S5_WIDTH = 256
S5_H = 16
S5_GROUPS = 16
S5_P = 64
CMP_STRIDE = 16
CMP_LEN = 32
CMP_HIDDEN = 128
SLC_BLOCK = 64
SLC_SHIFT = 6
N_SELECT = 16
N_FORCED = 3
WINDOW = 512
Q_BLOCK = 256
ROT_DIM = 16
ROPE_THETA = 500000.0
EPS = 1e-6
Q_SCALE = HEAD_DIM ** -0.5 * math.log2(math.e)
NEG_INF = -1e30
MASK_VALUE = -(2.0 ** 100)

OFF_POOL = 0
OFF_KA = OFF_POOL + POOL_WIDTH
OFF_KC = OFF_KA + NSA_KV_HEADS * LANES
OFF_VC = OFF_KC + LANES
OFF_S5 = OFF_VC + LANES
N_COLS = OFF_S5 + S5_WIDTH
GATE_ROWS = 16
ROW_Q = 0
ROW_V = ROW_Q + NSA_HEADS * HEAD_DIM
ROW_GATE = ROW_V + NSA_KV_HEADS * LANES
N_ROWS_T = ROW_GATE + NSA_KV_HEADS * GATE_ROWS

S5_CHUNK = 8
S5_FOLD = S5_CHUNK * S5_WIDTH
S5_STATE = S5_GROUPS * S5_P
S5_HALVES = S5_WIDTH // LANES
S5_HALF_FOLD = S5_FOLD // S5_HALVES
KEY_TILE = 512
V_ROWS = HEAD_DIM + 16


def _dot(a, b):
    return jnp.dot(a, b, preferred_element_type=F32)


def _dot_nt(a, b):
    return lax.dot_general(a, b, (((1,), (1,)), ((), ())), preferred_element_type=F32)


def _sigmoid(x):
    return 1.0 / (1.0 + jnp.exp(-x))


def _gelu_tanh(x):
    return 0.5 * x * (1.0 + jnp.tanh(math.sqrt(2.0 / math.pi) * (x + 0.044715 * (x * x * x))))


def _params(sem):
    return pltpu.CompilerParams(dimension_semantics=sem, vmem_limit_bytes=VMEM_LIMIT_BYTES)


def _const_spec(shape):
    nd = len(shape)
    return pl.BlockSpec(shape, lambda *_: (0,) * nd, pipeline_mode=pl.Buffered(1))


def _layer_spec(shape, layer):
    nd = len(shape)
    return pl.BlockSpec((None,) + shape, lambda *_: (layer,) + (0,) * nd, pipeline_mode=pl.Buffered(1))


def _mod_kernel(ct_ref, w_ref, b_ref, o_ref, *, batch):
    w = w_ref[...]
    rows = []
    for b in range(batch):
        col = ct_ref[:, b:b + 1]
        rows.append(jnp.sum(w * (col * _sigmoid(col)), axis=0, keepdims=True) + b_ref[...])
    o_ref[...] = jnp.concatenate(rows + [jnp.zeros((SUBLANES - batch, w.shape[1]), F32)], axis=0)


def _modulation(c, ada_w, ada_b):
    n_layers, d, n = ada_w.shape
    b = c.shape[0]
    tn = 2304
    out = pl.pallas_call(
        functools.partial(_mod_kernel, batch=b),
        grid=(n_layers, n // tn),
        in_specs=[pl.BlockSpec((d, b), lambda l, j: (0, 0)),
                  pl.BlockSpec((None, d, tn), lambda l, j: (l, 0, j)),
                  pl.BlockSpec((None, 1, tn), lambda l, j: (l, 0, j))],
        out_specs=pl.BlockSpec((None, SUBLANES, tn), lambda l, j: (l, 0, j)),
        out_shape=jax.ShapeDtypeStruct((n_layers, SUBLANES, n), F32),
        compiler_params=_params(("arbitrary", "arbitrary")),
        name="adaln_mod",
    )(c.T, ada_w, ada_b.reshape(n_layers, 1, n))
    return out[:, :b].reshape(n_layers, b, N_MOD, d)


def _norm_modulate(x, gain, mod_ref, first_row):
    ms = jnp.mean(x * x, axis=-1, keepdims=True)
    y = x * lax.rsqrt(ms + EPS) * gain
    return y * (1.0 + mod_ref[first_row + 1:first_row + 2, :]) + mod_ref[first_row:first_row + 1, :]


def _ffn_kernel(x_ref, mod_ref, g_ref, win_ref, wout_ref, o_ref, *, first_row):
    x = x_ref[...]
    h = _norm_modulate(x, g_ref[...], mod_ref, first_row).astype(BF16)
    gu = _dot(h, win_ref[...])
    gate, up = gu[:, :D_FF], gu[:, D_FF:]
    a = (gate * _sigmoid(gate) * up).astype(BF16)
    y = _dot(a, wout_ref[...])
    o_ref[...] = x + 0.5 * mod_ref[first_row + 2:first_row + 3, :] * y


def _mod_spec(layer):
    return pl.BlockSpec((None, None, N_MOD, D_MODEL), lambda i, *_: (layer, i, 0, 0))


def _ffn(x, layer, mod, gain, w_in, w_out, first_row, tm=512):
    b, s, d = x.shape
    return pl.pallas_call(
        functools.partial(_ffn_kernel, first_row=first_row),
        grid=(b, s // tm),
        in_specs=[pl.BlockSpec((None, tm, d), lambda i, j: (i, j, 0)),
                  _mod_spec(layer),
                  _layer_spec((1, d), layer),
                  _layer_spec((d, 2 * D_FF), layer),
                  _layer_spec((D_FF, d), layer)],
        out_specs=pl.BlockSpec((None, tm, d), lambda i, j: (i, j, 0)),
        out_shape=jax.ShapeDtypeStruct((b, s, d), F32),
        compiler_params=_params(("parallel", "parallel")),
        name="ffn_half_step",
    )(x, mod, gain, w_in, w_out)


def _rope(v, cos_t, sin_lo, sin_hi):
    return (v * cos_t + pltpu.roll(v, LANES - ROT_DIM // 2, 1) * sin_lo
            + pltpu.roll(v, ROT_DIM // 2, 1) * sin_hi)


def _inproj_kernel(x_ref, mod_ref, g_ref, w_ref, wt_ref, cos_ref, slo_ref, shi_ref, cost_ref, sint_ref, qg_ref, kg_ref,
                   pool_ref, q_ref, ka_ref, kb_ref, va_ref, vb_ref, kc_ref, vc_ref, gate_ref, s5_ref):
    h = _norm_modulate(x_ref[...], g_ref[...], mod_ref, 3).astype(BF16)
    u = _dot(h, w_ref[...])
    ut = _dot_nt(wt_ref[...], h)
    tm = u.shape[0]
    cos_t, sin_lo, sin_hi = cos_ref[...], slo_ref[...], shi_ref[...]
    lane = lax.broadcasted_iota(jnp.int32, (tm, LANES), 1)
    low_half = lane < HEAD_DIM
    pos = pl.program_id(1) * tm + lax.broadcasted_iota(jnp.int32, (tm, LANES), 0)
    block_one_hot = jnp.where((pos >> SLC_SHIFT) == lane, 1.0, 0.0).astype(BF16)

    pool_ref[...] = u[:, OFF_POOL:OFF_POOL + POOL_WIDTH]
    kc_ref[...] = u[:, OFF_KC:OFF_KC + LANES]
    vc_ref[...] = u[:, OFF_VC:OFF_VC + LANES]
    for half in range(S5_HALVES):
        s5_ref[half] = u[:, OFF_S5 + half * LANES:OFF_S5 + (half + 1) * LANES]
    for g in range(NSA_KV_HEADS):
        v = u[:, OFF_KA + g * LANES:OFF_KA + (g + 1) * LANES]
        sq = v * v
        ms_lo = jnp.sum(jnp.where(low_half, sq, 0.0), axis=-1, keepdims=True) * (1.0 / HEAD_DIM)
        ms_hi = jnp.sum(jnp.where(low_half, 0.0, sq), axis=-1, keepdims=True) * (1.0 / HEAD_DIM)
        r = jnp.where(low_half, lax.rsqrt(ms_lo + EPS), lax.rsqrt(ms_hi + EPS))
        kn = _rope(v * r * kg_ref[...], cos_t, sin_lo, sin_hi)
        ka_ref[g, :, 0:LANES] = kn.astype(BF16)
        ka_ref[g, :, LANES:2 * LANES] = block_one_hot
        kb_ref[g] = pltpu.roll(kn, HEAD_DIM, 1).astype(BF16)

    half_rot = ROT_DIM // 2
    cos8, sin8 = cost_ref[...], sint_ref[...]
    zero_rows = jnp.zeros((LANES - HEAD_DIM, tm), F32)
    for hd in range(NSA_HEADS):
        v = ut[ROW_Q + hd * HEAD_DIM:ROW_Q + (hd + 1) * HEAD_DIM, :]
        ms = jnp.sum(v * v, axis=0, keepdims=True) * (1.0 / HEAD_DIM)
        vn = v * lax.rsqrt(ms + EPS) * qg_ref[...]
        x1, x2 = vn[0:half_rot], vn[half_rot:ROT_DIM]
        roped = jnp.concatenate([x1 * cos8 - x2 * sin8, x2 * cos8 + x1 * sin8, vn[ROT_DIM:]], axis=0)
        q_ref[hd] = jnp.concatenate([roped * Q_SCALE, zero_rows], axis=0).astype(BF16)
    ones = jnp.ones((HEAD_DIM, tm), F32)
    for g in range(NSA_KV_HEADS):
        vt = ut[ROW_V + g * LANES:ROW_V + (g + 1) * LANES, :]
        va = jnp.concatenate([vt[0:HEAD_DIM], ones], axis=0).astype(BF16)
        vb = jnp.concatenate([vt[HEAD_DIM:], ones], axis=0).astype(BF16)
        for blk in range(tm // LANES):
            va_ref[g, blk] = va[:, blk * LANES:(blk + 1) * LANES]
            vb_ref[g, blk] = vb[:, blk * LANES:(blk + 1) * LANES]
        gate_ref[g] = _sigmoid(ut[ROW_GATE + g * GATE_ROWS:ROW_GATE + (g + 1) * GATE_ROWS, :])


def _inproj(x, layer, mod, gain, w_row, w_t, rope_tabs, rope_tabs_t, q_gain, k_gain, tm=1024):
    b, s, d = x.shape
    g = NSA_KV_HEADS
    tok = lambda width: pl.BlockSpec((None, tm, width), lambda i, j: (i, j, 0))
    grp = lambda n, width=LANES: pl.BlockSpec((None, n, tm, width), lambda i, j: (i, 0, j, 0))
    lanes_tok = lambda n, rows: pl.BlockSpec((None, n, rows, tm), lambda i, j: (i, 0, 0, j))
    v_blocks = pl.BlockSpec((None, g, tm // LANES, LANES, LANES), lambda i, j: (i, 0, j, 0, 0))
    tab = pl.BlockSpec((tm, LANES), lambda i, j: (j, 0))
    tab_t = pl.BlockSpec((ROT_DIM // 2, tm), lambda i, j: (0, j))
    sds = jax.ShapeDtypeStruct
    return pl.pallas_call(
        _inproj_kernel,
        grid=(b, s // tm),
        in_specs=[tok(d),
                  _mod_spec(layer),
                  _layer_spec((1, d), layer),
                  _layer_spec((d, N_COLS), layer), _layer_spec((N_ROWS_T, d), layer),
                  tab, tab, tab, tab_t, tab_t,
                  _layer_spec((HEAD_DIM, 1), layer), _layer_spec((1, LANES), layer)],
        out_specs=[tok(POOL_WIDTH), lanes_tok(NSA_HEADS, LANES), grp(g, 2 * LANES), grp(g), v_blocks, v_blocks,
                   tok(LANES), tok(LANES), lanes_tok(g, GATE_ROWS), grp(S5_HALVES)],
        out_shape=[sds((b, s, POOL_WIDTH), F32), sds((b, NSA_HEADS, LANES, s), BF16),
                   sds((b, g, s, 2 * LANES), BF16), sds((b, g, s, LANES), BF16),
                   sds((b, g, s // LANES, LANES, LANES), BF16), sds((b, g, s // LANES, LANES, LANES), BF16),
                   sds((b, s, LANES), F32), sds((b, s, LANES), F32),
                   sds((b, g, GATE_ROWS, s), F32), sds((b, S5_HALVES, s, LANES), F32)],
        compiler_params=_params(("parallel", "parallel")),
        name="mixer_in_proj",
    )(x, mod, gain, w_row, w_t, *rope_tabs, *rope_tabs_t, q_gain, k_gain)


def _fold_rows(ref, n):
    rows = ref.shape[0] // n
    return jnp.concatenate([ref[pl.ds(k, rows, stride=n), :] for k in range(n)], axis=1)


def _unfold_rows(ref, value, n):
    rows = ref.shape[0] // n
    for k in range(n):
        ref[pl.ds(k, rows, stride=n), :] = value[:, k * LANES:(k + 1) * LANES]


def _compress_kernel(kc_ref, vc_ref, w1k_ref, w1v_ref, w1k_raw_ref, w1v_raw_ref, pe_ref, w2k_ref, w2vt_ref,
                     kg_ref, cos_ref, slo_ref, shi_ref, ko_ref, vo_ref, *, n_cmp):
    ncp = kc_ref.shape[0] // CMP_STRIDE
    for src_ref, w1_ref, raw_ref, pe_row, is_key in ((kc_ref, w1k_ref, w1k_raw_ref, 0, True),
                                                    (vc_ref, w1v_ref, w1v_raw_ref, 1, False)):
        chunks = _fold_rows(src_ref, CMP_STRIDE).astype(BF16)
        pe_term = jnp.sum(raw_ref[...] * pe_ref[:, pe_row:pe_row + 1], axis=0, keepdims=True)
        for g in range(NSA_KV_HEADS):
            a = _dot(chunks, w1_ref[g])
            pre = a[:, :CMP_HIDDEN] + pltpu.roll(a[:, CMP_HIDDEN:], ncp - 1, 0) + pe_term
            hidden = _gelu_tanh(pre).astype(BF16)
            if is_key:
                out = _dot(hidden, w2k_ref[...])
                ms = jnp.sum(out * out, axis=-1, keepdims=True) * (1.0 / HEAD_DIM)
                out = _rope(out * lax.rsqrt(ms + EPS) * kg_ref[...], cos_ref[...], slo_ref[...], shi_ref[...])
                real_row = lax.broadcasted_iota(jnp.int32, (ncp, LANES), 0) < n_cmp
                ko_ref[g] = jnp.where(real_row, out, 0.0).astype(BF16)
            else:
                out_t = _dot_nt(w2vt_ref[...], hidden)
                real_col = lax.broadcasted_iota(jnp.int32, (LANES, ncp), 1) < n_cmp
                value_row = lax.broadcasted_iota(jnp.int32, (LANES, ncp), 0) < HEAD_DIM
                vo_ref[g] = jnp.where(value_row, jnp.where(real_col, out_t, 0.0), 1.0).astype(BF16)


def _compress(kc, vc, layer, w1k, w1v, w1k_raw, w1v_raw, pe, w2k, w2v_t, k_gain, cmp_tabs):
    b, s, _ = kc.shape
    ncp = s // CMP_STRIDE
    fold = CMP_STRIDE * LANES
    src = pl.BlockSpec((None, s, LANES), lambda i: (i, 0, 0))
    raw = _layer_spec((CMP_LEN * HEAD_DIM, CMP_HIDDEN), layer)
    w1 = _layer_spec((NSA_KV_HEADS, fold, 2 * CMP_HIDDEN), layer)
    w2 = _layer_spec((CMP_HIDDEN, LANES), layer)
    tab = _const_spec((ncp, LANES))
    return pl.pallas_call(
        functools.partial(_compress_kernel, n_cmp=ncp - 1),
        grid=(b,),
        in_specs=[src, src, w1, w1, raw, raw, _layer_spec((CMP_LEN * HEAD_DIM, 2), layer), w2, w2,
                  _layer_spec((1, LANES), layer), tab, tab, tab],
        out_specs=[pl.BlockSpec((None, NSA_KV_HEADS, ncp, LANES), lambda i: (i, 0, 0, 0)),
                   pl.BlockSpec((None, NSA_KV_HEADS, LANES, ncp), lambda i: (i, 0, 0, 0))],
        out_shape=[jax.ShapeDtypeStruct((b, NSA_KV_HEADS, ncp, LANES), BF16),
                   jax.ShapeDtypeStruct((b, NSA_KV_HEADS, LANES, ncp), BF16)],
        compiler_params=_params(("parallel",)),
        name="nsa_compress",
    )(kc, vc, w1k, w1v, w1k_raw, w1v_raw, pe, w2k, w2v_t, k_gain, *cmp_tabs)


def _selection_bias(imp_t, qb):
    nsb, nq = imp_t.shape
    j = lax.broadcasted_iota(jnp.int32, (nsb, nq), 0)
    t = qb * Q_BLOCK + lax.broadcasted_iota(jnp.int32, (nsb, nq), 1)
    cur = t >> SLC_SHIFT
    valid = j * SLC_BLOCK <= t
    forced = (j == 0) | (j == cur) | (j == cur - 1)
    j_f = j.astype(F32)
    start = jnp.where(valid & jnp.logical_not(forced), imp_t, -1.0)
    vals = start
    for _ in range(N_SELECT - N_FORCED):
        m = jnp.max(vals, axis=0, keepdims=True)
        idx = jnp.min(jnp.where(vals == m, j_f, float(nsb)), axis=0, keepdims=True)
        vals = jnp.where(j_f == idx, -2.0, vals)
    return jnp.where((forced & valid) | (vals != start), 0.0, MASK_VALUE)


def _attn_kernel(q_ref, kc_ref, vct_ref, ka_ref, kb_ref, va_ref, vb_ref, gate_ref, ovl_ref, o_ref,
                 sa_ref, sb_ref, acc_ref, *, seq):
    qb = pl.program_id(2)
    cols = NSA_REP * Q_BLOCK
    ncp = seq // CMP_STRIDE
    q_t = jnp.concatenate([q_ref[r] for r in range(NSA_REP)], axis=1)
    t_col = qb * Q_BLOCK + lax.broadcasted_iota(jnp.int32, (1, cols), 1) % Q_BLOCK
    t_q = qb * Q_BLOCK + lax.broadcasted_iota(jnp.int32, (1, Q_BLOCK), 1)

    def all_heads(per_query):
        return jnp.concatenate([per_query] * NSA_REP, axis=1)

    cmp_end = lax.broadcasted_iota(jnp.int32, (ncp, Q_BLOCK), 0) * CMP_STRIDE + (CMP_LEN - 1)
    s = _dot(kc_ref[...], q_t) + all_heads(jnp.where(cmp_end <= t_q, 0.0, NEG_INF))
    e = jnp.exp2(s - jnp.max(s, axis=0, keepdims=True)).astype(BF16)
    oc = _dot(jnp.concatenate([vct_ref[0:V_ROWS, :], ovl_ref[...]], axis=0), e)
    oc = oc * jnp.where(t_col >= CMP_LEN - 1, 1.0 / oc[HEAD_DIM:HEAD_DIM + 1, :], 0.0)
    o_cmp, imp = oc[0:V_ROWS], oc[V_ROWS:V_ROWS + LANES]
    imp_t = (imp[:, 0:Q_BLOCK] + imp[:, Q_BLOCK:2 * Q_BLOCK]
             + imp[:, 2 * Q_BLOCK:3 * Q_BLOCK] + imp[:, 3 * Q_BLOCK:4 * Q_BLOCK])
    bias = _selection_bias(imp_t, qb).astype(BF16)

    q_aug = jnp.concatenate([q_t, jnp.concatenate([bias] * NSA_REP, axis=1)], axis=0)

    span = WINDOW + Q_BLOCK
    start = pl.multiple_of(jnp.maximum(qb * Q_BLOCK - WINDOW, 0), LANES)
    start_blk = start // LANES
    kpos = start + lax.broadcasted_iota(jnp.int32, (span, Q_BLOCK), 0)
    in_window = (kpos <= t_q) & (kpos > t_q - WINDOW)
    s = _dot(kb_ref[pl.ds(start, span), :], q_t) + all_heads(jnp.where(in_window, 0.0, NEG_INF))
    e = jnp.exp2(s - jnp.max(s, axis=0, keepdims=True)).astype(BF16)
    acc_win = _dot(jnp.concatenate([vb_ref[start_blk + i, 0:V_ROWS, :] for i in range(span // LANES)], axis=1), e)
    o_win = acc_win * (1.0 / acc_win[HEAD_DIM:HEAD_DIM + 1, :])

    blocks_per_tile = KEY_TILE // LANES

    def score_tile(kt, s_ref):
        k0 = pl.multiple_of(kt * KEY_TILE, KEY_TILE)
        s_ref[...] = _dot(ka_ref[pl.ds(k0, KEY_TILE), :], q_aug)

    def absorb_tile(kt, s_ref, carry, causal):
        m_i, acc = carry
        blk0 = kt * blocks_per_tile
        s = s_ref[...]
        if causal:
            kpos = kt * KEY_TILE + lax.broadcasted_iota(jnp.int32, (KEY_TILE, Q_BLOCK), 0)
            s = s + all_heads(jnp.where(kpos <= t_q, 0.0, MASK_VALUE))
        m_new = jnp.maximum(m_i, jnp.max(s, axis=0, keepdims=True))
        p = jnp.exp2(s - m_new).astype(BF16)
        v_t = jnp.concatenate([va_ref[blk0 + i, 0:V_ROWS, :] for i in range(blocks_per_tile)], axis=1)
        return m_new, jnp.exp2(m_i - m_new) * acc + _dot(v_t, p)

    def pair_trip(j, carry):
        score_tile(2 * j + 1, sb_ref)
        carry = absorb_tile(2 * j, sa_ref, carry, causal=False)
        score_tile(2 * j + 2, sa_ref)
        return absorb_tile(2 * j + 1, sb_ref, carry, causal=False)

    def double_trip(j, carry):
        return pair_trip(2 * j + 1, pair_trip(2 * j, carry))

    last_pair = qb // (2 * KEY_TILE // Q_BLOCK)
    score_tile(0, sa_ref)
    carry = (jnp.full((1, cols), NEG_INF, F32), jnp.zeros((V_ROWS, cols), F32))
    carry = lax.fori_loop(0, last_pair // 2, double_trip, carry)
    carry = lax.fori_loop(2 * (last_pair // 2), last_pair, pair_trip, carry)
    second_tile_live = (qb // (KEY_TILE // Q_BLOCK)) % 2 == 1

    @pl.when(second_tile_live)
    def _():
        score_tile(2 * last_pair + 1, sb_ref)
        both = absorb_tile(2 * last_pair, sa_ref, carry, causal=False)
        acc_ref[...] = absorb_tile(2 * last_pair + 1, sb_ref, both, causal=True)[1]

    @pl.when(jnp.logical_not(second_tile_live))
    def _():
        acc_ref[...] = absorb_tile(2 * last_pair, sa_ref, carry, causal=True)[1]

    acc_slc = acc_ref[...]
    o_slc = acc_slc * (1.0 / acc_slc[HEAD_DIM:HEAD_DIM + 1, :])

    gates = gate_ref[...]
    heads = []
    for r in range(NSA_REP):
        cs = slice(r * Q_BLOCK, (r + 1) * Q_BLOCK)
        c = r * N_BRANCH
        heads.append(gates[c:c + 1, :] * o_cmp[0:HEAD_DIM, cs] + gates[c + 1:c + 2, :] * o_slc[0:HEAD_DIM, cs]
                     + gates[c + 2:c + 3, :] * o_win[0:HEAD_DIM, cs])
    o_ref[...] = jnp.concatenate(heads, axis=0).T


def _overlap_matrix(seq):
    ncp, nsb = seq // CMP_STRIDE, seq // SLC_BLOCK
    c_start = np.arange(ncp)[None, :] * CMP_STRIDE
    s_start = np.arange(nsb)[:, None] * SLC_BLOCK
    ovl = np.clip(np.minimum(c_start + CMP_LEN, s_start + SLC_BLOCK) - np.maximum(c_start, s_start), 0, None)
    ovl = ovl.astype(np.float32) / CMP_LEN
    ovl[:, ncp - 1] = 0.0
    return jnp.asarray(np.pad(ovl, ((0, LANES - nsb), (0, 0))))


def _nsa_attention(q_t, k_cmp, v_cmp_t, ka, kb, va_t, vb_t, gates_t):
    b, _, _, s = q_t.shape
    g = NSA_KV_HEADS
    ncp = s // CMP_STRIDE
    assert s // SLC_BLOCK <= LANES
    per_group = lambda *shape: pl.BlockSpec((None, None) + shape, lambda i, j, k: (i, j) + (0,) * len(shape))
    return pl.pallas_call(
        functools.partial(_attn_kernel, seq=s),
        grid=(b, g, s // Q_BLOCK),
        in_specs=[pl.BlockSpec((None, NSA_REP, LANES, Q_BLOCK), lambda i, j, k: (i, j, 0, k)),
                  per_group(ncp, LANES), per_group(LANES, ncp), per_group(s, 2 * LANES), per_group(s, LANES),
                  per_group(s // LANES, LANES, LANES), per_group(s // LANES, LANES, LANES),
                  pl.BlockSpec((None, None, GATE_ROWS, Q_BLOCK), lambda i, j, k: (i, j, 0, k)),
                  _const_spec((LANES, ncp))],
        out_specs=pl.BlockSpec((None, Q_BLOCK, NSA_REP * HEAD_DIM), lambda i, j, k: (i, k, j)),
        out_shape=jax.ShapeDtypeStruct((b, s, NSA_WIDTH), F32),
        scratch_shapes=[pltpu.VMEM((KEY_TILE, NSA_REP * Q_BLOCK), F32)] * 2
        + [pltpu.VMEM((V_ROWS, NSA_REP * Q_BLOCK), F32)],
        compiler_params=_params(("parallel", "parallel", "arbitrary")),
        name="nsa_attention",
    )(q_t, k_cmp, v_cmp_t, ka, kb, va_t, vb_t, gates_t, _overlap_matrix(s).astype(BF16))


def _s5_kernel(u_ref, mt_ref, bc_ref, cc_ref, a1_ref, a2_ref, d_ref, y_ref, h_ref, g_scr, hp_scr):
    @pl.when(pl.program_id(1) == 0)
    def _():
        h_ref[...] = jnp.zeros_like(h_ref)

    halves = range(S5_HALVES)
    u = [_fold_rows(u_ref.at[a], S5_CHUNK) for a in halves]
    ub = [x.astype(BF16) for x in u]
    y_local = [_dot(ub[a], mt_ref[a]) for a in halves]
    for a in halves:
        g_scr[:, a * S5_HALF_FOLD:(a + 1) * S5_HALF_FOLD] = _dot(ub[a], bc_ref[a])
    a1, a2 = a1_ref[...], a2_ref[...]
    n_state = S5_STATE // S5_HALVES

    def swap_re_im(h):
        parts = [h[:, k * n_state:(k + 1) * n_state] for k in range(2 * S5_HALVES)]
        return jnp.concatenate([parts[k ^ 1] for k in range(2 * S5_HALVES)], axis=1)

    def step(i, h):
        hp_scr[pl.ds(i, 1), :] = h
        return a1 * h + a2 * swap_re_im(h) + g_scr[pl.ds(i, 1), :]

    h_ref[...] = lax.fori_loop(0, g_scr.shape[0], step, h_ref[...], unroll=8)
    for a in halves:
        carried = _dot(hp_scr[:, a * S5_HALF_FOLD:(a + 1) * S5_HALF_FOLD].astype(BF16), cc_ref[a])
        _unfold_rows(y_ref.at[a], y_local[a] + carried + u[a] * d_ref[a:a + 1, :], S5_CHUNK)


def _s5_scan(u, layer, mats):
    b, _, s, _ = u.shape
    rows = s // S5_CHUNK
    tc = min(256, rows)
    tile = pl.BlockSpec((None, S5_HALVES, tc * S5_CHUNK, LANES), lambda i, j: (i, 0, j, 0))
    mat = _layer_spec((S5_HALVES, S5_HALF_FOLD, S5_HALF_FOLD), layer)
    row = _layer_spec((1, S5_HALVES * S5_HALF_FOLD), layer)
    return pl.pallas_call(
        _s5_kernel,
        grid=(b, rows // tc),
        in_specs=[tile, mat, mat, mat, row, row, _layer_spec((S5_HALVES, S5_HALF_FOLD), layer)],
        out_specs=tile,
        out_shape=jax.ShapeDtypeStruct((b, S5_HALVES, s, LANES), F32),
        scratch_shapes=[pltpu.VMEM((1, S5_HALVES * S5_HALF_FOLD), F32),
                        pltpu.VMEM((tc, S5_HALVES * S5_HALF_FOLD), F32),
                        pltpu.VMEM((tc, S5_HALVES * S5_HALF_FOLD), F32)],
        compiler_params=_params(("parallel", "arbitrary")),
        name="s5_scan",
    )(u, *mats)


def _s5_matrices(lam_re, lam_im, log_dt, b_re, b_im, c_re, c_im, d_skip):
    t0, ng, nh, npm = S5_CHUNK, S5_GROUPS, S5_H, S5_P
    gh = ng // S5_HALVES
    ein = functools.partial(jnp.einsum, precision=HIGHEST)
    lam = lax.complex(lam_re, lam_im)
    step = jnp.exp(log_dt)[:, None]
    lam_bar = jnp.exp(lam * step)
    b_bar = lax.complex(b_re, b_im) * ((lam_bar - 1.0) / lam)[..., None]
    c_mat = lax.complex(c_re, c_im)
    k = jnp.arange(t0 + 1, dtype=F32)[:, None, None]
    pw = jnp.exp((lam * step)[None] * k)

    def same_group(rows_per_group, cols_per_group):
        r = np.arange(gh * rows_per_group)[:, None] // rows_per_group
        c = np.arange(gh * cols_per_group)[None, :] // cols_per_group
        return jnp.asarray((r == c).astype(np.float32))

    def per_group_blocks(x, rows_per_group, cols_per_group):
        return jnp.tile(x, (1,) * (x.ndim - 1) + (gh,)) * same_group(rows_per_group, cols_per_group)

    kern = jnp.real(ein('ghp,kgp,gpq->kghq', c_mat, pw[:t0], b_bar))
    d_k = per_group_blocks(kern.transpose(0, 1, 3, 2).reshape(t0, S5_HALVES, gh * nh, nh), nh, nh)
    d_k = jnp.concatenate([d_k, jnp.zeros_like(d_k[:1])], axis=0)
    lag = np.arange(t0)[None, :] - np.arange(t0)[:, None]
    mt = d_k[np.where(lag >= 0, lag, t0)]
    mt = mt.transpose(2, 0, 3, 1, 4).reshape(S5_HALVES, S5_HALF_FOLD, S5_HALF_FOLD)
    b_j = (pw[t0 - 1 - np.arange(t0)][..., None] * b_bar[None]).transpose(0, 1, 3, 2)
    b_j = b_j.reshape(t0, S5_HALVES, gh * nh, npm)
    bc = jnp.concatenate([per_group_blocks(jnp.real(b_j), nh, npm), per_group_blocks(jnp.imag(b_j), nh, npm)],
                         axis=-1)
    bc = bc.transpose(1, 0, 2, 3).reshape(S5_HALVES, S5_HALF_FOLD, S5_HALF_FOLD)
    c_i = (c_mat[None] * pw[1:t0 + 1][:, :, None, :]).transpose(1, 3, 0, 2)
    c_i = c_i.reshape(S5_HALVES, gh * npm, t0, nh)
    mask = same_group(npm, nh)[:, None, :]
    cc = jnp.concatenate([jnp.tile(jnp.real(c_i), (1, 1, 1, gh)) * mask,
                          jnp.tile(-jnp.imag(c_i), (1, 1, 1, gh)) * mask], axis=1)
    cc = cc.reshape(S5_HALVES, S5_HALF_FOLD, S5_HALF_FOLD)
    a_chunk = pw[t0].reshape(S5_HALVES, gh * npm)
    a1 = jnp.concatenate([jnp.real(a_chunk), jnp.real(a_chunk)], axis=1).reshape(1, -1)
    a2 = jnp.concatenate([-jnp.imag(a_chunk), jnp.imag(a_chunk)], axis=1).reshape(1, -1)
    d_vec = jnp.tile(d_skip.reshape(S5_HALVES, gh * nh), (1, t0))
    return mt.astype(BF16), bc.astype(BF16), cc.astype(BF16), a1, a2, d_vec


def _rms_gain(y, gain):
    return y * lax.rsqrt(jnp.mean(y * y, axis=-1, keepdims=True) + EPS) * gain


def _outproj_kernel(x_ref, mod_ref, pool_ref, halo_ref, nsa_ref, s5_ref, pw_ref, pb_ref, ps_ref, on_ref,
                    gw_ref, gb_ref, wo_ref, o_ref, buf):
    j = pl.program_id(1)
    tm = x_ref.shape[0]
    v = pool_ref[...]
    buf[0:POOL_HALO, :] = jnp.where(j > 0, halo_ref[...], 0.0)
    buf[POOL_HALO:POOL_HALO + tm, :] = v
    second_group = (lax.broadcasted_iota(jnp.int32, (tm, LANES), 1) >> SLC_SHIFT) == 1
    t1 = (j * tm + 1 + lax.broadcasted_iota(jnp.int32, (tm, 1), 0)).astype(F32)
    pooled = []
    for half in range(POOL_WIDTH // LANES):
        cols = slice(half * LANES, (half + 1) * LANES)
        v_half = v[:, cols]
        run, k, means = v_half, 1, []
        for w in POOL_WINDOWS[2 * half:2 * half + 2]:
            while k < w:
                run = run + buf[POOL_HALO - k:POOL_HALO - k + tm, cols]
                k += 1
            means.append(run / jnp.minimum(t1, float(w)))
        pooled.append(jnp.where(second_group, means[1], means[0]) - v_half)
    y_pool = (_dot(jnp.concatenate(pooled, axis=1).astype(BF16), pw_ref[...]) + pb_ref[...]) * ps_ref[...]

    y = _gelu_tanh(jnp.concatenate([s5_ref[half] for half in range(S5_HALVES)], axis=-1))
    y_s5 = y * _sigmoid(_dot(y.astype(BF16), gw_ref[...]) + gb_ref[...])

    cat = jnp.concatenate(
        [_rms_gain(y_pool, on_ref[:, 0:POOL_WIDTH]),
         _rms_gain(nsa_ref[...], on_ref[:, POOL_WIDTH:POOL_WIDTH + NSA_WIDTH]),
         _rms_gain(y_s5, on_ref[:, POOL_WIDTH + NSA_WIDTH:])], axis=-1).astype(BF16)
    o_ref[...] = x_ref[...] + mod_ref[5:6, :] * _dot(cat, wo_ref[...])


def _outproj(x, layer, mod, u_pool, o_nsa, y_s5, pool_w_bd, pool_b, pool_scale, out_norm, glu_w, glu_b, w_out,
             tm=1024):
    b, s, d = x.shape
    tok = lambda width: pl.BlockSpec((None, tm, width), lambda i, j: (i, j, 0))
    halo_blocks = tm // POOL_HALO
    return pl.pallas_call(
        _outproj_kernel,
        grid=(b, s // tm),
        in_specs=[tok(d),
                  _mod_spec(layer),
                  tok(POOL_WIDTH),
                  pl.BlockSpec((None, POOL_HALO, POOL_WIDTH),
                               lambda i, j: (i, jnp.maximum(j * halo_blocks - 1, 0), 0)),
                  tok(NSA_WIDTH),
                  pl.BlockSpec((None, S5_HALVES, tm, LANES), lambda i, j: (i, 0, j, 0)),
                  _layer_spec((POOL_WIDTH, POOL_WIDTH), layer), _layer_spec((1, POOL_WIDTH), layer),
                  _layer_spec((1, POOL_WIDTH), layer), _layer_spec((1, d), layer),
                  _layer_spec((S5_WIDTH, S5_WIDTH), layer), _layer_spec((1, S5_WIDTH), layer),
                  _layer_spec((d, d), layer)],
        out_specs=tok(d),
        out_shape=jax.ShapeDtypeStruct((b, s, d), F32),
        scratch_shapes=[pltpu.VMEM((POOL_HALO + tm, POOL_WIDTH), F32)],
        compiler_params=_params(("parallel", "arbitrary")),
        name="mixer_out_proj",
    )(x, mod, u_pool, u_pool, o_nsa, y_s5, pool_w_bd, pool_b, pool_scale, out_norm, glu_w, glu_b, w_out)


def _rope_angles(pos):
    half = ROT_DIM // 2
    inv_freq = np.exp(-math.log(ROPE_THETA) * np.arange(half, dtype=np.float64) * (2.0 / ROT_DIM))
    return np.asarray(pos, np.float64)[:, None] * inv_freq[None, :]


def _rope_tables(pos):
    ang = _rope_angles(pos)
    cos, sin = np.cos(ang), np.sin(ang)
    n, half = ang.shape
    rest = HEAD_DIM - ROT_DIM
    cos_t = np.concatenate([cos, cos, np.ones((n, rest))], axis=1)
    sin_lo = np.concatenate([-sin, np.zeros((n, half + rest))], axis=1)
    sin_hi = np.concatenate([np.zeros((n, half)), sin, np.zeros((n, rest))], axis=1)
    return tuple(jnp.asarray(np.tile(t, (1, 2)), F32) for t in (cos_t, sin_lo, sin_hi))


def _pad_lanes(v):
    return jnp.pad(v, [(0, 0)] * (v.ndim - 1) + [(0, LANES - v.shape[-1])])


def _rope_tables_t(pos):
    ang = _rope_angles(pos).T
    return jnp.asarray(np.cos(ang), F32), jnp.asarray(np.sin(ang), F32)


def _arrange_w_in(w):
    d = w.shape[0]
    o1, o2, o3 = POOL_WIDTH, POOL_WIDTH + NSA_WIDTH, POOL_WIDTH + NSA_WIDTH + 6 * LANES
    kv = w[:, o2:o3].reshape(d, 6, NSA_KV_HEADS, HEAD_DIM)
    ka = jnp.concatenate([kv[:, 2], kv[:, 4]], axis=-1).reshape(d, NSA_KV_HEADS * LANES)
    n_gate = NSA_REP * N_BRANCH
    w_row = jnp.concatenate([w[:, :o1], ka, kv[:, 0].reshape(d, LANES), kv[:, 1].reshape(d, LANES),
                             w[:, o3 + NSA_KV_HEADS * n_gate:]], axis=1)
    q_t = w[:, o1:o2].T
    v_t = jnp.concatenate([kv[:, 3], kv[:, 5]], axis=-1).reshape(d, NSA_KV_HEADS * LANES).T
    gate_t = w[:, o3:o3 + NSA_KV_HEADS * n_gate].T.reshape(NSA_KV_HEADS, n_gate, d)
    gate_t = jnp.pad(gate_t, ((0, 0), (0, GATE_ROWS - n_gate), (0, 0))).reshape(NSA_KV_HEADS * GATE_ROWS, d)
    return w_row.astype(BF16), jnp.concatenate([q_t, v_t, gate_t], axis=0).astype(BF16)


def _expand_cmp_w1(w1):
    halves = w1.reshape(2, CMP_STRIDE, HEAD_DIM, CMP_HIDDEN)
    both = jnp.concatenate([halves[0], halves[1]], axis=-1)
    out = []
    for g in range(NSA_KV_HEADS):
        z = jnp.zeros((CMP_STRIDE, NSA_KV_HEADS, HEAD_DIM, 2 * CMP_HIDDEN), F32).at[:, g].set(both)
        out.append(z.reshape(CMP_STRIDE * LANES, 2 * CMP_HIDDEN))
    return jnp.stack(out).astype(BF16)


def _block_diag(w):
    g, c, _ = w.shape
    return jnp.einsum('gcd,gf->gcfd', w, jnp.eye(g, dtype=w.dtype)).reshape(g * c, g * c)


def _prepare_parameters(norm_ffn1, ffn1_w_in, ffn1_w_out, norm_mix, w_in, w_out, out_norm, pool_w, pool_b, pool_scale,
                        q_norm, k_norm, cmp_pe, cmp_k_w1, cmp_k_w2, cmp_v_w1, cmp_v_w2, s5_lam_re, s5_lam_im,
                        s5_log_dt, s5_b_re, s5_b_im, s5_c_re, s5_c_im, s5_d, glu_w, glu_b, norm_ffn2, ffn2_w_in,
                        ffn2_w_out):
    n_layers = norm_mix.shape[0]
    row = lambda v: v.reshape(n_layers, 1, -1)
    w_row, w_t = jax.vmap(_arrange_w_in)(w_in)
    return dict(
        ffn1=(row(norm_ffn1), ffn1_w_in.astype(BF16), ffn1_w_out.astype(BF16)),
        ffn2=(row(norm_ffn2), ffn2_w_in.astype(BF16), ffn2_w_out.astype(BF16)),
        inproj=(row(norm_mix), w_row, w_t, q_norm.reshape(n_layers, HEAD_DIM, 1),
                jnp.concatenate([k_norm[:, 1], k_norm[:, 2]], axis=-1).reshape(n_layers, 1, LANES)),
        compress=(jax.vmap(_expand_cmp_w1)(cmp_k_w1), jax.vmap(_expand_cmp_w1)(cmp_v_w1), cmp_k_w1, cmp_v_w1,
                  jnp.swapaxes(cmp_pe.reshape(n_layers, 2, CMP_LEN * HEAD_DIM), 1, 2),
                  _pad_lanes(cmp_k_w2).astype(BF16),
                  jnp.swapaxes(_pad_lanes(cmp_v_w2), 1, 2).astype(BF16), row(_pad_lanes(k_norm[:, 0]))),
        s5=jax.vmap(_s5_matrices)(s5_lam_re, s5_lam_im, s5_log_dt, s5_b_re, s5_b_im, s5_c_re, s5_c_im, s5_d),
        outproj=(jax.vmap(_block_diag)(pool_w).astype(BF16), row(pool_b), row(pool_scale), row(out_norm),
                 glu_w.astype(BF16), row(glu_b), w_out.astype(BF16)))


def _hybrid_layer(x, layer, mod, params, tabs, tabs_t, cmp_tabs):
    x = _ffn(x, layer, mod, *params["ffn1"], 0)
    u_pool, q_t, ka, kb, va_t, vb_t, kc, vc, gates_t, u_s5 = _inproj(
        x, layer, mod, *params["inproj"][:3], tabs, tabs_t, *params["inproj"][3:])
    k_cmp, v_cmp_t = _compress(kc, vc, layer, *params["compress"], cmp_tabs)
    o_nsa = _nsa_attention(q_t, k_cmp, v_cmp_t, ka, kb, va_t, vb_t, gates_t)
    y_s5 = _s5_scan(u_s5, layer, params["s5"])
    x = _outproj(x, layer, mod, u_pool, o_nsa, y_s5, *params["outproj"])
    return _ffn(x, layer, mod, *params["ffn2"], 6)


def kernel(x, c, ada_w, ada_b, norm_ffn1, ffn1_w_in, ffn1_w_out, norm_mix, w_in, w_out, out_norm, pool_w, pool_b, pool_scale, q_norm, k_norm, cmp_pe, cmp_k_w1, cmp_k_w2, cmp_v_w1, cmp_v_w2, s5_lam_re, s5_lam_im, s5_log_dt, s5_b_re, s5_b_im, s5_c_re, s5_c_im, s5_d, glu_w, glu_b, norm_ffn2, ffn2_w_in, ffn2_w_out):
    seq = x.shape[1]
    assert seq % (KEY_TILE * 4) == 0 and seq >= WINDOW + Q_BLOCK
    mod = _modulation(c, ada_w, ada_b)
    params = _prepare_parameters(norm_ffn1, ffn1_w_in, ffn1_w_out, norm_mix, w_in, w_out, out_norm, pool_w, pool_b,
                                 pool_scale, q_norm, k_norm, cmp_pe, cmp_k_w1, cmp_k_w2, cmp_v_w1, cmp_v_w2,
                                 s5_lam_re, s5_lam_im, s5_log_dt, s5_b_re, s5_b_im, s5_c_re, s5_c_im, s5_d, glu_w,
                                 glu_b, norm_ffn2, ffn2_w_in, ffn2_w_out)
    tabs = _rope_tables(np.arange(seq))
    tabs_t = _rope_tables_t(np.arange(seq))
    cmp_tabs = _rope_tables(np.arange(seq // CMP_STRIDE) * CMP_STRIDE + CMP_LEN - 1)
    for layer in range(ada_w.shape[0]):
        x = _hybrid_layer(x, layer, mod, params, tabs, tabs_t, cmp_tabs)
    return x
```

```python
import functools
import math

import jax
import jax.numpy as jnp
import numpy as np
from jax import lax
from jax.experimental import pallas as pl
from jax.experimental.pallas import tpu as pltpu

F32 = jnp.float32
BF16 = jnp.bfloat16
HIGHEST = lax.Precision.HIGHEST

LANES = 128
SUBLANES = 8
VMEM_LIMIT_BYTES = 40 * 1024 * 1024

D_MODEL = 1024
D_FF = 2816
N_MOD = 9
POOL_WIDTH = 256
POOL_GC = 64
POOL_WINDOWS = (2, 4, 8, 16)
POOL_HALO = 16
HEAD_DIM = 64
NSA_WIDTH = 512
NSA_HEADS = 8
NSA_KV_HEADS = 2
NSA_REP = 4
N_BRANCH = 3
S5_WIDTH = 256
S5_H = 16
S5_GROUPS = 16
S5_P = 64
CMP_STRIDE = 16
CMP_LEN = 32
CMP_HIDDEN = 128
SLC_BLOCK = 64
SLC_SHIFT = 6
N_SELECT = 16
N_FORCED = 3
WINDOW = 512
Q_BLOCK = 256
ROT_DIM = 16
ROPE_THETA = 500000.0
EPS = 1e-6
Q_SCALE = HEAD_DIM ** -0.5 * math.log2(math.e)
NEG_INF = -1e30
MASK_VALUE = -(2.0 ** 100)

OFF_POOL = 0
OFF_KA = OFF_POOL + POOL_WIDTH
OFF_KC = OFF_KA + NSA_KV_HEADS * LANES
OFF_VC = OFF_KC + LANES
OFF_S5 = OFF_VC + LANES
N_COLS = OFF_S5 + S5_WIDTH
GATE_ROWS = 16
ROW_Q = 0
ROW_V = ROW_Q + NSA_HEADS * HEAD_DIM
ROW_GATE = ROW_V + NSA_KV_HEADS * LANES
N_ROWS_T = ROW_GATE + NSA_KV_HEADS * GATE_ROWS

S5_CHUNK = 8
S5_FOLD = S5_CHUNK * S5_WIDTH
S5_STATE = S5_GROUPS * S5_P
S5_HALVES = S5_WIDTH // LANES
S5_HALF_FOLD = S5_FOLD // S5_HALVES
KEY_TILE = 512
V_ROWS = HEAD_DIM + 16


def _dot(a, b):
    return jnp.dot(a, b, preferred_element_type=F32)


def _dot_nt(a, b):
    return lax.dot_general(a, b, (((1,), (1,)), ((), ())), preferred_element_type=F32)


def _sigmoid(x):
    return 1.0 / (1.0 + jnp.exp(-x))


def _gelu_tanh(x):
    return 0.5 * x * (1.0 + jnp.tanh(math.sqrt(2.0 / math.pi) * (x + 0.044715 * (x * x * x))))


def _params(sem):
    return pltpu.CompilerParams(dimension_semantics=sem, vmem_limit_bytes=VMEM_LIMIT_BYTES)


def _const_spec(shape):
    nd = len(shape)
    return pl.BlockSpec(shape, lambda *_: (0,) * nd, pipeline_mode=pl.Buffered(1))


def _layer_spec(shape, layer):
    nd = len(shape)
    return pl.BlockSpec((None,) + shape, lambda *_: (layer,) + (0,) * nd, pipeline_mode=pl.Buffered(1))


def _mod_kernel(ct_ref, w_ref, b_ref, o_ref, *, batch):
    w = w_ref[...]
    rows = []
    for b in range(batch):
        col = ct_ref[:, b:b + 1]
        rows.append(jnp.sum(w * (col * _sigmoid(col)), axis=0, keepdims=True) + b_ref[...])
    o_ref[...] = jnp.concatenate(rows + [jnp.zeros((SUBLANES - batch, w.shape[1]), F32)], axis=0)


def _modulation(c, ada_w, ada_b):
    n_layers, d, n = ada_w.shape
    b = c.shape[0]
    tn = 2304
    out = pl.pallas_call(
        functools.partial(_mod_kernel, batch=b),
        grid=(n_layers, n // tn),
        in_specs=[pl.BlockSpec((d, b), lambda l, j: (0, 0)),
                  pl.BlockSpec((None, d, tn), lambda l, j: (l, 0, j)),
                  pl.BlockSpec((None, 1, tn), lambda l, j: (l, 0, j))],
        out_specs=pl.BlockSpec((None, SUBLANES, tn), lambda l, j: (l, 0, j)),
        out_shape=jax.ShapeDtypeStruct((n_layers, SUBLANES, n), F32),
        compiler_params=_params(("arbitrary", "arbitrary")),
        name="adaln_mod",
    )(c.T, ada_w, ada_b.reshape(n_layers, 1, n))
    return out[:, :b].reshape(n_layers, b, N_MOD, d)


def _norm_modulate(x, gain, mod_ref, first_row):
    ms = jnp.mean(x * x, axis=-1, keepdims=True)
    y = x * lax.rsqrt(ms + EPS) * gain
    return y * (1.0 + mod_ref[first_row + 1:first_row + 2, :]) + mod_ref[first_row:first_row + 1, :]


def _ffn_kernel(x_ref, mod_ref, g_ref, win_ref, wout_ref, o_ref, *, first_row):
    x = x_ref[...]
    h = _norm_modulate(x, g_ref[...], mod_ref, first_row).astype(BF16)
    gu = _dot(h, win_ref[...])
    gate, up = gu[:, :D_FF], gu[:, D_FF:]
    a = (gate * _sigmoid(gate) * up).astype(BF16)
    y = _dot(a, wout_ref[...])
    o_ref[...] = x + 0.5 * mod_ref[first_row + 2:first_row + 3, :] * y


def _mod_spec(layer):
    return pl.BlockSpec((None, None, N_MOD, D_MODEL), lambda i, *_: (layer, i, 0, 0))


def _ffn(x, layer, mod, gain, w_in, w_out, first_row, tm=512):
    b, s, d = x.shape
    return pl.pallas_call(
        functools.partial(_ffn_kernel, first_row=first_row),
        grid=(b, s // tm),
        in_specs=[pl.BlockSpec((None, tm, d), lambda i, j: (i, j, 0)),
                  _mod_spec(layer),
                  _layer_spec((1, d), layer),
                  _layer_spec((d, 2 * D_FF), layer),
                  _layer_spec((D_FF, d), layer)],
        out_specs=pl.BlockSpec((None, tm, d), lambda i, j: (i, j, 0)),
        out_shape=jax.ShapeDtypeStruct((b, s, d), F32),
        compiler_params=_params(("parallel", "parallel")),
        name="ffn_half_step",
    )(x, mod, gain, w_in, w_out)


def _rope(v, cos_t, sin_lo, sin_hi):
    return (v * cos_t + pltpu.roll(v, LANES - ROT_DIM // 2, 1) * sin_lo
            + pltpu.roll(v, ROT_DIM // 2, 1) * sin_hi)


def _inproj_kernel(x_ref, mod_ref, g_ref, w_ref, wt_ref, cos_ref, slo_ref, shi_ref, cost_ref, sint_ref, qg_ref, kg_ref,
                   pool_ref, q_ref, ka_ref, kb_ref, va_ref, vb_ref, kc_ref, vc_ref, gate_ref, s5_ref):
    h = _norm_modulate(x_ref[...], g_ref[...], mod_ref, 3).astype(BF16)
    u = _dot(h, w_ref[...])
    ut = _dot_nt(wt_ref[...], h)
    tm = u.shape[0]
    cos_t, sin_lo, sin_hi = cos_ref[...], slo_ref[...], shi_ref[...]
    lane = lax.broadcasted_iota(jnp.int32, (tm, LANES), 1)
    low_half = lane < HEAD_DIM
    pos = pl.program_id(1) * tm + lax.broadcasted_iota(jnp.int32, (tm, LANES), 0)
    block_one_hot = jnp.where((pos >> SLC_SHIFT) == lane, 1.0, 0.0).astype(BF16)

    pool_ref[...] = u[:, OFF_POOL:OFF_POOL + POOL_WIDTH]
    kc_ref[...] = u[:, OFF_KC:OFF_KC + LANES]
    vc_ref[...] = u[:, OFF_VC:OFF_VC + LANES]
    for half in range(S5_HALVES):
        s5_ref[half] = u[:, OFF_S5 + half * LANES:OFF_S5 + (half + 1) * LANES]
    for g in range(NSA_KV_HEADS):
        v = u[:, OFF_KA + g * LANES:OFF_KA + (g + 1) * LANES]
        sq = v * v
        ms_lo = jnp.sum(jnp.where(low_half, sq, 0.0), axis=-1, keepdims=True) * (1.0 / HEAD_DIM)
        ms_hi = jnp.sum(jnp.where(low_half, 0.0, sq), axis=-1, keepdims=True) * (1.0 / HEAD_DIM)
        r = jnp.where(low_half, lax.rsqrt(ms_lo + EPS), lax.rsqrt(ms_hi + EPS))
        kn = _rope(v * r * kg_ref[...], cos_t, sin_lo, sin_hi)
        ka_ref[g, :, 0:LANES] = kn.astype(BF16)
        ka_ref[g, :, LANES:2 * LANES] = block_one_hot
        kb_ref[g] = pltpu.roll(kn, HEAD_DIM, 1).astype(BF16)

    half_rot = ROT_DIM // 2
    cos8, sin8 = cost_ref[...], sint_ref[...]
    zero_rows = jnp.zeros((LANES - HEAD_DIM, tm), F32)
    for hd in range(NSA_HEADS):
        v = ut[ROW_Q + hd * HEAD_DIM:ROW_Q + (hd + 1) * HEAD_DIM, :]
        ms = jnp.sum(v * v, axis=0, keepdims=True) * (1.0 / HEAD_DIM)
        vn = v * lax.rsqrt(ms + EPS) * qg_ref[...]
        x1, x2 = vn[0:half_rot], vn[half_rot:ROT_DIM]
        roped = jnp.concatenate([x1 * cos8 - x2 * sin8, x2 * cos8 + x1 * sin8, vn[ROT_DIM:]], axis=0)
        q_ref[hd] = jnp.concatenate([roped * Q_SCALE, zero_rows], axis=0).astype(BF16)
    ones = jnp.ones((HEAD_DIM, tm), F32)
    for g in range(NSA_KV_HEADS):
        vt = ut[ROW_V + g * LANES:ROW_V + (g + 1) * LANES, :]
        va = jnp.concatenate([vt[0:HEAD_DIM], ones], axis=0).astype(BF16)
        vb = jnp.concatenate([vt[HEAD_DIM:], ones], axis=0).astype(BF16)
        for blk in range(tm // LANES):
            va_ref[g, blk] = va[:, blk * LANES:(blk + 1) * LANES]
            vb_ref[g, blk] = vb[:, blk * LANES:(blk + 1) * LANES]
        gate_ref[g] = _sigmoid(ut[ROW_GATE + g * GATE_ROWS:ROW_GATE + (g + 1) * GATE_ROWS, :])


def _inproj(x, layer, mod, gain, w_row, w_t, rope_tabs, rope_tabs_t, q_gain, k_gain, tm=1024):
    b, s, d = x.shape
    g = NSA_KV_HEADS
    tok = lambda width: pl.BlockSpec((None, tm, width), lambda i, j: (i, j, 0))
    grp = lambda n, width=LANES: pl.BlockSpec((None, n, tm, width), lambda i, j: (i, 0, j, 0))
    lanes_tok = lambda n, rows: pl.BlockSpec((None, n, rows, tm), lambda i, j: (i, 0, 0, j))
    v_blocks = pl.BlockSpec((None, g, tm // LANES, LANES, LANES), lambda i, j: (i, 0, j, 0, 0))
    tab = pl.BlockSpec((tm, LANES), lambda i, j: (j, 0))
    tab_t = pl.BlockSpec((ROT_DIM // 2, tm), lambda i, j: (0, j))
    sds = jax.ShapeDtypeStruct
    return pl.pallas_call(
        _inproj_kernel,
        grid=(b, s // tm),
        in_specs=[tok(d),
                  _mod_spec(layer),
                  _layer_spec((1, d), layer),
                  _layer_spec((d, N_COLS), layer), _layer_spec((N_ROWS_T, d), layer),
                  tab, tab, tab, tab_t, tab_t,
                  _layer_spec((HEAD_DIM, 1), layer), _layer_spec((1, LANES), layer)],
        out_specs=[tok(POOL_WIDTH), lanes_tok(NSA_HEADS, LANES), grp(g, 2 * LANES), grp(g), v_blocks, v_blocks,
                   tok(LANES), tok(LANES), lanes_tok(g, GATE_ROWS), grp(S5_HALVES)],
        out_shape=[sds((b, s, POOL_WIDTH), F32), sds((b, NSA_HEADS, LANES, s), BF16),
                   sds((b, g, s, 2 * LANES), BF16), sds((b, g, s, LANES), BF16),
                   sds((b, g, s // LANES, LANES, LANES), BF16), sds((b, g, s // LANES, LANES, LANES), BF16),
                   sds((b, s, LANES), F32), sds((b, s, LANES), F32),
                   sds((b, g, GATE_ROWS, s), F32), sds((b, S5_HALVES, s, LANES), F32)],
        compiler_params=_params(("parallel", "parallel")),
        name="mixer_in_proj",
    )(x, mod, gain, w_row, w_t, *rope_tabs, *rope_tabs_t, q_gain, k_gain)


def _fold_rows(ref, n):
    rows = ref.shape[0] // n
    return jnp.concatenate([ref[pl.ds(k, rows, stride=n), :] for k in range(n)], axis=1)


def _unfold_rows(ref, value, n):
    rows = ref.shape[0] // n
    for k in range(n):
        ref[pl.ds(k, rows, stride=n), :] = value[:, k * LANES:(k + 1) * LANES]


def _compress_kernel(kc_ref, vc_ref, w1k_ref, w1v_ref, w1k_raw_ref, w1v_raw_ref, pe_ref, w2k_ref, w2vt_ref,
                     kg_ref, cos_ref, slo_ref, shi_ref, ko_ref, vo_ref, *, n_cmp):
    ncp = kc_ref.shape[0] // CMP_STRIDE
    for src_ref, w1_ref, raw_ref, pe_row, is_key in ((kc_ref, w1k_ref, w1k_raw_ref, 0, True),
                                                    (vc_ref, w1v_ref, w1v_raw_ref, 1, False)):
        chunks = _fold_rows(src_ref, CMP_STRIDE).astype(BF16)
        pe_term = jnp.sum(raw_ref[...] * pe_ref[:, pe_row:pe_row + 1], axis=0, keepdims=True)
        for g in range(NSA_KV_HEADS):
            a = _dot(chunks, w1_ref[g])
            pre = a[:, :CMP_HIDDEN] + pltpu.roll(a[:, CMP_HIDDEN:], ncp - 1, 0) + pe_term
            hidden = _gelu_tanh(pre).astype(BF16)
            if is_key:
                out = _dot(hidden, w2k_ref[...])
                ms = jnp.sum(out * out, axis=-1, keepdims=True) * (1.0 / HEAD_DIM)
                out = _rope(out * lax.rsqrt(ms + EPS) * kg_ref[...], cos_ref[...], slo_ref[...], shi_ref[...])
                real_row = lax.broadcasted_iota(jnp.int32, (ncp, LANES), 0) < n_cmp
                ko_ref[g] = jnp.where(real_row, out, 0.0).astype(BF16)
            else:
                out_t = _dot_nt(w2vt_ref[...], hidden)
                real_col = lax.broadcasted_iota(jnp.int32, (LANES, ncp), 1) < n_cmp
                value_row = lax.broadcasted_iota(jnp.int32, (LANES, ncp), 0) < HEAD_DIM
                vo_ref[g] = jnp.where(value_row, jnp.where(real_col, out_t, 0.0), 1.0).astype(BF16)


def _compress(kc, vc, layer, w1k, w1v, w1k_raw, w1v_raw, pe, w2k, w2v_t, k_gain, cmp_tabs):
    b, s, _ = kc.shape
    ncp = s // CMP_STRIDE
    fold = CMP_STRIDE * LANES
    src = pl.BlockSpec((None, s, LANES), lambda i: (i, 0, 0))
    raw = _layer_spec((CMP_LEN * HEAD_DIM, CMP_HIDDEN), layer)
    w1 = _layer_spec((NSA_KV_HEADS, fold, 2 * CMP_HIDDEN), layer)
    w2 = _layer_spec((CMP_HIDDEN, LANES), layer)
    tab = _const_spec((ncp, LANES))
    return pl.pallas_call(
        functools.partial(_compress_kernel, n_cmp=ncp - 1),
        grid=(b,),
        in_specs=[src, src, w1, w1, raw, raw, _layer_spec((CMP_LEN * HEAD_DIM, 2), layer), w2, w2,
                  _layer_spec((1, LANES), layer), tab, tab, tab],
        out_specs=[pl.BlockSpec((None, NSA_KV_HEADS, ncp, LANES), lambda i: (i, 0, 0, 0)),
                   pl.BlockSpec((None, NSA_KV_HEADS, LANES, ncp), lambda i: (i, 0, 0, 0))],
        out_shape=[jax.ShapeDtypeStruct((b, NSA_KV_HEADS, ncp, LANES), BF16),
                   jax.ShapeDtypeStruct((b, NSA_KV_HEADS, LANES, ncp), BF16)],
        compiler_params=_params(("parallel",)),
        name="nsa_compress",
    )(kc, vc, w1k, w1v, w1k_raw, w1v_raw, pe, w2k, w2v_t, k_gain, *cmp_tabs)


def _selection_bias(imp_t, qb):
    nsb, nq = imp_t.shape
    j = lax.broadcasted_iota(jnp.int32, (nsb, nq), 0)
    t = qb * Q_BLOCK + lax.broadcasted_iota(jnp.int32, (nsb, nq), 1)
    cur = t >> SLC_SHIFT
    valid = j * SLC_BLOCK <= t
    forced = (j == 0) | (j == cur) | (j == cur - 1)
    j_f = j.astype(F32)
    start = jnp.where(valid & jnp.logical_not(forced), imp_t, -1.0)
    vals = start
    for _ in range(N_SELECT - N_FORCED):
        m = jnp.max(vals, axis=0, keepdims=True)
        idx = jnp.min(jnp.where(vals == m, j_f, float(nsb)), axis=0, keepdims=True)
        vals = jnp.where(j_f == idx, -2.0, vals)
    return jnp.where((forced & valid) | (vals != start), 0.0, MASK_VALUE)


def _attn_kernel(q_ref, kc_ref, vct_ref, ka_ref, kb_ref, va_ref, vb_ref, gate_ref, ovl_ref, o_ref,
                 sa_ref, sb_ref, acc_ref, *, seq):
    qb = pl.program_id(2)
    cols = NSA_REP * Q_BLOCK
    ncp = seq // CMP_STRIDE
    q_t = jnp.concatenate([q_ref[r] for r in range(NSA_REP)], axis=1)
    t_col = qb * Q_BLOCK + lax.broadcasted_iota(jnp.int32, (1, cols), 1) % Q_BLOCK
    t_q = qb * Q_BLOCK + lax.broadcasted_iota(jnp.int32, (1, Q_BLOCK), 1)

    def all_heads(per_query):
        return jnp.concatenate([per_query] * NSA_REP, axis=1)

    cmp_end = lax.broadcasted_iota(jnp.int32, (ncp, Q_BLOCK), 0) * CMP_STRIDE + (CMP_LEN - 1)
    s = _dot(kc_ref[...], q_t) + all_heads(jnp.where(cmp_end <= t_q, 0.0, NEG_INF))
    e = jnp.exp2(s - jnp.max(s, axis=0, keepdims=True)).astype(BF16)
    oc = _dot(jnp.concatenate([vct_ref[0:V_ROWS, :], ovl_ref[...]], axis=0), e)
    oc = oc * jnp.where(t_col >= CMP_LEN - 1, 1.0 / oc[HEAD_DIM:HEAD_DIM + 1, :], 0.0)
    o_cmp, imp = oc[0:V_ROWS], oc[V_ROWS:V_ROWS + LANES]
    imp_t = (imp[:, 0:Q_BLOCK] + imp[:, Q_BLOCK:2 * Q_BLOCK]
             + imp[:, 2 * Q_BLOCK:3 * Q_BLOCK] + imp[:, 3 * Q_BLOCK:4 * Q_BLOCK])
    bias = _selection_bias(imp_t, qb).astype(BF16)

    q_aug = jnp.concatenate([q_t, jnp.concatenate([bias] * NSA_REP, axis=1)], axis=0)

    span = WINDOW + Q_BLOCK
    start = pl.multiple_of(jnp.maximum(qb * Q_BLOCK - WINDOW, 0), LANES)
    start_blk = start // LANES
    kpos = start + lax.broadcasted_iota(jnp.int32, (span, Q_BLOCK), 0)
    in_window = (kpos <= t_q) & (kpos > t_q - WINDOW)
    s = _dot(kb_ref[pl.ds(start, span), :], q_t) + all_heads(jnp.where(in_window, 0.0, NEG_INF))
    e = jnp.exp2(s - jnp.max(s, axis=0, keepdims=True)).astype(BF16)
    acc_win = _dot(jnp.concatenate([vb_ref[start_blk + i, 0:V_ROWS, :] for i in range(span // LANES)], axis=1), e)
    o_win = acc_win * (1.0 / acc_win[HEAD_DIM:HEAD_DIM + 1, :])

    blocks_per_tile = KEY_TILE // LANES

    def score_tile(kt, s_ref, keys=KEY_TILE):
        k0 = pl.multiple_of(kt * KEY_TILE, KEY_TILE)
        s_ref[0:keys, :] = _dot(ka_ref[pl.ds(k0, keys), :], q_aug)

    def absorb_tile(kt, s_ref, carry, causal, keys=KEY_TILE):
        m_i, acc = carry
        blk0 = kt * blocks_per_tile
        s = s_ref[0:keys, :]
        if causal:
            kpos = kt * KEY_TILE + lax.broadcasted_iota(jnp.int32, (keys, Q_BLOCK), 0)
            s = s + all_heads(jnp.where(kpos <= t_q, 0.0, MASK_VALUE))
        m_new = jnp.maximum(m_i, jnp.max(s, axis=0, keepdims=True))
        p = jnp.exp2(s - m_new).astype(BF16)
        v_t = jnp.concatenate([va_ref[blk0 + i, 0:V_ROWS, :] for i in range(keys // LANES)], axis=1)
        return m_new, jnp.exp2(m_i - m_new) * acc + _dot(v_t, p)

    def pair_trip(j, carry):
        score_tile(2 * j + 1, sb_ref)
        carry = absorb_tile(2 * j, sa_ref, carry, causal=False)
        score_tile(2 * j + 2, sa_ref)
        return absorb_tile(2 * j + 1, sb_ref, carry, causal=False)

    def double_trip(j, carry):
        return pair_trip(2 * j + 1, pair_trip(2 * j, carry))

    last_pair = qb // (2 * KEY_TILE // Q_BLOCK)
    score_tile(0, sa_ref)
    carry = (jnp.full((1, cols), NEG_INF, F32), jnp.zeros((V_ROWS, cols), F32))
    carry = lax.fori_loop(0, last_pair // 2, double_trip, carry)
    carry = lax.fori_loop(2 * (last_pair // 2), last_pair, pair_trip, carry)
    blocks_per_key_tile = KEY_TILE // Q_BLOCK
    second_tile_live = (qb // blocks_per_key_tile) % 2 == 1
    for second_half_live in (False, True):
        keys = KEY_TILE if second_half_live else Q_BLOCK
        half_matches = (qb % blocks_per_key_tile == blocks_per_key_tile - 1) == second_half_live

        @pl.when(second_tile_live & half_matches)
        def _(keys=keys):
            score_tile(2 * last_pair + 1, sb_ref, keys)
            both = absorb_tile(2 * last_pair, sa_ref, carry, causal=False)
            acc_ref[...] = absorb_tile(2 * last_pair + 1, sb_ref, both, causal=True, keys=keys)[1]

        @pl.when(jnp.logical_not(second_tile_live) & half_matches)
        def _(keys=keys):
            acc_ref[...] = absorb_tile(2 * last_pair, sa_ref, carry, causal=True, keys=keys)[1]

    acc_slc = acc_ref[...]
    o_slc = acc_slc * (1.0 / acc_slc[HEAD_DIM:HEAD_DIM + 1, :])

    gates = gate_ref[...]
    heads = []
    for r in range(NSA_REP):
        cs = slice(r * Q_BLOCK, (r + 1) * Q_BLOCK)
        c = r * N_BRANCH
        heads.append(gates[c:c + 1, :] * o_cmp[0:HEAD_DIM, cs] + gates[c + 1:c + 2, :] * o_slc[0:HEAD_DIM, cs]
                     + gates[c + 2:c + 3, :] * o_win[0:HEAD_DIM, cs])
    o_ref[...] = jnp.concatenate(heads, axis=0).T


def _overlap_matrix(seq):
    ncp, nsb = seq // CMP_STRIDE, seq // SLC_BLOCK
    c_start = np.arange(ncp)[None, :] * CMP_STRIDE
    s_start = np.arange(nsb)[:, None] * SLC_BLOCK
    ovl = np.clip(np.minimum(c_start + CMP_LEN, s_start + SLC_BLOCK) - np.maximum(c_start, s_start), 0, None)
    ovl = ovl.astype(np.float32) / CMP_LEN
    ovl[:, ncp - 1] = 0.0
    return jnp.asarray(np.pad(ovl, ((0, LANES - nsb), (0, 0))))


def _nsa_attention(q_t, k_cmp, v_cmp_t, ka, kb, va_t, vb_t, gates_t):
    b, _, _, s = q_t.shape
    g = NSA_KV_HEADS
    ncp = s // CMP_STRIDE
    assert s // SLC_BLOCK <= LANES
    per_group = lambda *shape: pl.BlockSpec((None, None) + shape, lambda i, j, k: (i, j) + (0,) * len(shape))
    return pl.pallas_call(
        functools.partial(_attn_kernel, seq=s),
        grid=(b, g, s // Q_BLOCK),
        in_specs=[pl.BlockSpec((None, NSA_REP, LANES, Q_BLOCK), lambda i, j, k: (i, j, 0, k)),
                  per_group(ncp, LANES), per_group(LANES, ncp), per_group(s, 2 * LANES), per_group(s, LANES),
                  per_group(s // LANES, LANES, LANES), per_group(s // LANES, LANES, LANES),
                  pl.BlockSpec((None, None, GATE_ROWS, Q_BLOCK), lambda i, j, k: (i, j, 0, k)),
                  _const_spec((LANES, ncp))],
        out_specs=pl.BlockSpec((None, Q_BLOCK, NSA_REP * HEAD_DIM), lambda i, j, k: (i, k, j)),
        out_shape=jax.ShapeDtypeStruct((b, s, NSA_WIDTH), F32),
        scratch_shapes=[pltpu.VMEM((KEY_TILE, NSA_REP * Q_BLOCK), F32)] * 2
        + [pltpu.VMEM((V_ROWS, NSA_REP * Q_BLOCK), F32)],
        compiler_params=_params(("parallel", "parallel", "arbitrary")),
        name="nsa_attention",
    )(q_t, k_cmp, v_cmp_t, ka, kb, va_t, vb_t, gates_t, _overlap_matrix(s).astype(BF16))


def _s5_kernel(u_ref, mt_ref, bc_ref, cc_ref, a1_ref, a2_ref, d_ref, y_ref, h_ref, g_scr, hp_scr):
    @pl.when(pl.program_id(1) == 0)
    def _():
        h_ref[...] = jnp.zeros_like(h_ref)

    halves = range(S5_HALVES)
    u = [_fold_rows(u_ref.at[a], S5_CHUNK) for a in halves]
    ub = [x.astype(BF16) for x in u]
    y_local = [_dot(ub[a], mt_ref[a]) for a in halves]
    for a in halves:
        g_scr[:, a * S5_HALF_FOLD:(a + 1) * S5_HALF_FOLD] = _dot(ub[a], bc_ref[a])
    a1, a2 = a1_ref[...], a2_ref[...]
    n_state = S5_STATE // S5_HALVES

    def swap_re_im(h):
        parts = [h[:, k * n_state:(k + 1) * n_state] for k in range(2 * S5_HALVES)]
        return jnp.concatenate([parts[k ^ 1] for k in range(2 * S5_HALVES)], axis=1)

    def step(i, h):
        hp_scr[pl.ds(i, 1), :] = h
        return a1 * h + a2 * swap_re_im(h) + g_scr[pl.ds(i, 1), :]

    h_ref[...] = lax.fori_loop(0, g_scr.shape[0], step, h_ref[...], unroll=8)
    for a in halves:
        carried = _dot(hp_scr[:, a * S5_HALF_FOLD:(a + 1) * S5_HALF_FOLD].astype(BF16), cc_ref[a])
        _unfold_rows(y_ref.at[a], y_local[a] + carried + u[a] * d_ref[a:a + 1, :], S5_CHUNK)


def _s5_scan(u, layer, mats):
    b, _, s, _ = u.shape
    rows = s // S5_CHUNK
    tc = min(256, rows)
    tile = pl.BlockSpec((None, S5_HALVES, tc * S5_CHUNK, LANES), lambda i, j: (i, 0, j, 0))
    mat = _layer_spec((S5_HALVES, S5_HALF_FOLD, S5_HALF_FOLD), layer)
    row = _layer_spec((1, S5_HALVES * S5_HALF_FOLD), layer)
    return pl.pallas_call(
        _s5_kernel,
        grid=(b, rows // tc),
        in_specs=[tile, mat, mat, mat, row, row, _layer_spec((S5_HALVES, S5_HALF_FOLD), layer)],
        out_specs=tile,
        out_shape=jax.ShapeDtypeStruct((b, S5_HALVES, s, LANES), F32),
        scratch_shapes=[pltpu.VMEM((1, S5_HALVES * S5_HALF_FOLD), F32),
                        pltpu.VMEM((tc, S5_HALVES * S5_HALF_FOLD), F32),
                        pltpu.VMEM((tc, S5_HALVES * S5_HALF_FOLD), F32)],
        compiler_params=_params(("parallel", "arbitrary")),
        name="s5_scan",
    )(u, *mats)


def _s5_matrices(lam_re, lam_im, log_dt, b_re, b_im, c_re, c_im, d_skip):
    t0, ng, nh, npm = S5_CHUNK, S5_GROUPS, S5_H, S5_P
    gh = ng // S5_HALVES
    ein = functools.partial(jnp.einsum, precision=HIGHEST)
    lam = lax.complex(lam_re, lam_im)
    step = jnp.exp(log_dt)[:, None]
    lam_bar = jnp.exp(lam * step)
    b_bar = lax.complex(b_re, b_im) * ((lam_bar - 1.0) / lam)[..., None]
    c_mat = lax.complex(c_re, c_im)
    k = jnp.arange(t0 + 1, dtype=F32)[:, None, None]
    pw = jnp.exp((lam * step)[None] * k)

    def same_group(rows_per_group, cols_per_group):
        r = np.arange(gh * rows_per_group)[:, None] // rows_per_group
        c = np.arange(gh * cols_per_group)[None, :] // cols_per_group
        return jnp.asarray((r == c).astype(np.float32))

    def per_group_blocks(x, rows_per_group, cols_per_group):
        return jnp.tile(x, (1,) * (x.ndim - 1) + (gh,)) * same_group(rows_per_group, cols_per_group)

    kern = jnp.real(ein('ghp,kgp,gpq->kghq', c_mat, pw[:t0], b_bar))
    d_k = per_group_blocks(kern.transpose(0, 1, 3, 2).reshape(t0, S5_HALVES, gh * nh, nh), nh, nh)
    d_k = jnp.concatenate([d_k, jnp.zeros_like(d_k[:1])], axis=0)
    lag = np.arange(t0)[None, :] - np.arange(t0)[:, None]
    mt = d_k[np.where(lag >= 0, lag, t0)]
    mt = mt.transpose(2, 0, 3, 1, 4).reshape(S5_HALVES, S5_HALF_FOLD, S5_HALF_FOLD)
    b_j = (pw[t0 - 1 - np.arange(t0)][..., None] * b_bar[None]).transpose(0, 1, 3, 2)
    b_j = b_j.reshape(t0, S5_HALVES, gh * nh, npm)
    bc = jnp.concatenate([per_group_blocks(jnp.real(b_j), nh, npm), per_group_blocks(jnp.imag(b_j), nh, npm)],
                         axis=-1)
    bc = bc.transpose(1, 0, 2, 3).reshape(S5_HALVES, S5_HALF_FOLD, S5_HALF_FOLD)
    c_i = (c_mat[None] * pw[1:t0 + 1][:, :, None, :]).transpose(1, 3, 0, 2)
    c_i = c_i.reshape(S5_HALVES, gh * npm, t0, nh)
    mask = same_group(npm, nh)[:, None, :]
    cc = jnp.concatenate([jnp.tile(jnp.real(c_i), (1, 1, 1, gh)) * mask,
                          jnp.tile(-jnp.imag(c_i), (1, 1, 1, gh)) * mask], axis=1)
    cc = cc.reshape(S5_HALVES, S5_HALF_FOLD, S5_HALF_FOLD)
    a_chunk = pw[t0].reshape(S5_HALVES, gh * npm)
    a1 = jnp.concatenate([jnp.real(a_chunk), jnp.real(a_chunk)], axis=1).reshape(1, -1)
    a2 = jnp.concatenate([-jnp.imag(a_chunk), jnp.imag(a_chunk)], axis=1).reshape(1, -1)
    d_vec = jnp.tile(d_skip.reshape(S5_HALVES, gh * nh), (1, t0))
    return mt.astype(BF16), bc.astype(BF16), cc.astype(BF16), a1, a2, d_vec


def _rms_gain(y, gain):
    return y * lax.rsqrt(jnp.mean(y * y, axis=-1, keepdims=True) + EPS) * gain


def _outproj_kernel(x_ref, mod_ref, pool_ref, halo_ref, nsa_ref, s5_ref, pw_ref, pb_ref, ps_ref, on_ref,
                    gw_ref, gb_ref, wo_ref, o_ref, buf):
    j = pl.program_id(1)
    tm = x_ref.shape[0]
    v = pool_ref[...]
    buf[0:POOL_HALO, :] = jnp.where(j > 0, halo_ref[...], 0.0)
    buf[POOL_HALO:POOL_HALO + tm, :] = v
    second_group = (lax.broadcasted_iota(jnp.int32, (tm, LANES), 1) >> SLC_SHIFT) == 1
    t1 = (j * tm + 1 + lax.broadcasted_iota(jnp.int32, (tm, 1), 0)).astype(F32)
    pooled = []
    for half in range(POOL_WIDTH // LANES):
        cols = slice(half * LANES, (half + 1) * LANES)
        v_half = v[:, cols]
        run, k, means = v_half, 1, []
        for w in POOL_WINDOWS[2 * half:2 * half + 2]:
            while k < w:
                run = run + buf[POOL_HALO - k:POOL_HALO - k + tm, cols]
                k += 1
            means.append(run / jnp.minimum(t1, float(w)))
        pooled.append(jnp.where(second_group, means[1], means[0]) - v_half)
    y_pool = (_dot(jnp.concatenate(pooled, axis=1).astype(BF16), pw_ref[...]) + pb_ref[...]) * ps_ref[...]

    y = _gelu_tanh(jnp.concatenate([s5_ref[half] for half in range(S5_HALVES)], axis=-1))
    y_s5 = y * _sigmoid(_dot(y.astype(BF16), gw_ref[...]) + gb_ref[...])

    cat = jnp.concatenate(
        [_rms_gain(y_pool, on_ref[:, 0:POOL_WIDTH]),
         _rms_gain(nsa_ref[...], on_ref[:, POOL_WIDTH:POOL_WIDTH + NSA_WIDTH]),
         _rms_gain(y_s5, on_ref[:, POOL_WIDTH + NSA_WIDTH:])], axis=-1).astype(BF16)
    o_ref[...] = x_ref[...] + mod_ref[5:6, :] * _dot(cat, wo_ref[...])


def _outproj(x, layer, mod, u_pool, o_nsa, y_s5, pool_w_bd, pool_b, pool_scale, out_norm, glu_w, glu_b, w_out,
             tm=1024):
    b, s, d = x.shape
    tok = lambda width: pl.BlockSpec((None, tm, width), lambda i, j: (i, j, 0))
    halo_blocks = tm // POOL_HALO
    return pl.pallas_call(
        _outproj_kernel,
        grid=(b, s // tm),
        in_specs=[tok(d),
                  _mod_spec(layer),
                  tok(POOL_WIDTH),
                  pl.BlockSpec((None, POOL_HALO, POOL_WIDTH),
                               lambda i, j: (i, jnp.maximum(j * halo_blocks - 1, 0), 0)),
                  tok(NSA_WIDTH),
                  pl.BlockSpec((None, S5_HALVES, tm, LANES), lambda i, j: (i, 0, j, 0)),
                  _layer_spec((POOL_WIDTH, POOL_WIDTH), layer), _layer_spec((1, POOL_WIDTH), layer),
                  _layer_spec((1, POOL_WIDTH), layer), _layer_spec((1, d), layer),
                  _layer_spec((S5_WIDTH, S5_WIDTH), layer), _layer_spec((1, S5_WIDTH), layer),
                  _layer_spec((d, d), layer)],
        out_specs=tok(d),
        out_shape=jax.ShapeDtypeStruct((b, s, d), F32),
        scratch_shapes=[pltpu.VMEM((POOL_HALO + tm, POOL_WIDTH), F32)],
        compiler_params=_params(("parallel", "arbitrary")),
        name="mixer_out_proj",
    )(x, mod, u_pool, u_pool, o_nsa, y_s5, pool_w_bd, pool_b, pool_scale, out_norm, glu_w, glu_b, w_out)


def _rope_angles(pos):
    half = ROT_DIM // 2
    inv_freq = np.exp(-math.log(ROPE_THETA) * np.arange(half, dtype=np.float64) * (2.0 / ROT_DIM))
    return np.asarray(pos, np.float64)[:, None] * inv_freq[None, :]


def _rope_tables(pos):
    ang = _rope_angles(pos)
    cos, sin = np.cos(ang), np.sin(ang)
    n, half = ang.shape
    rest = HEAD_DIM - ROT_DIM
    cos_t = np.concatenate([cos, cos, np.ones((n, rest))], axis=1)
    sin_lo = np.concatenate([-sin, np.zeros((n, half + rest))], axis=1)
    sin_hi = np.concatenate([np.zeros((n, half)), sin, np.zeros((n, rest))], axis=1)
    return tuple(jnp.asarray(np.tile(t, (1, 2)), F32) for t in (cos_t, sin_lo, sin_hi))


def _pad_lanes(v):
    return jnp.pad(v, [(0, 0)] * (v.ndim - 1) + [(0, LANES - v.shape[-1])])


def _rope_tables_t(pos):
    ang = _rope_angles(pos).T
    return jnp.asarray(np.cos(ang), F32), jnp.asarray(np.sin(ang), F32)


def _arrange_w_in(w):
    d = w.shape[0]
    o1, o2, o3 = POOL_WIDTH, POOL_WIDTH + NSA_WIDTH, POOL_WIDTH + NSA_WIDTH + 6 * LANES
    kv = w[:, o2:o3].reshape(d, 6, NSA_KV_HEADS, HEAD_DIM)
    ka = jnp.concatenate([kv[:, 2], kv[:, 4]], axis=-1).reshape(d, NSA_KV_HEADS * LANES)
    n_gate = NSA_REP * N_BRANCH
    w_row = jnp.concatenate([w[:, :o1], ka, kv[:, 0].reshape(d, LANES), kv[:, 1].reshape(d, LANES),
                             w[:, o3 + NSA_KV_HEADS * n_gate:]], axis=1)
    q_t = w[:, o1:o2].T
    v_t = jnp.concatenate([kv[:, 3], kv[:, 5]], axis=-1).reshape(d, NSA_KV_HEADS * LANES).T
    gate_t = w[:, o3:o3 + NSA_KV_HEADS * n_gate].T.reshape(NSA_KV_HEADS, n_gate, d)
    gate_t = jnp.pad(gate_t, ((0, 0), (0, GATE_ROWS - n_gate), (0, 0))).reshape(NSA_KV_HEADS * GATE_ROWS, d)
    return w_row.astype(BF16), jnp.concatenate([q_t, v_t, gate_t], axis=0).astype(BF16)


def _expand_cmp_w1(w1):
    halves = w1.reshape(2, CMP_STRIDE, HEAD_DIM, CMP_HIDDEN)
    both = jnp.concatenate([halves[0], halves[1]], axis=-1)
    out = []
    for g in range(NSA_KV_HEADS):
        z = jnp.zeros((CMP_STRIDE, NSA_KV_HEADS, HEAD_DIM, 2 * CMP_HIDDEN), F32).at[:, g].set(both)
        out.append(z.reshape(CMP_STRIDE * LANES, 2 * CMP_HIDDEN))
    return jnp.stack(out).astype(BF16)


def _block_diag(w):
    g, c, _ = w.shape
    return jnp.einsum('gcd,gf->gcfd', w, jnp.eye(g, dtype=w.dtype)).reshape(g * c, g * c)


def _prepare_parameters(norm_ffn1, ffn1_w_in, ffn1_w_out, norm_mix, w_in, w_out, out_norm, pool_w, pool_b, pool_scale,
                        q_norm, k_norm, cmp_pe, cmp_k_w1, cmp_k_w2, cmp_v_w1, cmp_v_w2, s5_lam_re, s5_lam_im,
                        s5_log_dt, s5_b_re, s5_b_im, s5_c_re, s5_c_im, s5_d, glu_w, glu_b, norm_ffn2, ffn2_w_in,
                        ffn2_w_out):
    n_layers = norm_mix.shape[0]
    row = lambda v: v.reshape(n_layers, 1, -1)
    w_row, w_t = jax.vmap(_arrange_w_in)(w_in)
    return dict(
        ffn1=(row(norm_ffn1), ffn1_w_in.astype(BF16), ffn1_w_out.astype(BF16)),
        ffn2=(row(norm_ffn2), ffn2_w_in.astype(BF16), ffn2_w_out.astype(BF16)),
        inproj=(row(norm_mix), w_row, w_t, q_norm.reshape(n_layers, HEAD_DIM, 1),
                jnp.concatenate([k_norm[:, 1], k_norm[:, 2]], axis=-1).reshape(n_layers, 1, LANES)),
        compress=(jax.vmap(_expand_cmp_w1)(cmp_k_w1), jax.vmap(_expand_cmp_w1)(cmp_v_w1), cmp_k_w1, cmp_v_w1,
                  jnp.swapaxes(cmp_pe.reshape(n_layers, 2, CMP_LEN * HEAD_DIM), 1, 2),
                  _pad_lanes(cmp_k_w2).astype(BF16),
                  jnp.swapaxes(_pad_lanes(cmp_v_w2), 1, 2).astype(BF16), row(_pad_lanes(k_norm[:, 0]))),
        s5=jax.vmap(_s5_matrices)(s5_lam_re, s5_lam_im, s5_log_dt, s5_b_re, s5_b_im, s5_c_re, s5_c_im, s5_d),
        outproj=(jax.vmap(_block_diag)(pool_w).astype(BF16), row(pool_b), row(pool_scale), row(out_norm),
                 glu_w.astype(BF16), row(glu_b), w_out.astype(BF16)))


def _hybrid_layer(x, layer, mod, params, tabs, tabs_t, cmp_tabs):
    x = _ffn(x, layer, mod, *params["ffn1"], 0)
    u_pool, q_t, ka, kb, va_t, vb_t, kc, vc, gates_t, u_s5 = _inproj(
        x, layer, mod, *params["inproj"][:3], tabs, tabs_t, *params["inproj"][3:])
    k_cmp, v_cmp_t = _compress(kc, vc, layer, *params["compress"], cmp_tabs)
    o_nsa = _nsa_attention(q_t, k_cmp, v_cmp_t, ka, kb, va_t, vb_t, gates_t)
    y_s5 = _s5_scan(u_s5, layer, params["s5"])
    x = _outproj(x, layer, mod, u_pool, o_nsa, y_s5, *params["outproj"])
    return _ffn(x, layer, mod, *params["ffn2"], 6)


def kernel(x, c, ada_w, ada_b, norm_ffn1, ffn1_w_in, ffn1_w_out, norm_mix, w_in, w_out, out_norm, pool_w, pool_b, pool_scale, q_norm, k_norm, cmp_pe, cmp_k_w1, cmp_k_w2, cmp_v_w1, cmp_v_w2, s5_lam_re, s5_lam_im, s5_log_dt, s5_b_re, s5_b_im, s5_c_re, s5_c_im, s5_d, glu_w, glu_b, norm_ffn2, ffn2_w_in, ffn2_w_out):
    seq = x.shape[1]
    assert seq % (KEY_TILE * 4) == 0 and seq >= WINDOW + Q_BLOCK
    mod = _modulation(c, ada_w, ada_b)
    params = _prepare_parameters(norm_ffn1, ffn1_w_in, ffn1_w_out, norm_mix, w_in, w_out, out_norm, pool_w, pool_b,
                                 pool_scale, q_norm, k_norm, cmp_pe, cmp_k_w1, cmp_k_w2, cmp_v_w1, cmp_v_w2,
                                 s5_lam_re, s5_lam_im, s5_log_dt, s5_b_re, s5_b_im, s5_c_re, s5_c_im, s5_d, glu_w,
                                 glu_b, norm_ffn2, ffn2_w_in, ffn2_w_out)
    tabs = _rope_tables(np.arange(seq))
    tabs_t = _rope_tables_t(np.arange(seq))
    cmp_tabs = _rope_tables(np.arange(seq // CMP_STRIDE) * CMP_STRIDE + CMP_LEN - 1)
    for layer in range(ada_w.shape[0]):
        x = _hybrid_layer(x, layer, mod, params, tabs, tabs_t, cmp_tabs)
    return x
```

```python
import functools
import math

import jax
import jax.numpy as jnp
import numpy as np
from jax import lax
from jax.experimental import pallas as pl
from jax.experimental.pallas import tpu as pltpu

F32 = jnp.float32
BF16 = jnp.bfloat16
HIGHEST = lax.Precision.HIGHEST

LANES = 128
SUBLANES = 8
VMEM_LIMIT_BYTES = 40 * 1024 * 1024

D_MODEL = 1024
D_FF = 2816
N_MOD = 9
POOL_WIDTH = 256
POOL_GC = 64
POOL_WINDOWS = (2, 4, 8, 16)
POOL_HALO = 16
HEAD_DIM = 64
NSA_WIDTH = 512
NSA_HEADS = 8
NSA_KV_HEADS = 2
NSA_REP = 4
N_BRANCH = 3
S5_WIDTH = 256
S5_H = 16
S5_GROUPS = 16
S5_P = 64
CMP_STRIDE = 16
CMP_LEN = 32
CMP_HIDDEN = 128
SLC_BLOCK = 64
SLC_SHIFT = 6
N_SELECT = 16
N_FORCED = 3
WINDOW = 512
Q_BLOCK = 256
ROT_DIM = 16
ROPE_THETA = 500000.0
EPS = 1e-6
Q_SCALE = HEAD_DIM ** -0.5 * math.log2(math.e)
NEG_INF = -1e30
MASK_VALUE = -(2.0 ** 100)

OFF_POOL = 0
OFF_KA = OFF_POOL + POOL_WIDTH
OFF_KC = OFF_KA + NSA_KV_HEADS * LANES
OFF_VC = OFF_KC + LANES
OFF_S5 = OFF_VC + LANES
N_COLS = OFF_S5 + S5_WIDTH
GATE_ROWS = 16
ROW_Q = 0
ROW_V = ROW_Q + NSA_HEADS * HEAD_DIM
ROW_GATE = ROW_V + NSA_KV_HEADS * LANES
N_ROWS_T = ROW_GATE + NSA_KV_HEADS * GATE_ROWS

S5_CHUNK = 8
S5_FOLD = S5_CHUNK * S5_WIDTH
S5_STATE = S5_GROUPS * S5_P
S5_HALVES = S5_WIDTH // LANES
S5_HALF_FOLD = S5_FOLD // S5_HALVES
KEY_TILE = 512
V_ROWS = HEAD_DIM + 16


def _dot(a, b):
    return jnp.dot(a, b, preferred_element_type=F32)


def _dot_nt(a, b):
    return lax.dot_general(a, b, (((1,), (1,)), ((), ())), preferred_element_type=F32)


def _sigmoid(x):
    return 1.0 / (1.0 + jnp.exp(-x))


def _gelu_tanh(x):
    return 0.5 * x * (1.0 + jnp.tanh(math.sqrt(2.0 / math.pi) * (x + 0.044715 * (x * x * x))))


def _params(sem):
    return pltpu.CompilerParams(dimension_semantics=sem, vmem_limit_bytes=VMEM_LIMIT_BYTES)


def _const_spec(shape):
    nd = len(shape)
    return pl.BlockSpec(shape, lambda *_: (0,) * nd, pipeline_mode=pl.Buffered(1))


def _layer_spec(shape, layer):
    nd = len(shape)
    return pl.BlockSpec((None,) + shape, lambda *_: (layer,) + (0,) * nd, pipeline_mode=pl.Buffered(1))


def _mod_kernel(ct_ref, w_ref, b_ref, o_ref, *, batch):
    w = w_ref[...]
    rows = []
    for b in range(batch):
        col = ct_ref[:, b:b + 1]
        rows.append(jnp.sum(w * (col * _sigmoid(col)), axis=0, keepdims=True) + b_ref[...])
    o_ref[...] = jnp.concatenate(rows + [jnp.zeros((SUBLANES - batch, w.shape[1]), F32)], axis=0)


def _modulation(c, ada_w, ada_b):
    n_layers, d, n = ada_w.shape
    b = c.shape[0]
    tn = 2304
    out = pl.pallas_call(
        functools.partial(_mod_kernel, batch=b),
        grid=(n_layers, n // tn),
        in_specs=[pl.BlockSpec((d, b), lambda l, j: (0, 0)),
                  pl.BlockSpec((None, d, tn), lambda l, j: (l, 0, j)),
                  pl.BlockSpec((None, 1, tn), lambda l, j: (l, 0, j))],
        out_specs=pl.BlockSpec((None, SUBLANES, tn), lambda l, j: (l, 0, j)),
        out_shape=jax.ShapeDtypeStruct((n_layers, SUBLANES, n), F32),
        compiler_params=_params(("arbitrary", "arbitrary")),
        name="adaln_mod",
    )(c.T, ada_w, ada_b.reshape(n_layers, 1, n))
    return out[:, :b].reshape(n_layers, b, N_MOD, d)


def _norm_modulate(x, gain, mod_ref, first_row):
    ms = jnp.mean(x * x, axis=-1, keepdims=True)
    y = x * lax.rsqrt(ms + EPS) * gain
    return y * (1.0 + mod_ref[first_row + 1:first_row + 2, :]) + mod_ref[first_row:first_row + 1, :]


def _ffn_kernel(x_ref, mod_ref, g_ref, win_ref, wout_ref, o_ref, *, first_row):
    x = x_ref[...]
    h = _norm_modulate(x, g_ref[...], mod_ref, first_row).astype(BF16)
    gu = _dot(h, win_ref[...])
    gate, up = gu[:, :D_FF], gu[:, D_FF:]
    a = (gate * _sigmoid(gate) * up).astype(BF16)
    y = _dot(a, wout_ref[...])
    o_ref[...] = x + 0.5 * mod_ref[first_row + 2:first_row + 3, :] * y


def _mod_spec(layer):
    return pl.BlockSpec((None, None, N_MOD, D_MODEL), lambda i, *_: (layer, i, 0, 0))


def _ffn(x, layer, mod, gain, w_in, w_out, first_row, tm=512):
    b, s, d = x.shape
    return pl.pallas_call(
        functools.partial(_ffn_kernel, first_row=first_row),
        grid=(b, s // tm),
        in_specs=[pl.BlockSpec((None, tm, d), lambda i, j: (i, j, 0)),
                  _mod_spec(layer),
                  _layer_spec((1, d), layer),
                  _layer_spec((d, 2 * D_FF), layer),
                  _layer_spec((D_FF, d), layer)],
        out_specs=pl.BlockSpec((None, tm, d), lambda i, j: (i, j, 0)),
        out_shape=jax.ShapeDtypeStruct((b, s, d), F32),
        compiler_params=_params(("parallel", "parallel")),
        name="ffn_half_step",
    )(x, mod, gain, w_in, w_out)


def _rope(v, cos_t, sin_lo, sin_hi):
    return (v * cos_t + pltpu.roll(v, LANES - ROT_DIM // 2, 1) * sin_lo
            + pltpu.roll(v, ROT_DIM // 2, 1) * sin_hi)


def _inproj_kernel(x_ref, mod_ref, g_ref, w_ref, wt_ref, cos_ref, slo_ref, shi_ref, cost_ref, sint_ref, qg_ref, kg_ref,
                   pool_ref, q_ref, ka_ref, kb_ref, va_ref, vb_ref, kc_ref, vc_ref, gate_ref, s5_ref):
    h = _norm_modulate(x_ref[...], g_ref[...], mod_ref, 3).astype(BF16)
    u = _dot(h, w_ref[...])
    ut = _dot_nt(wt_ref[...], h)
    tm = u.shape[0]
    cos_t, sin_lo, sin_hi = cos_ref[...], slo_ref[...], shi_ref[...]
    lane = lax.broadcasted_iota(jnp.int32, (tm, LANES), 1)
    low_half = lane < HEAD_DIM
    pos = pl.program_id(1) * tm + lax.broadcasted_iota(jnp.int32, (tm, LANES), 0)
    block_one_hot = jnp.where((pos >> SLC_SHIFT) == lane, 1.0, 0.0).astype(BF16)

    pool_ref[...] = u[:, OFF_POOL:OFF_POOL + POOL_WIDTH]
    kc_ref[...] = u[:, OFF_KC:OFF_KC + LANES]
    vc_ref[...] = u[:, OFF_VC:OFF_VC + LANES]
    for half in range(S5_HALVES):
        s5_ref[half] = u[:, OFF_S5 + half * LANES:OFF_S5 + (half + 1) * LANES]
    for g in range(NSA_KV_HEADS):
        v = u[:, OFF_KA + g * LANES:OFF_KA + (g + 1) * LANES]
        sq = v * v
        ms_lo = jnp.sum(jnp.where(low_half, sq, 0.0), axis=-1, keepdims=True) * (1.0 / HEAD_DIM)
        ms_hi = jnp.sum(jnp.where(low_half, 0.0, sq), axis=-1, keepdims=True) * (1.0 / HEAD_DIM)
        r = jnp.where(low_half, lax.rsqrt(ms_lo + EPS), lax.rsqrt(ms_hi + EPS))
        kn = _rope(v * r * kg_ref[...], cos_t, sin_lo, sin_hi)
        ka_ref[g, :, 0:LANES] = kn.astype(BF16)
        ka_ref[g, :, LANES:2 * LANES] = block_one_hot
        kb_ref[g] = pltpu.roll(kn, HEAD_DIM, 1).astype(BF16)

    half_rot = ROT_DIM // 2
    cos8, sin8 = cost_ref[...], sint_ref[...]
    zero_rows = jnp.zeros((LANES - HEAD_DIM, tm), F32)
    for hd in range(NSA_HEADS):
        v = ut[ROW_Q + hd * HEAD_DIM:ROW_Q + (hd + 1) * HEAD_DIM, :]
        ms = jnp.sum(v * v, axis=0, keepdims=True) * (1.0 / HEAD_DIM)
        vn = v * lax.rsqrt(ms + EPS) * qg_ref[...]
        x1, x2 = vn[0:half_rot], vn[half_rot:ROT_DIM]
        roped = jnp.concatenate([x1 * cos8 - x2 * sin8, x2 * cos8 + x1 * sin8, vn[ROT_DIM:]], axis=0)
        q_ref[hd] = jnp.concatenate([roped * Q_SCALE, zero_rows], axis=0).astype(BF16)
    ones = jnp.ones((HEAD_DIM, tm), F32)
    for g in range(NSA_KV_HEADS):
        vt = ut[ROW_V + g * LANES:ROW_V + (g + 1) * LANES, :]
        va = jnp.concatenate([vt[0:HEAD_DIM], ones], axis=0).astype(BF16)
        vb = jnp.concatenate([vt[HEAD_DIM:], ones], axis=0).astype(BF16)
        for blk in range(tm // LANES):
            va_ref[g, blk] = va[:, blk * LANES:(blk + 1) * LANES]
            vb_ref[g, blk] = vb[:, blk * LANES:(blk + 1) * LANES]
        gate_ref[g] = _sigmoid(ut[ROW_GATE + g * GATE_ROWS:ROW_GATE + (g + 1) * GATE_ROWS, :])


def _inproj(x, layer, mod, gain, w_row, w_t, rope_tabs, rope_tabs_t, q_gain, k_gain, tm=1024):
    b, s, d = x.shape
    g = NSA_KV_HEADS
    tok = lambda width: pl.BlockSpec((None, tm, width), lambda i, j: (i, j, 0))
    grp = lambda n, width=LANES: pl.BlockSpec((None, n, tm, width), lambda i, j: (i, 0, j, 0))
    lanes_tok = lambda n, rows: pl.BlockSpec((None, n, rows, tm), lambda i, j: (i, 0, 0, j))
    v_blocks = pl.BlockSpec((None, g, tm // LANES, LANES, LANES), lambda i, j: (i, 0, j, 0, 0))
    tab = pl.BlockSpec((tm, LANES), lambda i, j: (j, 0))
    tab_t = pl.BlockSpec((ROT_DIM // 2, tm), lambda i, j: (0, j))
    sds = jax.ShapeDtypeStruct
    return pl.pallas_call(
        _inproj_kernel,
        grid=(b, s // tm),
        in_specs=[tok(d),
                  _mod_spec(layer),
                  _layer_spec((1, d), layer),
                  _layer_spec((d, N_COLS), layer), _layer_spec((N_ROWS_T, d), layer),
                  tab, tab, tab, tab_t, tab_t,
                  _layer_spec((HEAD_DIM, 1), layer), _layer_spec((1, LANES), layer)],
        out_specs=[tok(POOL_WIDTH), lanes_tok(NSA_HEADS, LANES), grp(g, 2 * LANES), grp(g), v_blocks, v_blocks,
                   tok(LANES), tok(LANES), lanes_tok(g, GATE_ROWS), grp(S5_HALVES)],
        out_shape=[sds((b, s, POOL_WIDTH), F32), sds((b, NSA_HEADS, LANES, s), BF16),
                   sds((b, g, s, 2 * LANES), BF16), sds((b, g, s, LANES), BF16),
                   sds((b, g, s // LANES, LANES, LANES), BF16), sds((b, g, s // LANES, LANES, LANES), BF16),
                   sds((b, s, LANES), F32), sds((b, s, LANES), F32),
                   sds((b, g, GATE_ROWS, s), F32), sds((b, S5_HALVES, s, LANES), F32)],
        compiler_params=_params(("parallel", "parallel")),
        name="mixer_in_proj",
    )(x, mod, gain, w_row, w_t, *rope_tabs, *rope_tabs_t, q_gain, k_gain)


def _fold_rows(ref, n):
    rows = ref.shape[0] // n
    return jnp.concatenate([ref[pl.ds(k, rows, stride=n), :] for k in range(n)], axis=1)


def _unfold_rows(ref, value, n):
    rows = ref.shape[0] // n
    for k in range(n):
        ref[pl.ds(k, rows, stride=n), :] = value[:, k * LANES:(k + 1) * LANES]


def _compress_kernel(kc_ref, vc_ref, w1k_ref, w1v_ref, w1k_raw_ref, w1v_raw_ref, pe_ref, w2k_ref, w2vt_ref,
                     kg_ref, cos_ref, slo_ref, shi_ref, ko_ref, vo_ref, *, n_cmp):
    ncp = kc_ref.shape[0] // CMP_STRIDE
    for src_ref, w1_ref, raw_ref, pe_row, is_key in ((kc_ref, w1k_ref, w1k_raw_ref, 0, True),
                                                    (vc_ref, w1v_ref, w1v_raw_ref, 1, False)):
        chunks = _fold_rows(src_ref, CMP_STRIDE).astype(BF16)
        pe_term = jnp.sum(raw_ref[...] * pe_ref[:, pe_row:pe_row + 1], axis=0, keepdims=True)
        for g in range(NSA_KV_HEADS):
            a = _dot(chunks, w1_ref[g])
            pre = a[:, :CMP_HIDDEN] + pltpu.roll(a[:, CMP_HIDDEN:], ncp - 1, 0) + pe_term
            hidden = _gelu_tanh(pre).astype(BF16)
            if is_key:
                out = _dot(hidden, w2k_ref[...])
                ms = jnp.sum(out * out, axis=-1, keepdims=True) * (1.0 / HEAD_DIM)
                out = _rope(out * lax.rsqrt(ms + EPS) * kg_ref[...], cos_ref[...], slo_ref[...], shi_ref[...])
                real_row = lax.broadcasted_iota(jnp.int32, (ncp, LANES), 0) < n_cmp
                ko_ref[g] = jnp.where(real_row, out, 0.0).astype(BF16)
            else:
                out_t = _dot_nt(w2vt_ref[...], hidden)
                real_col = lax.broadcasted_iota(jnp.int32, (LANES, ncp), 1) < n_cmp
                value_row = lax.broadcasted_iota(jnp.int32, (LANES, ncp), 0) < HEAD_DIM
                vo_ref[g] = jnp.where(value_row, jnp.where(real_col, out_t, 0.0), 1.0).astype(BF16)


def _compress(kc, vc, layer, w1k, w1v, w1k_raw, w1v_raw, pe, w2k, w2v_t, k_gain, cmp_tabs):
    b, s, _ = kc.shape
    ncp = s // CMP_STRIDE
    fold = CMP_STRIDE * LANES
    src = pl.BlockSpec((None, s, LANES), lambda i: (i, 0, 0))
    raw = _layer_spec((CMP_LEN * HEAD_DIM, CMP_HIDDEN), layer)
    w1 = _layer_spec((NSA_KV_HEADS, fold, 2 * CMP_HIDDEN), layer)
    w2 = _layer_spec((CMP_HIDDEN, LANES), layer)
    tab = _const_spec((ncp, LANES))
    return pl.pallas_call(
        functools.partial(_compress_kernel, n_cmp=ncp - 1),
        grid=(b,),
        in_specs=[src, src, w1, w1, raw, raw, _layer_spec((CMP_LEN * HEAD_DIM, 2), layer), w2, w2,
                  _layer_spec((1, LANES), layer), tab, tab, tab],
        out_specs=[pl.BlockSpec((None, NSA_KV_HEADS, ncp, LANES), lambda i: (i, 0, 0, 0)),
                   pl.BlockSpec((None, NSA_KV_HEADS, LANES, ncp), lambda i: (i, 0, 0, 0))],
        out_shape=[jax.ShapeDtypeStruct((b, NSA_KV_HEADS, ncp, LANES), BF16),
                   jax.ShapeDtypeStruct((b, NSA_KV_HEADS, LANES, ncp), BF16)],
        compiler_params=_params(("parallel",)),
        name="nsa_compress",
    )(kc, vc, w1k, w1v, w1k_raw, w1v_raw, pe, w2k, w2v_t, k_gain, *cmp_tabs)


def _selection_bias(imp_t, qb):
    nsb, nq = imp_t.shape
    j = lax.broadcasted_iota(jnp.int32, (nsb, nq), 0)
    t = qb * Q_BLOCK + lax.broadcasted_iota(jnp.int32, (nsb, nq), 1)
    cur = t >> SLC_SHIFT
    valid = j * SLC_BLOCK <= t
    forced = (j == 0) | (j == cur) | (j == cur - 1)
    j_f = j.astype(F32)
    start = jnp.where(valid & jnp.logical_not(forced), imp_t, -1.0)
    vals = start
    for _ in range(N_SELECT - N_FORCED):
        m = jnp.max(vals, axis=0, keepdims=True)
        idx = jnp.min(jnp.where(vals == m, j_f, float(nsb)), axis=0, keepdims=True)
        vals = jnp.where(j_f == idx, -2.0, vals)
    return jnp.where((forced & valid) | (vals != start), 0.0, MASK_VALUE)


def _attn_kernel(q_ref, kc_ref, vct_ref, ka_ref, kb_ref, va_ref, vb_ref, gate_ref, ovl_ref, o_ref,
                 sa_ref, sb_ref, acc_ref, *, seq):
    qb = pl.program_id(2)
    cols = NSA_REP * Q_BLOCK
    ncp = seq // CMP_STRIDE
    q_t = jnp.concatenate([q_ref[r] for r in range(NSA_REP)], axis=1)
    t_col = qb * Q_BLOCK + lax.broadcasted_iota(jnp.int32, (1, cols), 1) % Q_BLOCK
    t_q = qb * Q_BLOCK + lax.broadcasted_iota(jnp.int32, (1, Q_BLOCK), 1)

    def all_heads(per_query):
        return jnp.concatenate([per_query] * NSA_REP, axis=1)

    cmp_end = lax.broadcasted_iota(jnp.int32, (ncp, Q_BLOCK), 0) * CMP_STRIDE + (CMP_LEN - 1)
    s = _dot(kc_ref[...], q_t) + all_heads(jnp.where(cmp_end <= t_q, 0.0, NEG_INF))
    e = jnp.exp2(s - jnp.max(s, axis=0, keepdims=True)).astype(BF16)
    oc = _dot(jnp.concatenate([vct_ref[0:V_ROWS, :], ovl_ref[...]], axis=0), e)
    oc = oc * jnp.where(t_col >= CMP_LEN - 1, 1.0 / oc[HEAD_DIM:HEAD_DIM + 1, :], 0.0)
    o_cmp, imp = oc[0:V_ROWS], oc[V_ROWS:V_ROWS + LANES]
    imp_t = (imp[:, 0:Q_BLOCK] + imp[:, Q_BLOCK:2 * Q_BLOCK]
             + imp[:, 2 * Q_BLOCK:3 * Q_BLOCK] + imp[:, 3 * Q_BLOCK:4 * Q_BLOCK])
    bias = _selection_bias(imp_t, qb).astype(BF16)

    q_aug = jnp.concatenate([q_t, jnp.concatenate([bias] * NSA_REP, axis=1)], axis=0)

    span = WINDOW + Q_BLOCK
    start = pl.multiple_of(jnp.maximum(qb * Q_BLOCK - WINDOW, 0), LANES)
    start_blk = start // LANES
    kpos = start + lax.broadcasted_iota(jnp.int32, (span, Q_BLOCK), 0)
    in_window = (kpos <= t_q) & (kpos > t_q - WINDOW)
    s = _dot(kb_ref[pl.ds(start, span), :], q_t) + all_heads(jnp.where(in_window, 0.0, NEG_INF))
    e = jnp.exp2(s - jnp.max(s, axis=0, keepdims=True)).astype(BF16)
    acc_win = _dot(jnp.concatenate([vb_ref[start_blk + i, 0:V_ROWS, :] for i in range(span // LANES)], axis=1), e)
    o_win = acc_win * (1.0 / acc_win[HEAD_DIM:HEAD_DIM + 1, :])

    blocks_per_tile = KEY_TILE // LANES

    def score_tile(kt, s_ref, keys=KEY_TILE):
        k0 = pl.multiple_of(kt * KEY_TILE, KEY_TILE)
        s_ref[0:keys, :] = _dot(ka_ref[pl.ds(k0, keys), :], q_aug)

    def absorb_tile(kt, s_ref, carry, causal, keys=KEY_TILE):
        m_i, acc = carry
        blk0 = kt * blocks_per_tile
        s = s_ref[0:keys, :]
        if causal:
            kpos = kt * KEY_TILE + lax.broadcasted_iota(jnp.int32, (keys, Q_BLOCK), 0)
            s = s + all_heads(jnp.where(kpos <= t_q, 0.0, MASK_VALUE))
        m_new = jnp.maximum(m_i, jnp.max(s, axis=0, keepdims=True))
        p = jnp.exp2(s - m_new).astype(BF16)
        v_t = jnp.concatenate([va_ref[blk0 + i, 0:V_ROWS, :] for i in range(keys // LANES)], axis=1)
        return m_new, jnp.exp2(m_i - m_new) * acc + _dot(v_t, p)

    def pair_trip(j, carry):
        score_tile(2 * j + 1, sb_ref)
        carry = absorb_tile(2 * j, sa_ref, carry, causal=False)
        score_tile(2 * j + 2, sa_ref)
        return absorb_tile(2 * j + 1, sb_ref, carry, causal=False)

    def double_trip(j, carry):
        return pair_trip(2 * j + 1, pair_trip(2 * j, carry))

    def quad_trip(j, carry):
        return double_trip(2 * j + 1, double_trip(2 * j, carry))

    last_pair = qb // (2 * KEY_TILE // Q_BLOCK)
    score_tile(0, sa_ref)
    carry = (jnp.full((1, cols), NEG_INF, F32), jnp.zeros((V_ROWS, cols), F32))
    carry = lax.fori_loop(0, last_pair // 4, quad_trip, carry)
    carry = lax.fori_loop(2 * (last_pair // 4), last_pair // 2, double_trip, carry)
    carry = lax.fori_loop(2 * (last_pair // 2), last_pair, pair_trip, carry)
    blocks_per_key_tile = KEY_TILE // Q_BLOCK
    second_tile_live = (qb // blocks_per_key_tile) % 2 == 1
    for second_half_live in (False, True):
        keys = KEY_TILE if second_half_live else Q_BLOCK
        half_matches = (qb % blocks_per_key_tile == blocks_per_key_tile - 1) == second_half_live

        @pl.when(second_tile_live & half_matches)
        def _(keys=keys):
            score_tile(2 * last_pair + 1, sb_ref, keys)
            both = absorb_tile(2 * last_pair, sa_ref, carry, causal=False)
            acc_ref[...] = absorb_tile(2 * last_pair + 1, sb_ref, both, causal=True, keys=keys)[1]

        @pl.when(jnp.logical_not(second_tile_live) & half_matches)
        def _(keys=keys):
            acc_ref[...] = absorb_tile(2 * last_pair, sa_ref, carry, causal=True, keys=keys)[1]

    acc_slc = acc_ref[...]
    o_slc = acc_slc * (1.0 / acc_slc[HEAD_DIM:HEAD_DIM + 1, :])

    gates = gate_ref[...]
    heads = []
    for r in range(NSA_REP):
        cs = slice(r * Q_BLOCK, (r + 1) * Q_BLOCK)
        c = r * N_BRANCH
        heads.append(gates[c:c + 1, :] * o_cmp[0:HEAD_DIM, cs] + gates[c + 1:c + 2, :] * o_slc[0:HEAD_DIM, cs]
                     + gates[c + 2:c + 3, :] * o_win[0:HEAD_DIM, cs])
    o_ref[...] = jnp.concatenate(heads, axis=0).T


def _overlap_matrix(seq):
    ncp, nsb = seq // CMP_STRIDE, seq // SLC_BLOCK
    c_start = np.arange(ncp)[None, :] * CMP_STRIDE
    s_start = np.arange(nsb)[:, None] * SLC_BLOCK
    ovl = np.clip(np.minimum(c_start + CMP_LEN, s_start + SLC_BLOCK) - np.maximum(c_start, s_start), 0, None)
    ovl = ovl.astype(np.float32) / CMP_LEN
    ovl[:, ncp - 1] = 0.0
    return jnp.asarray(np.pad(ovl, ((0, LANES - nsb), (0, 0))))


def _nsa_attention(q_t, k_cmp, v_cmp_t, ka, kb, va_t, vb_t, gates_t):
    b, _, _, s = q_t.shape
    g = NSA_KV_HEADS
    ncp = s // CMP_STRIDE
    assert s // SLC_BLOCK <= LANES
    per_group = lambda *shape: pl.BlockSpec((None, None) + shape, lambda i, j, k: (i, j) + (0,) * len(shape))
    return pl.pallas_call(
        functools.partial(_attn_kernel, seq=s),
        grid=(b, g, s // Q_BLOCK),
        in_specs=[pl.BlockSpec((None, NSA_REP, LANES, Q_BLOCK), lambda i, j, k: (i, j, 0, k)),
                  per_group(ncp, LANES), per_group(LANES, ncp), per_group(s, 2 * LANES), per_group(s, LANES),
                  per_group(s // LANES, LANES, LANES), per_group(s // LANES, LANES, LANES),
                  pl.BlockSpec((None, None, GATE_ROWS, Q_BLOCK), lambda i, j, k: (i, j, 0, k)),
                  _const_spec((LANES, ncp))],
        out_specs=pl.BlockSpec((None, Q_BLOCK, NSA_REP * HEAD_DIM), lambda i, j, k: (i, k, j)),
        out_shape=jax.ShapeDtypeStruct((b, s, NSA_WIDTH), F32),
        scratch_shapes=[pltpu.VMEM((KEY_TILE, NSA_REP * Q_BLOCK), F32)] * 2
        + [pltpu.VMEM((V_ROWS, NSA_REP * Q_BLOCK), F32)],
        compiler_params=_params(("parallel", "parallel", "arbitrary")),
        name="nsa_attention",
    )(q_t, k_cmp, v_cmp_t, ka, kb, va_t, vb_t, gates_t, _overlap_matrix(s).astype(BF16))


def _s5_kernel(u_ref, mt_ref, bc_ref, cc_ref, a1_ref, a2_ref, d_ref, y_ref, h_ref, g_scr, hp_scr):
    @pl.when(pl.program_id(1) == 0)
    def _():
        h_ref[...] = jnp.zeros_like(h_ref)

    halves = range(S5_HALVES)
    u = [_fold_rows(u_ref.at[a], S5_CHUNK) for a in halves]
    ub = [x.astype(BF16) for x in u]
    for a in halves:
        g_scr[:, a * S5_HALF_FOLD:(a + 1) * S5_HALF_FOLD] = _dot(ub[a], bc_ref[a])
    y_local = [_dot(ub[a], mt_ref[a]) for a in halves]
    a1, a2 = a1_ref[...], a2_ref[...]
    n_state = S5_STATE // S5_HALVES

    def swap_re_im(h):
        parts = [h[:, k * n_state:(k + 1) * n_state] for k in range(2 * S5_HALVES)]
        return jnp.concatenate([parts[k ^ 1] for k in range(2 * S5_HALVES)], axis=1)

    def step(i, h):
        hp_scr[pl.ds(i, 1), :] = h
        return a1 * h + a2 * swap_re_im(h) + g_scr[pl.ds(i, 1), :]

    h_ref[...] = lax.fori_loop(0, g_scr.shape[0], step, h_ref[...], unroll=True)
    for a in halves:
        carried = _dot(hp_scr[:, a * S5_HALF_FOLD:(a + 1) * S5_HALF_FOLD].astype(BF16), cc_ref[a])
        _unfold_rows(y_ref.at[a], y_local[a] + carried + u[a] * d_ref[a:a + 1, :], S5_CHUNK)


def _s5_scan(u, layer, mats):
    b, _, s, _ = u.shape
    rows = s // S5_CHUNK
    tc = min(256, rows)
    tile = pl.BlockSpec((None, S5_HALVES, tc * S5_CHUNK, LANES), lambda i, j: (i, 0, j, 0))
    mat = _layer_spec((S5_HALVES, S5_HALF_FOLD, S5_HALF_FOLD), layer)
    row = _layer_spec((1, S5_HALVES * S5_HALF_FOLD), layer)
    return pl.pallas_call(
        _s5_kernel,
        grid=(b, rows // tc),
        in_specs=[tile, mat, mat, mat, row, row, _layer_spec((S5_HALVES, S5_HALF_FOLD), layer)],
        out_specs=tile,
        out_shape=jax.ShapeDtypeStruct((b, S5_HALVES, s, LANES), F32),
        scratch_shapes=[pltpu.VMEM((1, S5_HALVES * S5_HALF_FOLD), F32),
                        pltpu.VMEM((tc, S5_HALVES * S5_HALF_FOLD), F32),
                        pltpu.VMEM((tc, S5_HALVES * S5_HALF_FOLD), F32)],
        compiler_params=_params(("parallel", "arbitrary")),
        name="s5_scan",
    )(u, *mats)


def _s5_matrices(lam_re, lam_im, log_dt, b_re, b_im, c_re, c_im, d_skip):
    t0, ng, nh, npm = S5_CHUNK, S5_GROUPS, S5_H, S5_P
    gh = ng // S5_HALVES
    ein = functools.partial(jnp.einsum, precision=HIGHEST)
    lam = lax.complex(lam_re, lam_im)
    step = jnp.exp(log_dt)[:, None]
    lam_bar = jnp.exp(lam * step)
    b_bar = lax.complex(b_re, b_im) * ((lam_bar - 1.0) / lam)[..., None]
    c_mat = lax.complex(c_re, c_im)
    k = jnp.arange(t0 + 1, dtype=F32)[:, None, None]
    pw = jnp.exp((lam * step)[None] * k)

    def same_group(rows_per_group, cols_per_group):
        r = np.arange(gh * rows_per_group)[:, None] // rows_per_group
        c = np.arange(gh * cols_per_group)[None, :] // cols_per_group
        return jnp.asarray((r == c).astype(np.float32))

    def per_group_blocks(x, rows_per_group, cols_per_group):
        return jnp.tile(x, (1,) * (x.ndim - 1) + (gh,)) * same_group(rows_per_group, cols_per_group)

    kern = jnp.real(ein('ghp,kgp,gpq->kghq', c_mat, pw[:t0], b_bar))
    d_k = per_group_blocks(kern.transpose(0, 1, 3, 2).reshape(t0, S5_HALVES, gh * nh, nh), nh, nh)
    d_k = jnp.concatenate([d_k, jnp.zeros_like(d_k[:1])], axis=0)
    lag = np.arange(t0)[None, :] - np.arange(t0)[:, None]
    mt = d_k[np.where(lag >= 0, lag, t0)]
    mt = mt.transpose(2, 0, 3, 1, 4).reshape(S5_HALVES, S5_HALF_FOLD, S5_HALF_FOLD)
    b_j = (pw[t0 - 1 - np.arange(t0)][..., None] * b_bar[None]).transpose(0, 1, 3, 2)
    b_j = b_j.reshape(t0, S5_HALVES, gh * nh, npm)
    bc = jnp.concatenate([per_group_blocks(jnp.real(b_j), nh, npm), per_group_blocks(jnp.imag(b_j), nh, npm)],
                         axis=-1)
    bc = bc.transpose(1, 0, 2, 3).reshape(S5_HALVES, S5_HALF_FOLD, S5_HALF_FOLD)
    c_i = (c_mat[None] * pw[1:t0 + 1][:, :, None, :]).transpose(1, 3, 0, 2)
    c_i = c_i.reshape(S5_HALVES, gh * npm, t0, nh)
    mask = same_group(npm, nh)[:, None, :]
    cc = jnp.concatenate([jnp.tile(jnp.real(c_i), (1, 1, 1, gh)) * mask,
                          jnp.tile(-jnp.imag(c_i), (1, 1, 1, gh)) * mask], axis=1)
    cc = cc.reshape(S5_HALVES, S5_HALF_FOLD, S5_HALF_FOLD)
    a_chunk = pw[t0].reshape(S5_HALVES, gh * npm)
    a1 = jnp.concatenate([jnp.real(a_chunk), jnp.real(a_chunk)], axis=1).reshape(1, -1)
    a2 = jnp.concatenate([-jnp.imag(a_chunk), jnp.imag(a_chunk)], axis=1).reshape(1, -1)
    d_vec = jnp.tile(d_skip.reshape(S5_HALVES, gh * nh), (1, t0))
    return mt.astype(BF16), bc.astype(BF16), cc.astype(BF16), a1, a2, d_vec


def _rms_gain(y, gain):
    return y * lax.rsqrt(jnp.mean(y * y, axis=-1, keepdims=True) + EPS) * gain


def _outproj_kernel(x_ref, mod_ref, pool_ref, halo_ref, nsa_ref, s5_ref, pw_ref, pb_ref, ps_ref, on_ref,
                    gw_ref, gb_ref, wo_ref, o_ref, buf):
    j = pl.program_id(1)
    tm = x_ref.shape[0]
    v = pool_ref[...]
    buf[0:POOL_HALO, :] = jnp.where(j > 0, halo_ref[...], 0.0)
    buf[POOL_HALO:POOL_HALO + tm, :] = v
    second_group = (lax.broadcasted_iota(jnp.int32, (tm, LANES), 1) >> SLC_SHIFT) == 1
    t1 = (j * tm + 1 + lax.broadcasted_iota(jnp.int32, (tm, 1), 0)).astype(F32)
    pooled = []
    for half in range(POOL_WIDTH // LANES):
        cols = slice(half * LANES, (half + 1) * LANES)
        v_half = v[:, cols]
        run, k, means = v_half, 1, []
        for w in POOL_WINDOWS[2 * half:2 * half + 2]:
            while k < w:
                run = run + buf[POOL_HALO - k:POOL_HALO - k + tm, cols]
                k += 1
            means.append(run / jnp.minimum(t1, float(w)))
        pooled.append(jnp.where(second_group, means[1], means[0]) - v_half)
    y_pool = (_dot(jnp.concatenate(pooled, axis=1).astype(BF16), pw_ref[...]) + pb_ref[...]) * ps_ref[...]

    y = _gelu_tanh(jnp.concatenate([s5_ref[half] for half in range(S5_HALVES)], axis=-1))
    y_s5 = y * _sigmoid(_dot(y.astype(BF16), gw_ref[...]) + gb_ref[...])

    cat = jnp.concatenate(
        [_rms_gain(y_pool, on_ref[:, 0:POOL_WIDTH]),
         _rms_gain(nsa_ref[...], on_ref[:, POOL_WIDTH:POOL_WIDTH + NSA_WIDTH]),
         _rms_gain(y_s5, on_ref[:, POOL_WIDTH + NSA_WIDTH:])], axis=-1).astype(BF16)
    o_ref[...] = x_ref[...] + mod_ref[5:6, :] * _dot(cat, wo_ref[...])


def _outproj(x, layer, mod, u_pool, o_nsa, y_s5, pool_w_bd, pool_b, pool_scale, out_norm, glu_w, glu_b, w_out,
             tm=1024):
    b, s, d = x.shape
    tok = lambda width: pl.BlockSpec((None, tm, width), lambda i, j: (i, j, 0))
    halo_blocks = tm // POOL_HALO
    return pl.pallas_call(
        _outproj_kernel,
        grid=(b, s // tm),
        in_specs=[tok(d),
                  _mod_spec(layer),
                  tok(POOL_WIDTH),
                  pl.BlockSpec((None, POOL_HALO, POOL_WIDTH),
                               lambda i, j: (i, jnp.maximum(j * halo_blocks - 1, 0), 0)),
                  tok(NSA_WIDTH),
                  pl.BlockSpec((None, S5_HALVES, tm, LANES), lambda i, j: (i, 0, j, 0)),
                  _layer_spec((POOL_WIDTH, POOL_WIDTH), layer), _layer_spec((1, POOL_WIDTH), layer),
                  _layer_spec((1, POOL_WIDTH), layer), _layer_spec((1, d), layer),
                  _layer_spec((S5_WIDTH, S5_WIDTH), layer), _layer_spec((1, S5_WIDTH), layer),
                  _layer_spec((d, d), layer)],
        out_specs=tok(d),
        out_shape=jax.ShapeDtypeStruct((b, s, d), F32),
        scratch_shapes=[pltpu.VMEM((POOL_HALO + tm, POOL_WIDTH), F32)],
        compiler_params=_params(("parallel", "arbitrary")),
        name="mixer_out_proj",
    )(x, mod, u_pool, u_pool, o_nsa, y_s5, pool_w_bd, pool_b, pool_scale, out_norm, glu_w, glu_b, w_out)


def _rope_angles(pos):
    half = ROT_DIM // 2
    inv_freq = np.exp(-math.log(ROPE_THETA) * np.arange(half, dtype=np.float64) * (2.0 / ROT_DIM))
    return np.asarray(pos, np.float64)[:, None] * inv_freq[None, :]


def _rope_tables(pos):
    ang = _rope_angles(pos)
    cos, sin = np.cos(ang), np.sin(ang)
    n, half = ang.shape
    rest = HEAD_DIM - ROT_DIM
    cos_t = np.concatenate([cos, cos, np.ones((n, rest))], axis=1)
    sin_lo = np.concatenate([-sin, np.zeros((n, half + rest))], axis=1)
    sin_hi = np.concatenate([np.zeros((n, half)), sin, np.zeros((n, rest))], axis=1)
    return tuple(jnp.asarray(np.tile(t, (1, 2)), F32) for t in (cos_t, sin_lo, sin_hi))


def _pad_lanes(v):
    return jnp.pad(v, [(0, 0)] * (v.ndim - 1) + [(0, LANES - v.shape[-1])])


def _rope_tables_t(pos):
    ang = _rope_angles(pos).T
    return jnp.asarray(np.cos(ang), F32), jnp.asarray(np.sin(ang), F32)


def _arrange_w_in(w):
    d = w.shape[0]
    o1, o2, o3 = POOL_WIDTH, POOL_WIDTH + NSA_WIDTH, POOL_WIDTH + NSA_WIDTH + 6 * LANES
    kv = w[:, o2:o3].reshape(d, 6, NSA_KV_HEADS, HEAD_DIM)
    ka = jnp.concatenate([kv[:, 2], kv[:, 4]], axis=-1).reshape(d, NSA_KV_HEADS * LANES)
    n_gate = NSA_REP * N_BRANCH
    w_row = jnp.concatenate([w[:, :o1], ka, kv[:, 0].reshape(d, LANES), kv[:, 1].reshape(d, LANES),
                             w[:, o3 + NSA_KV_HEADS * n_gate:]], axis=1)
    q_t = w[:, o1:o2].T
    v_t = jnp.concatenate([kv[:, 3], kv[:, 5]], axis=-1).reshape(d, NSA_KV_HEADS * LANES).T
    gate_t = w[:, o3:o3 + NSA_KV_HEADS * n_gate].T.reshape(NSA_KV_HEADS, n_gate, d)
    gate_t = jnp.pad(gate_t, ((0, 0), (0, GATE_ROWS - n_gate), (0, 0))).reshape(NSA_KV_HEADS * GATE_ROWS, d)
    return w_row.astype(BF16), jnp.concatenate([q_t, v_t, gate_t], axis=0).astype(BF16)


def _expand_cmp_w1(w1):
    halves = w1.reshape(2, CMP_STRIDE, HEAD_DIM, CMP_HIDDEN)
    both = jnp.concatenate([halves[0], halves[1]], axis=-1)
    out = []
    for g in range(NSA_KV_HEADS):
        z = jnp.zeros((CMP_STRIDE, NSA_KV_HEADS, HEAD_DIM, 2 * CMP_HIDDEN), F32).at[:, g].set(both)
        out.append(z.reshape(CMP_STRIDE * LANES, 2 * CMP_HIDDEN))
    return jnp.stack(out).astype(BF16)


def _block_diag(w):
    g, c, _ = w.shape
    return jnp.einsum('gcd,gf->gcfd', w, jnp.eye(g, dtype=w.dtype)).reshape(g * c, g * c)


def _prepare_parameters(norm_ffn1, ffn1_w_in, ffn1_w_out, norm_mix, w_in, w_out, out_norm, pool_w, pool_b, pool_scale,
                        q_norm, k_norm, cmp_pe, cmp_k_w1, cmp_k_w2, cmp_v_w1, cmp_v_w2, s5_lam_re, s5_lam_im,
                        s5_log_dt, s5_b_re, s5_b_im, s5_c_re, s5_c_im, s5_d, glu_w, glu_b, norm_ffn2, ffn2_w_in,
                        ffn2_w_out):
    n_layers = norm_mix.shape[0]
    row = lambda v: v.reshape(n_layers, 1, -1)
    w_row, w_t = jax.vmap(_arrange_w_in)(w_in)
    return dict(
        ffn1=(row(norm_ffn1), ffn1_w_in.astype(BF16), ffn1_w_out.astype(BF16)),
        ffn2=(row(norm_ffn2), ffn2_w_in.astype(BF16), ffn2_w_out.astype(BF16)),
        inproj=(row(norm_mix), w_row, w_t, q_norm.reshape(n_layers, HEAD_DIM, 1),
                jnp.concatenate([k_norm[:, 1], k_norm[:, 2]], axis=-1).reshape(n_layers, 1, LANES)),
        compress=(jax.vmap(_expand_cmp_w1)(cmp_k_w1), jax.vmap(_expand_cmp_w1)(cmp_v_w1), cmp_k_w1, cmp_v_w1,
                  jnp.swapaxes(cmp_pe.reshape(n_layers, 2, CMP_LEN * HEAD_DIM), 1, 2),
                  _pad_lanes(cmp_k_w2).astype(BF16),
                  jnp.swapaxes(_pad_lanes(cmp_v_w2), 1, 2).astype(BF16), row(_pad_lanes(k_norm[:, 0]))),
        s5=jax.vmap(_s5_matrices)(s5_lam_re, s5_lam_im, s5_log_dt, s5_b_re, s5_b_im, s5_c_re, s5_c_im, s5_d),
        outproj=(jax.vmap(_block_diag)(pool_w).astype(BF16), row(pool_b), row(pool_scale), row(out_norm),
                 glu_w.astype(BF16), row(glu_b), w_out.astype(BF16)))


def _hybrid_layer(x, layer, mod, params, tabs, tabs_t, cmp_tabs):
    x = _ffn(x, layer, mod, *params["ffn1"], 0)
    u_pool, q_t, ka, kb, va_t, vb_t, kc, vc, gates_t, u_s5 = _inproj(
        x, layer, mod, *params["inproj"][:3], tabs, tabs_t, *params["inproj"][3:])
    k_cmp, v_cmp_t = _compress(kc, vc, layer, *params["compress"], cmp_tabs)
    o_nsa = _nsa_attention(q_t, k_cmp, v_cmp_t, ka, kb, va_t, vb_t, gates_t)
    y_s5 = _s5_scan(u_s5, layer, params["s5"])
    x = _outproj(x, layer, mod, u_pool, o_nsa, y_s5, *params["outproj"])
    return _ffn(x, layer, mod, *params["ffn2"], 6)


def kernel(x, c, ada_w, ada_b, norm_ffn1, ffn1_w_in, ffn1_w_out, norm_mix, w_in, w_out, out_norm, pool_w, pool_b, pool_scale, q_norm, k_norm, cmp_pe, cmp_k_w1, cmp_k_w2, cmp_v_w1, cmp_v_w2, s5_lam_re, s5_lam_im, s5_log_dt, s5_b_re, s5_b_im, s5_c_re, s5_c_im, s5_d, glu_w, glu_b, norm_ffn2, ffn2_w_in, ffn2_w_out):
    seq = x.shape[1]
    assert seq % (KEY_TILE * 4) == 0 and seq >= WINDOW + Q_BLOCK
    mod = _modulation(c, ada_w, ada_b)
    params = _prepare_parameters(norm_ffn1, ffn1_w_in, ffn1_w_out, norm_mix, w_in, w_out, out_norm, pool_w, pool_b,
                                 pool_scale, q_norm, k_norm, cmp_pe, cmp_k_w1, cmp_k_w2, cmp_v_w1, cmp_v_w2,
                                 s5_lam_re, s5_lam_im, s5_log_dt, s5_b_re, s5_b_im, s5_c_re, s5_c_im, s5_d, glu_w,
                                 glu_b, norm_ffn2, ffn2_w_in, ffn2_w_out)
    tabs = _rope_tables(np.arange(seq))
    tabs_t = _rope_tables_t(np.arange(seq))
    cmp_tabs = _rope_tables(np.arange(seq // CMP_STRIDE) * CMP_STRIDE + CMP_LEN - 1)
    for layer in range(ada_w.shape[0]):
        x = _hybrid_layer(x, layer, mod, params, tabs, tabs_t, cmp_tabs)
    return x
```

```python
import functools
import math

import jax
import jax.numpy as jnp
import numpy as np
from jax import lax
from jax.experimental import pallas as pl
from jax.experimental.pallas import tpu as pltpu

F32 = jnp.float32
BF16 = jnp.bfloat16
HIGHEST = lax.Precision.HIGHEST

LANES = 128
SUBLANES = 8
VMEM_LIMIT_BYTES = 40 * 1024 * 1024

D_MODEL = 1024
D_FF = 2816
N_MOD = 9
POOL_WIDTH = 256
POOL_GC = 64
POOL_WINDOWS = (2, 4, 8, 16)
POOL_HALO = 16
HEAD_DIM = 64
NSA_WIDTH = 512
NSA_HEADS = 8
NSA_KV_HEADS = 2
NSA_REP = 4
N_BRANCH = 3
S5_WIDTH = 256
S5_H = 16
S5_GROUPS = 16
S5_P = 64
CMP_STRIDE = 16
CMP_LEN = 32
CMP_HIDDEN = 128
SLC_BLOCK = 64
SLC_SHIFT = 6
N_SELECT = 16
N_FORCED = 3
WINDOW = 512
Q_BLOCK = 256
ROT_DIM = 16
ROPE_THETA = 500000.0
EPS = 1e-6
Q_SCALE = HEAD_DIM ** -0.5 * math.log2(math.e)
NEG_INF = -1e30
MASK_VALUE = -(2.0 ** 100)

OFF_POOL = 0
OFF_KA = OFF_POOL + POOL_WIDTH
OFF_KC = OFF_KA + NSA_KV_HEADS * LANES
OFF_VC = OFF_KC + LANES
OFF_S5 = OFF_VC + LANES
N_COLS = OFF_S5 + S5_WIDTH
GATE_ROWS = 16
ROW_Q = 0
ROW_V = ROW_Q + NSA_HEADS * HEAD_DIM
ROW_GATE = ROW_V + NSA_KV_HEADS * LANES
N_ROWS_T = ROW_GATE + NSA_KV_HEADS * GATE_ROWS

S5_CHUNK = 8
S5_FOLD = S5_CHUNK * S5_WIDTH
S5_STATE = S5_GROUPS * S5_P
S5_HALVES = S5_WIDTH // LANES
S5_HALF_FOLD = S5_FOLD // S5_HALVES
KEY_TILE = 512
V_ROWS = HEAD_DIM + 16


def _dot(a, b):
    return jnp.dot(a, b, preferred_element_type=F32)


def _dot_nt(a, b):
    return lax.dot_general(a, b, (((1,), (1,)), ((), ())), preferred_element_type=F32)


def _sigmoid(x):
    return 1.0 / (1.0 + jnp.exp(-x))


def _gelu_tanh(x):
    return 0.5 * x * (1.0 + jnp.tanh(math.sqrt(2.0 / math.pi) * (x + 0.044715 * (x * x * x))))


def _params(sem):
    return pltpu.CompilerParams(dimension_semantics=sem, vmem_limit_bytes=VMEM_LIMIT_BYTES)


def _const_spec(shape):
    nd = len(shape)
    return pl.BlockSpec(shape, lambda *_: (0,) * nd, pipeline_mode=pl.Buffered(1))


def _layer_spec(shape, layer):
    nd = len(shape)
    return pl.BlockSpec((None,) + shape, lambda *_: (layer,) + (0,) * nd, pipeline_mode=pl.Buffered(1))


def _mod_kernel(ct_ref, w_ref, b_ref, o_ref, *, batch):
    w = w_ref[...]
    rows = []
    for b in range(batch):
        col = ct_ref[:, b:b + 1]
        rows.append(jnp.sum(w * (col * _sigmoid(col)), axis=0, keepdims=True) + b_ref[...])
    o_ref[...] = jnp.concatenate(rows + [jnp.zeros((SUBLANES - batch, w.shape[1]), F32)], axis=0)


def _modulation(c, ada_w, ada_b):
    n_layers, d, n = ada_w.shape
    b = c.shape[0]
    tn = 2304
    out = pl.pallas_call(
        functools.partial(_mod_kernel, batch=b),
        grid=(n_layers, n // tn),
        in_specs=[pl.BlockSpec((d, b), lambda l, j: (0, 0)),
                  pl.BlockSpec((None, d, tn), lambda l, j: (l, 0, j)),
                  pl.BlockSpec((None, 1, tn), lambda l, j: (l, 0, j))],
        out_specs=pl.BlockSpec((None, SUBLANES, tn), lambda l, j: (l, 0, j)),
        out_shape=jax.ShapeDtypeStruct((n_layers, SUBLANES, n), F32),
        compiler_params=_params(("arbitrary", "arbitrary")),
        name="adaln_mod",
    )(c.T, ada_w, ada_b.reshape(n_layers, 1, n))
    return out[:, :b].reshape(n_layers, b, N_MOD, d)


def _norm_modulate(x, gain, mod_ref, first_row):
    ms = jnp.mean(x * x, axis=-1, keepdims=True)
    y = x * lax.rsqrt(ms + EPS) * gain
    return y * (1.0 + mod_ref[first_row + 1:first_row + 2, :]) + mod_ref[first_row:first_row + 1, :]


def _ffn_kernel(x_ref, mod_ref, g_ref, win_ref, wout_ref, o_ref, *, first_row):
    x = x_ref[...]
    h = _norm_modulate(x, g_ref[...], mod_ref, first_row).astype(BF16)
    gu = _dot(h, win_ref[...])
    gate, up = gu[:, :D_FF], gu[:, D_FF:]
    a = (gate * _sigmoid(gate) * up).astype(BF16)
    y = _dot(a, wout_ref[...])
    o_ref[...] = x + 0.5 * mod_ref[first_row + 2:first_row + 3, :] * y


def _mod_spec(layer):
    return pl.BlockSpec((None, None, N_MOD, D_MODEL), lambda i, *_: (layer, i, 0, 0))


def _ffn(x, layer, mod, gain, w_in, w_out, first_row, tm=512):
    b, s, d = x.shape
    return pl.pallas_call(
        functools.partial(_ffn_kernel, first_row=first_row),
        grid=(b, s // tm),
        in_specs=[pl.BlockSpec((None, tm, d), lambda i, j: (i, j, 0)),
                  _mod_spec(layer),
                  _layer_spec((1, d), layer),
                  _layer_spec((d, 2 * D_FF), layer),
                  _layer_spec((D_FF, d), layer)],
        out_specs=pl.BlockSpec((None, tm, d), lambda i, j: (i, j, 0)),
        out_shape=jax.ShapeDtypeStruct((b, s, d), F32),
        compiler_params=_params(("parallel", "parallel")),
        name="ffn_half_step",
    )(x, mod, gain, w_in, w_out)


def _rope(v, cos_t, sin_lo, sin_hi):
    return (v * cos_t + pltpu.roll(v, LANES - ROT_DIM // 2, 1) * sin_lo
            + pltpu.roll(v, ROT_DIM // 2, 1) * sin_hi)


def _inproj_kernel(x_ref, mod_ref, g_ref, w_ref, wt_ref, cos_ref, slo_ref, shi_ref, cost_ref, sint_ref, qg_ref, kg_ref,
                   pool_ref, q_ref, ka_ref, kb_ref, va_ref, vb_ref, kc_ref, vc_ref, gate_ref, s5_ref):
    h = _norm_modulate(x_ref[...], g_ref[...], mod_ref, 3).astype(BF16)
    u = _dot(h, w_ref[...])
    ut = _dot_nt(wt_ref[...], h)
    tm = u.shape[0]
    cos_t, sin_lo, sin_hi = cos_ref[...], slo_ref[...], shi_ref[...]
    lane = lax.broadcasted_iota(jnp.int32, (tm, LANES), 1)
    low_half = lane < HEAD_DIM
    pos = pl.program_id(1) * tm + lax.broadcasted_iota(jnp.int32, (tm, LANES), 0)
    block_one_hot = jnp.where((pos >> SLC_SHIFT) == lane, 1.0, 0.0).astype(BF16)

    pool_ref[...] = u[:, OFF_POOL:OFF_POOL + POOL_WIDTH]
    kc_ref[...] = u[:, OFF_KC:OFF_KC + LANES]
    vc_ref[...] = u[:, OFF_VC:OFF_VC + LANES]
    for half in range(S5_HALVES):
        s5_ref[half] = u[:, OFF_S5 + half * LANES:OFF_S5 + (half + 1) * LANES]
    for g in range(NSA_KV_HEADS):
        v = u[:, OFF_KA + g * LANES:OFF_KA + (g + 1) * LANES]
        sq = v * v
        ms_lo = jnp.sum(jnp.where(low_half, sq, 0.0), axis=-1, keepdims=True) * (1.0 / HEAD_DIM)
        ms_hi = jnp.sum(jnp.where(low_half, 0.0, sq), axis=-1, keepdims=True) * (1.0 / HEAD_DIM)
        r = jnp.where(low_half, lax.rsqrt(ms_lo + EPS), lax.rsqrt(ms_hi + EPS))
        kn = _rope(v * r * kg_ref[...], cos_t, sin_lo, sin_hi)
        ka_ref[g, :, 0:LANES] = kn.astype(BF16)
        ka_ref[g, :, LANES:2 * LANES] = block_one_hot
        kb_ref[g] = pltpu.roll(kn, HEAD_DIM, 1).astype(BF16)

    half_rot = ROT_DIM // 2
    cos8, sin8 = cost_ref[...], sint_ref[...]
    zero_rows = jnp.zeros((LANES - HEAD_DIM, tm), F32)
    for hd in range(NSA_HEADS):
        v = ut[ROW_Q + hd * HEAD_DIM:ROW_Q + (hd + 1) * HEAD_DIM, :]
        ms = jnp.sum(v * v, axis=0, keepdims=True) * (1.0 / HEAD_DIM)
        vn = v * lax.rsqrt(ms + EPS) * qg_ref[...]
        x1, x2 = vn[0:half_rot], vn[half_rot:ROT_DIM]
        roped = jnp.concatenate([x1 * cos8 - x2 * sin8, x2 * cos8 + x1 * sin8, vn[ROT_DIM:]], axis=0)
        q_ref[hd] = jnp.concatenate([roped * Q_SCALE, zero_rows], axis=0).astype(BF16)
    ones = jnp.ones((HEAD_DIM, tm), F32)
    for g in range(NSA_KV_HEADS):
        vt = ut[ROW_V + g * LANES:ROW_V + (g + 1) * LANES, :]
        va = jnp.concatenate([vt[0:HEAD_DIM], ones], axis=0).astype(BF16)
        vb = jnp.concatenate([vt[HEAD_DIM:], ones], axis=0).astype(BF16)
        for blk in range(tm // LANES):
            va_ref[g, blk] = va[:, blk * LANES:(blk + 1) * LANES]
            vb_ref[g, blk] = vb[:, blk * LANES:(blk + 1) * LANES]
        gate_ref[g] = _sigmoid(ut[ROW_GATE + g * GATE_ROWS:ROW_GATE + (g + 1) * GATE_ROWS, :])


def _inproj(x, layer, mod, gain, w_row, w_t, rope_tabs, rope_tabs_t, q_gain, k_gain, tm=1024):
    b, s, d = x.shape
    g = NSA_KV_HEADS
    tok = lambda width: pl.BlockSpec((None, tm, width), lambda i, j: (i, j, 0))
    grp = lambda n, width=LANES: pl.BlockSpec((None, n, tm, width), lambda i, j: (i, 0, j, 0))
    lanes_tok = lambda n, rows: pl.BlockSpec((None, n, rows, tm), lambda i, j: (i, 0, 0, j))
    v_blocks = pl.BlockSpec((None, g, tm // LANES, LANES, LANES), lambda i, j: (i, 0, j, 0, 0))
    tab = pl.BlockSpec((tm, LANES), lambda i, j: (j, 0))
    tab_t = pl.BlockSpec((ROT_DIM // 2, tm), lambda i, j: (0, j))
    sds = jax.ShapeDtypeStruct
    return pl.pallas_call(
        _inproj_kernel,
        grid=(b, s // tm),
        in_specs=[tok(d),
                  _mod_spec(layer),
                  _layer_spec((1, d), layer),
                  _layer_spec((d, N_COLS), layer), _layer_spec((N_ROWS_T, d), layer),
                  tab, tab, tab, tab_t, tab_t,
                  _layer_spec((HEAD_DIM, 1), layer), _layer_spec((1, LANES), layer)],
        out_specs=[tok(POOL_WIDTH), lanes_tok(NSA_HEADS, LANES), grp(g, 2 * LANES), grp(g), v_blocks, v_blocks,
                   tok(LANES), tok(LANES), lanes_tok(g, GATE_ROWS), grp(S5_HALVES)],
        out_shape=[sds((b, s, POOL_WIDTH), F32), sds((b, NSA_HEADS, LANES, s), BF16),
                   sds((b, g, s, 2 * LANES), BF16), sds((b, g, s, LANES), BF16),
                   sds((b, g, s // LANES, LANES, LANES), BF16), sds((b, g, s // LANES, LANES, LANES), BF16),
                   sds((b, s, LANES), F32), sds((b, s, LANES), F32),
                   sds((b, g, GATE_ROWS, s), F32), sds((b, S5_HALVES, s, LANES), F32)],
        compiler_params=_params(("parallel", "parallel")),
        name="mixer_in_proj",
    )(x, mod, gain, w_row, w_t, *rope_tabs, *rope_tabs_t, q_gain, k_gain)


def _fold_rows(ref, n):
    rows = ref.shape[0] // n
    return jnp.concatenate([ref[pl.ds(k, rows, stride=n), :] for k in range(n)], axis=1)


def _unfold_rows(ref, value, n):
    rows = ref.shape[0] // n
    for k in range(n):
        ref[pl.ds(k, rows, stride=n), :] = value[:, k * LANES:(k + 1) * LANES]


def _compress_kernel(kc_ref, vc_ref, w1k_ref, w1v_ref, w1k_raw_ref, w1v_raw_ref, pe_ref, w2k_ref, w2vt_ref,
                     kg_ref, cos_ref, slo_ref, shi_ref, ko_ref, vo_ref, *, n_cmp):
    ncp = kc_ref.shape[0] // CMP_STRIDE
    for src_ref, w1_ref, raw_ref, pe_row, is_key in ((kc_ref, w1k_ref, w1k_raw_ref, 0, True),
                                                    (vc_ref, w1v_ref, w1v_raw_ref, 1, False)):
        chunks = _fold_rows(src_ref, CMP_STRIDE).astype(BF16)
        pe_term = jnp.sum(raw_ref[...] * pe_ref[:, pe_row:pe_row + 1], axis=0, keepdims=True)
        for g in range(NSA_KV_HEADS):
            a = _dot(chunks, w1_ref[g])
            pre = a[:, :CMP_HIDDEN] + pltpu.roll(a[:, CMP_HIDDEN:], ncp - 1, 0) + pe_term
            hidden = _gelu_tanh(pre).astype(BF16)
            if is_key:
                out = _dot(hidden, w2k_ref[...])
                ms = jnp.sum(out * out, axis=-1, keepdims=True) * (1.0 / HEAD_DIM)
                out = _rope(out * lax.rsqrt(ms + EPS) * kg_ref[...], cos_ref[...], slo_ref[...], shi_ref[...])
                real_row = lax.broadcasted_iota(jnp.int32, (ncp, LANES), 0) < n_cmp
                ko_ref[g] = jnp.where(real_row, out, 0.0).astype(BF16)
            else:
                out_t = _dot_nt(w2vt_ref[...], hidden)
                real_col = lax.broadcasted_iota(jnp.int32, (LANES, ncp), 1) < n_cmp
                value_row = lax.broadcasted_iota(jnp.int32, (LANES, ncp), 0) < HEAD_DIM
                vo_ref[g] = jnp.where(value_row, jnp.where(real_col, out_t, 0.0), 1.0).astype(BF16)


def _compress(kc, vc, layer, w1k, w1v, w1k_raw, w1v_raw, pe, w2k, w2v_t, k_gain, cmp_tabs):
    b, s, _ = kc.shape
    ncp = s // CMP_STRIDE
    fold = CMP_STRIDE * LANES
    src = pl.BlockSpec((None, s, LANES), lambda i: (i, 0, 0))
    raw = _layer_spec((CMP_LEN * HEAD_DIM, CMP_HIDDEN), layer)
    w1 = _layer_spec((NSA_KV_HEADS, fold, 2 * CMP_HIDDEN), layer)
    w2 = _layer_spec((CMP_HIDDEN, LANES), layer)
    tab = _const_spec((ncp, LANES))
    return pl.pallas_call(
        functools.partial(_compress_kernel, n_cmp=ncp - 1),
        grid=(b,),
        in_specs=[src, src, w1, w1, raw, raw, _layer_spec((CMP_LEN * HEAD_DIM, 2), layer), w2, w2,
                  _layer_spec((1, LANES), layer), tab, tab, tab],
        out_specs=[pl.BlockSpec((None, NSA_KV_HEADS, ncp, LANES), lambda i: (i, 0, 0, 0)),
                   pl.BlockSpec((None, NSA_KV_HEADS, LANES, ncp), lambda i: (i, 0, 0, 0))],
        out_shape=[jax.ShapeDtypeStruct((b, NSA_KV_HEADS, ncp, LANES), BF16),
                   jax.ShapeDtypeStruct((b, NSA_KV_HEADS, LANES, ncp), BF16)],
        compiler_params=_params(("parallel",)),
        name="nsa_compress",
    )(kc, vc, w1k, w1v, w1k_raw, w1v_raw, pe, w2k, w2v_t, k_gain, *cmp_tabs)


def _selection_bias(imp_t, qb):
    nsb, nq = imp_t.shape
    j = lax.broadcasted_iota(jnp.int32, (nsb, nq), 0)
    t = qb * Q_BLOCK + lax.broadcasted_iota(jnp.int32, (nsb, nq), 1)
    cur = t >> SLC_SHIFT
    valid = j * SLC_BLOCK <= t
    forced = (j == 0) | (j == cur) | (j == cur - 1)
    j_f = j.astype(F32)
    start = jnp.where(valid & jnp.logical_not(forced), imp_t, -1.0)
    vals = start
    for _ in range(N_SELECT - N_FORCED):
        m = jnp.max(vals, axis=0, keepdims=True)
        idx = jnp.min(jnp.where(vals == m, j_f, float(nsb)), axis=0, keepdims=True)
        vals = jnp.where(j_f == idx, -2.0, vals)
    return jnp.where((forced & valid) | (vals != start), 0.0, MASK_VALUE)


def _attn_kernel(q_ref, kc_ref, vct_ref, ka_ref, kb_ref, va_ref, vb_ref, gate_ref, ovl_ref, o_ref,
                 sa_ref, sb_ref, acc_ref, *, seq):
    qb = pl.program_id(2)
    cols = NSA_REP * Q_BLOCK
    ncp = seq // CMP_STRIDE
    q_t = jnp.concatenate([q_ref[r] for r in range(NSA_REP)], axis=1)
    t_col = qb * Q_BLOCK + lax.broadcasted_iota(jnp.int32, (1, cols), 1) % Q_BLOCK
    t_q = qb * Q_BLOCK + lax.broadcasted_iota(jnp.int32, (1, Q_BLOCK), 1)

    def all_heads(per_query):
        return jnp.concatenate([per_query] * NSA_REP, axis=1)

    cmp_end = lax.broadcasted_iota(jnp.int32, (ncp, Q_BLOCK), 0) * CMP_STRIDE + (CMP_LEN - 1)
    s = _dot(kc_ref[...], q_t) + all_heads(jnp.where(cmp_end <= t_q, 0.0, NEG_INF))
    e = jnp.exp2(s - jnp.max(s, axis=0, keepdims=True)).astype(BF16)
    oc = _dot(jnp.concatenate([vct_ref[0:V_ROWS, :], ovl_ref[...]], axis=0), e)
    oc = oc * jnp.where(t_col >= CMP_LEN - 1, 1.0 / oc[HEAD_DIM:HEAD_DIM + 1, :], 0.0)
    o_cmp, imp = oc[0:V_ROWS], oc[V_ROWS:V_ROWS + LANES]
    imp_t = (imp[:, 0:Q_BLOCK] + imp[:, Q_BLOCK:2 * Q_BLOCK]
             + imp[:, 2 * Q_BLOCK:3 * Q_BLOCK] + imp[:, 3 * Q_BLOCK:4 * Q_BLOCK])
    bias = _selection_bias(imp_t, qb).astype(BF16)

    q_aug = jnp.concatenate([q_t, jnp.concatenate([bias] * NSA_REP, axis=1)], axis=0)

    span = WINDOW + Q_BLOCK
    start = pl.multiple_of(jnp.maximum(qb * Q_BLOCK - WINDOW, 0), LANES)
    start_blk = start // LANES
    kpos = start + lax.broadcasted_iota(jnp.int32, (span, Q_BLOCK), 0)
    in_window = (kpos <= t_q) & (kpos > t_q - WINDOW)
    s = _dot(kb_ref[pl.ds(start, span), :], q_t) + all_heads(jnp.where(in_window, 0.0, NEG_INF))
    e = jnp.exp2(s - jnp.max(s, axis=0, keepdims=True)).astype(BF16)
    acc_win = _dot(jnp.concatenate([vb_ref[start_blk + i, 0:V_ROWS, :] for i in range(span // LANES)], axis=1), e)
    o_win = acc_win * (1.0 / acc_win[HEAD_DIM:HEAD_DIM + 1, :])

    blocks_per_tile = KEY_TILE // LANES

    def score_tile(kt, s_ref, keys=KEY_TILE):
        k0 = pl.multiple_of(kt * KEY_TILE, KEY_TILE)
        s_ref[0:keys, :] = _dot(ka_ref[pl.ds(k0, keys), :], q_aug)

    def absorb_tile(kt, s_ref, carry, causal, keys=KEY_TILE):
        m_i, acc = carry
        blk0 = kt * blocks_per_tile
        s = s_ref[0:keys, :]
        if causal:
            kpos = kt * KEY_TILE + lax.broadcasted_iota(jnp.int32, (keys, Q_BLOCK), 0)
            s = s + all_heads(jnp.where(kpos <= t_q, 0.0, MASK_VALUE))
        m_new = jnp.maximum(m_i, jnp.max(s, axis=0, keepdims=True))
        p = jnp.exp2(s - m_new).astype(BF16)
        v_t = jnp.concatenate([va_ref[blk0 + i, 0:V_ROWS, :] for i in range(keys // LANES)], axis=1)
        return m_new, jnp.exp2(m_i - m_new) * acc + _dot(v_t, p)

    def pair_trip(j, carry):
        score_tile(2 * j + 1, sb_ref)
        carry = absorb_tile(2 * j, sa_ref, carry, causal=False)
        score_tile(2 * j + 2, sa_ref)
        return absorb_tile(2 * j + 1, sb_ref, carry, causal=False)

    def double_trip(j, carry):
        return pair_trip(2 * j + 1, pair_trip(2 * j, carry))

    def quad_trip(j, carry):
        return double_trip(2 * j + 1, double_trip(2 * j, carry))

    last_pair = qb // (2 * KEY_TILE // Q_BLOCK)
    score_tile(0, sa_ref)
    carry = (jnp.full((1, cols), NEG_INF, F32), jnp.zeros((V_ROWS, cols), F32))
    carry = lax.fori_loop(0, last_pair // 4, quad_trip, carry)
    carry = lax.fori_loop(2 * (last_pair // 4), last_pair // 2, double_trip, carry)
    carry = lax.fori_loop(2 * (last_pair // 2), last_pair, pair_trip, carry)
    blocks_per_key_tile = KEY_TILE // Q_BLOCK
    second_tile_live = (qb // blocks_per_key_tile) % 2 == 1
    for second_half_live in (False, True):
        keys = KEY_TILE if second_half_live else Q_BLOCK
        half_matches = (qb % blocks_per_key_tile == blocks_per_key_tile - 1) == second_half_live

        @pl.when(second_tile_live & half_matches)
        def _(keys=keys):
            score_tile(2 * last_pair + 1, sb_ref, keys)
            both = absorb_tile(2 * last_pair, sa_ref, carry, causal=False)
            acc_ref[...] = absorb_tile(2 * last_pair + 1, sb_ref, both, causal=True, keys=keys)[1]

        @pl.when(jnp.logical_not(second_tile_live) & half_matches)
        def _(keys=keys):
            acc_ref[...] = absorb_tile(2 * last_pair, sa_ref, carry, causal=True, keys=keys)[1]

    acc_slc = acc_ref[...]
    o_slc = acc_slc * (1.0 / acc_slc[HEAD_DIM:HEAD_DIM + 1, :])

    gates = gate_ref[...]
    heads = []
    for r in range(NSA_REP):
        cs = slice(r * Q_BLOCK, (r + 1) * Q_BLOCK)
        c = r * N_BRANCH
        heads.append(gates[c:c + 1, :] * o_cmp[0:HEAD_DIM, cs] + gates[c + 1:c + 2, :] * o_slc[0:HEAD_DIM, cs]
                     + gates[c + 2:c + 3, :] * o_win[0:HEAD_DIM, cs])
    o_ref[...] = jnp.concatenate(heads, axis=0).T


def _overlap_matrix(seq):
    ncp, nsb = seq // CMP_STRIDE, seq // SLC_BLOCK
    c_start = np.arange(ncp)[None, :] * CMP_STRIDE
    s_start = np.arange(nsb)[:, None] * SLC_BLOCK
    ovl = np.clip(np.minimum(c_start + CMP_LEN, s_start + SLC_BLOCK) - np.maximum(c_start, s_start), 0, None)
    ovl = ovl.astype(np.float32) / CMP_LEN
    ovl[:, ncp - 1] = 0.0
    return jnp.asarray(np.pad(ovl, ((0, LANES - nsb), (0, 0))))


def _nsa_attention(q_t, k_cmp, v_cmp_t, ka, kb, va_t, vb_t, gates_t):
    b, _, _, s = q_t.shape
    g = NSA_KV_HEADS
    ncp = s // CMP_STRIDE
    assert s // SLC_BLOCK <= LANES
    per_group = lambda *shape: pl.BlockSpec((None, None) + shape, lambda i, j, k: (i, j) + (0,) * len(shape),
                                            pipeline_mode=pl.Buffered(1))
    return pl.pallas_call(
        functools.partial(_attn_kernel, seq=s),
        grid=(b, g, s // Q_BLOCK),
        in_specs=[pl.BlockSpec((None, NSA_REP, LANES, Q_BLOCK), lambda i, j, k: (i, j, 0, k)),
                  per_group(ncp, LANES), per_group(LANES, ncp), per_group(s, 2 * LANES), per_group(s, LANES),
                  per_group(s // LANES, LANES, LANES), per_group(s // LANES, LANES, LANES),
                  pl.BlockSpec((None, None, GATE_ROWS, Q_BLOCK), lambda i, j, k: (i, j, 0, k)),
                  _const_spec((LANES, ncp))],
        out_specs=pl.BlockSpec((None, Q_BLOCK, NSA_REP * HEAD_DIM), lambda i, j, k: (i, k, j)),
        out_shape=jax.ShapeDtypeStruct((b, s, NSA_WIDTH), F32),
        scratch_shapes=[pltpu.VMEM((KEY_TILE, NSA_REP * Q_BLOCK), F32)] * 2
        + [pltpu.VMEM((V_ROWS, NSA_REP * Q_BLOCK), F32)],
        compiler_params=_params(("parallel", "parallel", "arbitrary")),
        name="nsa_attention",
    )(q_t, k_cmp, v_cmp_t, ka, kb, va_t, vb_t, gates_t, _overlap_matrix(s).astype(BF16))


def _s5_kernel(u_ref, mt_ref, bc_ref, cc_ref, a1_ref, a2_ref, d_ref, y_ref, h_ref, g_scr, hp_scr):
    @pl.when(pl.program_id(1) == 0)
    def _():
        h_ref[...] = jnp.zeros_like(h_ref)

    halves = range(S5_HALVES)
    u = [_fold_rows(u_ref.at[a], S5_CHUNK) for a in halves]
    ub = [x.astype(BF16) for x in u]
    for a in halves:
        g_scr[:, a * S5_HALF_FOLD:(a + 1) * S5_HALF_FOLD] = _dot(ub[a], bc_ref[a])
    y_local = [_dot(ub[a], mt_ref[a]) for a in halves]
    a1, a2 = a1_ref[...], a2_ref[...]
    n_state = S5_STATE // S5_HALVES

    def swap_re_im(h):
        parts = [h[:, k * n_state:(k + 1) * n_state] for k in range(2 * S5_HALVES)]
        return jnp.concatenate([parts[k ^ 1] for k in range(2 * S5_HALVES)], axis=1)

    def step(i, h):
        hp_scr[pl.ds(i, 1), :] = h
        return a1 * h + a2 * swap_re_im(h) + g_scr[pl.ds(i, 1), :]

    h_ref[...] = lax.fori_loop(0, g_scr.shape[0], step, h_ref[...], unroll=True)
    for a in halves:
        carried = _dot(hp_scr[:, a * S5_HALF_FOLD:(a + 1) * S5_HALF_FOLD].astype(BF16), cc_ref[a])
        _unfold_rows(y_ref.at[a], y_local[a] + carried + u[a] * d_ref[a:a + 1, :], S5_CHUNK)


def _s5_scan(u, layer, mats):
    b, _, s, _ = u.shape
    rows = s // S5_CHUNK
    tc = min(256, rows)
    tile = pl.BlockSpec((None, S5_HALVES, tc * S5_CHUNK, LANES), lambda i, j: (i, 0, j, 0))
    mat = _layer_spec((S5_HALVES, S5_HALF_FOLD, S5_HALF_FOLD), layer)
    row = _layer_spec((1, S5_HALVES * S5_HALF_FOLD), layer)
    return pl.pallas_call(
        _s5_kernel,
        grid=(b, rows // tc),
        in_specs=[tile, mat, mat, mat, row, row, _layer_spec((S5_HALVES, S5_HALF_FOLD), layer)],
        out_specs=tile,
        out_shape=jax.ShapeDtypeStruct((b, S5_HALVES, s, LANES), F32),
        scratch_shapes=[pltpu.VMEM((1, S5_HALVES * S5_HALF_FOLD), F32),
                        pltpu.VMEM((tc, S5_HALVES * S5_HALF_FOLD), F32),
                        pltpu.VMEM((tc, S5_HALVES * S5_HALF_FOLD), F32)],
        compiler_params=_params(("parallel", "arbitrary")),
        name="s5_scan",
    )(u, *mats)


def _s5_matrices(lam_re, lam_im, log_dt, b_re, b_im, c_re, c_im, d_skip):
    t0, ng, nh, npm = S5_CHUNK, S5_GROUPS, S5_H, S5_P
    gh = ng // S5_HALVES
    ein = functools.partial(jnp.einsum, precision=HIGHEST)
    lam = lax.complex(lam_re, lam_im)
    step = jnp.exp(log_dt)[:, None]
    lam_bar = jnp.exp(lam * step)
    b_bar = lax.complex(b_re, b_im) * ((lam_bar - 1.0) / lam)[..., None]
    c_mat = lax.complex(c_re, c_im)
    k = jnp.arange(t0 + 1, dtype=F32)[:, None, None]
    pw = jnp.exp((lam * step)[None] * k)

    def same_group(rows_per_group, cols_per_group):
        r = np.arange(gh * rows_per_group)[:, None] // rows_per_group
        c = np.arange(gh * cols_per_group)[None, :] // cols_per_group
        return jnp.asarray((r == c).astype(np.float32))

    def per_group_blocks(x, rows_per_group, cols_per_group):
        return jnp.tile(x, (1,) * (x.ndim - 1) + (gh,)) * same_group(rows_per_group, cols_per_group)

    kern = jnp.real(ein('ghp,kgp,gpq->kghq', c_mat, pw[:t0], b_bar))
    d_k = per_group_blocks(kern.transpose(0, 1, 3, 2).reshape(t0, S5_HALVES, gh * nh, nh), nh, nh)
    d_k = jnp.concatenate([d_k, jnp.zeros_like(d_k[:1])], axis=0)
    lag = np.arange(t0)[None, :] - np.arange(t0)[:, None]
    mt = d_k[np.where(lag >= 0, lag, t0)]
    mt = mt.transpose(2, 0, 3, 1, 4).reshape(S5_HALVES, S5_HALF_FOLD, S5_HALF_FOLD)
    b_j = (pw[t0 - 1 - np.arange(t0)][..., None] * b_bar[None]).transpose(0, 1, 3, 2)
    b_j = b_j.reshape(t0, S5_HALVES, gh * nh, npm)
    bc = jnp.concatenate([per_group_blocks(jnp.real(b_j), nh, npm), per_group_blocks(jnp.imag(b_j), nh, npm)],
                         axis=-1)
    bc = bc.transpose(1, 0, 2, 3).reshape(S5_HALVES, S5_HALF_FOLD, S5_HALF_FOLD)
    c_i = (c_mat[None] * pw[1:t0 + 1][:, :, None, :]).transpose(1, 3, 0, 2)
    c_i = c_i.reshape(S5_HALVES, gh * npm, t0, nh)
    mask = same_group(npm, nh)[:, None, :]
    cc = jnp.concatenate([jnp.tile(jnp.real(c_i), (1, 1, 1, gh)) * mask,
                          jnp.tile(-jnp.imag(c_i), (1, 1, 1, gh)) * mask], axis=1)
    cc = cc.reshape(S5_HALVES, S5_HALF_FOLD, S5_HALF_FOLD)
    a_chunk = pw[t0].reshape(S5_HALVES, gh * npm)
    a1 = jnp.concatenate([jnp.real(a_chunk), jnp.real(a_chunk)], axis=1).reshape(1, -1)
    a2 = jnp.concatenate([-jnp.imag(a_chunk), jnp.imag(a_chunk)], axis=1).reshape(1, -1)
    d_vec = jnp.tile(d_skip.reshape(S5_HALVES, gh * nh), (1, t0))
    return mt.astype(BF16), bc.astype(BF16), cc.astype(BF16), a1, a2, d_vec


def _rms_gain(y, gain):
    return y * lax.rsqrt(jnp.mean(y * y, axis=-1, keepdims=True) + EPS) * gain


def _outproj_kernel(x_ref, mod_ref, pool_ref, halo_ref, nsa_ref, s5_ref, pw_ref, pb_ref, ps_ref, on_ref,
                    gw_ref, gb_ref, wo_ref, o_ref, buf):
    j = pl.program_id(1)
    tm = x_ref.shape[0]
    v = pool_ref[...]
    buf[0:POOL_HALO, :] = jnp.where(j > 0, halo_ref[...], 0.0)
    buf[POOL_HALO:POOL_HALO + tm, :] = v
    second_group = (lax.broadcasted_iota(jnp.int32, (tm, LANES), 1) >> SLC_SHIFT) == 1
    t1 = (j * tm + 1 + lax.broadcasted_iota(jnp.int32, (tm, 1), 0)).astype(F32)
    pooled = []
    for half in range(POOL_WIDTH // LANES):
        cols = slice(half * LANES, (half + 1) * LANES)
        v_half = v[:, cols]
        run, k, means = v_half, 1, []
        for w in POOL_WINDOWS[2 * half:2 * half + 2]:
            while k < w:
                run = run + buf[POOL_HALO - k:POOL_HALO - k + tm, cols]
                k += 1
            means.append(run / jnp.minimum(t1, float(w)))
        pooled.append(jnp.where(second_group, means[1], means[0]) - v_half)
    y_pool = (_dot(jnp.concatenate(pooled, axis=1).astype(BF16), pw_ref[...]) + pb_ref[...]) * ps_ref[...]

    y = _gelu_tanh(jnp.concatenate([s5_ref[half] for half in range(S5_HALVES)], axis=-1))
    y_s5 = y * _sigmoid(_dot(y.astype(BF16), gw_ref[...]) + gb_ref[...])

    cat = jnp.concatenate(
        [_rms_gain(y_pool, on_ref[:, 0:POOL_WIDTH]),
         _rms_gain(nsa_ref[...], on_ref[:, POOL_WIDTH:POOL_WIDTH + NSA_WIDTH]),
         _rms_gain(y_s5, on_ref[:, POOL_WIDTH + NSA_WIDTH:])], axis=-1).astype(BF16)
    o_ref[...] = x_ref[...] + mod_ref[5:6, :] * _dot(cat, wo_ref[...])


def _outproj(x, layer, mod, u_pool, o_nsa, y_s5, pool_w_bd, pool_b, pool_scale, out_norm, glu_w, glu_b, w_out,
             tm=1024):
    b, s, d = x.shape
    tok = lambda width: pl.BlockSpec((None, tm, width), lambda i, j: (i, j, 0))
    halo_blocks = tm // POOL_HALO
    return pl.pallas_call(
        _outproj_kernel,
        grid=(b, s // tm),
        in_specs=[tok(d),
                  _mod_spec(layer),
                  tok(POOL_WIDTH),
                  pl.BlockSpec((None, POOL_HALO, POOL_WIDTH),
                               lambda i, j: (i, jnp.maximum(j * halo_blocks - 1, 0), 0)),
                  tok(NSA_WIDTH),
                  pl.BlockSpec((None, S5_HALVES, tm, LANES), lambda i, j: (i, 0, j, 0)),
                  _layer_spec((POOL_WIDTH, POOL_WIDTH), layer), _layer_spec((1, POOL_WIDTH), layer),
                  _layer_spec((1, POOL_WIDTH), layer), _layer_spec((1, d), layer),
                  _layer_spec((S5_WIDTH, S5_WIDTH), layer), _layer_spec((1, S5_WIDTH), layer),
                  _layer_spec((d, d), layer)],
        out_specs=tok(d),
        out_shape=jax.ShapeDtypeStruct((b, s, d), F32),
        scratch_shapes=[pltpu.VMEM((POOL_HALO + tm, POOL_WIDTH), F32)],
        compiler_params=_params(("parallel", "arbitrary")),
        name="mixer_out_proj",
    )(x, mod, u_pool, u_pool, o_nsa, y_s5, pool_w_bd, pool_b, pool_scale, out_norm, glu_w, glu_b, w_out)


def _rope_angles(pos):
    half = ROT_DIM // 2
    inv_freq = np.exp(-math.log(ROPE_THETA) * np.arange(half, dtype=np.float64) * (2.0 / ROT_DIM))
    return np.asarray(pos, np.float64)[:, None] * inv_freq[None, :]


def _rope_tables(pos):
    ang = _rope_angles(pos)
    cos, sin = np.cos(ang), np.sin(ang)
    n, half = ang.shape
    rest = HEAD_DIM - ROT_DIM
    cos_t = np.concatenate([cos, cos, np.ones((n, rest))], axis=1)
    sin_lo = np.concatenate([-sin, np.zeros((n, half + rest))], axis=1)
    sin_hi = np.concatenate([np.zeros((n, half)), sin, np.zeros((n, rest))], axis=1)
    return tuple(jnp.asarray(np.tile(t, (1, 2)), F32) for t in (cos_t, sin_lo, sin_hi))


def _pad_lanes(v):
    return jnp.pad(v, [(0, 0)] * (v.ndim - 1) + [(0, LANES - v.shape[-1])])


def _rope_tables_t(pos):
    ang = _rope_angles(pos).T
    return jnp.asarray(np.cos(ang), F32), jnp.asarray(np.sin(ang), F32)


def _arrange_w_in(w):
    d = w.shape[0]
    o1, o2, o3 = POOL_WIDTH, POOL_WIDTH + NSA_WIDTH, POOL_WIDTH + NSA_WIDTH + 6 * LANES
    kv = w[:, o2:o3].reshape(d, 6, NSA_KV_HEADS, HEAD_DIM)
    ka = jnp.concatenate([kv[:, 2], kv[:, 4]], axis=-1).reshape(d, NSA_KV_HEADS * LANES)
    n_gate = NSA_REP * N_BRANCH
    w_row = jnp.concatenate([w[:, :o1], ka, kv[:, 0].reshape(d, LANES), kv[:, 1].reshape(d, LANES),
                             w[:, o3 + NSA_KV_HEADS * n_gate:]], axis=1)
    q_t = w[:, o1:o2].T
    v_t = jnp.concatenate([kv[:, 3], kv[:, 5]], axis=-1).reshape(d, NSA_KV_HEADS * LANES).T
    gate_t = w[:, o3:o3 + NSA_KV_HEADS * n_gate].T.reshape(NSA_KV_HEADS, n_gate, d)
    gate_t = jnp.pad(gate_t, ((0, 0), (0, GATE_ROWS - n_gate), (0, 0))).reshape(NSA_KV_HEADS * GATE_ROWS, d)
    return w_row.astype(BF16), jnp.concatenate([q_t, v_t, gate_t], axis=0).astype(BF16)


def _expand_cmp_w1(w1):
    halves = w1.reshape(2, CMP_STRIDE, HEAD_DIM, CMP_HIDDEN)
    both = jnp.concatenate([halves[0], halves[1]], axis=-1)
    out = []
    for g in range(NSA_KV_HEADS):
        z = jnp.zeros((CMP_STRIDE, NSA_KV_HEADS, HEAD_DIM, 2 * CMP_HIDDEN), F32).at[:, g].set(both)
        out.append(z.reshape(CMP_STRIDE * LANES, 2 * CMP_HIDDEN))
    return jnp.stack(out).astype(BF16)


def _block_diag(w):
    g, c, _ = w.shape
    return jnp.einsum('gcd,gf->gcfd', w, jnp.eye(g, dtype=w.dtype)).reshape(g * c, g * c)


def _prepare_parameters(norm_ffn1, ffn1_w_in, ffn1_w_out, norm_mix, w_in, w_out, out_norm, pool_w, pool_b, pool_scale,
                        q_norm, k_norm, cmp_pe, cmp_k_w1, cmp_k_w2, cmp_v_w1, cmp_v_w2, s5_lam_re, s5_lam_im,
                        s5_log_dt, s5_b_re, s5_b_im, s5_c_re, s5_c_im, s5_d, glu_w, glu_b, norm_ffn2, ffn2_w_in,
                        ffn2_w_out):
    n_layers = norm_mix.shape[0]
    row = lambda v: v.reshape(n_layers, 1, -1)
    w_row, w_t = jax.vmap(_arrange_w_in)(w_in)
    return dict(
        ffn1=(row(norm_ffn1), ffn1_w_in.astype(BF16), ffn1_w_out.astype(BF16)),
        ffn2=(row(norm_ffn2), ffn2_w_in.astype(BF16), ffn2_w_out.astype(BF16)),
        inproj=(row(norm_mix), w_row, w_t, q_norm.reshape(n_layers, HEAD_DIM, 1),
                jnp.concatenate([k_norm[:, 1], k_norm[:, 2]], axis=-1).reshape(n_layers, 1, LANES)),
        compress=(jax.vmap(_expand_cmp_w1)(cmp_k_w1), jax.vmap(_expand_cmp_w1)(cmp_v_w1), cmp_k_w1, cmp_v_w1,
                  jnp.swapaxes(cmp_pe.reshape(n_layers, 2, CMP_LEN * HEAD_DIM), 1, 2),
                  _pad_lanes(cmp_k_w2).astype(BF16),
                  jnp.swapaxes(_pad_lanes(cmp_v_w2), 1, 2).astype(BF16), row(_pad_lanes(k_norm[:, 0]))),
        s5=jax.vmap(_s5_matrices)(s5_lam_re, s5_lam_im, s5_log_dt, s5_b_re, s5_b_im, s5_c_re, s5_c_im, s5_d),
        outproj=(jax.vmap(_block_diag)(pool_w).astype(BF16), row(pool_b), row(pool_scale), row(out_norm),
                 glu_w.astype(BF16), row(glu_b), w_out.astype(BF16)))


def _hybrid_layer(x, layer, mod, params, tabs, tabs_t, cmp_tabs):
    x = _ffn(x, layer, mod, *params["ffn1"], 0)
    u_pool, q_t, ka, kb, va_t, vb_t, kc, vc, gates_t, u_s5 = _inproj(
        x, layer, mod, *params["inproj"][:3], tabs, tabs_t, *params["inproj"][3:])
    k_cmp, v_cmp_t = _compress(kc, vc, layer, *params["compress"], cmp_tabs)
    o_nsa = _nsa_attention(q_t, k_cmp, v_cmp_t, ka, kb, va_t, vb_t, gates_t)
    y_s5 = _s5_scan(u_s5, layer, params["s5"])
    x = _outproj(x, layer, mod, u_pool, o_nsa, y_s5, *params["outproj"])
    return _ffn(x, layer, mod, *params["ffn2"], 6)


def kernel(x, c, ada_w, ada_b, norm_ffn1, ffn1_w_in, ffn1_w_out, norm_mix, w_in, w_out, out_norm, pool_w, pool_b, pool_scale, q_norm, k_norm, cmp_pe, cmp_k_w1, cmp_k_w2, cmp_v_w1, cmp_v_w2, s5_lam_re, s5_lam_im, s5_log_dt, s5_b_re, s5_b_im, s5_c_re, s5_c_im, s5_d, glu_w, glu_b, norm_ffn2, ffn2_w_in, ffn2_w_out):
    seq = x.shape[1]
    assert seq % (KEY_TILE * 4) == 0 and seq >= WINDOW + Q_BLOCK
    mod = _modulation(c, ada_w, ada_b)
    params = _prepare_parameters(norm_ffn1, ffn1_w_in, ffn1_w_out, norm_mix, w_in, w_out, out_norm, pool_w, pool_b,
                                 pool_scale, q_norm, k_norm, cmp_pe, cmp_k_w1, cmp_k_w2, cmp_v_w1, cmp_v_w2,
                                 s5_lam_re, s5_lam_im, s5_log_dt, s5_b_re, s5_b_im, s5_c_re, s5_c_im, s5_d, glu_w,
                                 glu_b, norm_ffn2, ffn2_w_in, ffn2_w_out)
    tabs = _rope_tables(np.arange(seq))
    tabs_t = _rope_tables_t(np.arange(seq))
    cmp_tabs = _rope_tables(np.arange(seq // CMP_STRIDE) * CMP_STRIDE + CMP_LEN - 1)
    for layer in range(ada_w.shape[0]):
        x = _hybrid_layer(x, layer, mod, params, tabs, tabs_t, cmp_tabs)
    return x
```
